```python
import jax
import jax.numpy as jnp
from jax import lax
import numpy as np

D_MODEL = 1024
BATCH = 8
SEQ = 2048
DEPTH = 2

GRID_W = 64
CTX_LEN = 256
HEAD_DIM = 64
ROPE_BASE = 10000.0
NORM_EPS = 1e-6
QBLOCK = 128
WINDOW = 128
N_MOD = 6

RG_WIDTH = D_MODEL // 2
RG_HEADS = RG_WIDTH // HEAD_DIM
RG_BLOCK = RG_WIDTH // RG_HEADS
RG_CONV = 4
RG_C = 8.0
ATT_HEADS = (D_MODEL // 2) // HEAD_DIM
ATT_KV_HEADS = ATT_HEADS // 4

S5_WIDTH = D_MODEL // 2
S5_GROUP = 16
S5_GROUPS = S5_WIDTH // S5_GROUP
S5_STATE = 64
WIN_HEADS = (D_MODEL // 2) // HEAD_DIM
WIN_KV_HEADS = WIN_HEADS // 4

FFN_HIDDEN = ((8 * D_MODEL // 3 + 127) // 128) * 128
FFN_CONV = 3

EVEN_IN = 2 * RG_WIDTH + (ATT_HEADS + 2 * ATT_KV_HEADS) * HEAD_DIM
EVEN_MIX = RG_WIDTH + ATT_HEADS * HEAD_DIM
ODD_IN = S5_WIDTH + (WIN_HEADS + 2 * WIN_KV_HEADS) * HEAD_DIM
ODD_MIX = S5_WIDTH + WIN_HEADS * HEAD_DIM

kernel_name = "hybrid_rglru_gqa_s5_swa_dit_trunk"


def _split_cols(t, sizes):
    idx = [int(v) for v in np.cumsum(sizes)]
    return jnp.split(t, idx, axis=-1)


def _flip(t, rev):
    return jnp.flip(t, axis=1) if rev else t


def rms_norm(x, g):
    xf = x.astype(jnp.float32)
    y = xf * lax.rsqrt(jnp.mean(xf * xf, axis=-1, keepdims=True) + NORM_EPS)
    return (y * g.astype(jnp.float32)).astype(x.dtype)


def adaln_norm(x, g, shift, scale):
    return rms_norm(x, g) * (1.0 + scale) + shift


def depthwise_conv(x, w, b):
    k = w.shape[0]
    left = (k - 1) // 2
    n = x.shape[1]
    xp = jnp.pad(x, ((0, 0), (left, k - 1 - left), (0, 0)))
    out = b
    for j in range(k):
        out = out + xp[:, j:j + n] * w[j]
    return out


def axial_rope_tables(n_tokens):
    rows = n_tokens // GRID_W
    row = jnp.repeat(jnp.arange(rows, dtype=jnp.float32), GRID_W)
    col = jnp.tile(jnp.arange(GRID_W, dtype=jnp.float32), rows)
    quarter = HEAD_DIM // 4
    inv_freq = ROPE_BASE ** (-jnp.arange(quarter, dtype=jnp.float32) / quarter)
    ang = jnp.concatenate([row[:, None] * inv_freq, col[:, None] * inv_freq], axis=-1)
    return jnp.cos(ang), jnp.sin(ang)


def apply_axial_rope(x, cos, sin):
    quarter = HEAD_DIM // 4
    xs = x.astype(jnp.float32).reshape(x.shape[:-1] + (2, 2, quarter))
    x1, x2 = xs[..., 0, :], xs[..., 1, :]
    cs = cos.reshape(cos.shape[0], 1, 2, quarter)
    sn = sin.reshape(sin.shape[0], 1, 2, quarter)
    out = jnp.stack([x1 * cs - x2 * sn, x2 * cs + x1 * sn], axis=-2)
    return out.reshape(x.shape).astype(x.dtype)


def head_norm(t, n_heads, g):
    return rms_norm(t.reshape(t.shape[:-1] + (n_heads, HEAD_DIM)), g)


def split_heads(t, n_heads):
    return t.reshape(t.shape[:-1] + (n_heads, HEAD_DIM))


def ctx_attention(qc, kc, vc, sink):
    bsz, n_ctx, n_heads, _ = qc.shape
    n_kv = kc.shape[2]
    grp = n_heads // n_kv
    qg = qc.reshape(bsz, n_ctx, n_kv, grp, HEAD_DIM)
    s = jnp.einsum('bqkgd,bskd->bkgqs', qg, kc).astype(jnp.float32) * (HEAD_DIM ** -0.5)
    if sink is None:
        p = jax.nn.softmax(s, axis=-1)
    else:
        sink_col = jnp.broadcast_to(sink.astype(jnp.float32).reshape(1, n_kv, grp, 1, 1), s.shape[:-1] + (1,))
        p = jax.nn.softmax(jnp.concatenate([s, sink_col], axis=-1), axis=-1)[..., :-1]
    o = jnp.einsum('bkgqs,bskd->bqkgd', p.astype(vc.dtype), vc)
    return o.reshape(bsz, n_ctx, n_heads * HEAD_DIM)


def global_gqa(q, k, v, qc, kc, vc, cos, sin):
    bsz, n, n_heads, _ = q.shape
    n_kv = k.shape[2]
    grp = n_heads // n_kv
    scale = HEAD_DIM ** -0.5
    q = apply_axial_rope(q, cos, sin)
    k = apply_axial_rope(k, cos, sin)
    k_all = jnp.concatenate([kc, k], axis=1)
    v_all = jnp.concatenate([vc, v], axis=1)
    nb = n // QBLOCK
    q_blocks = q.reshape(bsz, nb, QBLOCK, n_kv, grp, HEAD_DIM).transpose(1, 0, 2, 3, 4, 5)

    def one_block(qb):
        s = jnp.einsum('bqkgd,bskd->bkgqs', qb, k_all).astype(jnp.float32) * scale
        p = jax.nn.softmax(s, axis=-1).astype(v_all.dtype)
        return jnp.einsum('bkgqs,bskd->bqkgd', p, v_all)

    o = lax.map(one_block, q_blocks)
    o = o.transpose(1, 0, 2, 3, 4, 5).reshape(bsz, n, n_heads * HEAD_DIM)
    o_c = ctx_attention(qc, kc, vc, None) if qc is not None else None
    return o, o_c


def window_gqa_sink(q, k, v, qc, kc, vc, sink, cos, sin):
    bsz, n, n_heads, _ = q.shape
    n_kv = k.shape[2]
    grp = n_heads // n_kv
    n_ctx = kc.shape[1]
    scale = HEAD_DIM ** -0.5
    q = apply_axial_rope(q, cos, sin)
    k = apply_axial_rope(k, cos, sin)
    nb = n // QBLOCK
    qb = q.reshape(bsz, nb, QBLOCK, n_kv, grp, HEAD_DIM)
    pad = ((0, 0), (WINDOW, WINDOW), (0, 0), (0, 0))
    kb = jnp.pad(k, pad).reshape(bsz, nb + 2, QBLOCK, n_kv, HEAD_DIM)
    vb = jnp.pad(v, pad).reshape(bsz, nb + 2, QBLOCK, n_kv, HEAD_DIM)
    k_win = jnp.concatenate([kb[:, :-2], kb[:, 1:-1], kb[:, 2:]], axis=2)
    v_win = jnp.concatenate([vb[:, :-2], vb[:, 1:-1], vb[:, 2:]], axis=2)
    starts = jnp.arange(nb) * QBLOCK
    q_pos = starts[:, None] + jnp.arange(QBLOCK)[None, :]
    k_pos = starts[:, None] - WINDOW + jnp.arange(3 * QBLOCK)[None, :]
    valid = ((jnp.abs(q_pos[:, :, None] - k_pos[:, None, :]) <= WINDOW)
             & (k_pos[:, None, :] >= 0) & (k_pos[:, None, :] < n))
    s_loc = jnp.einsum('bnqkgd,bnskd->bnkgqs', qb, k_win).astype(jnp.float32) * scale
    s_loc = jnp.where(valid[None, :, None, None], s_loc, -jnp.inf)
    s_ctx = jnp.einsum('bnqkgd,bskd->bnkgqs', qb, kc).astype(jnp.float32) * scale
    sink_col = jnp.broadcast_to(sink.astype(jnp.float32).reshape(1, 1, n_kv, grp, 1, 1), s_ctx.shape[:-1] + (1,))
    p = jax.nn.softmax(jnp.concatenate([s_ctx, s_loc, sink_col], axis=-1), axis=-1).astype(v.dtype)
    o = (jnp.einsum('bnkgqs,bskd->bnqkgd', p[..., :n_ctx], vc)
         + jnp.einsum('bnkgqs,bnskd->bnqkgd', p[..., n_ctx:n_ctx + 3 * QBLOCK], v_win))
    o = o.reshape(bsz, n, n_heads * HEAD_DIM)
    o_c = ctx_attention(qc, kc, vc, sink) if qc is not None else None
    return o, o_c


def linear_scan(a, b, h0):
    b = b.at[:, 0].add(a[:, 0] * h0)

    def combine(left, right):
        return (left[0] * right[0], right[0] * left[1] + right[1])

    _, h = lax.associative_scan(combine, (a, b), axis=1)
    return h


def rglru_coeffs(x, wa, ba, wx, bx, lam):
    bsz, n, width = x.shape
    xh = x.astype(jnp.float32).reshape(bsz, n, RG_HEADS, RG_BLOCK)
    r = jax.nn.sigmoid(jnp.einsum('blhi,hij->blhj', xh, wa.astype(jnp.float32)) + ba.astype(jnp.float32).reshape(RG_HEADS, RG_BLOCK))
    i = jax.nn.sigmoid(jnp.einsum('blhi,hij->blhj', xh, wx.astype(jnp.float32)) + bx.astype(jnp.float32).reshape(RG_HEADS, RG_BLOCK))
    log_a = -RG_C * r * jax.nn.softplus(-lam.astype(jnp.float32).reshape(RG_HEADS, RG_BLOCK))
    a = jnp.exp(log_a)
    b = jnp.sqrt(-jnp.expm1(2.0 * log_a)) * (i * xh)
    return a.reshape(bsz, n, width), b.reshape(bsz, n, width)


def rglru_mixer(u, gate, uc, gate_c, conv_w, conv_b, wa, ba, wx, bx, lam, emit_ctx):
    u = depthwise_conv(u, conv_w, conv_b)
    uc = depthwise_conv(uc, conv_w, conv_b)
    y = jnp.zeros(u.shape, jnp.float32)
    yc = jnp.zeros(uc.shape, jnp.float32)
    for d in range(2):
        rev = d == 1
        ac, bc = rglru_coeffs(_flip(uc, rev), wa[d], ba[d], wx[d], bx[d], lam[d])
        hc = linear_scan(ac, bc, jnp.zeros_like(bc[:, 0]))
        al, bl = rglru_coeffs(_flip(u, rev), wa[d], ba[d], wx[d], bx[d], lam[d])
        hl = linear_scan(al, bl, hc[:, -1])
        y = y + _flip(hl, rev)
        if emit_ctx:
            yc = yc + _flip(hc, rev)
    out = y.astype(u.dtype) * jax.nn.gelu(gate)
    out_c = yc.astype(uc.dtype) * jax.nn.gelu(gate_c) if emit_ctx else None
    return out, out_c


def s5_discretize(lam_re, lam_im, log_step, b_re, b_im):
    dt = jnp.exp(log_step.astype(jnp.float32))[:, None]
    lr = lam_re.astype(jnp.float32)
    li = lam_im.astype(jnp.float32)
    mag = jnp.exp(lr * dt)
    ang = li * dt
    abr, abi = mag * jnp.cos(ang), mag * jnp.sin(ang)
    den = lr * lr + li * li
    nr, ni = abr - 1.0, abi
    kr = (nr * lr + ni * li) / den
    ki = (ni * lr - nr * li) / den
    br = b_re.astype(jnp.float32)
    bi = b_im.astype(jnp.float32)
    bbr = kr[..., None] * br - ki[..., None] * bi
    bbi = kr[..., None] * bi + ki[..., None] * br
    return abr, abi, bbr, bbi


def complex_diag_scan(a_re, a_im, b_re, b_im, h_re, h_im):
    b_re = b_re.at[:, 0].add(a_re * h_re - a_im * h_im)
    b_im = b_im.at[:, 0].add(a_re * h_im + a_im * h_re)
    n = b_re.shape[1]
    ar = jnp.broadcast_to(a_re, (1, n) + a_re.shape)
    ai = jnp.broadcast_to(a_im, (1, n) + a_im.shape)

    def combine(left, right):
        lar, lai, lbr, lbi = left
        rar, rai, rbr, rbi = right
        return (rar * lar - rai * lai, rar * lai + rai * lar,
                rar * lbr - rai * lbi + rbr, rar * lbi + rai * lbr + rbi)

    _, _, sr, si = lax.associative_scan(combine, (ar, ai, b_re, b_im), axis=1)
    return sr, si


def s5_mixer(u, uc, lam_re, lam_im, log_step, b_re, b_im, c_re, c_im, d_skip, glu_w, glu_b, emit_ctx):
    bsz, n, width = u.shape
    ug = u.astype(jnp.float32).reshape(bsz, n, S5_GROUPS, S5_GROUP)
    ucg = uc.astype(jnp.float32).reshape(bsz, uc.shape[1], S5_GROUPS, S5_GROUP)
    dg = d_skip.astype(jnp.float32).reshape(S5_GROUPS, S5_GROUP)
    y = ug * dg
    yc = ucg * dg
    for d in range(2):
        rev = d == 1
        abr, abi, bbr, bbi = s5_discretize(lam_re[d], lam_im[d], log_step[d], b_re[d], b_im[d])
        cr = c_re[d].astype(jnp.float32)
        ci = c_im[d].astype(jnp.float32)

        def drive(t):
            return jnp.einsum('blgh,gph->blgp', t, bbr), jnp.einsum('blgh,gph->blgp', t, bbi)

        def readout(sr, si):
            return jnp.einsum('blgp,ghp->blgh', sr, cr) - jnp.einsum('blgp,ghp->blgh', si, ci)

        pc_r, pc_i = drive(_flip(ucg, rev))
        zeros = jnp.zeros_like(pc_r[:, 0])
        sc_r, sc_i = complex_diag_scan(abr, abi, pc_r, pc_i, zeros, zeros)
        pl_r, pl_i = drive(_flip(ug, rev))
        sl_r, sl_i = complex_diag_scan(abr, abi, pl_r, pl_i, sc_r[:, -1], sc_i[:, -1])
        y = y + _flip(readout(sl_r, sl_i), rev)
        if emit_ctx:
            yc = yc + _flip(readout(sc_r, sc_i), rev)

    def glu(t):
        z = jax.nn.gelu(t.reshape(t.shape[0], t.shape[1], width)).astype(u.dtype)
        return z * jax.nn.sigmoid(z @ glu_w + glu_b)

    return glu(y), (glu(yc) if emit_ctx else None)


def conv_ffn(x, w_up, conv_w, conv_b, w_down):
    hid = depthwise_conv(x @ w_up, conv_w, conv_b)
    val, gate = jnp.split(hid, 2, axis=-1)
    return (val * jax.nn.silu(gate)) @ w_down


def even_mixers(xn, xnc, w_in, conv_w, conv_b, wa, ba, wx, bx, lam, qn, kn, cos, sin, emit_ctx):
    sizes = [RG_WIDTH, RG_WIDTH, ATT_HEADS * HEAD_DIM, ATT_KV_HEADS * HEAD_DIM]
    u, gate, q, k, v = _split_cols(xn @ w_in, sizes)
    uc, gate_c, qc, kc, vc = _split_cols(xnc @ w_in, sizes)
    a_out, a_out_c = rglru_mixer(u, gate, uc, gate_c, conv_w, conv_b, wa, ba, wx, bx, lam, emit_ctx)
    qc_n = head_norm(qc, ATT_HEADS, qn) if emit_ctx else None
    b_out, b_out_c = global_gqa(head_norm(q, ATT_HEADS, qn), head_norm(k, ATT_KV_HEADS, kn), split_heads(v, ATT_KV_HEADS),
                                qc_n, head_norm(kc, ATT_KV_HEADS, kn), split_heads(vc, ATT_KV_HEADS), cos, sin)
    mix = jnp.concatenate([a_out, b_out], axis=-1)
    mix_c = jnp.concatenate([a_out_c, b_out_c], axis=-1) if emit_ctx else None
    return mix, mix_c


def odd_mixers(xn, xnc, w_in, lam_re, lam_im, log_step, b_re, b_im, c_re, c_im, d_skip, glu_w, glu_b,
               qn, kn, sink, cos, sin, emit_ctx):
    sizes = [S5_WIDTH, WIN_HEADS * HEAD_DIM, WIN_KV_HEADS * HEAD_DIM]
    u, q, k, v = _split_cols(xn @ w_in, sizes)
    uc, qc, kc, vc = _split_cols(xnc @ w_in, sizes)
    c_out, c_out_c = s5_mixer(u, uc, lam_re, lam_im, log_step, b_re, b_im, c_re, c_im, d_skip, glu_w, glu_b, emit_ctx)
    qc_n = head_norm(qc, WIN_HEADS, qn) if emit_ctx else None
    d_out, d_out_c = window_gqa_sink(head_norm(q, WIN_HEADS, qn), head_norm(k, WIN_KV_HEADS, kn), split_heads(v, WIN_KV_HEADS),
                                     qc_n, head_norm(kc, WIN_KV_HEADS, kn), split_heads(vc, WIN_KV_HEADS), sink, cos, sin)
    mix = jnp.concatenate([c_out, d_out], axis=-1)
    mix_c = jnp.concatenate([c_out_c, d_out_c], axis=-1) if emit_ctx else None
    return mix, mix_c


def setup_inputs(seed: int = 0) -> dict:
    key = jax.random.key(seed)
    keys = iter(jax.random.split(key, 64))
    f32 = jnp.float32

    def nrm(shape, std):
        return jax.random.normal(next(keys), shape, f32) * std

    d = D_MODEL
    n_even = (DEPTH + 1) // 2
    n_odd = DEPTH // 2
    x = nrm((BATCH, SEQ, d), 1.0)
    c = nrm((BATCH, d), 1.0)
    ctx = nrm((BATCH, CTX_LEN, d), 1.0)
    c_ctx = nrm((d,), 1.0)
    mod_w = nrm((DEPTH, d, N_MOD * d), 0.5 * d ** -0.5)
    mod_b = nrm((DEPTH, N_MOD * d), 0.02)
    norm_g = 1.0 + nrm((DEPTH, 2, d), 0.02)
    ffn_up = nrm((DEPTH, d, 2 * FFN_HIDDEN), d ** -0.5)
    ffn_conv_w = nrm((DEPTH, FFN_CONV, 2 * FFN_HIDDEN), FFN_CONV ** -0.5)
    ffn_conv_b = nrm((DEPTH, 2 * FFN_HIDDEN), 0.01)
    ffn_down = nrm((DEPTH, FFN_HIDDEN, d), FFN_HIDDEN ** -0.5)
    ev_w_in = nrm((n_even, d, EVEN_IN), d ** -0.5)
    ev_w_out = nrm((n_even, EVEN_MIX, d), EVEN_MIX ** -0.5)
    rg_conv_w = nrm((n_even, RG_CONV, RG_WIDTH), RG_CONV ** -0.5)
    rg_conv_b = nrm((n_even, RG_WIDTH), 0.01)
    rg_wa = nrm((n_even, 2, RG_HEADS, RG_BLOCK, RG_BLOCK), RG_BLOCK ** -0.5)
    rg_ba = nrm((n_even, 2, RG_WIDTH), 0.01)
    rg_wx = nrm((n_even, 2, RG_HEADS, RG_BLOCK, RG_BLOCK), RG_BLOCK ** -0.5)
    rg_bx = nrm((n_even, 2, RG_WIDTH), 0.01)
    a_pow = jax.random.uniform(next(keys), (n_even, 2, RG_WIDTH), f32, 0.9, 0.999)
    a_base = a_pow ** (1.0 / RG_C)
    rg_lam = jnp.log(a_base) - jnp.log1p(-a_base)
    ga_qn = 1.0 + nrm((n_even, HEAD_DIM), 0.02)
    ga_kn = 1.0 + nrm((n_even, HEAD_DIM), 0.02)
    od_w_in = nrm((n_odd, d, ODD_IN), d ** -0.5)
    od_w_out = nrm((n_odd, ODD_MIX, d), ODD_MIX ** -0.5)
    s5_lam_re = -0.5 + nrm((n_odd, 2, S5_GROUPS, S5_STATE), 0.01)
    s5_lam_im = jnp.pi * jnp.arange(S5_STATE, dtype=f32) + nrm((n_odd, 2, S5_GROUPS, S5_STATE), 0.01)
    s5_log_step = jax.random.uniform(next(keys), (n_odd, 2, S5_GROUPS), f32, float(np.log(1e-3)), float(np.log(1e-1)))
    s5_b_re = nrm((n_odd, 2, S5_GROUPS, S5_STATE, S5_GROUP), (2 * S5_GROUP) ** -0.5)
    s5_b_im = nrm((n_odd, 2, S5_GROUPS, S5_STATE, S5_GROUP), (2 * S5_GROUP) ** -0.5)
    s5_c_re = nrm((n_odd, 2, S5_GROUPS, S5_GROUP, S5_STATE), S5_STATE ** -0.5)
    s5_c_im = nrm((n_odd, 2, S5_GROUPS, S5_GROUP, S5_STATE), S5_STATE ** -0.5)
    s5_d = nrm((n_odd, S5_WIDTH), 1.0)
    s5_glu_w = nrm((n_odd, S5_WIDTH, S5_WIDTH), S5_WIDTH ** -0.5)
    s5_glu_b = nrm((n_odd, S5_WIDTH), 0.01)
    wa_qn = 1.0 + nrm((n_odd, HEAD_DIM), 0.02)
    wa_kn = 1.0 + nrm((n_odd, HEAD_DIM), 0.02)
    wa_sink = nrm((n_odd, WIN_HEADS), 1.0)
    return {"x": x, "c": c, "ctx": ctx, "c_ctx": c_ctx,
            "mod_w": mod_w, "mod_b": mod_b, "norm_g": norm_g,
            "ffn_up": ffn_up, "ffn_conv_w": ffn_conv_w, "ffn_conv_b": ffn_conv_b, "ffn_down": ffn_down,
            "ev_w_in": ev_w_in, "ev_w_out": ev_w_out, "rg_conv_w": rg_conv_w, "rg_conv_b": rg_conv_b,
            "rg_wa": rg_wa, "rg_ba": rg_ba, "rg_wx": rg_wx, "rg_bx": rg_bx, "rg_lam": rg_lam,
            "ga_qn": ga_qn, "ga_kn": ga_kn,
            "od_w_in": od_w_in, "od_w_out": od_w_out, "s5_lam_re": s5_lam_re, "s5_lam_im": s5_lam_im,
            "s5_log_step": s5_log_step, "s5_b_re": s5_b_re, "s5_b_im": s5_b_im, "s5_c_re": s5_c_re,
            "s5_c_im": s5_c_im, "s5_d": s5_d, "s5_glu_w": s5_glu_w, "s5_glu_b": s5_glu_b,
            "wa_qn": wa_qn, "wa_kn": wa_kn, "wa_sink": wa_sink}


def reference(x, c, ctx, c_ctx, mod_w, mod_b, norm_g, ffn_up, ffn_conv_w, ffn_conv_b, ffn_down,
              ev_w_in, ev_w_out, rg_conv_w, rg_conv_b, rg_wa, rg_ba, rg_wx, rg_bx, rg_lam, ga_qn, ga_kn,
              od_w_in, od_w_out, s5_lam_re, s5_lam_im, s5_log_step, s5_b_re, s5_b_im, s5_c_re, s5_c_im,
              s5_d, s5_glu_w, s5_glu_b, wa_qn, wa_kn, wa_sink):
    n = x.shape[1]
    cos, sin = axial_rope_tables(n)
    cond = jax.nn.silu(c)
    cond_c = jax.nn.silu(c_ctx)
    h, hc = x, ctx
    for layer in range(DEPTH):
        emit_ctx = layer < DEPTH - 1
        mod = jnp.split((cond @ mod_w[layer] + mod_b[layer])[:, None, :], N_MOD, axis=-1)
        mod_c = jnp.split(cond_c @ mod_w[layer] + mod_b[layer], N_MOD, axis=-1)
        xn = adaln_norm(h, norm_g[layer, 0], mod[0], mod[1])
        xnc = adaln_norm(hc, norm_g[layer, 0], mod_c[0], mod_c[1])
        j = layer // 2
        if layer % 2 == 0:
            mix, mix_c = even_mixers(xn, xnc, ev_w_in[j], rg_conv_w[j], rg_conv_b[j], rg_wa[j], rg_ba[j],
                                     rg_wx[j], rg_bx[j], rg_lam[j], ga_qn[j], ga_kn[j], cos, sin, emit_ctx)
            w_out = ev_w_out[j]
        else:
            mix, mix_c = odd_mixers(xn, xnc, od_w_in[j], s5_lam_re[j], s5_lam_im[j], s5_log_step[j],
                                    s5_b_re[j], s5_b_im[j], s5_c_re[j], s5_c_im[j], s5_d[j], s5_glu_w[j],
                                    s5_glu_b[j], wa_qn[j], wa_kn[j], wa_sink[j], cos, sin, emit_ctx)
            w_out = od_w_out[j]
        h = h + mod[2] * (mix @ w_out)
        xn = adaln_norm(h, norm_g[layer, 1], mod[3], mod[4])
        h = h + mod[5] * conv_ffn(xn, ffn_up[layer], ffn_conv_w[layer], ffn_conv_b[layer], ffn_down[layer])
        if emit_ctx:
            hc = hc + mod_c[2] * (mix_c @ w_out)
            xnc = adaln_norm(hc, norm_g[layer, 1], mod_c[3], mod_c[4])
            hc = hc + mod_c[5] * conv_ffn(xnc, ffn_up[layer], ffn_conv_w[layer], ffn_conv_b[layer], ffn_down[layer])
    return h
```

```python
import functools
import math

import jax
import jax.numpy as jnp
from jax import lax
from jax.experimental import pallas as pl
from jax.experimental.pallas import tpu as pltpu

F32 = jnp.float32
BF16 = jnp.bfloat16

D_MODEL = 1024
BATCH = 8
SEQ = 2048
CTX_LEN = 256
L_TOT = SEQ + CTX_LEN
DEPTH = 2
GRID_W = 64
HEAD_DIM = 64
ROPE_BASE = 10000.0
NORM_EPS = 1e-6
WINDOW = 128
N_MOD = 6
MIX_W = 512
N_HEADS = 8
N_KV = 2
KV_W = N_KV * HEAD_DIM
RG_HEADS = 8
RG_CONV = 4
RG_C = 8.0
S5_GROUP = 16
S5_GROUPS = 32
S5_STATE = 64
S5_NBLK = 4
S5_BLK_STATE = (S5_GROUPS // S5_NBLK) * S5_STATE
FFN_HIDDEN = 2816
FFN_CONV = 3

SUBLANE = 8
LANE = 128
BF16_ROWS = 16
VMEM_LIMIT = 56 * 1024 * 1024

TM = 256
N_CTX_TILES = CTX_LEN // TM
RG_T = 128
S5_T = 64
TQ_G = 128
TK_G = 512
FFN_CH = 256
HEAD_PERM = (0, 4, 1, 5, 2, 6, 3, 7)


def _cparams(sem):
    return pltpu.CompilerParams(dimension_semantics=sem, vmem_limit_bytes=VMEM_LIMIT)


def _dot(a, b):
    return jnp.dot(a, b, preferred_element_type=F32)


def _gelu_tanh(x):
    return x * (0.5 * (1.0 + jnp.tanh(math.sqrt(2.0 / math.pi) * (x + 0.044715 * (x * x * x)))))


def _rms(x, g):
    ms = jnp.mean(x * x, axis=-1, keepdims=True)
    return x * lax.rsqrt(ms + NORM_EPS) * g


MOD_TN = 1536


def _mod_kernel(c_ref, w_ref, b_ref, o_ref):
    c = c_ref[...]
    a = c * jax.nn.sigmoid(c)
    w = w_ref[0]
    ah = a.astype(BF16)
    al = (a - ah.astype(F32)).astype(BF16)
    wh = w.astype(BF16)
    wl = (w - wh.astype(F32)).astype(BF16)
    o_ref[0] = _dot(ah, wh) + _dot(ah, wl) + _dot(al, wh) + b_ref[0]


def _mod_call(cvec, mod_w, mod_b):
    n = N_MOD * D_MODEL
    return pl.pallas_call(
        _mod_kernel,
        out_shape=jax.ShapeDtypeStruct((DEPTH, 16, n), F32),
        grid=(DEPTH, n // MOD_TN),
        in_specs=[
            pl.BlockSpec((16, D_MODEL), lambda l, k: (0, 0)),
            pl.BlockSpec((1, D_MODEL, MOD_TN), lambda l, k: (l, 0, k)),
            pl.BlockSpec((1, 1, MOD_TN), lambda l, k: (l, 0, k)),
        ],
        out_specs=pl.BlockSpec((1, 16, MOD_TN), lambda l, k: (l, 0, k)),
        compiler_params=_cparams(("parallel", "parallel")),
        name="mod",
    )(cvec, mod_w, mod_b.reshape(DEPTH, 1, n))


def _mod_row(b, j):
    return jnp.where(j < N_CTX_TILES, BATCH, b)


def _in_kernel(h_ref, sh_ref, sc_ref, g_ref, w_ref, rope_ref, gq_ref, gk_ref, gm_ref, *out_refs, n_tb):
    xn = _rms(h_ref[0], g_ref[...]) * (1.0 + sc_ref[0]) + sh_ref[0]
    proj = _dot(xn.astype(BF16), w_ref[...])
    for i in range(n_tb):
        out_refs[i][...] = proj[:, i * MIX_W:(i + 1) * MIX_W]
    q_ref, k_ref, v_ref = out_refs[n_tb:]
    off = n_tb * MIX_W
    cos = rope_ref[:, 0:LANE]
    sin_up = rope_ref[:, LANE:2 * LANE]
    sin_dn = rope_ref[:, 2 * LANE:3 * LANE]
    gm = gm_ref[...]
    for p in range(5):
        blk = proj[:, off + p * LANE: off + (p + 1) * LANE]
        sq = blk * blk
        hi = sq.astype(BF16)
        lo = (sq - hi.astype(F32)).astype(BF16)
        ms = _dot(hi, gm) + _dot(lo, gm)
        g = gq_ref[...] if p < 4 else gk_ref[...]
        bn = blk * lax.rsqrt(ms + NORM_EPS) * g
        up = pltpu.roll(bn, LANE - HEAD_DIM // 4, 1)
        dn = pltpu.roll(bn, HEAD_DIM // 4, 1)
        ro = bn * cos + up * sin_up + dn * sin_dn
        if p < 4:
            q_ref[0, :, p * LANE:(p + 1) * LANE] = (ro * (HEAD_DIM ** -0.5)).astype(BF16)
        else:
            k_ref[0] = ro.astype(BF16)
    v_ref[0] = proj[:, off + 5 * LANE: off + 6 * LANE].astype(BF16)


def _in_call(h, mod3, g, w, rope, gq, gk, gm, n_tb):
    n = w.shape[1]
    tb_shape = jax.ShapeDtypeStruct((L_TOT, BATCH * MIX_W), F32)
    tb_spec = pl.BlockSpec((TM, MIX_W), lambda b, j: (j, b))
    outs = pl.pallas_call(
        functools.partial(_in_kernel, n_tb=n_tb),
        out_shape=[tb_shape] * n_tb + [
            jax.ShapeDtypeStruct((BATCH, L_TOT, MIX_W), BF16),
            jax.ShapeDtypeStruct((BATCH, L_TOT, KV_W), BF16),
            jax.ShapeDtypeStruct((BATCH, L_TOT, KV_W), BF16),
        ],
        grid=(BATCH, L_TOT // TM),
        in_specs=[
            pl.BlockSpec((1, TM, D_MODEL), lambda b, j: (b, j, 0)),
            pl.BlockSpec((1, 1, D_MODEL), lambda b, j: (_mod_row(b, j), 0, 0)),
            pl.BlockSpec((1, 1, D_MODEL), lambda b, j: (_mod_row(b, j), 0, 1)),
            pl.BlockSpec((1, D_MODEL), lambda b, j: (0, 0)),
            pl.BlockSpec((D_MODEL, n), lambda b, j: (0, 0)),
            pl.BlockSpec((TM, 3 * LANE), lambda b, j: (j, 0)),
            pl.BlockSpec((1, LANE), lambda b, j: (0, 0)),
            pl.BlockSpec((1, LANE), lambda b, j: (0, 0)),
            pl.BlockSpec((LANE, LANE), lambda b, j: (0, 0)),
        ],
        out_specs=[tb_spec] * n_tb + [
            pl.BlockSpec((1, TM, MIX_W), lambda b, j: (b, j, 0)),
            pl.BlockSpec((1, TM, KV_W), lambda b, j: (b, j, 0)),
            pl.BlockSpec((1, TM, KV_W), lambda b, j: (b, j, 0)),
        ],
        compiler_params=_cparams(("parallel", "parallel")),
        name="in_proj",
    )(h, mod3, mod3, g, w, rope, gq, gk, gm)
    return outs


def _chunk_of(i, reverse, n_ctx_chunks, n_chunks):
    if not reverse:
        return i
    return jnp.where(i < n_ctx_chunks, n_ctx_chunks - 1 - i, n_chunks - 1 + n_ctx_chunks - i)


RG_ROWS = RG_T * BATCH
RG_NC = L_TOT // RG_T
RG_NCC = CTX_LEN // RG_T


def _softplus(z):
    return jnp.maximum(z, 0.0) + jnp.log1p(jnp.exp(-jnp.abs(z)))


def _rg_kernel(*refs, reverse):
    if reverse:
        (u_ref, up_ref, un_ref, cw_ref, cb_ref, wa_ref, ba_ref, wx_ref, bx_ref, lam_ref,
         out_ref, xbuf, abuf, bbuf, hst) = refs
    else:
        (u_ref, up_ref, un_ref, gate_ref, hb_ref, cw_ref, cb_ref, wa_ref, ba_ref, wx_ref, bx_ref, lam_ref,
         out_ref, xbuf, abuf, bbuf, hst) = refs
    i = pl.program_id(0)
    c = _chunk_of(i, reverse, RG_NCC, RG_NC)
    prev_zero = jnp.logical_or(c == 0, c == RG_NCC)
    next_zero = jnp.logical_or(c == RG_NCC - 1, c == RG_NC - 1)

    @pl.when(i == 0)
    def _():
        hst[...] = jnp.zeros_like(hst)

    xbuf[0:BATCH, :] = up_ref[...] * jnp.where(prev_zero, 0.0, 1.0)
    xbuf[BATCH:BATCH + RG_ROWS, :] = u_ref[...]
    xbuf[BATCH + RG_ROWS:3 * BATCH + RG_ROWS, :] = un_ref[...] * jnp.where(next_zero, 0.0, 1.0)
    uc = cb_ref[...]
    for k in range(RG_CONV):
        uc = uc + xbuf[k * BATCH:k * BATCH + RG_ROWS, :] * cw_ref[k:k + 1, :]

    ub = uc.astype(BF16)
    half = MIX_W // 2

    def gate(w_ref, b_ref):
        z = jnp.concatenate([_dot(ub[:, :half], w_ref[0]), _dot(ub[:, half:], w_ref[1])], axis=1)
        return jax.nn.sigmoid(z + b_ref[...])

    r = gate(wa_ref, ba_ref)
    ig = gate(wx_ref, bx_ref)
    log_a = (-RG_C) * r * _softplus(-lam_ref[...])
    a = jnp.exp(log_a)
    abuf[...] = a
    bbuf[...] = jnp.sqrt(-jnp.tanh(log_a) * (a * a + 1.0)) * (ig * uc)

    def step(s, h):
        t = (RG_T - 1 - s) if reverse else s
        r0 = pl.multiple_of(t * BATCH, BATCH)
        h = abuf[pl.ds(r0, BATCH), :] * h + bbuf[pl.ds(r0, BATCH), :]
        bbuf[pl.ds(r0, BATCH), :] = h
        return h

    hst[...] = lax.fori_loop(0, RG_T, step, hst[...], unroll=8)

    if reverse:
        out_ref[...] = bbuf[...]
    else:
        y = bbuf[...] + hb_ref[...]
        out_ref[...] = (y * _gelu_tanh(gate_ref[...])).astype(BF16)


def _rg_call(u, gate, hb, cw, cb, wa, ba, wx, bx, lam, reverse):
    def cidx(i):
        return _chunk_of(i, reverse, RG_NCC, RG_NC)

    main = pl.BlockSpec((RG_ROWS, MIX_W), lambda i: (cidx(i), 0))
    prev = pl.BlockSpec((BATCH, MIX_W), lambda i: (jnp.maximum(cidx(i) * RG_T - 1, 0), 0))
    n_next = L_TOT * BATCH // (2 * BATCH)
    nxt = pl.BlockSpec((2 * BATCH, MIX_W), lambda i: (jnp.minimum((cidx(i) + 1) * (RG_T // 2), n_next - 1), 0))

    def full(shape):
        return pl.BlockSpec(shape, lambda i: (0,) * len(shape))

    params = [cw, cb, wa, ba, wx, bx, lam]
    pspecs = [full(p.shape) for p in params]
    if reverse:
        args = [u, u, u] + params
        specs = [main, prev, nxt] + pspecs
        out_dtype = F32
    else:
        args = [u, u, u, gate, hb] + params
        specs = [main, prev, nxt, main, main] + pspecs
        out_dtype = BF16
    return pl.pallas_call(
        functools.partial(_rg_kernel, reverse=reverse),
        out_shape=jax.ShapeDtypeStruct((L_TOT * BATCH, MIX_W), out_dtype),
        grid=(RG_NC,),
        in_specs=specs,
        out_specs=main,
        scratch_shapes=[
            pltpu.VMEM((RG_ROWS + 3 * BATCH, MIX_W), F32),
            pltpu.VMEM((RG_ROWS, MIX_W), F32),
            pltpu.VMEM((RG_ROWS, MIX_W), F32),
            pltpu.VMEM((BATCH, MIX_W), F32),
        ],
        compiler_params=_cparams(("arbitrary",)),
        name="rglru_bwd" if reverse else "rglru_fwd",
    )(*args)


def _stack_heads(q_ref, tq):
    lane = lax.broadcasted_iota(jnp.int32, (tq, LANE), 1)
    low = lane < HEAD_DIM
    zero = jnp.zeros((tq, LANE), BF16)
    parts = []
    for p in range(MIX_W // LANE):
        qb = q_ref[0, :, p * LANE:(p + 1) * LANE]
        parts.append(jnp.where(low, qb, zero))
        parts.append(jnp.where(low, zero, qb))
    return jnp.concatenate(parts, axis=0), low


def _unstack_heads(o, low, tq):
    cols = []
    for p in range(MIX_W // LANE):
        cols.append(jnp.where(low, o[(2 * p) * tq:(2 * p + 1) * tq], o[(2 * p + 1) * tq:(2 * p + 2) * tq]))
    return jnp.concatenate(cols, axis=1)


def _qk(q, k):
    return lax.dot_general(q, k, (((1,), (1,)), ((), ())), preferred_element_type=F32)


def _gattn_kernel(q_ref, k_ref, v_ref, o_ref, m_ref, l_ref, acc_ref):
    j = pl.program_id(1)
    q, low = _stack_heads(q_ref, TQ_G)

    def chunk(start, size, first):
        s = _qk(q, k_ref[0, pl.ds(start, size), :])
        m_cur = jnp.max(s, axis=-1, keepdims=True)
        if first:
            m_new = m_cur
            p = jnp.exp(s - m_new)
            l_ref[...] = jnp.sum(p, axis=-1, keepdims=True)
            acc_ref[...] = _dot(p.astype(BF16), v_ref[0, pl.ds(start, size), :])
        else:
            m_old = m_ref[...]
            m_new = jnp.maximum(m_old, m_cur)
            alpha = jnp.exp(m_old - m_new)
            p = jnp.exp(s - m_new)
            l_ref[...] = alpha * l_ref[...] + jnp.sum(p, axis=-1, keepdims=True)
            acc_ref[...] = alpha * acc_ref[...] + _dot(p.astype(BF16), v_ref[0, pl.ds(start, size), :])
        m_ref[...] = m_new

    chunk(0, CTX_LEN, True)

    @pl.when(j >= CTX_LEN // TQ_G)
    def _():
        def body(c, carry):
            chunk(pl.multiple_of(CTX_LEN + c * TK_G, TK_G // 2), TK_G, False)
            return carry
        lax.fori_loop(0, SEQ // TK_G, body, 0)

    o = acc_ref[...] / l_ref[...]
    o_ref[0] = _unstack_heads(o, low, TQ_G).astype(BF16)


def _gattn_call(q, k, v):
    m = N_HEADS * TQ_G
    return pl.pallas_call(
        _gattn_kernel,
        out_shape=jax.ShapeDtypeStruct((BATCH, L_TOT, MIX_W), BF16),
        grid=(BATCH, L_TOT // TQ_G),
        in_specs=[
            pl.BlockSpec((1, TQ_G, MIX_W), lambda b, j: (b, j, 0)),
            pl.BlockSpec((1, L_TOT, KV_W), lambda b, j: (b, 0, 0)),
            pl.BlockSpec((1, L_TOT, KV_W), lambda b, j: (b, 0, 0)),
        ],
        out_specs=pl.BlockSpec((1, TQ_G, MIX_W), lambda b, j: (b, j, 0)),
        scratch_shapes=[
            pltpu.VMEM((m, 1), F32),
            pltpu.VMEM((m, 1), F32),
            pltpu.VMEM((m, LANE), F32),
        ],
        compiler_params=_cparams(("parallel", "arbitrary")),
        name="global_attn",
    )(q, k, v)


TQ_W = WINDOW
N_WBLK = SEQ // TQ_W


def _wattn_kernel(q_ref, k_ref, v_ref, sink_ref, o_ref):
    i = pl.program_id(1)
    q, low = _stack_heads(q_ref, TQ_W)
    base = CTX_LEN + i * TQ_W
    prev_start = pl.multiple_of(jnp.maximum(base - TQ_W, CTX_LEN), TQ_W)
    cur_start = pl.multiple_of(base, TQ_W)
    next_start = pl.multiple_of(jnp.minimum(base + TQ_W, L_TOT - TQ_W), TQ_W)

    def rows(ref):
        return jnp.concatenate([ref[0, 0:CTX_LEN, :], ref[0, pl.ds(prev_start, TQ_W), :],
                                ref[0, pl.ds(cur_start, TQ_W), :], ref[0, pl.ds(next_start, TQ_W), :]], axis=0)

    nk = CTX_LEN + 3 * TQ_W
    s = _qk(q, rows(k_ref)).reshape(N_HEADS, TQ_W, nk)
    r = lax.broadcasted_iota(jnp.int32, (TQ_W, nk), 0)
    col = lax.broadcasted_iota(jnp.int32, (TQ_W, nk), 1)
    cp = col - CTX_LEN
    cn = col - (CTX_LEN + 2 * TQ_W)
    ninf = -jnp.inf
    pen_prev = jnp.where(i > 0, 0.0, ninf)
    pen_next = jnp.where(i < N_WBLK - 1, 0.0, ninf)
    bias = jnp.where(col < CTX_LEN, 0.0,
                     jnp.where(cp < TQ_W, jnp.where(cp >= r, pen_prev, ninf),
                               jnp.where(cn < 0, 0.0, jnp.where(cn <= r, pen_next, ninf))))
    s = s + bias[None]
    sink = sink_ref[...]
    m = jnp.maximum(jnp.max(s, axis=-1, keepdims=True), sink)
    p = jnp.exp(s - m)
    l = jnp.sum(p, axis=-1, keepdims=True) + jnp.exp(sink - m)
    o = _dot(p.reshape(N_HEADS * TQ_W, nk).astype(BF16), rows(v_ref))
    o = o / l.reshape(N_HEADS * TQ_W, 1)
    o_ref[0] = _unstack_heads(o, low, TQ_W).astype(BF16)


def _wattn_call(q, k, v, sink):
    off = CTX_LEN // TQ_W
    return pl.pallas_call(
        _wattn_kernel,
        out_shape=jax.ShapeDtypeStruct((BATCH, SEQ, MIX_W), BF16),
        grid=(BATCH, N_WBLK),
        in_specs=[
            pl.BlockSpec((1, TQ_W, MIX_W), lambda b, i: (b, i + off, 0)),
            pl.BlockSpec((1, L_TOT, KV_W), lambda b, i: (b, 0, 0)),
            pl.BlockSpec((1, L_TOT, KV_W), lambda b, i: (b, 0, 0)),
            pl.BlockSpec((N_HEADS, 1, 1), lambda b, i: (0, 0, 0)),
        ],
        out_specs=pl.BlockSpec((1, TQ_W, MIX_W), lambda b, i: (b, i, 0)),
        compiler_params=_cparams(("parallel", "arbitrary")),
        name="window_attn",
    )(q, k, v, sink)


def _s5_disc_kernel(lr_ref, li_ref, ls_ref, br_ref, bi_ref, ar_ref, ai_ref, bbr_ref, bbi_ref):
    lr = lr_ref[0]
    li = li_ref[0]
    dt = jnp.exp(ls_ref[0])
    mag = jnp.exp(lr * dt)
    ang = li * dt
    abr = mag * jnp.cos(ang)
    abi = mag * jnp.sin(ang)
    den = lr * lr + li * li
    nr = abr - 1.0
    kr = (nr * lr + abi * li) / den
    ki = (abi * lr - nr * li) / den
    br = br_ref[0]
    bi = bi_ref[0]
    ar_ref[0] = abr
    ai_ref[0] = abi
    bbr_ref[0] = kr * br - ki * bi
    bbi_ref[0] = kr * bi + ki * br


def _s5_disc_call(lam_re, lam_im, log_step, bt_re, bt_im):
    n = S5_GROUPS * S5_STATE
    row = pl.BlockSpec((1, 1, n), lambda d: (d, 0, 0))
    mat = pl.BlockSpec((1, S5_GROUP, n), lambda d: (d, 0, 0))
    return pl.pallas_call(
        _s5_disc_kernel,
        out_shape=[jax.ShapeDtypeStruct((2, 1, n), F32)] * 2 + [jax.ShapeDtypeStruct((2, S5_GROUP, n), F32)] * 2,
        grid=(2,),
        in_specs=[row, row, row, mat, mat],
        out_specs=[row, row, mat, mat],
        compiler_params=_cparams(("parallel",)),
        name="s5_discretize",
    )(lam_re, lam_im, log_step, bt_re, bt_im)


S5_ROWS = S5_T * BATCH
S5_NC = L_TOT // S5_T
S5_NCC = CTX_LEN // S5_T
S5_SW = 2 * S5_BLK_STATE * S5_NBLK


def _s5_kernel(*refs, reverse):
    if reverse:
        u_ref, wd_ref, ar_ref, ai_ref, wr_ref, out_ref, pbuf, xst = refs
    else:
        u_ref, yb_ref, wd_ref, ar_ref, ai_ref, wr_ref, dsk_ref, gw_ref, gb_ref, out_ref, pbuf, xst = refs
    i = pl.program_id(0)

    @pl.when(i == 0)
    def _():
        xst[...] = jnp.zeros_like(xst)

    u = u_ref[...]
    ub = u.astype(BF16)
    nb = 2 * S5_BLK_STATE
    for k in range(S5_NBLK):
        pbuf[:, k * nb:(k + 1) * nb] = _dot(ub[:, k * LANE:(k + 1) * LANE], wd_ref[k])

    def step(s, x):
        t = (S5_T - 1 - s) if reverse else s
        r0 = pl.multiple_of(t * BATCH, BATCH)
        halves = []
        for k in range(S5_NBLK):
            halves.append(x[:, k * nb + S5_BLK_STATE:(k + 1) * nb])
            halves.append(x[:, k * nb:k * nb + S5_BLK_STATE])
        xsw = jnp.concatenate(halves, axis=1)
        x = ar_ref[...] * x + ai_ref[...] * xsw + pbuf[pl.ds(r0, BATCH), :]
        pbuf[pl.ds(r0, BATCH), :] = x
        return x

    xst[...] = lax.fori_loop(0, S5_T, step, xst[...], unroll=4)

    y = jnp.concatenate(
        [_dot(pbuf[:, k * nb:(k + 1) * nb].astype(BF16), wr_ref[k]) for k in range(S5_NBLK)], axis=1)
    if reverse:
        out_ref[...] = y
    else:
        y = u * dsk_ref[...] + y + yb_ref[...]
        z = _gelu_tanh(y)
        out_ref[...] = (z * jax.nn.sigmoid(_dot(z.astype(BF16), gw_ref[...]) + gb_ref[...])).astype(BF16)


def _s5_call(u, yb, wd, ar, ai, wr, dsk, gw, gb, reverse):
    def cidx(i):
        return _chunk_of(i, reverse, S5_NCC, S5_NC)

    main = pl.BlockSpec((S5_ROWS, MIX_W), lambda i: (cidx(i), 0))

    def full(shape):
        return pl.BlockSpec(shape, lambda i: (0,) * len(shape))

    if reverse:
        args = [u, wd, ar, ai, wr]
        specs = [main] + [full(a.shape) for a in args[1:]]
        out_dtype = F32
    else:
        args = [u, yb, wd, ar, ai, wr, dsk, gw, gb]
        specs = [main, main] + [full(a.shape) for a in args[2:]]
        out_dtype = BF16
    return pl.pallas_call(
        functools.partial(_s5_kernel, reverse=reverse),
        out_shape=jax.ShapeDtypeStruct((L_TOT * BATCH, MIX_W), out_dtype),
        grid=(S5_NC,),
        in_specs=specs,
        out_specs=main,
        scratch_shapes=[
            pltpu.VMEM((S5_ROWS, S5_SW), F32),
            pltpu.VMEM((BATCH, S5_SW), F32),
        ],
        compiler_params=_cparams(("arbitrary",)),
        name="s5_bwd" if reverse else "s5_fwd",
    )(*args)


def _out_kernel(h_ref, a_ref, b_ref, w_ref, gt_ref, o_ref):
    mix = _dot(a_ref[...], w_ref[0:MIX_W, :]) + _dot(b_ref[0], w_ref[MIX_W:2 * MIX_W, :])
    o_ref[0] = h_ref[0] + gt_ref[0] * mix


def _out_call(h, a_tb, b_out, w, mod3, latent_only):
    off = N_CTX_TILES if latent_only else 0
    n_rows = SEQ if latent_only else L_TOT
    row = (lambda b, j: b) if latent_only else _mod_row
    return pl.pallas_call(
        _out_kernel,
        out_shape=jax.ShapeDtypeStruct((BATCH, n_rows, D_MODEL), F32),
        grid=(BATCH, n_rows // TM),
        in_specs=[
            pl.BlockSpec((1, TM, D_MODEL), lambda b, j: (b, j + off, 0)),
            pl.BlockSpec((TM, MIX_W), lambda b, j: (j + off, b)),
            pl.BlockSpec((1, TM, MIX_W), lambda b, j: (b, j, 0)),
            pl.BlockSpec((2 * MIX_W, D_MODEL), lambda b, j: (0, 0)),
            pl.BlockSpec((1, 1, D_MODEL), lambda b, j: (row(b, j), 0, 2)),
        ],
        out_specs=pl.BlockSpec((1, TM, D_MODEL), lambda b, j: (b, j, 0)),
        compiler_params=_cparams(("parallel", "parallel")),
        name="mix_out",
    )(h, a_tb, b_out, w, mod3)


FFN_HALO = BF16_ROWS
FFN_NCH = FFN_HIDDEN // FFN_CH


def _ffn_kernel(h_ref, hp_ref, hn_ref, sh_ref, sc_ref, gt_ref, g_ref, wu_ref, cw_ref, cb_ref, wd_ref, o_ref,
                xn_buf, hv_buf, hg_buf, *, seg_tiles):
    j = pl.program_id(1)
    first = functools.reduce(jnp.logical_or, [j == s for s, _ in seg_tiles])
    last = functools.reduce(jnp.logical_or, [j == e - 1 for _, e in seg_tiles])
    g = g_ref[...]
    sc = 1.0 + sc_ref[0]
    sh = sh_ref[0]

    def norm(x):
        return _rms(x, g) * sc + sh

    h = h_ref[0]
    xn_buf[0:FFN_HALO, :] = (norm(hp_ref[0]) * jnp.where(first, 0.0, 1.0)).astype(BF16)
    xn_buf[FFN_HALO:FFN_HALO + TM, :] = norm(h).astype(BF16)
    xn_buf[FFN_HALO + TM:2 * FFN_HALO + TM, :] = (norm(hn_ref[0]) * jnp.where(last, 0.0, 1.0)).astype(BF16)
    xn = xn_buf[...]

    acc = jnp.zeros((TM, D_MODEL), F32)
    for c in range(FFN_NCH):
        cv = slice(c * FFN_CH, (c + 1) * FFN_CH)
        cg = slice(FFN_HIDDEN + c * FFN_CH, FFN_HIDDEN + (c + 1) * FFN_CH)
        hv_buf[...] = _dot(xn, wu_ref[:, cv])
        hg_buf[...] = _dot(xn, wu_ref[:, cg])

        def conv(buf, cols):
            out = cb_ref[:, cols]
            for k in range(FFN_CONV):
                out = out + buf[pl.ds(FFN_HALO - 1 + k, TM), :] * cw_ref[k:k + 1, cols]
            return out

        val = conv(hv_buf, cv)
        gate = conv(hg_buf, cg)
        act = (val * (gate * jax.nn.sigmoid(gate))).astype(BF16)
        acc = acc + _dot(act, wd_ref[cv, :])
    o_ref[0] = h + gt_ref[0] * acc


def _ffn_call(h, mod3, g, wu, cw, cb, wd, seg_tiles, mod_row):
    n_rows = h.shape[1]
    n_halo = n_rows // FFN_HALO
    per = TM // FFN_HALO
    return pl.pallas_call(
        functools.partial(_ffn_kernel, seg_tiles=seg_tiles),
        out_shape=jax.ShapeDtypeStruct((BATCH, n_rows, D_MODEL), F32),
        grid=(BATCH, n_rows // TM),
        in_specs=[
            pl.BlockSpec((1, TM, D_MODEL), lambda b, j: (b, j, 0)),
            pl.BlockSpec((1, FFN_HALO, D_MODEL), lambda b, j: (b, jnp.maximum(j * per - 1, 0), 0)),
            pl.BlockSpec((1, FFN_HALO, D_MODEL), lambda b, j: (b, jnp.minimum((j + 1) * per, n_halo - 1), 0)),
            pl.BlockSpec((1, 1, D_MODEL), lambda b, j: (mod_row(b, j), 0, 3)),
            pl.BlockSpec((1, 1, D_MODEL), lambda b, j: (mod_row(b, j), 0, 4)),
            pl.BlockSpec((1, 1, D_MODEL), lambda b, j: (mod_row(b, j), 0, 5)),
            pl.BlockSpec((1, D_MODEL), lambda b, j: (0, 0)),
            pl.BlockSpec((D_MODEL, 2 * FFN_HIDDEN), lambda b, j: (0, 0)),
            pl.BlockSpec((FFN_CONV, 2 * FFN_HIDDEN), lambda b, j: (0, 0)),
            pl.BlockSpec((1, 2 * FFN_HIDDEN), lambda b, j: (0, 0)),
            pl.BlockSpec((FFN_HIDDEN, D_MODEL), lambda b, j: (0, 0)),
        ],
        out_specs=pl.BlockSpec((1, TM, D_MODEL), lambda b, j: (b, j, 0)),
        scratch_shapes=[
            pltpu.VMEM((TM + 2 * FFN_HALO, D_MODEL), BF16),
            pltpu.VMEM((TM + 2 * FFN_HALO, FFN_CH), F32),
            pltpu.VMEM((TM + 2 * FFN_HALO, FFN_CH), F32),
        ],
        compiler_params=_cparams(("parallel", "parallel")),
        name="conv_ffn",
    )(h, h, h, mod3, mod3, mod3, g, wu, cw, cb, wd)


def _rope_table():
    rows = SEQ // GRID_W
    row = jnp.repeat(jnp.arange(rows, dtype=F32), GRID_W)
    col = jnp.tile(jnp.arange(GRID_W, dtype=F32), rows)
    quarter = HEAD_DIM // 4
    inv_freq = ROPE_BASE ** (-jnp.arange(quarter, dtype=F32) / quarter)
    ang = jnp.stack([row[:, None] * inv_freq, col[:, None] * inv_freq], axis=1)
    cos = jnp.cos(ang)
    sin = jnp.sin(ang)
    zero = jnp.zeros_like(sin)

    def lanes(first, second):
        t = jnp.stack([first, second], axis=2).reshape(SEQ, HEAD_DIM)
        return jnp.tile(t, (1, LANE // HEAD_DIM))

    tab = jnp.concatenate([lanes(cos, cos), lanes(-sin, zero), lanes(zero, sin)], axis=1)
    ctx = jnp.concatenate([jnp.ones((CTX_LEN, LANE), F32), jnp.zeros((CTX_LEN, 2 * LANE), F32)], axis=1)
    return jnp.concatenate([ctx, tab], axis=0)


def _perm_heads_cols(w):
    d = w.shape[0]
    return w.reshape(d, N_HEADS, HEAD_DIM)[:, jnp.array(HEAD_PERM)].reshape(d, N_HEADS * HEAD_DIM)


def _perm_heads_rows(w):
    n = w.shape[1]
    return w.reshape(N_HEADS, HEAD_DIM, n)[jnp.array(HEAD_PERM)].reshape(N_HEADS * HEAD_DIM, n)


def _block_diag(w, per):
    n, a, b = w.shape
    eye = jnp.eye(per, dtype=w.dtype)
    w4 = w.reshape(n // per, per, a, b)
    return jnp.einsum("ihab,hk->ihakb", w4, eye).reshape(n // per, per * a, per * b)


def _head_gain(g):
    return jnp.tile(g, LANE // HEAD_DIM).reshape(1, LANE)


def kernel(x, c, ctx, c_ctx, mod_w, mod_b, norm_g, ffn_up, ffn_conv_w, ffn_conv_b, ffn_down, ev_w_in, ev_w_out, rg_conv_w, rg_conv_b, rg_wa, rg_ba, rg_wx, rg_bx, rg_lam, ga_qn, ga_kn, od_w_in, od_w_out, s5_lam_re, s5_lam_im, s5_log_step, s5_b_re, s5_b_im, s5_c_re, s5_c_im, s5_d, s5_glu_w, s5_glu_b, wa_qn, wa_kn, wa_sink):
    cvec = jnp.concatenate([c, c_ctx[None], jnp.zeros((16 - BATCH - 1, D_MODEL), F32)], axis=0)
    mod = _mod_call(cvec, mod_w, mod_b)
    rope = _rope_table()
    gm = _block_diag(jnp.full((LANE // HEAD_DIM, HEAD_DIM, HEAD_DIM), 1.0 / HEAD_DIM, F32), LANE // HEAD_DIM)[0].astype(BF16)
    h = jnp.concatenate([ctx, x], axis=1)

    mod3 = mod[0].reshape(16, 1, N_MOD * D_MODEL)
    w_in = ev_w_in[0]
    qo = 2 * MIX_W
    w_in = jnp.concatenate([w_in[:, :qo], _perm_heads_cols(w_in[:, qo:qo + MIX_W]), w_in[:, qo + MIX_W:]], axis=1).astype(BF16)
    u_tb, gate_tb, q, k, v = _in_call(h, mod3, norm_g[0, 0].reshape(1, D_MODEL), w_in, rope,
                                       _head_gain(ga_qn[0]), _head_gain(ga_kn[0]), gm, n_tb=2)
    u_tb = u_tb.reshape(L_TOT * BATCH, MIX_W)
    gate_tb = gate_tb.reshape(L_TOT * BATCH, MIX_W)
    rg = []
    for d in range(2):
        rg.append([rg_conv_w[0], rg_conv_b[0].reshape(1, MIX_W),
                   _block_diag(rg_wa[0, d], 4).astype(BF16), rg_ba[0, d].reshape(1, MIX_W),
                   _block_diag(rg_wx[0, d], 4).astype(BF16), rg_bx[0, d].reshape(1, MIX_W),
                   rg_lam[0, d].reshape(1, MIX_W)])
    hb = _rg_call(u_tb, None, None, *rg[1], reverse=True)
    a_out = _rg_call(u_tb, gate_tb, hb, *rg[0], reverse=False)
    b_out = _gattn_call(q, k, v)
    w_out = jnp.concatenate([ev_w_out[0][:MIX_W], _perm_heads_rows(ev_w_out[0][MIX_W:])], axis=0).astype(BF16)
    h = _out_call(h, a_out.reshape(L_TOT, BATCH * MIX_W), b_out, w_out, mod3, latent_only=False)
    h = _ffn_call(h, mod3, norm_g[0, 1].reshape(1, D_MODEL), ffn_up[0].astype(BF16), ffn_conv_w[0],
                  ffn_conv_b[0].reshape(1, 2 * FFN_HIDDEN), ffn_down[0].astype(BF16),
                  seg_tiles=((0, N_CTX_TILES), (N_CTX_TILES, L_TOT // TM)), mod_row=_mod_row)

    mod3 = mod[1].reshape(16, 1, N_MOD * D_MODEL)
    w_in = od_w_in[0]
    qo = MIX_W
    w_in = jnp.concatenate([w_in[:, :qo], _perm_heads_cols(w_in[:, qo:qo + MIX_W]), w_in[:, qo + MIX_W:]], axis=1).astype(BF16)
    u_tb, q, k, v = _in_call(h, mod3, norm_g[1, 0].reshape(1, D_MODEL), w_in, rope,
                             _head_gain(wa_qn[0]), _head_gain(wa_kn[0]), gm, n_tb=1)
    u_tb = u_tb.reshape(L_TOT * BATCH, MIX_W)
    n_state = S5_GROUPS * S5_STATE
    bt_re = s5_b_re[0].transpose(0, 3, 1, 2).reshape(2, S5_GROUP, n_state)
    bt_im = s5_b_im[0].transpose(0, 3, 1, 2).reshape(2, S5_GROUP, n_state)
    log_step = jnp.repeat(s5_log_step[0], S5_STATE, axis=-1).reshape(2, 1, n_state)
    abr, abi, bbr, bbi = _s5_disc_call(s5_lam_re[0].reshape(2, 1, n_state), s5_lam_im[0].reshape(2, 1, n_state),
                                       log_step, bt_re, bt_im)
    gpb = S5_GROUPS // S5_NBLK
    eye = jnp.eye(gpb, dtype=F32)

    def drive_w(bb):
        return jnp.einsum("hbgp,gk->bghkp", bb.reshape(S5_GROUP, S5_NBLK, gpb, S5_STATE), eye).reshape(
            S5_NBLK, gpb * S5_GROUP, S5_BLK_STATE)

    def read_w(cc):
        return jnp.einsum("bghp,gk->bgpkh", cc.reshape(S5_NBLK, gpb, S5_GROUP, S5_STATE), eye).reshape(
            S5_NBLK, S5_BLK_STATE, gpb * S5_GROUP)

    def blocks(re, im):
        return jnp.concatenate([re.reshape(1, S5_NBLK, S5_BLK_STATE), im.reshape(1, S5_NBLK, S5_BLK_STATE)],
                               axis=2).reshape(1, S5_SW)

    s5 = []
    for d in range(2):
        wd = jnp.concatenate([drive_w(bbr[d]), drive_w(bbi[d])], axis=2).astype(BF16)
        wr = jnp.concatenate([read_w(s5_c_re[0, d]), -read_w(s5_c_im[0, d])], axis=1).astype(BF16)
        s5.append([wd, blocks(abr[d], abr[d]), blocks(-abi[d], abi[d]), wr])
    yb = _s5_call(u_tb, None, *s5[1], None, None, None, reverse=True)
    c_out = _s5_call(u_tb, yb, *s5[0], s5_d[0].reshape(1, MIX_W), s5_glu_w[0].astype(BF16),
                     s5_glu_b[0].reshape(1, MIX_W), reverse=False)
    sink = wa_sink[0][jnp.array(HEAD_PERM)].reshape(N_HEADS, 1, 1)
    d_out = _wattn_call(q, k, v, sink)
    w_out = jnp.concatenate([od_w_out[0][:MIX_W], _perm_heads_rows(od_w_out[0][MIX_W:])], axis=0).astype(BF16)
    h = _out_call(h, c_out.reshape(L_TOT, BATCH * MIX_W), d_out, w_out, mod3, latent_only=True)
    h = _ffn_call(h, mod3, norm_g[1, 1].reshape(1, D_MODEL), ffn_up[1].astype(BF16), ffn_conv_w[1],
                  ffn_conv_b[1].reshape(1, 2 * FFN_HIDDEN), ffn_down[1].astype(BF16),
                  seg_tiles=((0, SEQ // TM),), mod_row=lambda b, j: b)
    return h
```

```python
import functools
import math

import jax
import jax.numpy as jnp
from jax import lax
from jax.experimental import pallas as pl
from jax.experimental.pallas import tpu as pltpu

F32 = jnp.float32
BF16 = jnp.bfloat16

D_MODEL = 1024
BATCH = 8
SEQ = 2048
CTX_LEN = 256
L_TOT = SEQ + CTX_LEN
DEPTH = 2
GRID_W = 64
HEAD_DIM = 64
ROPE_BASE = 10000.0
NORM_EPS = 1e-6
WINDOW = 128
N_MOD = 6
MIX_W = 512
N_HEADS = 8
N_KV = 2
KV_W = N_KV * HEAD_DIM
V_W = 2 * KV_W
Q_SCALE = HEAD_DIM ** -0.5 * math.log2(math.e)
RG_HEADS = 8
RG_CONV = 4
RG_C = 8.0
S5_GROUP = 16
S5_GROUPS = 32
S5_STATE = 64
S5_NBLK = 4
S5_BLK_STATE = (S5_GROUPS // S5_NBLK) * S5_STATE
FFN_HIDDEN = 2816
FFN_CONV = 3

SUBLANE = 8
LANE = 128
BF16_ROWS = 16
VMEM_LIMIT = 56 * 1024 * 1024

TM = 256
N_CTX_TILES = CTX_LEN // TM
RG_T = 128
S5_T = 64
TQ_G = 128
TK_G = 512
FFN_CH = 256
HEAD_PERM = (0, 4, 1, 5, 2, 6, 3, 7)


def _cparams(sem):
    return pltpu.CompilerParams(dimension_semantics=sem, vmem_limit_bytes=VMEM_LIMIT)


def _dot(a, b):
    return jnp.dot(a, b, preferred_element_type=F32)


def _gelu_tanh(x):
    return x * (0.5 * (1.0 + jnp.tanh(math.sqrt(2.0 / math.pi) * (x + 0.044715 * (x * x * x)))))


def _rms(x, g):
    ms = jnp.mean(x * x, axis=-1, keepdims=True)
    return x * lax.rsqrt(ms + NORM_EPS) * g


MOD_TN = 1536


def _mod_kernel(c_ref, w_ref, b_ref, o_ref):
    c = c_ref[...]
    a = c * jax.nn.sigmoid(c)
    w = w_ref[0]
    ah = a.astype(BF16)
    al = (a - ah.astype(F32)).astype(BF16)
    wh = w.astype(BF16)
    wl = (w - wh.astype(F32)).astype(BF16)
    o_ref[0] = _dot(ah, wh) + _dot(ah, wl) + _dot(al, wh) + b_ref[0]


def _mod_call(cvec, mod_w, mod_b):
    n = N_MOD * D_MODEL
    return pl.pallas_call(
        _mod_kernel,
        out_shape=jax.ShapeDtypeStruct((DEPTH, 16, n), F32),
        grid=(DEPTH, n // MOD_TN),
        in_specs=[
            pl.BlockSpec((16, D_MODEL), lambda l, k: (0, 0)),
            pl.BlockSpec((1, D_MODEL, MOD_TN), lambda l, k: (l, 0, k)),
            pl.BlockSpec((1, 1, MOD_TN), lambda l, k: (l, 0, k)),
        ],
        out_specs=pl.BlockSpec((1, 16, MOD_TN), lambda l, k: (l, 0, k)),
        compiler_params=_cparams(("parallel", "parallel")),
        name="mod",
    )(cvec, mod_w, mod_b.reshape(DEPTH, 1, n))


def _mod_row(b, j):
    return jnp.where(j < N_CTX_TILES, BATCH, b)


def _in_kernel(h_ref, sh_ref, sc_ref, g_ref, w_ref, rope_ref, gq_ref, gk_ref, gm_ref, *out_refs, n_tb):
    xn = _rms(h_ref[0], g_ref[...]) * (1.0 + sc_ref[0]) + sh_ref[0]
    proj = _dot(xn.astype(BF16), w_ref[...])
    for i in range(n_tb):
        out_refs[i][...] = proj[:, i * MIX_W:(i + 1) * MIX_W]
    q_ref, k_ref, v_ref = out_refs[n_tb:]
    off = n_tb * MIX_W
    cos = rope_ref[:, 0:LANE]
    sin_up = rope_ref[:, LANE:2 * LANE]
    sin_dn = rope_ref[:, 2 * LANE:3 * LANE]
    gm = gm_ref[...]
    for p in range(5):
        blk = proj[:, off + p * LANE: off + (p + 1) * LANE]
        sq = blk * blk
        hi = sq.astype(BF16)
        lo = (sq - hi.astype(F32)).astype(BF16)
        ms = _dot(hi, gm) + _dot(lo, gm)
        g = gq_ref[...] if p < 4 else gk_ref[...]
        bn = blk * lax.rsqrt(ms + NORM_EPS) * g
        up = pltpu.roll(bn, LANE - HEAD_DIM // 4, 1)
        dn = pltpu.roll(bn, HEAD_DIM // 4, 1)
        ro = bn * cos + up * sin_up + dn * sin_dn
        if p < 4:
            q_ref[0, :, p * LANE:(p + 1) * LANE] = (ro * Q_SCALE).astype(BF16)
        else:
            k_ref[0] = ro.astype(BF16)
    v = proj[:, off + 5 * LANE: off + 6 * LANE].astype(BF16)
    v_ref[0] = jnp.concatenate([v, jnp.ones((TM, LANE), BF16)], axis=1)


def _in_call(h, mod3, g, w, rope, gq, gk, gm, n_tb):
    n = w.shape[1]
    tb_shape = jax.ShapeDtypeStruct((L_TOT, BATCH * MIX_W), F32)
    tb_spec = pl.BlockSpec((TM, MIX_W), lambda b, j: (j, b))
    outs = pl.pallas_call(
        functools.partial(_in_kernel, n_tb=n_tb),
        out_shape=[tb_shape] * n_tb + [
            jax.ShapeDtypeStruct((BATCH, L_TOT, MIX_W), BF16),
            jax.ShapeDtypeStruct((BATCH, L_TOT, KV_W), BF16),
            jax.ShapeDtypeStruct((BATCH, L_TOT, V_W), BF16),
        ],
        grid=(BATCH, L_TOT // TM),
        in_specs=[
            pl.BlockSpec((1, TM, D_MODEL), lambda b, j: (b, j, 0)),
            pl.BlockSpec((1, 1, D_MODEL), lambda b, j: (_mod_row(b, j), 0, 0)),
            pl.BlockSpec((1, 1, D_MODEL), lambda b, j: (_mod_row(b, j), 0, 1)),
            pl.BlockSpec((1, D_MODEL), lambda b, j: (0, 0)),
            pl.BlockSpec((D_MODEL, n), lambda b, j: (0, 0)),
            pl.BlockSpec((TM, 3 * LANE), lambda b, j: (j, 0)),
            pl.BlockSpec((1, LANE), lambda b, j: (0, 0)),
            pl.BlockSpec((1, LANE), lambda b, j: (0, 0)),
            pl.BlockSpec((LANE, LANE), lambda b, j: (0, 0)),
        ],
        out_specs=[tb_spec] * n_tb + [
            pl.BlockSpec((1, TM, MIX_W), lambda b, j: (b, j, 0)),
            pl.BlockSpec((1, TM, KV_W), lambda b, j: (b, j, 0)),
            pl.BlockSpec((1, TM, V_W), lambda b, j: (b, j, 0)),
        ],
        compiler_params=_cparams(("parallel", "parallel")),
        name="in_proj",
    )(h, mod3, mod3, g, w, rope, gq, gk, gm)
    return outs


def _chunk_of(i, reverse, n_ctx_chunks, n_chunks):
    if not reverse:
        return i
    return jnp.where(i < n_ctx_chunks, n_ctx_chunks - 1 - i, n_chunks - 1 + n_ctx_chunks - i)


RG_ROWS = RG_T * BATCH
RG_NC = L_TOT // RG_T
RG_NCC = CTX_LEN // RG_T


def _softplus(z):
    return jnp.maximum(z, 0.0) + jnp.log1p(jnp.exp(-jnp.abs(z)))


def _rg_kernel(*refs, reverse):
    if reverse:
        (u_ref, up_ref, un_ref, cw_ref, cb_ref, wa_ref, ba_ref, wx_ref, bx_ref, lam_ref,
         out_ref, xbuf, abuf, bbuf, hst) = refs
    else:
        (u_ref, up_ref, un_ref, gate_ref, hb_ref, cw_ref, cb_ref, wa_ref, ba_ref, wx_ref, bx_ref, lam_ref,
         out_ref, xbuf, abuf, bbuf, hst) = refs
    i = pl.program_id(0)
    c = _chunk_of(i, reverse, RG_NCC, RG_NC)
    prev_zero = jnp.logical_or(c == 0, c == RG_NCC)
    next_zero = jnp.logical_or(c == RG_NCC - 1, c == RG_NC - 1)

    @pl.when(i == 0)
    def _():
        hst[...] = jnp.zeros_like(hst)

    xbuf[0:BATCH, :] = up_ref[...] * jnp.where(prev_zero, 0.0, 1.0)
    xbuf[BATCH:BATCH + RG_ROWS, :] = u_ref[...]
    xbuf[BATCH + RG_ROWS:3 * BATCH + RG_ROWS, :] = un_ref[...] * jnp.where(next_zero, 0.0, 1.0)
    uc = cb_ref[...]
    for k in range(RG_CONV):
        uc = uc + xbuf[k * BATCH:k * BATCH + RG_ROWS, :] * cw_ref[k:k + 1, :]

    ub = uc.astype(BF16)
    half = MIX_W // 2

    def gate(w_ref, b_ref):
        z = jnp.concatenate([_dot(ub[:, :half], w_ref[0]), _dot(ub[:, half:], w_ref[1])], axis=1)
        return jax.nn.sigmoid(z + b_ref[...])

    r = gate(wa_ref, ba_ref)
    ig = gate(wx_ref, bx_ref)
    log_a = (-RG_C) * r * _softplus(-lam_ref[...])
    a = jnp.exp(log_a)
    abuf[...] = a
    bbuf[...] = jnp.sqrt(-jnp.tanh(log_a) * (a * a + 1.0)) * (ig * uc)

    def step(s, h):
        t = (RG_T - 1 - s) if reverse else s
        r0 = pl.multiple_of(t * BATCH, BATCH)
        h = abuf[pl.ds(r0, BATCH), :] * h + bbuf[pl.ds(r0, BATCH), :]
        bbuf[pl.ds(r0, BATCH), :] = h
        return h

    hst[...] = lax.fori_loop(0, RG_T, step, hst[...], unroll=8)

    if reverse:
        out_ref[...] = bbuf[...]
    else:
        y = bbuf[...] + hb_ref[...]
        out_ref[...] = (y * _gelu_tanh(gate_ref[...])).astype(BF16)


def _rg_call(u, gate, hb, cw, cb, wa, ba, wx, bx, lam, reverse):
    def cidx(i):
        return _chunk_of(i, reverse, RG_NCC, RG_NC)

    main = pl.BlockSpec((RG_ROWS, MIX_W), lambda i: (cidx(i), 0))
    prev = pl.BlockSpec((BATCH, MIX_W), lambda i: (jnp.maximum(cidx(i) * RG_T - 1, 0), 0))
    n_next = L_TOT * BATCH // (2 * BATCH)
    nxt = pl.BlockSpec((2 * BATCH, MIX_W), lambda i: (jnp.minimum((cidx(i) + 1) * (RG_T // 2), n_next - 1), 0))

    def full(shape):
        return pl.BlockSpec(shape, lambda i: (0,) * len(shape))

    params = [cw, cb, wa, ba, wx, bx, lam]
    pspecs = [full(p.shape) for p in params]
    if reverse:
        args = [u, u, u] + params
        specs = [main, prev, nxt] + pspecs
        out_dtype = F32
    else:
        args = [u, u, u, gate, hb] + params
        specs = [main, prev, nxt, main, main] + pspecs
        out_dtype = BF16
    return pl.pallas_call(
        functools.partial(_rg_kernel, reverse=reverse),
        out_shape=jax.ShapeDtypeStruct((L_TOT * BATCH, MIX_W), out_dtype),
        grid=(RG_NC,),
        in_specs=specs,
        out_specs=main,
        scratch_shapes=[
            pltpu.VMEM((RG_ROWS + 3 * BATCH, MIX_W), F32),
            pltpu.VMEM((RG_ROWS, MIX_W), F32),
            pltpu.VMEM((RG_ROWS, MIX_W), F32),
            pltpu.VMEM((BATCH, MIX_W), F32),
        ],
        compiler_params=_cparams(("arbitrary",)),
        name="rglru_bwd" if reverse else "rglru_fwd",
    )(*args)


def _stack_heads(q_ref, tq):
    lane = lax.broadcasted_iota(jnp.int32, (tq, LANE), 1)
    low = lane < HEAD_DIM
    zero = jnp.zeros((tq, LANE), BF16)
    parts = []
    for p in range(MIX_W // LANE):
        qb = q_ref[0, :, p * LANE:(p + 1) * LANE]
        parts.append(jnp.where(low, qb, zero))
        parts.append(jnp.where(low, zero, qb))
    return jnp.concatenate(parts, axis=0), low


def _unstack_heads(o, low, tq):
    cols = []
    for p in range(MIX_W // LANE):
        cols.append(jnp.where(low, o[(2 * p) * tq:(2 * p + 1) * tq], o[(2 * p + 1) * tq:(2 * p + 2) * tq]))
    return jnp.concatenate(cols, axis=1)


def _qk(q, k):
    return lax.dot_general(q, k, (((1,), (1,)), ((), ())), preferred_element_type=F32)


def _gattn_kernel(q_ref, k_ref, v_ref, o_ref):
    j = pl.program_id(1)
    q, low = _stack_heads(q_ref, TQ_G)

    def attend(nk):
        s = _qk(q, k_ref[0, 0:nk, :])
        m = jnp.max(s, axis=-1, keepdims=True)
        p = jnp.exp2(s - m).astype(BF16)
        acc = _dot(p, v_ref[0, 0:nk, :])
        o = acc[:, :KV_W] / acc[:, KV_W:]
        o_ref[0] = _unstack_heads(o, low, TQ_G).astype(BF16)

    @pl.when(j < CTX_LEN // TQ_G)
    def _():
        attend(CTX_LEN)

    @pl.when(j >= CTX_LEN // TQ_G)
    def _():
        attend(L_TOT)


def _gattn_call(q, k, v):
    return pl.pallas_call(
        _gattn_kernel,
        out_shape=jax.ShapeDtypeStruct((BATCH, L_TOT, MIX_W), BF16),
        grid=(BATCH, L_TOT // TQ_G),
        in_specs=[
            pl.BlockSpec((1, TQ_G, MIX_W), lambda b, j: (b, j, 0)),
            pl.BlockSpec((1, L_TOT, KV_W), lambda b, j: (b, 0, 0)),
            pl.BlockSpec((1, L_TOT, V_W), lambda b, j: (b, 0, 0)),
        ],
        out_specs=pl.BlockSpec((1, TQ_G, MIX_W), lambda b, j: (b, j, 0)),
        compiler_params=_cparams(("parallel", "arbitrary")),
        name="global_attn",
    )(q, k, v)


TQ_W = WINDOW
N_WBLK = SEQ // TQ_W


def _wattn_kernel(q_ref, k_ref, v_ref, sink_ref, o_ref):
    i = pl.program_id(1)
    q, low = _stack_heads(q_ref, TQ_W)
    base = CTX_LEN + i * TQ_W
    prev_start = pl.multiple_of(jnp.maximum(base - TQ_W, CTX_LEN), TQ_W)
    cur_start = pl.multiple_of(base, TQ_W)
    next_start = pl.multiple_of(jnp.minimum(base + TQ_W, L_TOT - TQ_W), TQ_W)

    def rows(ref):
        return jnp.concatenate([ref[0, 0:CTX_LEN, :], ref[0, pl.ds(prev_start, TQ_W), :],
                                ref[0, pl.ds(cur_start, TQ_W), :], ref[0, pl.ds(next_start, TQ_W), :]], axis=0)

    nk = CTX_LEN + 3 * TQ_W
    s = _qk(q, rows(k_ref)).reshape(N_HEADS, TQ_W, nk)
    r = lax.broadcasted_iota(jnp.int32, (TQ_W, nk), 0)
    col = lax.broadcasted_iota(jnp.int32, (TQ_W, nk), 1)
    cp = col - CTX_LEN
    cn = col - (CTX_LEN + 2 * TQ_W)
    ninf = -jnp.inf
    pen_prev = jnp.where(i > 0, 0.0, ninf)
    pen_next = jnp.where(i < N_WBLK - 1, 0.0, ninf)
    bias = jnp.where(col < CTX_LEN, 0.0,
                     jnp.where(cp < TQ_W, jnp.where(cp >= r, pen_prev, ninf),
                               jnp.where(cn < 0, 0.0, jnp.where(cn <= r, pen_next, ninf))))
    s = s + bias[None]
    sink = sink_ref[...] * math.log2(math.e)
    m = jnp.maximum(jnp.max(s, axis=-1, keepdims=True), sink)
    p = jnp.exp2(s - m).reshape(N_HEADS * TQ_W, nk).astype(BF16)
    acc = _dot(p, rows(v_ref))
    l = acc[:, KV_W:].reshape(N_HEADS, TQ_W, KV_W) + jnp.exp2(sink - m)
    o = acc[:, :KV_W] / l.reshape(N_HEADS * TQ_W, KV_W)
    o_ref[0] = _unstack_heads(o, low, TQ_W).astype(BF16)


def _wattn_call(q, k, v, sink):
    off = CTX_LEN // TQ_W
    return pl.pallas_call(
        _wattn_kernel,
        out_shape=jax.ShapeDtypeStruct((BATCH, SEQ, MIX_W), BF16),
        grid=(BATCH, N_WBLK),
        in_specs=[
            pl.BlockSpec((1, TQ_W, MIX_W), lambda b, i: (b, i + off, 0)),
            pl.BlockSpec((1, L_TOT, KV_W), lambda b, i: (b, 0, 0)),
            pl.BlockSpec((1, L_TOT, V_W), lambda b, i: (b, 0, 0)),
            pl.BlockSpec((N_HEADS, 1, 1), lambda b, i: (0, 0, 0)),
        ],
        out_specs=pl.BlockSpec((1, TQ_W, MIX_W), lambda b, i: (b, i, 0)),
        compiler_params=_cparams(("parallel", "arbitrary")),
        name="window_attn",
    )(q, k, v, sink)


def _s5_disc_kernel(lr_ref, li_ref, ls_ref, br_ref, bi_ref, ar_ref, ai_ref, bbr_ref, bbi_ref):
    lr = lr_ref[0]
    li = li_ref[0]
    dt = jnp.exp(ls_ref[0])
    mag = jnp.exp(lr * dt)
    ang = li * dt
    abr = mag * jnp.cos(ang)
    abi = mag * jnp.sin(ang)
    den = lr * lr + li * li
    nr = abr - 1.0
    kr = (nr * lr + abi * li) / den
    ki = (abi * lr - nr * li) / den
    br = br_ref[0]
    bi = bi_ref[0]
    ar_ref[0] = abr
    ai_ref[0] = abi
    bbr_ref[0] = kr * br - ki * bi
    bbi_ref[0] = kr * bi + ki * br


def _s5_disc_call(lam_re, lam_im, log_step, bt_re, bt_im):
    n = S5_GROUPS * S5_STATE
    row = pl.BlockSpec((1, 1, n), lambda d: (d, 0, 0))
    mat = pl.BlockSpec((1, S5_GROUP, n), lambda d: (d, 0, 0))
    return pl.pallas_call(
        _s5_disc_kernel,
        out_shape=[jax.ShapeDtypeStruct((2, 1, n), F32)] * 2 + [jax.ShapeDtypeStruct((2, S5_GROUP, n), F32)] * 2,
        grid=(2,),
        in_specs=[row, row, row, mat, mat],
        out_specs=[row, row, mat, mat],
        compiler_params=_cparams(("parallel",)),
        name="s5_discretize",
    )(lam_re, lam_im, log_step, bt_re, bt_im)


S5_ROWS = S5_T * BATCH
S5_NC = L_TOT // S5_T
S5_NCC = CTX_LEN // S5_T
S5_SW = 2 * S5_BLK_STATE * S5_NBLK


def _s5_kernel(*refs, reverse):
    if reverse:
        u_ref, wd_ref, ar_ref, ai_ref, wr_ref, out_ref, pbuf, xst = refs
    else:
        u_ref, yb_ref, wd_ref, ar_ref, ai_ref, wr_ref, dsk_ref, gw_ref, gb_ref, out_ref, pbuf, xst = refs
    i = pl.program_id(0)

    @pl.when(i == 0)
    def _():
        xst[...] = jnp.zeros_like(xst)

    u = u_ref[...]
    ub = u.astype(BF16)
    nb = 2 * S5_BLK_STATE
    for k in range(S5_NBLK):
        pbuf[:, k * nb:(k + 1) * nb] = _dot(ub[:, k * LANE:(k + 1) * LANE], wd_ref[k])

    def step(s, x):
        t = (S5_T - 1 - s) if reverse else s
        r0 = pl.multiple_of(t * BATCH, BATCH)
        halves = []
        for k in range(S5_NBLK):
            halves.append(x[:, k * nb + S5_BLK_STATE:(k + 1) * nb])
            halves.append(x[:, k * nb:k * nb + S5_BLK_STATE])
        xsw = jnp.concatenate(halves, axis=1)
        x = ar_ref[...] * x + ai_ref[...] * xsw + pbuf[pl.ds(r0, BATCH), :]
        pbuf[pl.ds(r0, BATCH), :] = x
        return x

    xst[...] = lax.fori_loop(0, S5_T, step, xst[...], unroll=4)

    y = jnp.concatenate(
        [_dot(pbuf[:, k * nb:(k + 1) * nb].astype(BF16), wr_ref[k]) for k in range(S5_NBLK)], axis=1)
    if reverse:
        out_ref[...] = y
    else:
        y = u * dsk_ref[...] + y + yb_ref[...]
        z = _gelu_tanh(y)
        out_ref[...] = (z * jax.nn.sigmoid(_dot(z.astype(BF16), gw_ref[...]) + gb_ref[...])).astype(BF16)


def _s5_call(u, yb, wd, ar, ai, wr, dsk, gw, gb, reverse):
    def cidx(i):
        return _chunk_of(i, reverse, S5_NCC, S5_NC)

    main = pl.BlockSpec((S5_ROWS, MIX_W), lambda i: (cidx(i), 0))

    def full(shape):
        return pl.BlockSpec(shape, lambda i: (0,) * len(shape))

    if reverse:
        args = [u, wd, ar, ai, wr]
        specs = [main] + [full(a.shape) for a in args[1:]]
        out_dtype = F32
    else:
        args = [u, yb, wd, ar, ai, wr, dsk, gw, gb]
        specs = [main, main] + [full(a.shape) for a in args[2:]]
        out_dtype = BF16
    return pl.pallas_call(
        functools.partial(_s5_kernel, reverse=reverse),
        out_shape=jax.ShapeDtypeStruct((L_TOT * BATCH, MIX_W), out_dtype),
        grid=(S5_NC,),
        in_specs=specs,
        out_specs=main,
        scratch_shapes=[
            pltpu.VMEM((S5_ROWS, S5_SW), F32),
            pltpu.VMEM((BATCH, S5_SW), F32),
        ],
        compiler_params=_cparams(("arbitrary",)),
        name="s5_bwd" if reverse else "s5_fwd",
    )(*args)


def _out_kernel(h_ref, a_ref, b_ref, w_ref, gt_ref, o_ref):
    mix = _dot(a_ref[...], w_ref[0:MIX_W, :]) + _dot(b_ref[0], w_ref[MIX_W:2 * MIX_W, :])
    o_ref[0] = h_ref[0] + gt_ref[0] * mix


def _out_call(h, a_tb, b_out, w, mod3, latent_only):
    off = N_CTX_TILES if latent_only else 0
    n_rows = SEQ if latent_only else L_TOT
    row = (lambda b, j: b) if latent_only else _mod_row
    return pl.pallas_call(
        _out_kernel,
        out_shape=jax.ShapeDtypeStruct((BATCH, n_rows, D_MODEL), F32),
        grid=(BATCH, n_rows // TM),
        in_specs=[
            pl.BlockSpec((1, TM, D_MODEL), lambda b, j: (b, j + off, 0)),
            pl.BlockSpec((TM, MIX_W), lambda b, j: (j + off, b)),
            pl.BlockSpec((1, TM, MIX_W), lambda b, j: (b, j, 0)),
            pl.BlockSpec((2 * MIX_W, D_MODEL), lambda b, j: (0, 0)),
            pl.BlockSpec((1, 1, D_MODEL), lambda b, j: (row(b, j), 0, 2)),
        ],
        out_specs=pl.BlockSpec((1, TM, D_MODEL), lambda b, j: (b, j, 0)),
        compiler_params=_cparams(("parallel", "parallel")),
        name="mix_out",
    )(h, a_tb, b_out, w, mod3)


FFN_HALO = BF16_ROWS
FFN_NCH = FFN_HIDDEN // FFN_CH


def _ffn_kernel(h_ref, hp_ref, hn_ref, sh_ref, sc_ref, gt_ref, g_ref, wu_ref, cw_ref, cb_ref, wd_ref, o_ref,
                xn_buf, hv_buf, hg_buf, *, seg_tiles):
    j = pl.program_id(1)
    first = functools.reduce(jnp.logical_or, [j == s for s, _ in seg_tiles])
    last = functools.reduce(jnp.logical_or, [j == e - 1 for _, e in seg_tiles])
    g = g_ref[...]
    sc = 1.0 + sc_ref[0]
    sh = sh_ref[0]

    def norm(x):
        return _rms(x, g) * sc + sh

    h = h_ref[0]
    xn_buf[0:FFN_HALO, :] = (norm(hp_ref[0]) * jnp.where(first, 0.0, 1.0)).astype(BF16)
    xn_buf[FFN_HALO:FFN_HALO + TM, :] = norm(h).astype(BF16)
    xn_buf[FFN_HALO + TM:2 * FFN_HALO + TM, :] = (norm(hn_ref[0]) * jnp.where(last, 0.0, 1.0)).astype(BF16)
    xn = xn_buf[...]

    acts = []
    for c in range(FFN_NCH):
        cv = slice(c * FFN_CH, (c + 1) * FFN_CH)
        cg = slice(FFN_HIDDEN + c * FFN_CH, FFN_HIDDEN + (c + 1) * FFN_CH)
        hv_buf[c] = _dot(xn, wu_ref[:, cv])
        hg_buf[c] = _dot(xn, wu_ref[:, cg])

        def conv(buf, cols):
            out = cb_ref[:, cols]
            for k in range(FFN_CONV):
                out = out + buf[c, pl.ds(FFN_HALO - 1 + k, TM), :] * cw_ref[k:k + 1, cols]
            return out

        val = conv(hv_buf, cv)
        gate = conv(hg_buf, cg)
        acts.append((val * (gate * jax.nn.sigmoid(gate))).astype(BF16))
    o_ref[0] = h + gt_ref[0] * _dot(jnp.concatenate(acts, axis=1), wd_ref[...])


def _ffn_call(h, mod3, g, wu, cw, cb, wd, seg_tiles, mod_row):
    n_rows = h.shape[1]
    n_halo = n_rows // FFN_HALO
    per = TM // FFN_HALO
    return pl.pallas_call(
        functools.partial(_ffn_kernel, seg_tiles=seg_tiles),
        out_shape=jax.ShapeDtypeStruct((BATCH, n_rows, D_MODEL), F32),
        grid=(BATCH, n_rows // TM),
        in_specs=[
            pl.BlockSpec((1, TM, D_MODEL), lambda b, j: (b, j, 0)),
            pl.BlockSpec((1, FFN_HALO, D_MODEL), lambda b, j: (b, jnp.maximum(j * per - 1, 0), 0)),
            pl.BlockSpec((1, FFN_HALO, D_MODEL), lambda b, j: (b, jnp.minimum((j + 1) * per, n_halo - 1), 0)),
            pl.BlockSpec((1, 1, D_MODEL), lambda b, j: (mod_row(b, j), 0, 3)),
            pl.BlockSpec((1, 1, D_MODEL), lambda b, j: (mod_row(b, j), 0, 4)),
            pl.BlockSpec((1, 1, D_MODEL), lambda b, j: (mod_row(b, j), 0, 5)),
            pl.BlockSpec((1, D_MODEL), lambda b, j: (0, 0)),
            pl.BlockSpec((D_MODEL, 2 * FFN_HIDDEN), lambda b, j: (0, 0)),
            pl.BlockSpec((FFN_CONV, 2 * FFN_HIDDEN), lambda b, j: (0, 0)),
            pl.BlockSpec((1, 2 * FFN_HIDDEN), lambda b, j: (0, 0)),
            pl.BlockSpec((FFN_HIDDEN, D_MODEL), lambda b, j: (0, 0)),
        ],
        out_specs=pl.BlockSpec((1, TM, D_MODEL), lambda b, j: (b, j, 0)),
        scratch_shapes=[
            pltpu.VMEM((TM + 2 * FFN_HALO, D_MODEL), BF16),
            pltpu.VMEM((FFN_NCH, TM + 2 * FFN_HALO, FFN_CH), F32),
            pltpu.VMEM((FFN_NCH, TM + 2 * FFN_HALO, FFN_CH), F32),
        ],
        compiler_params=_cparams(("parallel", "parallel")),
        name="conv_ffn",
    )(h, h, h, mod3, mod3, mod3, g, wu, cw, cb, wd)


def _rope_table():
    rows = SEQ // GRID_W
    row = jnp.repeat(jnp.arange(rows, dtype=F32), GRID_W)
    col = jnp.tile(jnp.arange(GRID_W, dtype=F32), rows)
    quarter = HEAD_DIM // 4
    inv_freq = ROPE_BASE ** (-jnp.arange(quarter, dtype=F32) / quarter)
    ang = jnp.stack([row[:, None] * inv_freq, col[:, None] * inv_freq], axis=1)
    cos = jnp.cos(ang)
    sin = jnp.sin(ang)
    zero = jnp.zeros_like(sin)

    def lanes(first, second):
        t = jnp.stack([first, second], axis=2).reshape(SEQ, HEAD_DIM)
        return jnp.tile(t, (1, LANE // HEAD_DIM))

    tab = jnp.concatenate([lanes(cos, cos), lanes(-sin, zero), lanes(zero, sin)], axis=1)
    ctx = jnp.concatenate([jnp.ones((CTX_LEN, LANE), F32), jnp.zeros((CTX_LEN, 2 * LANE), F32)], axis=1)
    return jnp.concatenate([ctx, tab], axis=0)


def _perm_heads_cols(w):
    d = w.shape[0]
    return w.reshape(d, N_HEADS, HEAD_DIM)[:, jnp.array(HEAD_PERM)].reshape(d, N_HEADS * HEAD_DIM)


def _perm_heads_rows(w):
    n = w.shape[1]
    return w.reshape(N_HEADS, HEAD_DIM, n)[jnp.array(HEAD_PERM)].reshape(N_HEADS * HEAD_DIM, n)


def _block_diag(w, per):
    n, a, b = w.shape
    eye = jnp.eye(per, dtype=w.dtype)
    w4 = w.reshape(n // per, per, a, b)
    return jnp.einsum("ihab,hk->ihakb", w4, eye).reshape(n // per, per * a, per * b)


def _head_gain(g):
    return jnp.tile(g, LANE // HEAD_DIM).reshape(1, LANE)


def kernel(x, c, ctx, c_ctx, mod_w, mod_b, norm_g, ffn_up, ffn_conv_w, ffn_conv_b, ffn_down, ev_w_in, ev_w_out, rg_conv_w, rg_conv_b, rg_wa, rg_ba, rg_wx, rg_bx, rg_lam, ga_qn, ga_kn, od_w_in, od_w_out, s5_lam_re, s5_lam_im, s5_log_step, s5_b_re, s5_b_im, s5_c_re, s5_c_im, s5_d, s5_glu_w, s5_glu_b, wa_qn, wa_kn, wa_sink):
    cvec = jnp.concatenate([c, c_ctx[None], jnp.zeros((16 - BATCH - 1, D_MODEL), F32)], axis=0)
    mod = _mod_call(cvec, mod_w, mod_b)
    rope = _rope_table()
    gm = _block_diag(jnp.full((LANE // HEAD_DIM, HEAD_DIM, HEAD_DIM), 1.0 / HEAD_DIM, F32), LANE // HEAD_DIM)[0].astype(BF16)
    h = jnp.concatenate([ctx, x], axis=1)

    mod3 = mod[0].reshape(16, 1, N_MOD * D_MODEL)
    w_in = ev_w_in[0]
    qo = 2 * MIX_W
    w_in = jnp.concatenate([w_in[:, :qo], _perm_heads_cols(w_in[:, qo:qo + MIX_W]), w_in[:, qo + MIX_W:]], axis=1).astype(BF16)
    u_tb, gate_tb, q, k, v = _in_call(h, mod3, norm_g[0, 0].reshape(1, D_MODEL), w_in, rope,
                                       _head_gain(ga_qn[0]), _head_gain(ga_kn[0]), gm, n_tb=2)
    u_tb = u_tb.reshape(L_TOT * BATCH, MIX_W)
    gate_tb = gate_tb.reshape(L_TOT * BATCH, MIX_W)
    rg = []
    for d in range(2):
        rg.append([rg_conv_w[0], rg_conv_b[0].reshape(1, MIX_W),
                   _block_diag(rg_wa[0, d], 4).astype(BF16), rg_ba[0, d].reshape(1, MIX_W),
                   _block_diag(rg_wx[0, d], 4).astype(BF16), rg_bx[0, d].reshape(1, MIX_W),
                   rg_lam[0, d].reshape(1, MIX_W)])
    hb = _rg_call(u_tb, None, None, *rg[1], reverse=True)
    a_out = _rg_call(u_tb, gate_tb, hb, *rg[0], reverse=False)
    b_out = _gattn_call(q, k, v)
    w_out = jnp.concatenate([ev_w_out[0][:MIX_W], _perm_heads_rows(ev_w_out[0][MIX_W:])], axis=0).astype(BF16)
    h = _out_call(h, a_out.reshape(L_TOT, BATCH * MIX_W), b_out, w_out, mod3, latent_only=False)
    h = _ffn_call(h, mod3, norm_g[0, 1].reshape(1, D_MODEL), ffn_up[0].astype(BF16), ffn_conv_w[0],
                  ffn_conv_b[0].reshape(1, 2 * FFN_HIDDEN), ffn_down[0].astype(BF16),
                  seg_tiles=((0, N_CTX_TILES), (N_CTX_TILES, L_TOT // TM)), mod_row=_mod_row)

    mod3 = mod[1].reshape(16, 1, N_MOD * D_MODEL)
    w_in = od_w_in[0]
    qo = MIX_W
    w_in = jnp.concatenate([w_in[:, :qo], _perm_heads_cols(w_in[:, qo:qo + MIX_W]), w_in[:, qo + MIX_W:]], axis=1).astype(BF16)
    u_tb, q, k, v = _in_call(h, mod3, norm_g[1, 0].reshape(1, D_MODEL), w_in, rope,
                             _head_gain(wa_qn[0]), _head_gain(wa_kn[0]), gm, n_tb=1)
    u_tb = u_tb.reshape(L_TOT * BATCH, MIX_W)
    n_state = S5_GROUPS * S5_STATE
    bt_re = s5_b_re[0].transpose(0, 3, 1, 2).reshape(2, S5_GROUP, n_state)
    bt_im = s5_b_im[0].transpose(0, 3, 1, 2).reshape(2, S5_GROUP, n_state)
    log_step = jnp.repeat(s5_log_step[0], S5_STATE, axis=-1).reshape(2, 1, n_state)
    abr, abi, bbr, bbi = _s5_disc_call(s5_lam_re[0].reshape(2, 1, n_state), s5_lam_im[0].reshape(2, 1, n_state),
                                       log_step, bt_re, bt_im)
    gpb = S5_GROUPS // S5_NBLK
    eye = jnp.eye(gpb, dtype=F32)

    def drive_w(bb):
        return jnp.einsum("hbgp,gk->bghkp", bb.reshape(S5_GROUP, S5_NBLK, gpb, S5_STATE), eye).reshape(
            S5_NBLK, gpb * S5_GROUP, S5_BLK_STATE)

    def read_w(cc):
        return jnp.einsum("bghp,gk->bgpkh", cc.reshape(S5_NBLK, gpb, S5_GROUP, S5_STATE), eye).reshape(
            S5_NBLK, S5_BLK_STATE, gpb * S5_GROUP)

    def blocks(re, im):
        return jnp.concatenate([re.reshape(1, S5_NBLK, S5_BLK_STATE), im.reshape(1, S5_NBLK, S5_BLK_STATE)],
                               axis=2).reshape(1, S5_SW)

    s5 = []
    for d in range(2):
        wd = jnp.concatenate([drive_w(bbr[d]), drive_w(bbi[d])], axis=2).astype(BF16)
        wr = jnp.concatenate([read_w(s5_c_re[0, d]), -read_w(s5_c_im[0, d])], axis=1).astype(BF16)
        s5.append([wd, blocks(abr[d], abr[d]), blocks(-abi[d], abi[d]), wr])
    yb = _s5_call(u_tb, None, *s5[1], None, None, None, reverse=True)
    c_out = _s5_call(u_tb, yb, *s5[0], s5_d[0].reshape(1, MIX_W), s5_glu_w[0].astype(BF16),
                     s5_glu_b[0].reshape(1, MIX_W), reverse=False)
    sink = wa_sink[0][jnp.array(HEAD_PERM)].reshape(N_HEADS, 1, 1)
    d_out = _wattn_call(q, k, v, sink)
    w_out = jnp.concatenate([od_w_out[0][:MIX_W], _perm_heads_rows(od_w_out[0][MIX_W:])], axis=0).astype(BF16)
    h = _out_call(h, c_out.reshape(L_TOT, BATCH * MIX_W), d_out, w_out, mod3, latent_only=True)
    h = _ffn_call(h, mod3, norm_g[1, 1].reshape(1, D_MODEL), ffn_up[1].astype(BF16), ffn_conv_w[1],
                  ffn_conv_b[1].reshape(1, 2 * FFN_HIDDEN), ffn_down[1].astype(BF16),
                  seg_tiles=((0, SEQ // TM),), mod_row=lambda b, j: b)
    return h
```

```python
import functools
import math

import jax
import jax.numpy as jnp
from jax import lax
from jax.experimental import pallas as pl
from jax.experimental.pallas import tpu as pltpu

F32 = jnp.float32
BF16 = jnp.bfloat16

D_MODEL = 1024
BATCH = 8
SEQ = 2048
CTX_LEN = 256
L_TOT = SEQ + CTX_LEN
DEPTH = 2
GRID_W = 64
HEAD_DIM = 64
ROPE_BASE = 10000.0
NORM_EPS = 1e-6
WINDOW = 128
N_MOD = 6
MIX_W = 512
N_HEADS = 8
N_KV = 2
KV_W = N_KV * HEAD_DIM
V_W = 2 * KV_W
QKV_W = MIX_W + KV_W + KV_W
Q_SCALE = HEAD_DIM ** -0.5 * math.log2(math.e)
RG_CONV = 4
RG_C = 8.0
S5_GROUP = 16
S5_GROUPS = 32
S5_STATE = 64
S5_NBLK = 4
S5_BLK_STATE = (S5_GROUPS // S5_NBLK) * S5_STATE
FFN_HIDDEN = 2816
FFN_CONV = 3

LANE = 128
BF16_ROWS = 16
VMEM_LIMIT = 56 * 1024 * 1024

IN_T = 64
RG_T = 128
S5_T = 64
POST_T = 64
TQ_G = 128
FFN_CH = 256
HEAD_PERM = (0, 4, 1, 5, 2, 6, 3, 7)


def _cparams(sem):
    return pltpu.CompilerParams(dimension_semantics=sem, vmem_limit_bytes=VMEM_LIMIT)


def _dot(a, b):
    return jnp.dot(a, b, preferred_element_type=F32)


def _gelu_tanh(x):
    return x * (0.5 * (1.0 + jnp.tanh(math.sqrt(2.0 / math.pi) * (x + 0.044715 * (x * x * x)))))


def _rms(x, g):
    ms = jnp.mean(x * x, axis=-1, keepdims=True)
    return x * lax.rsqrt(ms + NORM_EPS) * g


def _to_time_major(x):
    b, t, c = x.shape
    return jnp.swapaxes(x, 0, 1).reshape(t * b, c)


def _to_batch_major(x):
    r, c = x.shape
    return jnp.swapaxes(x.reshape(r // BATCH, BATCH, c), 0, 1)


def _per_sample(x, v, op):
    r, c = x.shape
    return op(x.reshape(r // BATCH, BATCH, c), v[None]).reshape(r, c)


def _full(shape, single=False):
    kw = dict(pipeline_mode=pl.Buffered(1)) if single else {}
    return pl.BlockSpec(shape, lambda *_: (0,) * len(shape), **kw)


MOD_TN = 1536


def _mod_kernel(c_ref, w_ref, b_ref, o_ref):
    c = c_ref[...]
    a = c * jax.nn.sigmoid(c)
    w = w_ref[0]
    ah = a.astype(BF16)
    al = (a - ah.astype(F32)).astype(BF16)
    wh = w.astype(BF16)
    wl = (w - wh.astype(F32)).astype(BF16)
    o_ref[0] = _dot(ah, wh) + _dot(ah, wl) + _dot(al, wh) + b_ref[0]


def _mod_call(cvec, mod_w, mod_b):
    n = N_MOD * D_MODEL
    return pl.pallas_call(
        _mod_kernel,
        out_shape=jax.ShapeDtypeStruct((DEPTH, 2 * BATCH, n), F32),
        grid=(DEPTH, n // MOD_TN),
        in_specs=[
            pl.BlockSpec((2 * BATCH, D_MODEL), lambda l, k: (0, 0)),
            pl.BlockSpec((1, D_MODEL, MOD_TN), lambda l, k: (l, 0, k)),
            pl.BlockSpec((1, 1, MOD_TN), lambda l, k: (l, 0, k)),
        ],
        out_specs=pl.BlockSpec((1, 2 * BATCH, MOD_TN), lambda l, k: (l, 0, k)),
        compiler_params=_cparams(("parallel", "parallel")),
        name="mod",
    )(cvec, mod_w, mod_b.reshape(DEPTH, 1, n))


def _mod_spec(col, n_latent_tiles):
    return pl.BlockSpec((1, BATCH, D_MODEL), lambda j: (jnp.where(j < n_latent_tiles, 0, 1), 0, col))


IN_ROWS = IN_T * BATCH
IN_NLAT = SEQ // IN_T
IN_N = L_TOT // IN_T


def _in_kernel(*refs, n_tb, first_layer):
    if first_layer:
        x_ref, c_ref, sh_ref, sc_ref, g_ref, w_ref, rope_ref, gq_ref, gk_ref, gm_ref = refs[:10]
        outs = refs[10:-1]
        hbuf = refs[-1]
        j = pl.program_id(0)

        @pl.when(j < IN_NLAT)
        def _():
            hbuf[...] = _to_time_major(x_ref[...])

        @pl.when(j >= IN_NLAT)
        def _():
            hbuf[...] = _to_time_major(c_ref[...])

        h = hbuf[...]
        outs[0][...] = h
        outs = outs[1:]
    else:
        h_ref, sh_ref, sc_ref, g_ref, w_ref, rope_ref, gq_ref, gk_ref, gm_ref = refs[:9]
        outs = refs[9:]
        h = h_ref[...]
    xn = _per_sample(_rms(h, g_ref[...]), 1.0 + sc_ref[0], jnp.multiply)
    xn = _per_sample(xn, sh_ref[0], jnp.add)
    proj = _dot(xn.astype(BF16), w_ref[...])
    for i in range(n_tb):
        outs[i][...] = proj[:, i * MIX_W:(i + 1) * MIX_W]
    q_ref, k_ref, v_ref = outs[n_tb:]
    off = n_tb * MIX_W
    qkv = _to_batch_major(proj[:, off:off + QKV_W])
    cos = rope_ref[:, 0:LANE][None]
    sin_up = rope_ref[:, LANE:2 * LANE][None]
    sin_dn = rope_ref[:, 2 * LANE:3 * LANE][None]
    gm = gm_ref[...]
    for p in range(5):
        blk = qkv[:, :, p * LANE:(p + 1) * LANE].reshape(IN_ROWS, LANE)
        sq = blk * blk
        hi = sq.astype(BF16)
        lo = (sq - hi.astype(F32)).astype(BF16)
        ms = _dot(hi, gm) + _dot(lo, gm)
        g = gq_ref[...] if p < 4 else gk_ref[...]
        bn = blk * lax.rsqrt(ms + NORM_EPS) * g
        up = pltpu.roll(bn, LANE - HEAD_DIM // 4, 1).reshape(BATCH, IN_T, LANE)
        dn = pltpu.roll(bn, HEAD_DIM // 4, 1).reshape(BATCH, IN_T, LANE)
        ro = bn.reshape(BATCH, IN_T, LANE) * cos + up * sin_up + dn * sin_dn
        if p < 4:
            q_ref[:, :, p * LANE:(p + 1) * LANE] = (ro * Q_SCALE).astype(BF16)
        else:
            k_ref[...] = ro.astype(BF16)
    v = qkv[:, :, 5 * LANE:6 * LANE].astype(BF16)
    v_ref[...] = jnp.concatenate([v, jnp.ones((BATCH, IN_T, LANE), BF16)], axis=2)


def _in_call(h_args, modt, g, w, rope, gq, gk, gm, n_tb, first_layer):
    n = w.shape[1]
    tb = lambda width: pl.BlockSpec((IN_ROWS, width), lambda j: (j, 0))
    bm = lambda width: pl.BlockSpec((BATCH, IN_T, width), lambda j: (0, j, 0))
    if first_layer:
        h_specs = [
            pl.BlockSpec((BATCH, IN_T, D_MODEL), lambda j: (0, jnp.minimum(j, IN_NLAT - 1), 0)),
            pl.BlockSpec((BATCH, IN_T, D_MODEL), lambda j: (0, jnp.maximum(j - IN_NLAT, 0), 0)),
        ]
        extra_shape = [jax.ShapeDtypeStruct((L_TOT * BATCH, D_MODEL), F32)]
        extra_spec = [tb(D_MODEL)]
        scratch = [pltpu.VMEM((IN_ROWS, D_MODEL), F32)]
    else:
        h_specs = [tb(D_MODEL)]
        extra_shape, extra_spec, scratch = [], [], []
    return pl.pallas_call(
        functools.partial(_in_kernel, n_tb=n_tb, first_layer=first_layer),
        out_shape=extra_shape + [jax.ShapeDtypeStruct((L_TOT * BATCH, MIX_W), F32)] * n_tb + [
            jax.ShapeDtypeStruct((BATCH, L_TOT, MIX_W), BF16),
            jax.ShapeDtypeStruct((BATCH, L_TOT, KV_W), BF16),
            jax.ShapeDtypeStruct((BATCH, L_TOT, V_W), BF16),
        ],
        grid=(IN_N,),
        in_specs=h_specs + [
            _mod_spec(0, IN_NLAT), _mod_spec(1, IN_NLAT),
            _full((1, D_MODEL)),
            _full((D_MODEL, n), single=True),
            pl.BlockSpec((IN_T, 3 * LANE), lambda j: (j, 0)),
            _full((1, LANE)), _full((1, LANE)), _full((LANE, LANE)),
        ],
        out_specs=extra_spec + [tb(MIX_W)] * n_tb + [bm(MIX_W), bm(KV_W), bm(V_W)],
        scratch_shapes=scratch,
        compiler_params=_cparams(("parallel",)),
        name="in_proj",
    )(*h_args, modt, modt, g, w, rope, gq, gk, gm)


def _chunk_of(i, reverse, n_latent, n_chunks):
    if reverse:
        return n_chunks - 1 - i
    return jnp.where(i < n_chunks - n_latent, i + n_latent, i - (n_chunks - n_latent))


RG_ROWS = RG_T * BATCH
RG_NC = L_TOT // RG_T
RG_NLAT = SEQ // RG_T


def _softplus(z):
    return jnp.maximum(z, 0.0) + jnp.log1p(jnp.exp(-jnp.abs(z)))


def _rg_kernel(*refs, reverse):
    if reverse:
        (u_ref, up_ref, un_ref, cw_ref, cb_ref, wa_ref, ba_ref, wx_ref, bx_ref, lam_ref,
         out_ref, xbuf, abuf, bbuf, hst) = refs
    else:
        (u_ref, up_ref, un_ref, gate_ref, hb_ref, cw_ref, cb_ref, wa_ref, ba_ref, wx_ref, bx_ref, lam_ref,
         out_ref, xbuf, abuf, bbuf, hst) = refs
    i = pl.program_id(0)
    c = _chunk_of(i, reverse, RG_NLAT, RG_NC)
    prev_zero = jnp.logical_or(c == 0, c == RG_NLAT)
    next_zero = jnp.logical_or(c == RG_NLAT - 1, c == RG_NC - 1)

    @pl.when(i == 0)
    def _():
        hst[...] = jnp.zeros_like(hst)

    xbuf[0:BATCH, :] = up_ref[...] * jnp.where(prev_zero, 0.0, 1.0)
    xbuf[BATCH:BATCH + RG_ROWS, :] = u_ref[...]
    xbuf[BATCH + RG_ROWS:3 * BATCH + RG_ROWS, :] = un_ref[...] * jnp.where(next_zero, 0.0, 1.0)
    uc = cb_ref[...]
    for k in range(RG_CONV):
        uc = uc + xbuf[k * BATCH:k * BATCH + RG_ROWS, :] * cw_ref[k:k + 1, :]

    ub = uc.astype(BF16)
    half = MIX_W // 2

    def gate(w_ref, b_ref):
        z = jnp.concatenate([_dot(ub[:, :half], w_ref[0]), _dot(ub[:, half:], w_ref[1])], axis=1)
        return jax.nn.sigmoid(z + b_ref[...])

    r = gate(wa_ref, ba_ref)
    ig = gate(wx_ref, bx_ref)
    log_a = (-RG_C) * r * _softplus(-lam_ref[...])
    a = jnp.exp(log_a)
    abuf[...] = a
    bbuf[...] = jnp.sqrt(-jnp.tanh(log_a) * (a * a + 1.0)) * (ig * uc)

    def step(s, h):
        t = (RG_T - 1 - s) if reverse else s
        r0 = pl.multiple_of(t * BATCH, BATCH)
        h = abuf[pl.ds(r0, BATCH), :] * h + bbuf[pl.ds(r0, BATCH), :]
        bbuf[pl.ds(r0, BATCH), :] = h
        return h

    hst[...] = lax.fori_loop(0, RG_T, step, hst[...], unroll=8)

    if reverse:
        out_ref[...] = bbuf[...]
    else:
        y = bbuf[...] + hb_ref[...]
        out_ref[...] = (y * _gelu_tanh(gate_ref[...])).astype(BF16)


def _rg_call(u, gate, hb, cw, cb, wa, ba, wx, bx, lam, reverse):
    def cidx(i):
        return _chunk_of(i, reverse, RG_NLAT, RG_NC)

    main = pl.BlockSpec((RG_ROWS, MIX_W), lambda i: (cidx(i), 0))
    prev = pl.BlockSpec((BATCH, MIX_W), lambda i: (jnp.maximum(cidx(i) * RG_T - 1, 0), 0))
    n_next = L_TOT // 2
    nxt = pl.BlockSpec((2 * BATCH, MIX_W), lambda i: (jnp.minimum((cidx(i) + 1) * (RG_T // 2), n_next - 1), 0))
    params = [cw, cb, wa, ba, wx, bx, lam]
    pspecs = [_full(p.shape) for p in params]
    if reverse:
        args = [u, u, u] + params
        specs = [main, prev, nxt] + pspecs
        out_dtype = F32
    else:
        args = [u, u, u, gate, hb] + params
        specs = [main, prev, nxt, main, main] + pspecs
        out_dtype = BF16
    return pl.pallas_call(
        functools.partial(_rg_kernel, reverse=reverse),
        out_shape=jax.ShapeDtypeStruct((L_TOT * BATCH, MIX_W), out_dtype),
        grid=(RG_NC,),
        in_specs=specs,
        out_specs=main,
        scratch_shapes=[
            pltpu.VMEM((RG_ROWS + 3 * BATCH, MIX_W), F32),
            pltpu.VMEM((RG_ROWS, MIX_W), F32),
            pltpu.VMEM((RG_ROWS, MIX_W), F32),
            pltpu.VMEM((BATCH, MIX_W), F32),
        ],
        compiler_params=_cparams(("arbitrary",)),
        name="rglru_bwd" if reverse else "rglru_fwd",
    )(*args)


def _head_rows(q_ref, tq):
    lane = lax.broadcasted_iota(jnp.int32, (tq, LANE), 1)
    low = lane < HEAD_DIM
    zero = jnp.zeros((tq, LANE), BF16)
    parts = []
    for p in range(MIX_W // LANE):
        qb = q_ref[0, :, p * LANE:(p + 1) * LANE]
        parts.append(jnp.where(low, qb, zero))
        parts.append(jnp.where(low, zero, qb))
    return parts, low


def _merge_heads(outs, low):
    return jnp.concatenate([jnp.where(low, outs[2 * p], outs[2 * p + 1]) for p in range(MIX_W // LANE)], axis=1)


def _qk(q, k):
    return lax.dot_general(q, k, (((1,), (1,)), ((), ())), preferred_element_type=F32)


def _gattn_kernel(q_ref, k_ref, v_ref, o_ref):
    j = pl.program_id(1)
    heads, low = _head_rows(q_ref, TQ_G)

    def attend(k0, nk):
        outs = []
        for q in heads:
            s = _qk(q, k_ref[0, k0:k0 + nk, :])
            m = jnp.max(s, axis=-1, keepdims=True)
            p = jnp.exp2(s - m).astype(BF16)
            acc = _dot(p, v_ref[0, k0:k0 + nk, :])
            outs.append(acc[:, :KV_W] / acc[:, KV_W:])
        o_ref[0] = _merge_heads(outs, low).astype(BF16)

    @pl.when(j < SEQ // TQ_G)
    def _():
        attend(0, L_TOT)

    @pl.when(j >= SEQ // TQ_G)
    def _():
        attend(SEQ, CTX_LEN)


def _gattn_call(q, k, v):
    return pl.pallas_call(
        _gattn_kernel,
        out_shape=jax.ShapeDtypeStruct((BATCH, L_TOT, MIX_W), BF16),
        grid=(BATCH, L_TOT // TQ_G),
        in_specs=[
            pl.BlockSpec((1, TQ_G, MIX_W), lambda b, j: (b, j, 0)),
            pl.BlockSpec((1, L_TOT, KV_W), lambda b, j: (b, 0, 0)),
            pl.BlockSpec((1, L_TOT, V_W), lambda b, j: (b, 0, 0)),
        ],
        out_specs=pl.BlockSpec((1, TQ_G, MIX_W), lambda b, j: (b, j, 0)),
        compiler_params=_cparams(("parallel", "arbitrary")),
        name="global_attn",
    )(q, k, v)


TQ_W = WINDOW
N_WBLK = SEQ // TQ_W


def _wattn_kernel(q_ref, k_ref, v_ref, sink_ref, o_ref):
    i = pl.program_id(1)
    heads, low = _head_rows(q_ref, TQ_W)
    prev_start = pl.multiple_of(jnp.maximum(i - 1, 0) * TQ_W, TQ_W)
    cur_start = pl.multiple_of(i * TQ_W, TQ_W)
    next_start = pl.multiple_of(jnp.minimum(i + 1, N_WBLK - 1) * TQ_W, TQ_W)

    def rows(ref):
        return jnp.concatenate([ref[0, SEQ:L_TOT, :], ref[0, pl.ds(prev_start, TQ_W), :],
                                ref[0, pl.ds(cur_start, TQ_W), :], ref[0, pl.ds(next_start, TQ_W), :]], axis=0)

    nk = CTX_LEN + 3 * TQ_W
    r = lax.broadcasted_iota(jnp.int32, (TQ_W, nk), 0)
    col = lax.broadcasted_iota(jnp.int32, (TQ_W, nk), 1)
    cp = col - CTX_LEN
    cn = col - (CTX_LEN + 2 * TQ_W)
    ninf = -jnp.inf
    pen_prev = jnp.where(i > 0, 0.0, ninf)
    pen_next = jnp.where(i < N_WBLK - 1, 0.0, ninf)
    bias = jnp.where(col < CTX_LEN, 0.0,
                     jnp.where(cp < TQ_W, jnp.where(cp >= r, pen_prev, ninf),
                               jnp.where(cn < 0, 0.0, jnp.where(cn <= r, pen_next, ninf))))
    kk = rows(k_ref)
    vv = rows(v_ref)
    outs = []
    for hd, q in enumerate(heads):
        s = _qk(q, kk) + bias
        sink = sink_ref[hd] * math.log2(math.e)
        m = jnp.maximum(jnp.max(s, axis=-1, keepdims=True), sink)
        p = jnp.exp2(s - m).astype(BF16)
        acc = _dot(p, vv)
        outs.append(acc[:, :KV_W] / (acc[:, KV_W:] + jnp.exp2(sink - m)))
    o_ref[0] = _merge_heads(outs, low).astype(BF16)


def _wattn_call(q, k, v, sink):
    return pl.pallas_call(
        _wattn_kernel,
        out_shape=jax.ShapeDtypeStruct((BATCH, SEQ, MIX_W), BF16),
        grid=(BATCH, N_WBLK),
        in_specs=[
            pl.BlockSpec((1, TQ_W, MIX_W), lambda b, i: (b, i, 0)),
            pl.BlockSpec((1, L_TOT, KV_W), lambda b, i: (b, 0, 0)),
            pl.BlockSpec((1, L_TOT, V_W), lambda b, i: (b, 0, 0)),
            pl.BlockSpec((N_HEADS, 1, 1), lambda b, i: (0, 0, 0)),
        ],
        out_specs=pl.BlockSpec((1, TQ_W, MIX_W), lambda b, i: (b, i, 0)),
        compiler_params=_cparams(("parallel", "arbitrary")),
        name="window_attn",
    )(q, k, v, sink)


def _s5_disc_kernel(lr_ref, li_ref, ls_ref, br_ref, bi_ref, ar_ref, ai_ref, bbr_ref, bbi_ref):
    lr = lr_ref[0]
    li = li_ref[0]
    dt = jnp.exp(ls_ref[0])
    mag = jnp.exp(lr * dt)
    ang = li * dt
    abr = mag * jnp.cos(ang)
    abi = mag * jnp.sin(ang)
    den = lr * lr + li * li
    nr = abr - 1.0
    kr = (nr * lr + abi * li) / den
    ki = (abi * lr - nr * li) / den
    br = br_ref[0]
    bi = bi_ref[0]
    ar_ref[0] = abr
    ai_ref[0] = abi
    bbr_ref[0] = kr * br - ki * bi
    bbi_ref[0] = kr * bi + ki * br


def _s5_disc_call(lam_re, lam_im, log_step, bt_re, bt_im):
    n = S5_GROUPS * S5_STATE
    row = pl.BlockSpec((1, 1, n), lambda d: (d, 0, 0))
    mat = pl.BlockSpec((1, S5_GROUP, n), lambda d: (d, 0, 0))
    return pl.pallas_call(
        _s5_disc_kernel,
        out_shape=[jax.ShapeDtypeStruct((2, 1, n), F32)] * 2 + [jax.ShapeDtypeStruct((2, S5_GROUP, n), F32)] * 2,
        grid=(2,),
        in_specs=[row, row, row, mat, mat],
        out_specs=[row, row, mat, mat],
        compiler_params=_cparams(("parallel",)),
        name="s5_discretize",
    )(lam_re, lam_im, log_step, bt_re, bt_im)


S5_ROWS = S5_T * BATCH
S5_NC = L_TOT // S5_T
S5_NLAT = SEQ // S5_T
S5_SW = 2 * S5_BLK_STATE * S5_NBLK


def _s5_kernel(*refs, reverse):
    if reverse:
        u_ref, wd_ref, ar_ref, ai_ref, wr_ref, out_ref, pbuf, xst = refs
    else:
        u_ref, yb_ref, wd_ref, ar_ref, ai_ref, wr_ref, dsk_ref, gw_ref, gb_ref, out_ref, pbuf, xst = refs
    i = pl.program_id(0)

    @pl.when(i == 0)
    def _():
        xst[...] = jnp.zeros_like(xst)

    u = u_ref[...]
    ub = u.astype(BF16)
    nb = 2 * S5_BLK_STATE
    for k in range(S5_NBLK):
        pbuf[:, k * nb:(k + 1) * nb] = _dot(ub[:, k * LANE:(k + 1) * LANE], wd_ref[k])

    def step(s, x):
        t = (S5_T - 1 - s) if reverse else s
        r0 = pl.multiple_of(t * BATCH, BATCH)
        halves = []
        for k in range(S5_NBLK):
            halves.append(x[:, k * nb + S5_BLK_STATE:(k + 1) * nb])
            halves.append(x[:, k * nb:k * nb + S5_BLK_STATE])
        xsw = jnp.concatenate(halves, axis=1)
        x = ar_ref[...] * x + ai_ref[...] * xsw + pbuf[pl.ds(r0, BATCH), :]
        pbuf[pl.ds(r0, BATCH), :] = x
        return x

    xst[...] = lax.fori_loop(0, S5_T, step, xst[...], unroll=4)

    y = jnp.concatenate(
        [_dot(pbuf[:, k * nb:(k + 1) * nb].astype(BF16), wr_ref[k]) for k in range(S5_NBLK)], axis=1)
    if reverse:
        out_ref[...] = y
    else:
        y = u * dsk_ref[...] + y + yb_ref[...]
        z = _gelu_tanh(y)
        out_ref[...] = (z * jax.nn.sigmoid(_dot(z.astype(BF16), gw_ref[...]) + gb_ref[...])).astype(BF16)


def _s5_call(u, yb, wd, ar, ai, wr, dsk, gw, gb, reverse):
    main = pl.BlockSpec((S5_ROWS, MIX_W), lambda i: (_chunk_of(i, reverse, S5_NLAT, S5_NC), 0))
    if reverse:
        args = [u, wd, ar, ai, wr]
        specs = [main] + [_full(a.shape) for a in args[1:]]
        out_dtype = F32
    else:
        args = [u, yb, wd, ar, ai, wr, dsk, gw, gb]
        specs = [main, main] + [_full(a.shape) for a in args[2:]]
        out_dtype = BF16
    return pl.pallas_call(
        functools.partial(_s5_kernel, reverse=reverse),
        out_shape=jax.ShapeDtypeStruct((L_TOT * BATCH, MIX_W), out_dtype),
        grid=(S5_NC,),
        in_specs=specs,
        out_specs=main,
        scratch_shapes=[
            pltpu.VMEM((S5_ROWS, S5_SW), F32),
            pltpu.VMEM((BATCH, S5_SW), F32),
        ],
        compiler_params=_cparams(("arbitrary",)),
        name="s5_bwd" if reverse else "s5_fwd",
    )(*args)


POST_ROWS = POST_T * BATCH
POST_HALO_T = BF16_ROWS // BATCH
POST_HALO = POST_HALO_T * BATCH
POST_NLAT = SEQ // POST_T
FFN_NCH = FFN_HIDDEN // FFN_CH


def _post_kernel(h_ref, hp_ref, hn_ref, a_ref, ap_ref, an_ref, b_ref, bp_ref, bn_ref, wo_ref,
                 g2_ref, sh_ref, sc_ref, g5_ref, g_ref, wu_ref, cw_ref, cb_ref, wd_ref, o_ref, *, n_tiles, out_batch_major):
    j = pl.program_id(0)
    first = jnp.logical_or(j == 0, j == POST_NLAT)
    last = jnp.logical_or(j == POST_NLAT - 1, j == n_tiles - 1)
    gate2 = g2_ref[0]
    gain = g_ref[...]
    scale = 1.0 + sc_ref[0]
    shift = sh_ref[0]

    def mix(h, a, b_tm):
        m = _dot(a, wo_ref[0:MIX_W, :]) + _dot(b_tm.astype(BF16), wo_ref[MIX_W:2 * MIX_W, :])
        return h + _per_sample(m, gate2, jnp.multiply)

    def norm(x):
        return _per_sample(_per_sample(_rms(x, gain), scale, jnp.multiply), shift, jnp.add)

    def halo_b(ref, t0):
        x = jnp.swapaxes(ref[...].astype(F32), 0, 1)[t0:t0 + POST_HALO_T]
        return x.reshape(POST_HALO, MIX_W)

    h_mid = mix(h_ref[...], a_ref[...], _to_time_major(b_ref[...].astype(F32)))
    h_prev = mix(hp_ref[...], ap_ref[...], halo_b(bp_ref, BF16_ROWS - POST_HALO_T))
    h_next = mix(hn_ref[...], an_ref[...], halo_b(bn_ref, 0))
    xn = jnp.concatenate([
        (norm(h_prev) * jnp.where(first, 0.0, 1.0)).astype(BF16),
        norm(h_mid).astype(BF16),
        (norm(h_next) * jnp.where(last, 0.0, 1.0)).astype(BF16)], axis=0)

    acts = []
    for c in range(FFN_NCH):
        cv = slice(c * FFN_CH, (c + 1) * FFN_CH)
        cg = slice(FFN_HIDDEN + c * FFN_CH, FFN_HIDDEN + (c + 1) * FFN_CH)

        def conv(cols):
            hid = _dot(xn, wu_ref[:, cols])
            out = cb_ref[:, cols]
            for k in range(FFN_CONV):
                r0 = POST_HALO + (k - 1) * BATCH
                out = out + hid[r0:r0 + POST_ROWS] * cw_ref[k:k + 1, cols]
            return out

        val = conv(cv)
        gate = conv(cg)
        acts.append((val * (gate * jax.nn.sigmoid(gate))).astype(BF16))
    ffn = _dot(jnp.concatenate(acts, axis=1), wd_ref[...])
    out = h_mid + _per_sample(ffn, g5_ref[0], jnp.multiply)
    if out_batch_major:
        o_ref[...] = _to_batch_major(out)
    else:
        o_ref[...] = out


def _post_call(h, a_tb, b_out, w_out, modt, g, wu, cw, cb, wd, n_tiles, out_batch_major):
    per = POST_ROWS // POST_HALO
    n_halo = h.shape[0] // POST_HALO
    per_b = POST_T // BF16_ROWS
    n_halo_b = b_out.shape[1] // BF16_ROWS

    def main(width):
        return pl.BlockSpec((POST_ROWS, width), lambda j: (j, 0))

    def prev(width):
        return pl.BlockSpec((POST_HALO, width), lambda j: (jnp.maximum(j * per - 1, 0), 0))

    def nxt(width):
        return pl.BlockSpec((POST_HALO, width), lambda j: (jnp.minimum((j + 1) * per, n_halo - 1), 0))

    if out_batch_major:
        out_shape = jax.ShapeDtypeStruct((BATCH, n_tiles * POST_T, D_MODEL), F32)
        out_spec = pl.BlockSpec((BATCH, POST_T, D_MODEL), lambda j: (0, j, 0))
    else:
        out_shape = jax.ShapeDtypeStruct((n_tiles * POST_ROWS, D_MODEL), F32)
        out_spec = main(D_MODEL)
    return pl.pallas_call(
        functools.partial(_post_kernel, n_tiles=n_tiles, out_batch_major=out_batch_major),
        out_shape=out_shape,
        grid=(n_tiles,),
        in_specs=[
            main(D_MODEL), prev(D_MODEL), nxt(D_MODEL),
            main(MIX_W), prev(MIX_W), nxt(MIX_W),
            pl.BlockSpec((BATCH, POST_T, MIX_W), lambda j: (0, j, 0)),
            pl.BlockSpec((BATCH, BF16_ROWS, MIX_W), lambda j: (0, jnp.maximum(j * per_b - 1, 0), 0)),
            pl.BlockSpec((BATCH, BF16_ROWS, MIX_W), lambda j: (0, jnp.minimum((j + 1) * per_b, n_halo_b - 1), 0)),
            _full((2 * MIX_W, D_MODEL), single=True),
            _mod_spec(2, POST_NLAT), _mod_spec(3, POST_NLAT), _mod_spec(4, POST_NLAT), _mod_spec(5, POST_NLAT),
            _full((1, D_MODEL)),
            _full((D_MODEL, 2 * FFN_HIDDEN), single=True),
            _full((FFN_CONV, 2 * FFN_HIDDEN)),
            _full((1, 2 * FFN_HIDDEN)),
            _full((FFN_HIDDEN, D_MODEL), single=True),
        ],
        out_specs=out_spec,
        compiler_params=_cparams(("parallel",)),
        name="post_ffn",
    )(h, h, h, a_tb, a_tb, a_tb, b_out, b_out, b_out, w_out, modt, modt, modt, modt, g, wu, cw, cb, wd)


def _rope_table():
    rows = SEQ // GRID_W
    row = jnp.repeat(jnp.arange(rows, dtype=F32), GRID_W)
    col = jnp.tile(jnp.arange(GRID_W, dtype=F32), rows)
    quarter = HEAD_DIM // 4
    inv_freq = ROPE_BASE ** (-jnp.arange(quarter, dtype=F32) / quarter)
    ang = jnp.stack([row[:, None] * inv_freq, col[:, None] * inv_freq], axis=1)
    cos = jnp.cos(ang)
    sin = jnp.sin(ang)
    zero = jnp.zeros_like(sin)

    def lanes(first, second):
        t = jnp.stack([first, second], axis=2).reshape(SEQ, HEAD_DIM)
        return jnp.tile(t, (1, LANE // HEAD_DIM))

    tab = jnp.concatenate([lanes(cos, cos), lanes(-sin, zero), lanes(zero, sin)], axis=1)
    ctx = jnp.concatenate([jnp.ones((CTX_LEN, LANE), F32), jnp.zeros((CTX_LEN, 2 * LANE), F32)], axis=1)
    return jnp.concatenate([tab, ctx], axis=0)


def _perm_heads_cols(w):
    d = w.shape[0]
    return w.reshape(d, N_HEADS, HEAD_DIM)[:, jnp.array(HEAD_PERM)].reshape(d, N_HEADS * HEAD_DIM)


def _perm_heads_rows(w):
    n = w.shape[1]
    return w.reshape(N_HEADS, HEAD_DIM, n)[jnp.array(HEAD_PERM)].reshape(N_HEADS * HEAD_DIM, n)


def _block_diag(w, per):
    n, a, b = w.shape
    eye = jnp.eye(per, dtype=w.dtype)
    w4 = w.reshape(n // per, per, a, b)
    return jnp.einsum("ihab,hk->ihakb", w4, eye).reshape(n // per, per * a, per * b)


def _head_gain(g):
    return jnp.tile(g, LANE // HEAD_DIM).reshape(1, LANE)


def _w_in(w, q_off):
    return jnp.concatenate([w[:, :q_off], _perm_heads_cols(w[:, q_off:q_off + MIX_W]), w[:, q_off + MIX_W:]],
                           axis=1).astype(BF16)


def _w_out(w):
    return jnp.concatenate([w[:MIX_W], _perm_heads_rows(w[MIX_W:])], axis=0).astype(BF16)


def kernel(x, c, ctx, c_ctx, mod_w, mod_b, norm_g, ffn_up, ffn_conv_w, ffn_conv_b, ffn_down, ev_w_in, ev_w_out, rg_conv_w, rg_conv_b, rg_wa, rg_ba, rg_wx, rg_bx, rg_lam, ga_qn, ga_kn, od_w_in, od_w_out, s5_lam_re, s5_lam_im, s5_log_step, s5_b_re, s5_b_im, s5_c_re, s5_c_im, s5_d, s5_glu_w, s5_glu_b, wa_qn, wa_kn, wa_sink):
    cvec = jnp.concatenate([c, jnp.broadcast_to(c_ctx[None], (BATCH, D_MODEL))], axis=0)
    mod = _mod_call(cvec, mod_w, mod_b).reshape(DEPTH, 2, BATCH, N_MOD * D_MODEL)
    rope = _rope_table()
    gm = _block_diag(jnp.full((LANE // HEAD_DIM, HEAD_DIM, HEAD_DIM), 1.0 / HEAD_DIM, F32), LANE // HEAD_DIM)[0].astype(BF16)

    def ffn_args(layer):
        return (norm_g[layer, 1].reshape(1, D_MODEL), ffn_up[layer].astype(BF16), ffn_conv_w[layer],
                ffn_conv_b[layer].reshape(1, 2 * FFN_HIDDEN), ffn_down[layer].astype(BF16))

    h, u_tb, gate_tb, q, k, v = _in_call(
        (x, ctx), mod[0], norm_g[0, 0].reshape(1, D_MODEL), _w_in(ev_w_in[0], 2 * MIX_W), rope,
        _head_gain(ga_qn[0]), _head_gain(ga_kn[0]), gm, n_tb=2, first_layer=True)
    rg = []
    for d in range(2):
        rg.append([rg_conv_w[0], rg_conv_b[0].reshape(1, MIX_W),
                   _block_diag(rg_wa[0, d], 4).astype(BF16), rg_ba[0, d].reshape(1, MIX_W),
                   _block_diag(rg_wx[0, d], 4).astype(BF16), rg_bx[0, d].reshape(1, MIX_W),
                   rg_lam[0, d].reshape(1, MIX_W)])
    hb = _rg_call(u_tb, None, None, *rg[1], reverse=True)
    a_out = _rg_call(u_tb, gate_tb, hb, *rg[0], reverse=False)
    b_out = _gattn_call(q, k, v)
    h = _post_call(h, a_out, b_out, _w_out(ev_w_out[0]), mod[0], *ffn_args(0),
                   n_tiles=L_TOT // POST_T, out_batch_major=False)

    u_tb, q, k, v = _in_call(
        (h,), mod[1], norm_g[1, 0].reshape(1, D_MODEL), _w_in(od_w_in[0], MIX_W), rope,
        _head_gain(wa_qn[0]), _head_gain(wa_kn[0]), gm, n_tb=1, first_layer=False)
    n_state = S5_GROUPS * S5_STATE
    bt_re = s5_b_re[0].transpose(0, 3, 1, 2).reshape(2, S5_GROUP, n_state)
    bt_im = s5_b_im[0].transpose(0, 3, 1, 2).reshape(2, S5_GROUP, n_state)
    log_step = jnp.repeat(s5_log_step[0], S5_STATE, axis=-1).reshape(2, 1, n_state)
    abr, abi, bbr, bbi = _s5_disc_call(s5_lam_re[0].reshape(2, 1, n_state), s5_lam_im[0].reshape(2, 1, n_state),
                                       log_step, bt_re, bt_im)
    gpb = S5_GROUPS // S5_NBLK
    eye = jnp.eye(gpb, dtype=F32)

    def drive_w(bb):
        return jnp.einsum("hbgp,gk->bghkp", bb.reshape(S5_GROUP, S5_NBLK, gpb, S5_STATE), eye).reshape(
            S5_NBLK, gpb * S5_GROUP, S5_BLK_STATE)

    def read_w(cc):
        return jnp.einsum("bghp,gk->bgpkh", cc.reshape(S5_NBLK, gpb, S5_GROUP, S5_STATE), eye).reshape(
            S5_NBLK, S5_BLK_STATE, gpb * S5_GROUP)

    def blocks(re, im):
        return jnp.concatenate([re.reshape(1, S5_NBLK, S5_BLK_STATE), im.reshape(1, S5_NBLK, S5_BLK_STATE)],
                               axis=2).reshape(1, S5_SW)

    s5 = []
    for d in range(2):
        wd = jnp.concatenate([drive_w(bbr[d]), drive_w(bbi[d])], axis=2).astype(BF16)
        wr = jnp.concatenate([read_w(s5_c_re[0, d]), -read_w(s5_c_im[0, d])], axis=1).astype(BF16)
        s5.append([wd, blocks(abr[d], abr[d]), blocks(-abi[d], abi[d]), wr])
    yb = _s5_call(u_tb, None, *s5[1], None, None, None, reverse=True)
    c_out = _s5_call(u_tb, yb, *s5[0], s5_d[0].reshape(1, MIX_W), s5_glu_w[0].astype(BF16),
                     s5_glu_b[0].reshape(1, MIX_W), reverse=False)
    sink = wa_sink[0][jnp.array(HEAD_PERM)].reshape(N_HEADS, 1, 1)
    d_out = _wattn_call(q, k, v, sink)
    return _post_call(h, c_out, d_out, _w_out(od_w_out[0]), mod[1], *ffn_args(1),
                      n_tiles=SEQ // POST_T, out_batch_major=True)
```

```python
import functools
import math

import jax
import jax.numpy as jnp
from jax import lax
from jax.experimental import pallas as pl
from jax.experimental.pallas import tpu as pltpu

F32 = jnp.float32
BF16 = jnp.bfloat16

D_MODEL = 1024
BATCH = 8
SEQ = 2048
CTX_LEN = 256
L_TOT = SEQ + CTX_LEN
DEPTH = 2
GRID_W = 64
HEAD_DIM = 64
ROPE_BASE = 10000.0
NORM_EPS = 1e-6
WINDOW = 128
N_MOD = 6
MIX_W = 512
N_HEADS = 8
N_KV = 2
KV_W = N_KV * HEAD_DIM
V_W = 2 * KV_W
QKV_W = MIX_W + KV_W + KV_W
Q_SCALE = HEAD_DIM ** -0.5 * math.log2(math.e)
RG_CONV = 4
RG_C = 8.0
S5_GROUP = 16
S5_GROUPS = 32
S5_STATE = 64
S5_NBLK = 4
S5_BLK_STATE = (S5_GROUPS // S5_NBLK) * S5_STATE
FFN_HIDDEN = 2816
FFN_CONV = 3

LANE = 128
BF16_ROWS = 16
VMEM_LIMIT = 56 * 1024 * 1024

IN_T = 64
RG_T = 128
S5_T = 64
POST_T = 64
TQ_G = 128
FFN_CH = 256
HEAD_PERM = (0, 4, 1, 5, 2, 6, 3, 7)


def _cparams(sem):
    return pltpu.CompilerParams(dimension_semantics=sem, vmem_limit_bytes=VMEM_LIMIT)


def _dot(a, b):
    return jnp.dot(a, b, preferred_element_type=F32)


def _gelu_tanh(x):
    return x * (0.5 * (1.0 + jnp.tanh(math.sqrt(2.0 / math.pi) * (x + 0.044715 * (x * x * x)))))


def _rms(x, g):
    ms = jnp.mean(x * x, axis=-1, keepdims=True)
    return x * lax.rsqrt(ms + NORM_EPS) * g


def _to_time_major(x):
    b, t, c = x.shape
    return jnp.swapaxes(x, 0, 1).reshape(t * b, c)


def _to_batch_major(x):
    r, c = x.shape
    return jnp.swapaxes(x.reshape(r // BATCH, BATCH, c), 0, 1)


def _per_sample(x, v, op):
    r, c = x.shape
    return op(x.reshape(r // BATCH, BATCH, c), v[None]).reshape(r, c)


def _full(shape, single=False):
    kw = dict(pipeline_mode=pl.Buffered(1)) if single else {}
    return pl.BlockSpec(shape, lambda *_: (0,) * len(shape), **kw)


MOD_TN = 1536


def _mod_kernel(c_ref, w_ref, b_ref, o_ref):
    c = c_ref[...]
    a = c * jax.nn.sigmoid(c)
    w = w_ref[0]
    ah = a.astype(BF16)
    al = (a - ah.astype(F32)).astype(BF16)
    wh = w.astype(BF16)
    wl = (w - wh.astype(F32)).astype(BF16)
    o_ref[0] = _dot(ah, wh) + _dot(ah, wl) + _dot(al, wh) + b_ref[0]


def _mod_call(cvec, mod_w, mod_b):
    n = N_MOD * D_MODEL
    return pl.pallas_call(
        _mod_kernel,
        out_shape=jax.ShapeDtypeStruct((DEPTH, 2 * BATCH, n), F32),
        grid=(DEPTH, n // MOD_TN),
        in_specs=[
            pl.BlockSpec((2 * BATCH, D_MODEL), lambda l, k: (0, 0)),
            pl.BlockSpec((1, D_MODEL, MOD_TN), lambda l, k: (l, 0, k)),
            pl.BlockSpec((1, 1, MOD_TN), lambda l, k: (l, 0, k)),
        ],
        out_specs=pl.BlockSpec((1, 2 * BATCH, MOD_TN), lambda l, k: (l, 0, k)),
        compiler_params=_cparams(("parallel", "parallel")),
        name="mod",
    )(cvec, mod_w, mod_b.reshape(DEPTH, 1, n))


def _mod_spec(col, n_latent_tiles):
    return pl.BlockSpec((1, BATCH, D_MODEL), lambda j: (jnp.where(j < n_latent_tiles, 0, 1), 0, col))


IN_ROWS = IN_T * BATCH
IN_NLAT = SEQ // IN_T
IN_N = L_TOT // IN_T
IN_SUB = 2
IN_SUB_T = IN_T // IN_SUB
IN_SUB_ROWS = IN_SUB_T * BATCH


def _in_kernel(*refs, n_tb, first_layer):
    if first_layer:
        x_ref, c_ref, sh_ref, sc_ref, g_ref, w_ref, rope_ref, gq_ref, gk_ref, gm_ref = refs[:10]
        h_out = refs[10]
        outs = refs[11:]
        is_latent = pl.program_id(0) < IN_NLAT
    else:
        h_ref, sh_ref, sc_ref, g_ref, w_ref, rope_ref, gq_ref, gk_ref, gm_ref = refs[:9]
        outs = refs[9:]
    q_ref, k_ref, v_ref = outs[n_tb:]
    off = n_tb * MIX_W
    gm = gm_ref[...]
    for s in range(IN_SUB):
        ts = slice(s * IN_SUB_T, (s + 1) * IN_SUB_T)
        rs = slice(s * IN_SUB_ROWS, (s + 1) * IN_SUB_ROWS)
        if first_layer:
            h = _to_time_major(jnp.where(is_latent, x_ref[:, ts, :], c_ref[:, ts, :]))
            h_out[rs, :] = h
        else:
            h = h_ref[rs, :]
        xn = _per_sample(_rms(h, g_ref[...]), 1.0 + sc_ref[0], jnp.multiply)
        xn = _per_sample(xn, sh_ref[0], jnp.add)
        proj = _dot(xn.astype(BF16), w_ref[...])
        for i in range(n_tb):
            outs[i][rs, :] = proj[:, i * MIX_W:(i + 1) * MIX_W]
        qkv = _to_batch_major(proj[:, off:off + QKV_W])
        cos = rope_ref[ts, 0:LANE][None]
        sin_up = rope_ref[ts, LANE:2 * LANE][None]
        sin_dn = rope_ref[ts, 2 * LANE:3 * LANE][None]
        for p in range(5):
            blk = qkv[:, :, p * LANE:(p + 1) * LANE].reshape(IN_SUB_ROWS, LANE)
            sq = blk * blk
            hi = sq.astype(BF16)
            lo = (sq - hi.astype(F32)).astype(BF16)
            ms = _dot(hi, gm) + _dot(lo, gm)
            g = gq_ref[...] if p < 4 else gk_ref[...]
            bn = blk * lax.rsqrt(ms + NORM_EPS) * g
            up = pltpu.roll(bn, LANE - HEAD_DIM // 4, 1).reshape(BATCH, IN_SUB_T, LANE)
            dn = pltpu.roll(bn, HEAD_DIM // 4, 1).reshape(BATCH, IN_SUB_T, LANE)
            ro = bn.reshape(BATCH, IN_SUB_T, LANE) * cos + up * sin_up + dn * sin_dn
            if p < 4:
                q_ref[:, ts, p * LANE:(p + 1) * LANE] = (ro * Q_SCALE).astype(BF16)
            else:
                k_ref[:, ts, :] = ro.astype(BF16)
        v = qkv[:, :, 5 * LANE:6 * LANE].astype(BF16)
        v_ref[:, ts, :] = jnp.concatenate([v, jnp.ones((BATCH, IN_SUB_T, LANE), BF16)], axis=2)


def _in_call(h_args, modt, g, w, rope, gq, gk, gm, n_tb, first_layer):
    n = w.shape[1]
    tb = lambda width: pl.BlockSpec((IN_ROWS, width), lambda j: (j, 0))
    bm = lambda width: pl.BlockSpec((BATCH, IN_T, width), lambda j: (0, j, 0))
    if first_layer:
        h_specs = [
            pl.BlockSpec((BATCH, IN_T, D_MODEL), lambda j: (0, jnp.minimum(j, IN_NLAT - 1), 0)),
            pl.BlockSpec((BATCH, IN_T, D_MODEL), lambda j: (0, jnp.maximum(j - IN_NLAT, 0), 0)),
        ]
        extra_shape = [jax.ShapeDtypeStruct((L_TOT * BATCH, D_MODEL), F32)]
        extra_spec = [tb(D_MODEL)]
    else:
        h_specs = [tb(D_MODEL)]
        extra_shape, extra_spec = [], []
    return pl.pallas_call(
        functools.partial(_in_kernel, n_tb=n_tb, first_layer=first_layer),
        out_shape=extra_shape + [jax.ShapeDtypeStruct((L_TOT * BATCH, MIX_W), F32)] * n_tb + [
            jax.ShapeDtypeStruct((BATCH, L_TOT, MIX_W), BF16),
            jax.ShapeDtypeStruct((BATCH, L_TOT, KV_W), BF16),
            jax.ShapeDtypeStruct((BATCH, L_TOT, V_W), BF16),
        ],
        grid=(IN_N,),
        in_specs=h_specs + [
            _mod_spec(0, IN_NLAT), _mod_spec(1, IN_NLAT),
            _full((1, D_MODEL)),
            _full((D_MODEL, n), single=True),
            pl.BlockSpec((IN_T, 3 * LANE), lambda j: (j, 0)),
            _full((1, LANE)), _full((1, LANE)), _full((LANE, LANE)),
        ],
        out_specs=extra_spec + [tb(MIX_W)] * n_tb + [bm(MIX_W), bm(KV_W), bm(V_W)],
        compiler_params=_cparams(("parallel",)),
        name="in_proj",
    )(*h_args, modt, modt, g, w, rope, gq, gk, gm)


def _chunk_of(i, reverse, n_latent, n_chunks):
    if reverse:
        return n_chunks - 1 - i
    return jnp.where(i < n_chunks - n_latent, i + n_latent, i - (n_chunks - n_latent))


RG_ROWS = RG_T * BATCH
RG_NC = L_TOT // RG_T
RG_NLAT = SEQ // RG_T


def _softplus(z):
    return jnp.maximum(z, 0.0) + jnp.log1p(jnp.exp(-jnp.abs(z)))


def _rg_kernel(*refs, reverse):
    if reverse:
        (u_ref, up_ref, un_ref, cw_ref, cb_ref, wa_ref, ba_ref, wx_ref, bx_ref, lam_ref,
         out_ref, xbuf, abuf, bbuf, hst) = refs
    else:
        (u_ref, up_ref, un_ref, gate_ref, hb_ref, cw_ref, cb_ref, wa_ref, ba_ref, wx_ref, bx_ref, lam_ref,
         out_ref, xbuf, abuf, bbuf, hst) = refs
    i = pl.program_id(0)
    c = _chunk_of(i, reverse, RG_NLAT, RG_NC)
    prev_zero = jnp.logical_or(c == 0, c == RG_NLAT)
    next_zero = jnp.logical_or(c == RG_NLAT - 1, c == RG_NC - 1)

    @pl.when(i == 0)
    def _():
        hst[...] = jnp.zeros_like(hst)

    xbuf[0:BATCH, :] = up_ref[...] * jnp.where(prev_zero, 0.0, 1.0)
    xbuf[BATCH:BATCH + RG_ROWS, :] = u_ref[...]
    xbuf[BATCH + RG_ROWS:3 * BATCH + RG_ROWS, :] = un_ref[...] * jnp.where(next_zero, 0.0, 1.0)
    uc = cb_ref[...]
    for k in range(RG_CONV):
        uc = uc + xbuf[k * BATCH:k * BATCH + RG_ROWS, :] * cw_ref[k:k + 1, :]

    ub = uc.astype(BF16)
    half = MIX_W // 2

    def gate(w_ref, b_ref):
        z = jnp.concatenate([_dot(ub[:, :half], w_ref[0]), _dot(ub[:, half:], w_ref[1])], axis=1)
        return jax.nn.sigmoid(z + b_ref[...])

    r = gate(wa_ref, ba_ref)
    ig = gate(wx_ref, bx_ref)
    log_a = (-RG_C) * r * _softplus(-lam_ref[...])
    a = jnp.exp(log_a)
    abuf[...] = a
    bbuf[...] = jnp.sqrt(-jnp.tanh(log_a) * (a * a + 1.0)) * (ig * uc)

    def step(s, h):
        t = (RG_T - 1 - s) if reverse else s
        r0 = pl.multiple_of(t * BATCH, BATCH)
        h = abuf[pl.ds(r0, BATCH), :] * h + bbuf[pl.ds(r0, BATCH), :]
        bbuf[pl.ds(r0, BATCH), :] = h
        return h

    hst[...] = lax.fori_loop(0, RG_T, step, hst[...], unroll=8)

    if reverse:
        out_ref[...] = bbuf[...]
    else:
        y = bbuf[...] + hb_ref[...]
        out_ref[...] = (y * _gelu_tanh(gate_ref[...])).astype(BF16)


def _rg_call(u, gate, hb, cw, cb, wa, ba, wx, bx, lam, reverse):
    def cidx(i):
        return _chunk_of(i, reverse, RG_NLAT, RG_NC)

    main = pl.BlockSpec((RG_ROWS, MIX_W), lambda i: (cidx(i), 0))
    prev = pl.BlockSpec((BATCH, MIX_W), lambda i: (jnp.maximum(cidx(i) * RG_T - 1, 0), 0))
    n_next = L_TOT // 2
    nxt = pl.BlockSpec((2 * BATCH, MIX_W), lambda i: (jnp.minimum((cidx(i) + 1) * (RG_T // 2), n_next - 1), 0))
    params = [cw, cb, wa, ba, wx, bx, lam]
    pspecs = [_full(p.shape) for p in params]
    if reverse:
        args = [u, u, u] + params
        specs = [main, prev, nxt] + pspecs
        out_dtype = F32
    else:
        args = [u, u, u, gate, hb] + params
        specs = [main, prev, nxt, main, main] + pspecs
        out_dtype = BF16
    return pl.pallas_call(
        functools.partial(_rg_kernel, reverse=reverse),
        out_shape=jax.ShapeDtypeStruct((L_TOT * BATCH, MIX_W), out_dtype),
        grid=(RG_NC,),
        in_specs=specs,
        out_specs=main,
        scratch_shapes=[
            pltpu.VMEM((RG_ROWS + 3 * BATCH, MIX_W), F32),
            pltpu.VMEM((RG_ROWS, MIX_W), F32),
            pltpu.VMEM((RG_ROWS, MIX_W), F32),
            pltpu.VMEM((BATCH, MIX_W), F32),
        ],
        compiler_params=_cparams(("arbitrary",)),
        name="rglru_bwd" if reverse else "rglru_fwd",
    )(*args)


def _head_rows(q_ref, tq):
    lane = lax.broadcasted_iota(jnp.int32, (tq, LANE), 1)
    low = lane < HEAD_DIM
    zero = jnp.zeros((tq, LANE), BF16)
    parts = []
    for p in range(MIX_W // LANE):
        qb = q_ref[0, :, p * LANE:(p + 1) * LANE]
        parts.append(jnp.where(low, qb, zero))
        parts.append(jnp.where(low, zero, qb))
    return parts, low


def _merge_heads(outs, low):
    return jnp.concatenate([jnp.where(low, outs[2 * p], outs[2 * p + 1]) for p in range(MIX_W // LANE)], axis=1)


def _qk(q, k):
    return lax.dot_general(q, k, (((1,), (1,)), ((), ())), preferred_element_type=F32)


def _gattn_kernel(q_ref, k_ref, v_ref, o_ref):
    j = pl.program_id(1)
    heads, low = _head_rows(q_ref, TQ_G)

    def attend(k0, nk):
        outs = []
        for q in heads:
            s = _qk(q, k_ref[0, k0:k0 + nk, :])
            m = jnp.max(s, axis=-1, keepdims=True)
            p = jnp.exp2(s - m).astype(BF16)
            acc = _dot(p, v_ref[0, k0:k0 + nk, :])
            outs.append(acc[:, :KV_W] / acc[:, KV_W:])
        o_ref[0] = _merge_heads(outs, low).astype(BF16)

    @pl.when(j < SEQ // TQ_G)
    def _():
        attend(0, L_TOT)

    @pl.when(j >= SEQ // TQ_G)
    def _():
        attend(SEQ, CTX_LEN)


def _gattn_call(q, k, v):
    return pl.pallas_call(
        _gattn_kernel,
        out_shape=jax.ShapeDtypeStruct((BATCH, L_TOT, MIX_W), BF16),
        grid=(BATCH, L_TOT // TQ_G),
        in_specs=[
            pl.BlockSpec((1, TQ_G, MIX_W), lambda b, j: (b, j, 0)),
            pl.BlockSpec((1, L_TOT, KV_W), lambda b, j: (b, 0, 0)),
            pl.BlockSpec((1, L_TOT, V_W), lambda b, j: (b, 0, 0)),
        ],
        out_specs=pl.BlockSpec((1, TQ_G, MIX_W), lambda b, j: (b, j, 0)),
        compiler_params=_cparams(("parallel", "arbitrary")),
        name="global_attn",
    )(q, k, v)


TQ_W = WINDOW
N_WBLK = SEQ // TQ_W


def _wattn_kernel(q_ref, k_ref, v_ref, sink_ref, o_ref):
    i = pl.program_id(1)
    heads, low = _head_rows(q_ref, TQ_W)
    prev_start = pl.multiple_of(jnp.maximum(i - 1, 0) * TQ_W, TQ_W)
    cur_start = pl.multiple_of(i * TQ_W, TQ_W)
    next_start = pl.multiple_of(jnp.minimum(i + 1, N_WBLK - 1) * TQ_W, TQ_W)

    def rows(ref):
        return jnp.concatenate([ref[0, SEQ:L_TOT, :], ref[0, pl.ds(prev_start, TQ_W), :],
                                ref[0, pl.ds(cur_start, TQ_W), :], ref[0, pl.ds(next_start, TQ_W), :]], axis=0)

    nk = CTX_LEN + 3 * TQ_W
    r = lax.broadcasted_iota(jnp.int32, (TQ_W, nk), 0)
    col = lax.broadcasted_iota(jnp.int32, (TQ_W, nk), 1)
    cp = col - CTX_LEN
    cn = col - (CTX_LEN + 2 * TQ_W)
    ninf = -jnp.inf
    pen_prev = jnp.where(i > 0, 0.0, ninf)
    pen_next = jnp.where(i < N_WBLK - 1, 0.0, ninf)
    bias = jnp.where(col < CTX_LEN, 0.0,
                     jnp.where(cp < TQ_W, jnp.where(cp >= r, pen_prev, ninf),
                               jnp.where(cn < 0, 0.0, jnp.where(cn <= r, pen_next, ninf))))
    kk = rows(k_ref)
    vv = rows(v_ref)
    outs = []
    for hd, q in enumerate(heads):
        s = _qk(q, kk) + bias
        sink = sink_ref[hd] * math.log2(math.e)
        m = jnp.maximum(jnp.max(s, axis=-1, keepdims=True), sink)
        p = jnp.exp2(s - m).astype(BF16)
        acc = _dot(p, vv)
        outs.append(acc[:, :KV_W] / (acc[:, KV_W:] + jnp.exp2(sink - m)))
    o_ref[0] = _merge_heads(outs, low).astype(BF16)


def _wattn_call(q, k, v, sink):
    return pl.pallas_call(
        _wattn_kernel,
        out_shape=jax.ShapeDtypeStruct((BATCH, SEQ, MIX_W), BF16),
        grid=(BATCH, N_WBLK),
        in_specs=[
            pl.BlockSpec((1, TQ_W, MIX_W), lambda b, i: (b, i, 0)),
            pl.BlockSpec((1, L_TOT, KV_W), lambda b, i: (b, 0, 0)),
            pl.BlockSpec((1, L_TOT, V_W), lambda b, i: (b, 0, 0)),
            pl.BlockSpec((N_HEADS, 1, 1), lambda b, i: (0, 0, 0)),
        ],
        out_specs=pl.BlockSpec((1, TQ_W, MIX_W), lambda b, i: (b, i, 0)),
        compiler_params=_cparams(("parallel", "arbitrary")),
        name="window_attn",
    )(q, k, v, sink)


def _s5_disc_kernel(lr_ref, li_ref, ls_ref, br_ref, bi_ref, ar_ref, ai_ref, bbr_ref, bbi_ref):
    lr = lr_ref[0]
    li = li_ref[0]
    dt = jnp.exp(ls_ref[0])
    mag = jnp.exp(lr * dt)
    ang = li * dt
    abr = mag * jnp.cos(ang)
    abi = mag * jnp.sin(ang)
    den = lr * lr + li * li
    nr = abr - 1.0
    kr = (nr * lr + abi * li) / den
    ki = (abi * lr - nr * li) / den
    br = br_ref[0]
    bi = bi_ref[0]
    ar_ref[0] = abr
    ai_ref[0] = abi
    bbr_ref[0] = kr * br - ki * bi
    bbi_ref[0] = kr * bi + ki * br


def _s5_disc_call(lam_re, lam_im, log_step, bt_re, bt_im):
    n = S5_GROUPS * S5_STATE
    row = pl.BlockSpec((1, 1, n), lambda d: (d, 0, 0))
    mat = pl.BlockSpec((1, S5_GROUP, n), lambda d: (d, 0, 0))
    return pl.pallas_call(
        _s5_disc_kernel,
        out_shape=[jax.ShapeDtypeStruct((2, 1, n), F32)] * 2 + [jax.ShapeDtypeStruct((2, S5_GROUP, n), F32)] * 2,
        grid=(2,),
        in_specs=[row, row, row, mat, mat],
        out_specs=[row, row, mat, mat],
        compiler_params=_cparams(("parallel",)),
        name="s5_discretize",
    )(lam_re, lam_im, log_step, bt_re, bt_im)


S5_ROWS = S5_T * BATCH
S5_NC = L_TOT // S5_T
S5_NLAT = SEQ // S5_T
S5_SW = 2 * S5_BLK_STATE * S5_NBLK


def _s5_kernel(*refs, reverse):
    if reverse:
        u2_ref, un_ref, wd_ref, ar_ref, ai_ref, wr_ref, out_ref, buf0, buf1, xst = refs
    else:
        (u2_ref, un_ref, yb2_ref, wd_ref, ar_ref, ai_ref, wr_ref, dsk_ref, gw_ref, gb_ref,
         out_ref, buf0, buf1, xst) = refs
    nb = 2 * S5_BLK_STATE
    lo, hi = slice(0, S5_ROWS), slice(S5_ROWS, 2 * S5_ROWS)
    first, second = (hi, lo) if reverse else (lo, hi)

    def drive(u, buf):
        ub = u.astype(BF16)
        for k in range(S5_NBLK):
            buf[:, k * nb:(k + 1) * nb] = _dot(ub[:, k * LANE:(k + 1) * LANE], wd_ref[k])

    def scan(buf, x):
        for s in range(S5_T):
            t = (S5_T - 1 - s) if reverse else s
            rows = slice(t * BATCH, (t + 1) * BATCH)
            halves = []
            for k in range(S5_NBLK):
                halves.append(x[:, k * nb + S5_BLK_STATE:(k + 1) * nb])
                halves.append(x[:, k * nb:k * nb + S5_BLK_STATE])
            xsw = jnp.concatenate(halves, axis=1)
            x = ar_ref[...] * x + ai_ref[...] * xsw + buf[rows, :]
            buf[rows, :] = x
        return x

    def readout(buf, rows):
        y = jnp.concatenate(
            [_dot(buf[:, k * nb:(k + 1) * nb].astype(BF16), wr_ref[k]) for k in range(S5_NBLK)], axis=1)
        if reverse:
            out_ref[rows, :] = y
        else:
            y = u2_ref[rows, :] * dsk_ref[...] + y + yb2_ref[rows, :]
            z = _gelu_tanh(y)
            out_ref[rows, :] = (z * jax.nn.sigmoid(_dot(z.astype(BF16), gw_ref[...]) + gb_ref[...])).astype(BF16)

    @pl.when(pl.program_id(0) == 0)
    def _():
        xst[...] = jnp.zeros_like(xst)
        drive(u2_ref[first, :], buf0)

    x = xst[...]
    drive(u2_ref[second, :], buf1)
    x = scan(buf0, x)
    readout(buf0, first)
    drive(un_ref[...], buf0)
    x = scan(buf1, x)
    readout(buf1, second)
    xst[...] = x


def _s5_call(u, yb, wd, ar, ai, wr, dsk, gw, gb, reverse):
    def chunk(p):
        return _chunk_of(p, reverse, S5_NLAT, S5_NC)

    pair = pl.BlockSpec((2 * S5_ROWS, MIX_W), lambda k: (chunk(2 * k) // 2, 0))
    nxt = pl.BlockSpec((S5_ROWS, MIX_W), lambda k: (chunk(jnp.minimum(2 * k + 2, S5_NC - 1)), 0))
    if reverse:
        args = [u, u, wd, ar, ai, wr]
        specs = [pair, nxt] + [_full(a.shape) for a in args[2:]]
        out_dtype = F32
    else:
        args = [u, u, yb, wd, ar, ai, wr, dsk, gw, gb]
        specs = [pair, nxt, pair] + [_full(a.shape) for a in args[3:]]
        out_dtype = BF16
    return pl.pallas_call(
        functools.partial(_s5_kernel, reverse=reverse),
        out_shape=jax.ShapeDtypeStruct((L_TOT * BATCH, MIX_W), out_dtype),
        grid=(S5_NC // 2,),
        in_specs=specs,
        out_specs=pair,
        scratch_shapes=[
            pltpu.VMEM((S5_ROWS, S5_SW), F32),
            pltpu.VMEM((S5_ROWS, S5_SW), F32),
            pltpu.VMEM((BATCH, S5_SW), F32),
        ],
        compiler_params=_cparams(("arbitrary",)),
        name="s5_bwd" if reverse else "s5_fwd",
    )(*args)


POST_ROWS = POST_T * BATCH
POST_HALO_T = BF16_ROWS // BATCH
POST_HALO = POST_HALO_T * BATCH
POST_NLAT = SEQ // POST_T
FFN_NCH = FFN_HIDDEN // FFN_CH


def _post_kernel(h_ref, hp_ref, hn_ref, a_ref, ap_ref, an_ref, b_ref, bp_ref, bn_ref, wo_ref,
                 g2_ref, sh_ref, sc_ref, g5_ref, g_ref, wu_ref, cw_ref, cb_ref, wd_ref, o_ref, *, n_tiles, out_batch_major):
    j = pl.program_id(0)
    first = jnp.logical_or(j == 0, j == POST_NLAT)
    last = jnp.logical_or(j == POST_NLAT - 1, j == n_tiles - 1)
    gate2 = g2_ref[0]
    gain = g_ref[...]
    scale = 1.0 + sc_ref[0]
    shift = sh_ref[0]

    def mix(h, a, b_tm):
        m = _dot(a, wo_ref[0:MIX_W, :]) + _dot(b_tm.astype(BF16), wo_ref[MIX_W:2 * MIX_W, :])
        return h + _per_sample(m, gate2, jnp.multiply)

    def norm(x):
        return _per_sample(_per_sample(_rms(x, gain), scale, jnp.multiply), shift, jnp.add)

    def halo_b(ref, t0):
        x = jnp.swapaxes(ref[...].astype(F32), 0, 1)[t0:t0 + POST_HALO_T]
        return x.reshape(POST_HALO, MIX_W)

    h_mid = mix(h_ref[...], a_ref[...], _to_time_major(b_ref[...].astype(F32)))
    h_prev = mix(hp_ref[...], ap_ref[...], halo_b(bp_ref, BF16_ROWS - POST_HALO_T))
    h_next = mix(hn_ref[...], an_ref[...], halo_b(bn_ref, 0))
    xn = jnp.concatenate([
        (norm(h_prev) * jnp.where(first, 0.0, 1.0)).astype(BF16),
        norm(h_mid).astype(BF16),
        (norm(h_next) * jnp.where(last, 0.0, 1.0)).astype(BF16)], axis=0)

    acts = []
    for c in range(FFN_NCH):
        cv = slice(c * FFN_CH, (c + 1) * FFN_CH)
        cg = slice(FFN_HIDDEN + c * FFN_CH, FFN_HIDDEN + (c + 1) * FFN_CH)

        def conv(cols):
            hid = _dot(xn, wu_ref[:, cols])
            out = cb_ref[:, cols]
            for k in range(FFN_CONV):
                r0 = POST_HALO + (k - 1) * BATCH
                out = out + hid[r0:r0 + POST_ROWS] * cw_ref[k:k + 1, cols]
            return out

        val = conv(cv)
        gate = conv(cg)
        acts.append((val * (gate * jax.nn.sigmoid(gate))).astype(BF16))
    ffn = _dot(jnp.concatenate(acts, axis=1), wd_ref[...])
    out = h_mid + _per_sample(ffn, g5_ref[0], jnp.multiply)
    if out_batch_major:
        o_ref[...] = _to_batch_major(out)
    else:
        o_ref[...] = out


def _post_call(h, a_tb, b_out, w_out, modt, g, wu, cw, cb, wd, n_tiles, out_batch_major):
    per = POST_ROWS // POST_HALO
    n_halo = h.shape[0] // POST_HALO
    per_b = POST_T // BF16_ROWS
    n_halo_b = b_out.shape[1] // BF16_ROWS

    def main(width):
        return pl.BlockSpec((POST_ROWS, width), lambda j: (j, 0))

    def prev(width):
        return pl.BlockSpec((POST_HALO, width), lambda j: (jnp.maximum(j * per - 1, 0), 0))

    def nxt(width):
        return pl.BlockSpec((POST_HALO, width), lambda j: (jnp.minimum((j + 1) * per, n_halo - 1), 0))

    if out_batch_major:
        out_shape = jax.ShapeDtypeStruct((BATCH, n_tiles * POST_T, D_MODEL), F32)
        out_spec = pl.BlockSpec((BATCH, POST_T, D_MODEL), lambda j: (0, j, 0))
    else:
        out_shape = jax.ShapeDtypeStruct((n_tiles * POST_ROWS, D_MODEL), F32)
        out_spec = main(D_MODEL)
    return pl.pallas_call(
        functools.partial(_post_kernel, n_tiles=n_tiles, out_batch_major=out_batch_major),
        out_shape=out_shape,
        grid=(n_tiles,),
        in_specs=[
            main(D_MODEL), prev(D_MODEL), nxt(D_MODEL),
            main(MIX_W), prev(MIX_W), nxt(MIX_W),
            pl.BlockSpec((BATCH, POST_T, MIX_W), lambda j: (0, j, 0)),
            pl.BlockSpec((BATCH, BF16_ROWS, MIX_W), lambda j: (0, jnp.maximum(j * per_b - 1, 0), 0)),
            pl.BlockSpec((BATCH, BF16_ROWS, MIX_W), lambda j: (0, jnp.minimum((j + 1) * per_b, n_halo_b - 1), 0)),
            _full((2 * MIX_W, D_MODEL), single=True),
            _mod_spec(2, POST_NLAT), _mod_spec(3, POST_NLAT), _mod_spec(4, POST_NLAT), _mod_spec(5, POST_NLAT),
            _full((1, D_MODEL)),
            _full((D_MODEL, 2 * FFN_HIDDEN), single=True),
            _full((FFN_CONV, 2 * FFN_HIDDEN)),
            _full((1, 2 * FFN_HIDDEN)),
            _full((FFN_HIDDEN, D_MODEL), single=True),
        ],
        out_specs=out_spec,
        compiler_params=_cparams(("parallel",)),
        name="post_ffn",
    )(h, h, h, a_tb, a_tb, a_tb, b_out, b_out, b_out, w_out, modt, modt, modt, modt, g, wu, cw, cb, wd)


CAST_STEPS = 4


def _cast_kernel(w_ref, o_ref):
    o_ref[...] = w_ref[0].astype(BF16)


def _cast_call(w, layer):
    _, r, c = w.shape
    tr = r // CAST_STEPS
    return pl.pallas_call(
        _cast_kernel,
        out_shape=jax.ShapeDtypeStruct((r, c), BF16),
        grid=(CAST_STEPS,),
        in_specs=[pl.BlockSpec((1, tr, c), lambda i: (layer, i, 0))],
        out_specs=pl.BlockSpec((tr, c), lambda i: (i, 0)),
        compiler_params=_cparams(("parallel",)),
        name="cast_bf16",
    )(w)


def _rope_table():
    rows = SEQ // GRID_W
    row = jnp.repeat(jnp.arange(rows, dtype=F32), GRID_W)
    col = jnp.tile(jnp.arange(GRID_W, dtype=F32), rows)
    quarter = HEAD_DIM // 4
    inv_freq = ROPE_BASE ** (-jnp.arange(quarter, dtype=F32) / quarter)
    ang = jnp.stack([row[:, None] * inv_freq, col[:, None] * inv_freq], axis=1)
    cos = jnp.cos(ang)
    sin = jnp.sin(ang)
    zero = jnp.zeros_like(sin)

    def lanes(first, second):
        t = jnp.stack([first, second], axis=2).reshape(SEQ, HEAD_DIM)
        return jnp.tile(t, (1, LANE // HEAD_DIM))

    tab = jnp.concatenate([lanes(cos, cos), lanes(-sin, zero), lanes(zero, sin)], axis=1)
    ctx = jnp.concatenate([jnp.ones((CTX_LEN, LANE), F32), jnp.zeros((CTX_LEN, 2 * LANE), F32)], axis=1)
    return jnp.concatenate([tab, ctx], axis=0)


def _perm_heads_cols(w):
    d = w.shape[0]
    return w.reshape(d, N_HEADS, HEAD_DIM)[:, jnp.array(HEAD_PERM)].reshape(d, N_HEADS * HEAD_DIM)


def _perm_heads_rows(w):
    n = w.shape[1]
    return w.reshape(N_HEADS, HEAD_DIM, n)[jnp.array(HEAD_PERM)].reshape(N_HEADS * HEAD_DIM, n)


def _block_diag(w, per):
    n, a, b = w.shape
    eye = jnp.eye(per, dtype=w.dtype)
    w4 = w.reshape(n // per, per, a, b)
    return jnp.einsum("ihab,hk->ihakb", w4, eye).reshape(n // per, per * a, per * b)


def _head_gain(g):
    return jnp.tile(g, LANE // HEAD_DIM).reshape(1, LANE)


def _w_in(w, q_off):
    return jnp.concatenate([w[:, :q_off], _perm_heads_cols(w[:, q_off:q_off + MIX_W]), w[:, q_off + MIX_W:]],
                           axis=1).astype(BF16)


def _w_out(w):
    return jnp.concatenate([w[:MIX_W], _perm_heads_rows(w[MIX_W:])], axis=0).astype(BF16)


def kernel(x, c, ctx, c_ctx, mod_w, mod_b, norm_g, ffn_up, ffn_conv_w, ffn_conv_b, ffn_down, ev_w_in, ev_w_out, rg_conv_w, rg_conv_b, rg_wa, rg_ba, rg_wx, rg_bx, rg_lam, ga_qn, ga_kn, od_w_in, od_w_out, s5_lam_re, s5_lam_im, s5_log_step, s5_b_re, s5_b_im, s5_c_re, s5_c_im, s5_d, s5_glu_w, s5_glu_b, wa_qn, wa_kn, wa_sink):
    cvec = jnp.concatenate([c, jnp.broadcast_to(c_ctx[None], (BATCH, D_MODEL))], axis=0)
    mod = _mod_call(cvec, mod_w, mod_b).reshape(DEPTH, 2, BATCH, N_MOD * D_MODEL)
    rope = _rope_table()
    gm = _block_diag(jnp.full((LANE // HEAD_DIM, HEAD_DIM, HEAD_DIM), 1.0 / HEAD_DIM, F32), LANE // HEAD_DIM)[0].astype(BF16)

    def ffn_args(layer):
        return (norm_g[layer, 1].reshape(1, D_MODEL), _cast_call(ffn_up, layer), ffn_conv_w[layer],
                ffn_conv_b[layer].reshape(1, 2 * FFN_HIDDEN), _cast_call(ffn_down, layer))

    h, u_tb, gate_tb, q, k, v = _in_call(
        (x, ctx), mod[0], norm_g[0, 0].reshape(1, D_MODEL), _w_in(ev_w_in[0], 2 * MIX_W), rope,
        _head_gain(ga_qn[0]), _head_gain(ga_kn[0]), gm, n_tb=2, first_layer=True)
    rg = []
    for d in range(2):
        rg.append([rg_conv_w[0], rg_conv_b[0].reshape(1, MIX_W),
                   _block_diag(rg_wa[0, d], 4).astype(BF16), rg_ba[0, d].reshape(1, MIX_W),
                   _block_diag(rg_wx[0, d], 4).astype(BF16), rg_bx[0, d].reshape(1, MIX_W),
                   rg_lam[0, d].reshape(1, MIX_W)])
    hb = _rg_call(u_tb, None, None, *rg[1], reverse=True)
    a_out = _rg_call(u_tb, gate_tb, hb, *rg[0], reverse=False)
    b_out = _gattn_call(q, k, v)
    h = _post_call(h, a_out, b_out, _w_out(ev_w_out[0]), mod[0], *ffn_args(0),
                   n_tiles=L_TOT // POST_T, out_batch_major=False)

    u_tb, q, k, v = _in_call(
        (h,), mod[1], norm_g[1, 0].reshape(1, D_MODEL), _w_in(od_w_in[0], MIX_W), rope,
        _head_gain(wa_qn[0]), _head_gain(wa_kn[0]), gm, n_tb=1, first_layer=False)
    n_state = S5_GROUPS * S5_STATE
    bt_re = s5_b_re[0].transpose(0, 3, 1, 2).reshape(2, S5_GROUP, n_state)
    bt_im = s5_b_im[0].transpose(0, 3, 1, 2).reshape(2, S5_GROUP, n_state)
    log_step = jnp.repeat(s5_log_step[0], S5_STATE, axis=-1).reshape(2, 1, n_state)
    abr, abi, bbr, bbi = _s5_disc_call(s5_lam_re[0].reshape(2, 1, n_state), s5_lam_im[0].reshape(2, 1, n_state),
                                       log_step, bt_re, bt_im)
    gpb = S5_GROUPS // S5_NBLK
    eye = jnp.eye(gpb, dtype=F32)

    def drive_w(bb):
        return jnp.einsum("hbgp,gk->bghkp", bb.reshape(S5_GROUP, S5_NBLK, gpb, S5_STATE), eye).reshape(
            S5_NBLK, gpb * S5_GROUP, S5_BLK_STATE)

    def read_w(cc):
        return jnp.einsum("bghp,gk->bgpkh", cc.reshape(S5_NBLK, gpb, S5_GROUP, S5_STATE), eye).reshape(
            S5_NBLK, S5_BLK_STATE, gpb * S5_GROUP)

    def blocks(re, im):
        row = jnp.concatenate([re.reshape(1, S5_NBLK, S5_BLK_STATE), im.reshape(1, S5_NBLK, S5_BLK_STATE)],
                              axis=2).reshape(1, S5_SW)
        return jnp.broadcast_to(row, (BATCH, S5_SW))

    s5 = []
    for d in range(2):
        wd = jnp.concatenate([drive_w(bbr[d]), drive_w(bbi[d])], axis=2).astype(BF16)
        wr = jnp.concatenate([read_w(s5_c_re[0, d]), -read_w(s5_c_im[0, d])], axis=1).astype(BF16)
        s5.append([wd, blocks(abr[d], abr[d]), blocks(-abi[d], abi[d]), wr])
    yb = _s5_call(u_tb, None, *s5[1], None, None, None, reverse=True)
    c_out = _s5_call(u_tb, yb, *s5[0], s5_d[0].reshape(1, MIX_W), s5_glu_w[0].astype(BF16),
                     s5_glu_b[0].reshape(1, MIX_W), reverse=False)
    sink = wa_sink[0][jnp.array(HEAD_PERM)].reshape(N_HEADS, 1, 1)
    d_out = _wattn_call(q, k, v, sink)
    return _post_call(h, c_out, d_out, _w_out(od_w_out[0]), mod[1], *ffn_args(1),
                      n_tiles=SEQ // POST_T, out_batch_major=True)
```

```python
import functools
import math

import jax
import jax.numpy as jnp
from jax import lax
from jax.experimental import pallas as pl
from jax.experimental.pallas import tpu as pltpu

F32 = jnp.float32
BF16 = jnp.bfloat16

D_MODEL = 1024
BATCH = 8
SEQ = 2048
CTX_LEN = 256
L_TOT = SEQ + CTX_LEN
DEPTH = 2
GRID_W = 64
HEAD_DIM = 64
ROPE_BASE = 10000.0
NORM_EPS = 1e-6
WINDOW = 128
N_MOD = 6
MIX_W = 512
N_HEADS = 8
N_KV = 2
KV_W = N_KV * HEAD_DIM
V_W = 2 * KV_W
QKV_W = MIX_W + KV_W + KV_W
Q_SCALE = HEAD_DIM ** -0.5 * math.log2(math.e)
RG_CONV = 4
RG_C = 8.0
S5_GROUP = 16
S5_GROUPS = 32
S5_STATE = 64
S5_NBLK = 4
S5_BLK_STATE = (S5_GROUPS // S5_NBLK) * S5_STATE
FFN_HIDDEN = 2816
FFN_CONV = 3

LANE = 128
BF16_ROWS = 16
VMEM_LIMIT = 56 * 1024 * 1024

IN_T = 64
RG_T = 128
S5_T = 64
POST_T = 64
TQ_G = 256
FFN_CH = 256
HEAD_PERM = (0, 4, 1, 5, 2, 6, 3, 7)


def _cparams(sem):
    return pltpu.CompilerParams(dimension_semantics=sem, vmem_limit_bytes=VMEM_LIMIT)


def _dot(a, b):
    return jnp.dot(a, b, preferred_element_type=F32)


def _gelu_tanh(x):
    return x * (0.5 * (1.0 + jnp.tanh(math.sqrt(2.0 / math.pi) * (x + 0.044715 * (x * x * x)))))


def _rms(x, g):
    ms = jnp.mean(x * x, axis=-1, keepdims=True)
    return x * lax.rsqrt(ms + NORM_EPS) * g


def _to_time_major(x):
    b, t, c = x.shape
    return jnp.swapaxes(x, 0, 1).reshape(t * b, c)


def _to_batch_major(x):
    r, c = x.shape
    return jnp.swapaxes(x.reshape(r // BATCH, BATCH, c), 0, 1)


def _per_sample(x, v, op):
    r, c = x.shape
    return op(x.reshape(r // BATCH, BATCH, c), v[None]).reshape(r, c)


def _full(shape, single=False):
    kw = dict(pipeline_mode=pl.Buffered(1)) if single else {}
    return pl.BlockSpec(shape, lambda *_: (0,) * len(shape), **kw)


MOD_TN = 1536


def _mod_kernel(c_ref, w_ref, b_ref, o_ref):
    c = c_ref[...]
    a = c * jax.nn.sigmoid(c)
    w = w_ref[0]
    ah = a.astype(BF16)
    al = (a - ah.astype(F32)).astype(BF16)
    wh = w.astype(BF16)
    wl = (w - wh.astype(F32)).astype(BF16)
    o_ref[0] = _dot(ah, wh) + _dot(ah, wl) + _dot(al, wh) + b_ref[0]


def _mod_call(cvec, mod_w, mod_b):
    n = N_MOD * D_MODEL
    return pl.pallas_call(
        _mod_kernel,
        out_shape=jax.ShapeDtypeStruct((DEPTH, 2 * BATCH, n), F32),
        grid=(DEPTH, n // MOD_TN),
        in_specs=[
            pl.BlockSpec((2 * BATCH, D_MODEL), lambda l, k: (0, 0)),
            pl.BlockSpec((1, D_MODEL, MOD_TN), lambda l, k: (l, 0, k)),
            pl.BlockSpec((1, 1, MOD_TN), lambda l, k: (l, 0, k)),
        ],
        out_specs=pl.BlockSpec((1, 2 * BATCH, MOD_TN), lambda l, k: (l, 0, k)),
        compiler_params=_cparams(("parallel", "parallel")),
        name="mod",
    )(cvec, mod_w, mod_b.reshape(DEPTH, 1, n))


def _mod_spec(col, n_latent_tiles):
    return pl.BlockSpec((1, BATCH, D_MODEL), lambda j: (jnp.where(j < n_latent_tiles, 0, 1), 0, col))


IN_ROWS = IN_T * BATCH
IN_NLAT = SEQ // IN_T
IN_N = L_TOT // IN_T
IN_SUB = 2
IN_SUB_T = IN_T // IN_SUB
IN_SUB_ROWS = IN_SUB_T * BATCH


def _in_kernel(*refs, n_tb, first_layer):
    if first_layer:
        x_ref, c_ref, sh_ref, sc_ref, g_ref, w_ref, rope_ref, gq_ref, gk_ref, gm_ref = refs[:10]
        h_out = refs[10]
        outs = refs[11:]
        is_latent = pl.program_id(0) < IN_NLAT
    else:
        h_ref, sh_ref, sc_ref, g_ref, w_ref, rope_ref, gq_ref, gk_ref, gm_ref = refs[:9]
        outs = refs[9:]
    q_ref, k_ref, v_ref = outs[n_tb:]
    off = n_tb * MIX_W
    gm = gm_ref[...]
    for s in range(IN_SUB):
        ts = slice(s * IN_SUB_T, (s + 1) * IN_SUB_T)
        rs = slice(s * IN_SUB_ROWS, (s + 1) * IN_SUB_ROWS)
        if first_layer:
            h = _to_time_major(jnp.where(is_latent, x_ref[:, ts, :], c_ref[:, ts, :]))
            h_out[rs, :] = h
        else:
            h = h_ref[rs, :]
        xn = _per_sample(_rms(h, g_ref[...]), 1.0 + sc_ref[0], jnp.multiply)
        xn = _per_sample(xn, sh_ref[0], jnp.add)
        proj = _dot(xn.astype(BF16), w_ref[...])
        for i in range(n_tb):
            outs[i][rs, :] = proj[:, i * MIX_W:(i + 1) * MIX_W]
        qkv = _to_batch_major(proj[:, off:off + QKV_W])
        cos = rope_ref[ts, 0:LANE][None]
        sin_up = rope_ref[ts, LANE:2 * LANE][None]
        sin_dn = rope_ref[ts, 2 * LANE:3 * LANE][None]
        for p in range(5):
            blk = qkv[:, :, p * LANE:(p + 1) * LANE].reshape(IN_SUB_ROWS, LANE)
            sq = blk * blk
            hi = sq.astype(BF16)
            lo = (sq - hi.astype(F32)).astype(BF16)
            ms = _dot(hi, gm) + _dot(lo, gm)
            g = gq_ref[...] if p < 4 else gk_ref[...]
            bn = blk * lax.rsqrt(ms + NORM_EPS) * g
            up = pltpu.roll(bn, LANE - HEAD_DIM // 4, 1).reshape(BATCH, IN_SUB_T, LANE)
            dn = pltpu.roll(bn, HEAD_DIM // 4, 1).reshape(BATCH, IN_SUB_T, LANE)
            ro = bn.reshape(BATCH, IN_SUB_T, LANE) * cos + up * sin_up + dn * sin_dn
            if p < 4:
                q_ref[:, ts, p * LANE:(p + 1) * LANE] = (ro * Q_SCALE).astype(BF16)
            else:
                k_ref[:, ts, :] = ro.astype(BF16)
        v = qkv[:, :, 5 * LANE:6 * LANE].astype(BF16)
        v_ref[:, ts, :] = jnp.concatenate([v, jnp.ones((BATCH, IN_SUB_T, LANE), BF16)], axis=2)


def _in_call(h_args, modt, g, w, rope, gq, gk, gm, n_tb, first_layer):
    n = w.shape[1]
    tb = lambda width: pl.BlockSpec((IN_ROWS, width), lambda j: (j, 0))
    bm = lambda width: pl.BlockSpec((BATCH, IN_T, width), lambda j: (0, j, 0))
    if first_layer:
        h_specs = [
            pl.BlockSpec((BATCH, IN_T, D_MODEL), lambda j: (0, jnp.minimum(j, IN_NLAT - 1), 0)),
            pl.BlockSpec((BATCH, IN_T, D_MODEL), lambda j: (0, jnp.maximum(j - IN_NLAT, 0), 0)),
        ]
        extra_shape = [jax.ShapeDtypeStruct((L_TOT * BATCH, D_MODEL), F32)]
        extra_spec = [tb(D_MODEL)]
    else:
        h_specs = [tb(D_MODEL)]
        extra_shape, extra_spec = [], []
    return pl.pallas_call(
        functools.partial(_in_kernel, n_tb=n_tb, first_layer=first_layer),
        out_shape=extra_shape + [jax.ShapeDtypeStruct((L_TOT * BATCH, MIX_W), F32)] * n_tb + [
            jax.ShapeDtypeStruct((BATCH, L_TOT, MIX_W), BF16),
            jax.ShapeDtypeStruct((BATCH, L_TOT, KV_W), BF16),
            jax.ShapeDtypeStruct((BATCH, L_TOT, V_W), BF16),
        ],
        grid=(IN_N,),
        in_specs=h_specs + [
            _mod_spec(0, IN_NLAT), _mod_spec(1, IN_NLAT),
            _full((1, D_MODEL)),
            _full((D_MODEL, n), single=True),
            pl.BlockSpec((IN_T, 3 * LANE), lambda j: (j, 0)),
            _full((1, LANE)), _full((1, LANE)), _full((LANE, LANE)),
        ],
        out_specs=extra_spec + [tb(MIX_W)] * n_tb + [bm(MIX_W), bm(KV_W), bm(V_W)],
        compiler_params=_cparams(("parallel",)),
        name="in_proj",
    )(*h_args, modt, modt, g, w, rope, gq, gk, gm)


def _chunk_of(i, reverse, n_latent, n_chunks):
    if reverse:
        return n_chunks - 1 - i
    return jnp.where(i < n_chunks - n_latent, i + n_latent, i - (n_chunks - n_latent))


RG_ROWS = RG_T * BATCH
RG_NC = L_TOT // RG_T
RG_NLAT = SEQ // RG_T


def _softplus(z):
    return jnp.maximum(z, 0.0) + jnp.log1p(jnp.exp(-jnp.abs(z)))


def _rg_kernel(*refs, reverse):
    if reverse:
        (u_ref, up_ref, un_ref, cw_ref, cb_ref, wa_ref, ba_ref, wx_ref, bx_ref, lam_ref,
         out_ref, xbuf, abuf, bbuf, hst) = refs
    else:
        (u_ref, up_ref, un_ref, gate_ref, hb_ref, cw_ref, cb_ref, wa_ref, ba_ref, wx_ref, bx_ref, lam_ref,
         out_ref, xbuf, abuf, bbuf, hst) = refs
    i = pl.program_id(0)
    c = _chunk_of(i, reverse, RG_NLAT, RG_NC)
    prev_zero = jnp.logical_or(c == 0, c == RG_NLAT)
    next_zero = jnp.logical_or(c == RG_NLAT - 1, c == RG_NC - 1)

    @pl.when(i == 0)
    def _():
        hst[...] = jnp.zeros_like(hst)

    xbuf[0:BATCH, :] = up_ref[...] * jnp.where(prev_zero, 0.0, 1.0)
    xbuf[BATCH:BATCH + RG_ROWS, :] = u_ref[...]
    xbuf[BATCH + RG_ROWS:3 * BATCH + RG_ROWS, :] = un_ref[...] * jnp.where(next_zero, 0.0, 1.0)
    uc = cb_ref[...]
    for k in range(RG_CONV):
        uc = uc + xbuf[k * BATCH:k * BATCH + RG_ROWS, :] * cw_ref[k:k + 1, :]

    ub = uc.astype(BF16)
    half = MIX_W // 2

    def gate(w_ref, b_ref):
        z = jnp.concatenate([_dot(ub[:, :half], w_ref[0]), _dot(ub[:, half:], w_ref[1])], axis=1)
        return jax.nn.sigmoid(z + b_ref[...])

    r = gate(wa_ref, ba_ref)
    ig = gate(wx_ref, bx_ref)
    log_a = (-RG_C) * r * _softplus(-lam_ref[...])
    a = jnp.exp(log_a)
    abuf[...] = a
    bbuf[...] = jnp.sqrt(-jnp.tanh(log_a) * (a * a + 1.0)) * (ig * uc)

    def step(s, h):
        t = (RG_T - 1 - s) if reverse else s
        r0 = pl.multiple_of(t * BATCH, BATCH)
        h = abuf[pl.ds(r0, BATCH), :] * h + bbuf[pl.ds(r0, BATCH), :]
        bbuf[pl.ds(r0, BATCH), :] = h
        return h

    hst[...] = lax.fori_loop(0, RG_T, step, hst[...], unroll=8)

    if reverse:
        out_ref[...] = bbuf[...]
    else:
        y = bbuf[...] + hb_ref[...]
        out_ref[...] = (y * _gelu_tanh(gate_ref[...])).astype(BF16)


def _rg_call(u, gate, hb, cw, cb, wa, ba, wx, bx, lam, reverse):
    def cidx(i):
        return _chunk_of(i, reverse, RG_NLAT, RG_NC)

    main = pl.BlockSpec((RG_ROWS, MIX_W), lambda i: (cidx(i), 0))
    prev = pl.BlockSpec((BATCH, MIX_W), lambda i: (jnp.maximum(cidx(i) * RG_T - 1, 0), 0))
    n_next = L_TOT // 2
    nxt = pl.BlockSpec((2 * BATCH, MIX_W), lambda i: (jnp.minimum((cidx(i) + 1) * (RG_T // 2), n_next - 1), 0))
    params = [cw, cb, wa, ba, wx, bx, lam]
    pspecs = [_full(p.shape) for p in params]
    if reverse:
        args = [u, u, u] + params
        specs = [main, prev, nxt] + pspecs
        out_dtype = F32
    else:
        args = [u, u, u, gate, hb] + params
        specs = [main, prev, nxt, main, main] + pspecs
        out_dtype = BF16
    return pl.pallas_call(
        functools.partial(_rg_kernel, reverse=reverse),
        out_shape=jax.ShapeDtypeStruct((L_TOT * BATCH, MIX_W), out_dtype),
        grid=(RG_NC,),
        in_specs=specs,
        out_specs=main,
        scratch_shapes=[
            pltpu.VMEM((RG_ROWS + 3 * BATCH, MIX_W), F32),
            pltpu.VMEM((RG_ROWS, MIX_W), F32),
            pltpu.VMEM((RG_ROWS, MIX_W), F32),
            pltpu.VMEM((BATCH, MIX_W), F32),
        ],
        compiler_params=_cparams(("arbitrary",)),
        name="rglru_bwd" if reverse else "rglru_fwd",
    )(*args)


def _head_rows(q_ref, tq):
    lane = lax.broadcasted_iota(jnp.int32, (tq, LANE), 1)
    low = lane < HEAD_DIM
    zero = jnp.zeros((tq, LANE), BF16)
    parts = []
    for p in range(MIX_W // LANE):
        qb = q_ref[0, :, p * LANE:(p + 1) * LANE]
        parts.append(jnp.where(low, qb, zero))
        parts.append(jnp.where(low, zero, qb))
    return parts, low


def _merge_heads(outs, low):
    return jnp.concatenate([jnp.where(low, outs[2 * p], outs[2 * p + 1]) for p in range(MIX_W // LANE)], axis=1)


def _qk(q, k):
    return lax.dot_general(q, k, (((1,), (1,)), ((), ())), preferred_element_type=F32)


def _gattn_kernel(q_ref, k_ref, v_ref, o_ref):
    j = pl.program_id(1)
    heads, low = _head_rows(q_ref, TQ_G)

    def attend(k0, nk):
        outs = []
        for q in heads:
            s = _qk(q, k_ref[0, k0:k0 + nk, :])
            m = jnp.max(s, axis=-1, keepdims=True)
            p = jnp.exp2(s - m).astype(BF16)
            acc = _dot(p, v_ref[0, k0:k0 + nk, :])
            outs.append(acc[:, :KV_W] / acc[:, KV_W:])
        o_ref[0] = _merge_heads(outs, low).astype(BF16)

    @pl.when(j < SEQ // TQ_G)
    def _():
        attend(0, L_TOT)

    @pl.when(j >= SEQ // TQ_G)
    def _():
        attend(SEQ, CTX_LEN)


def _gattn_call(q, k, v):
    return pl.pallas_call(
        _gattn_kernel,
        out_shape=jax.ShapeDtypeStruct((BATCH, L_TOT, MIX_W), BF16),
        grid=(BATCH, L_TOT // TQ_G),
        in_specs=[
            pl.BlockSpec((1, TQ_G, MIX_W), lambda b, j: (b, j, 0)),
            pl.BlockSpec((1, L_TOT, KV_W), lambda b, j: (b, 0, 0)),
            pl.BlockSpec((1, L_TOT, V_W), lambda b, j: (b, 0, 0)),
        ],
        out_specs=pl.BlockSpec((1, TQ_G, MIX_W), lambda b, j: (b, j, 0)),
        compiler_params=_cparams(("parallel", "arbitrary")),
        name="global_attn",
    )(q, k, v)


TQ_W = 2 * WINDOW
N_WBLK = SEQ // WINDOW
N_WSTEP = SEQ // TQ_W


def _wattn_kernel(q_ref, k_ref, v_ref, sink_ref, o_ref):
    i = pl.program_id(1)
    heads, low = _head_rows(q_ref, TQ_W)
    prev_start = pl.multiple_of(jnp.maximum(2 * i - 1, 0) * WINDOW, WINDOW)
    mid_start = pl.multiple_of(i * TQ_W, TQ_W)
    next_start = pl.multiple_of(jnp.minimum(2 * i + 2, N_WBLK - 1) * WINDOW, WINDOW)

    def rows(ref):
        return jnp.concatenate([ref[0, SEQ:L_TOT, :], ref[0, pl.ds(prev_start, WINDOW), :],
                                ref[0, pl.ds(mid_start, TQ_W), :], ref[0, pl.ds(next_start, WINDOW), :]], axis=0)

    nk = CTX_LEN + 2 * WINDOW + TQ_W
    r = lax.broadcasted_iota(jnp.int32, (TQ_W, nk), 0)
    col = lax.broadcasted_iota(jnp.int32, (TQ_W, nk), 1)
    c = col - CTX_LEN
    ninf = -jnp.inf
    pen_prev = jnp.where(i > 0, 0.0, ninf)
    pen_next = jnp.where(i < N_WSTEP - 1, 0.0, ninf)
    edge = jnp.where(c < WINDOW, pen_prev, jnp.where(c >= WINDOW + TQ_W, pen_next, 0.0))
    band = jnp.where(c >= r, jnp.where(c <= r + 2 * WINDOW, edge, ninf), ninf)
    bias = jnp.where(col < CTX_LEN, 0.0, band)
    kk = rows(k_ref)
    vv = rows(v_ref)
    outs = []
    for hd, q in enumerate(heads):
        s = _qk(q, kk) + bias
        sink = sink_ref[hd] * math.log2(math.e)
        m = jnp.maximum(jnp.max(s, axis=-1, keepdims=True), sink)
        p = jnp.exp2(s - m).astype(BF16)
        acc = _dot(p, vv)
        outs.append(acc[:, :KV_W] / (acc[:, KV_W:] + jnp.exp2(sink - m)))
    o_ref[0] = _merge_heads(outs, low).astype(BF16)


def _wattn_call(q, k, v, sink):
    return pl.pallas_call(
        _wattn_kernel,
        out_shape=jax.ShapeDtypeStruct((BATCH, SEQ, MIX_W), BF16),
        grid=(BATCH, N_WSTEP),
        in_specs=[
            pl.BlockSpec((1, TQ_W, MIX_W), lambda b, i: (b, i, 0)),
            pl.BlockSpec((1, L_TOT, KV_W), lambda b, i: (b, 0, 0)),
            pl.BlockSpec((1, L_TOT, V_W), lambda b, i: (b, 0, 0)),
            pl.BlockSpec((N_HEADS, 1, 1), lambda b, i: (0, 0, 0)),
        ],
        out_specs=pl.BlockSpec((1, TQ_W, MIX_W), lambda b, i: (b, i, 0)),
        compiler_params=_cparams(("parallel", "arbitrary")),
        name="window_attn",
    )(q, k, v, sink)


def _s5_disc_kernel(lr_ref, li_ref, ls_ref, br_ref, bi_ref, ar_ref, ai_ref, bbr_ref, bbi_ref):
    lr = lr_ref[0]
    li = li_ref[0]
    dt = jnp.exp(ls_ref[0])
    mag = jnp.exp(lr * dt)
    ang = li * dt
    abr = mag * jnp.cos(ang)
    abi = mag * jnp.sin(ang)
    den = lr * lr + li * li
    nr = abr - 1.0
    kr = (nr * lr + abi * li) / den
    ki = (abi * lr - nr * li) / den
    br = br_ref[0]
    bi = bi_ref[0]
    ar_ref[0] = abr
    ai_ref[0] = abi
    bbr_ref[0] = kr * br - ki * bi
    bbi_ref[0] = kr * bi + ki * br


def _s5_disc_call(lam_re, lam_im, log_step, bt_re, bt_im):
    n = S5_GROUPS * S5_STATE
    row = pl.BlockSpec((1, 1, n), lambda d: (d, 0, 0))
    mat = pl.BlockSpec((1, S5_GROUP, n), lambda d: (d, 0, 0))
    return pl.pallas_call(
        _s5_disc_kernel,
        out_shape=[jax.ShapeDtypeStruct((2, 1, n), F32)] * 2 + [jax.ShapeDtypeStruct((2, S5_GROUP, n), F32)] * 2,
        grid=(2,),
        in_specs=[row, row, row, mat, mat],
        out_specs=[row, row, mat, mat],
        compiler_params=_cparams(("parallel",)),
        name="s5_discretize",
    )(lam_re, lam_im, log_step, bt_re, bt_im)


S5_ROWS = S5_T * BATCH
S5_NC = L_TOT // S5_T
S5_NLAT = SEQ // S5_T
S5_SW = 2 * S5_BLK_STATE * S5_NBLK


def _s5_kernel(*refs, reverse):
    if reverse:
        u2_ref, un_ref, wd_ref, ar_ref, ai_ref, wr_ref, out_ref, buf0, buf1, xst = refs
    else:
        (u2_ref, un_ref, yb2_ref, wd_ref, ar_ref, ai_ref, wr_ref, dsk_ref, gw_ref, gb_ref,
         out_ref, buf0, buf1, xst) = refs
    nb = 2 * S5_BLK_STATE
    lo, hi = slice(0, S5_ROWS), slice(S5_ROWS, 2 * S5_ROWS)
    first, second = (hi, lo) if reverse else (lo, hi)

    def drive(u, buf):
        ub = u.astype(BF16)
        for k in range(S5_NBLK):
            buf[:, k * nb:(k + 1) * nb] = _dot(ub[:, k * LANE:(k + 1) * LANE], wd_ref[k])

    def scan(buf, x):
        for s in range(S5_T):
            t = (S5_T - 1 - s) if reverse else s
            rows = slice(t * BATCH, (t + 1) * BATCH)
            halves = []
            for k in range(S5_NBLK):
                halves.append(x[:, k * nb + S5_BLK_STATE:(k + 1) * nb])
                halves.append(x[:, k * nb:k * nb + S5_BLK_STATE])
            xsw = jnp.concatenate(halves, axis=1)
            x = ar_ref[...] * x + ai_ref[...] * xsw + buf[rows, :]
            buf[rows, :] = x
        return x

    def readout(buf, rows):
        y = jnp.concatenate(
            [_dot(buf[:, k * nb:(k + 1) * nb].astype(BF16), wr_ref[k]) for k in range(S5_NBLK)], axis=1)
        if reverse:
            out_ref[rows, :] = y
        else:
            y = u2_ref[rows, :] * dsk_ref[...] + y + yb2_ref[rows, :]
            z = _gelu_tanh(y)
            out_ref[rows, :] = (z * jax.nn.sigmoid(_dot(z.astype(BF16), gw_ref[...]) + gb_ref[...])).astype(BF16)

    @pl.when(pl.program_id(0) == 0)
    def _():
        xst[...] = jnp.zeros_like(xst)
        drive(u2_ref[first, :], buf0)

    x = xst[...]
    drive(u2_ref[second, :], buf1)
    x = scan(buf0, x)
    readout(buf0, first)
    drive(un_ref[...], buf0)
    x = scan(buf1, x)
    readout(buf1, second)
    xst[...] = x


def _s5_call(u, yb, wd, ar, ai, wr, dsk, gw, gb, reverse):
    def chunk(p):
        return _chunk_of(p, reverse, S5_NLAT, S5_NC)

    pair = pl.BlockSpec((2 * S5_ROWS, MIX_W), lambda k: (chunk(2 * k) // 2, 0))
    nxt = pl.BlockSpec((S5_ROWS, MIX_W), lambda k: (chunk(jnp.minimum(2 * k + 2, S5_NC - 1)), 0))
    if reverse:
        args = [u, u, wd, ar, ai, wr]
        specs = [pair, nxt] + [_full(a.shape) for a in args[2:]]
        out_dtype = F32
    else:
        args = [u, u, yb, wd, ar, ai, wr, dsk, gw, gb]
        specs = [pair, nxt, pair] + [_full(a.shape) for a in args[3:]]
        out_dtype = BF16
    return pl.pallas_call(
        functools.partial(_s5_kernel, reverse=reverse),
        out_shape=jax.ShapeDtypeStruct((L_TOT * BATCH, MIX_W), out_dtype),
        grid=(S5_NC // 2,),
        in_specs=specs,
        out_specs=pair,
        scratch_shapes=[
            pltpu.VMEM((S5_ROWS, S5_SW), F32),
            pltpu.VMEM((S5_ROWS, S5_SW), F32),
            pltpu.VMEM((BATCH, S5_SW), F32),
        ],
        compiler_params=_cparams(("arbitrary",)),
        name="s5_bwd" if reverse else "s5_fwd",
    )(*args)


POST_ROWS = POST_T * BATCH
POST_HALO_T = BF16_ROWS // BATCH
POST_HALO = POST_HALO_T * BATCH
POST_NLAT = SEQ // POST_T
FFN_NCH = FFN_HIDDEN // FFN_CH


def _post_kernel(h_ref, hp_ref, hn_ref, a_ref, ap_ref, an_ref, b_ref, bp_ref, bn_ref, wo_ref,
                 g2_ref, sh_ref, sc_ref, g5_ref, g_ref, wu_ref, cw_ref, cb_ref, wd_ref, o_ref, *, n_tiles, out_batch_major):
    j = pl.program_id(0)
    first = jnp.logical_or(j == 0, j == POST_NLAT)
    last = jnp.logical_or(j == POST_NLAT - 1, j == n_tiles - 1)
    gate2 = g2_ref[0]
    gain = g_ref[...]
    scale = 1.0 + sc_ref[0]
    shift = sh_ref[0]

    def mix(h, a, b_tm):
        m = _dot(a, wo_ref[0:MIX_W, :]) + _dot(b_tm.astype(BF16), wo_ref[MIX_W:2 * MIX_W, :])
        return h + _per_sample(m, gate2, jnp.multiply)

    def norm(x):
        return _per_sample(_per_sample(_rms(x, gain), scale, jnp.multiply), shift, jnp.add)

    def halo_b(ref, t0):
        x = jnp.swapaxes(ref[...].astype(F32), 0, 1)[t0:t0 + POST_HALO_T]
        return x.reshape(POST_HALO, MIX_W)

    h_mid = mix(h_ref[...], a_ref[...], _to_time_major(b_ref[...].astype(F32)))
    h_prev = mix(hp_ref[...], ap_ref[...], halo_b(bp_ref, BF16_ROWS - POST_HALO_T))
    h_next = mix(hn_ref[...], an_ref[...], halo_b(bn_ref, 0))
    xn = jnp.concatenate([
        (norm(h_prev) * jnp.where(first, 0.0, 1.0)).astype(BF16),
        norm(h_mid).astype(BF16),
        (norm(h_next) * jnp.where(last, 0.0, 1.0)).astype(BF16)], axis=0)

    acts = []
    for c in range(FFN_NCH):
        cv = slice(c * FFN_CH, (c + 1) * FFN_CH)
        cg = slice(FFN_HIDDEN + c * FFN_CH, FFN_HIDDEN + (c + 1) * FFN_CH)

        def conv(cols):
            hid = _dot(xn, wu_ref[:, cols])
            out = cb_ref[:, cols]
            for k in range(FFN_CONV):
                r0 = POST_HALO + (k - 1) * BATCH
                out = out + hid[r0:r0 + POST_ROWS] * cw_ref[k:k + 1, cols]
            return out

        val = conv(cv)
        gate = conv(cg)
        acts.append((val * (gate * jax.nn.sigmoid(gate))).astype(BF16))
    ffn = _dot(jnp.concatenate(acts, axis=1), wd_ref[...])
    out = h_mid + _per_sample(ffn, g5_ref[0], jnp.multiply)
    if out_batch_major:
        o_ref[...] = _to_batch_major(out)
    else:
        o_ref[...] = out


def _post_call(h, a_tb, b_out, w_out, modt, g, wu, cw, cb, wd, n_tiles, out_batch_major):
    per = POST_ROWS // POST_HALO
    n_halo = h.shape[0] // POST_HALO
    per_b = POST_T // BF16_ROWS
    n_halo_b = b_out.shape[1] // BF16_ROWS

    def main(width):
        return pl.BlockSpec((POST_ROWS, width), lambda j: (j, 0))

    def prev(width):
        return pl.BlockSpec((POST_HALO, width), lambda j: (jnp.maximum(j * per - 1, 0), 0))

    def nxt(width):
        return pl.BlockSpec((POST_HALO, width), lambda j: (jnp.minimum((j + 1) * per, n_halo - 1), 0))

    if out_batch_major:
        out_shape = jax.ShapeDtypeStruct((BATCH, n_tiles * POST_T, D_MODEL), F32)
        out_spec = pl.BlockSpec((BATCH, POST_T, D_MODEL), lambda j: (0, j, 0))
    else:
        out_shape = jax.ShapeDtypeStruct((n_tiles * POST_ROWS, D_MODEL), F32)
        out_spec = main(D_MODEL)
    return pl.pallas_call(
        functools.partial(_post_kernel, n_tiles=n_tiles, out_batch_major=out_batch_major),
        out_shape=out_shape,
        grid=(n_tiles,),
        in_specs=[
            main(D_MODEL), prev(D_MODEL), nxt(D_MODEL),
            main(MIX_W), prev(MIX_W), nxt(MIX_W),
            pl.BlockSpec((BATCH, POST_T, MIX_W), lambda j: (0, j, 0)),
            pl.BlockSpec((BATCH, BF16_ROWS, MIX_W), lambda j: (0, jnp.maximum(j * per_b - 1, 0), 0)),
            pl.BlockSpec((BATCH, BF16_ROWS, MIX_W), lambda j: (0, jnp.minimum((j + 1) * per_b, n_halo_b - 1), 0)),
            _full((2 * MIX_W, D_MODEL), single=True),
            _mod_spec(2, POST_NLAT), _mod_spec(3, POST_NLAT), _mod_spec(4, POST_NLAT), _mod_spec(5, POST_NLAT),
            _full((1, D_MODEL)),
            _full((D_MODEL, 2 * FFN_HIDDEN), single=True),
            _full((FFN_CONV, 2 * FFN_HIDDEN)),
            _full((1, 2 * FFN_HIDDEN)),
            _full((FFN_HIDDEN, D_MODEL), single=True),
        ],
        out_specs=out_spec,
        compiler_params=_cparams(("parallel",)),
        name="post_ffn",
    )(h, h, h, a_tb, a_tb, a_tb, b_out, b_out, b_out, w_out, modt, modt, modt, modt, g, wu, cw, cb, wd)


CAST_STEPS = 4


def _cast_kernel(w_ref, o_ref):
    o_ref[...] = w_ref[0].astype(BF16)


def _cast_call(w, layer):
    _, r, c = w.shape
    tr = r // CAST_STEPS
    return pl.pallas_call(
        _cast_kernel,
        out_shape=jax.ShapeDtypeStruct((r, c), BF16),
        grid=(CAST_STEPS,),
        in_specs=[pl.BlockSpec((1, tr, c), lambda i: (layer, i, 0))],
        out_specs=pl.BlockSpec((tr, c), lambda i: (i, 0)),
        compiler_params=_cparams(("parallel",)),
        name="cast_bf16",
    )(w)


def _rope_table():
    rows = SEQ // GRID_W
    row = jnp.repeat(jnp.arange(rows, dtype=F32), GRID_W)
    col = jnp.tile(jnp.arange(GRID_W, dtype=F32), rows)
    quarter = HEAD_DIM // 4
    inv_freq = ROPE_BASE ** (-jnp.arange(quarter, dtype=F32) / quarter)
    ang = jnp.stack([row[:, None] * inv_freq, col[:, None] * inv_freq], axis=1)
    cos = jnp.cos(ang)
    sin = jnp.sin(ang)
    zero = jnp.zeros_like(sin)

    def lanes(first, second):
        t = jnp.stack([first, second], axis=2).reshape(SEQ, HEAD_DIM)
        return jnp.tile(t, (1, LANE // HEAD_DIM))

    tab = jnp.concatenate([lanes(cos, cos), lanes(-sin, zero), lanes(zero, sin)], axis=1)
    ctx = jnp.concatenate([jnp.ones((CTX_LEN, LANE), F32), jnp.zeros((CTX_LEN, 2 * LANE), F32)], axis=1)
    return jnp.concatenate([tab, ctx], axis=0)


def _perm_heads_cols(w):
    d = w.shape[0]
    return w.reshape(d, N_HEADS, HEAD_DIM)[:, jnp.array(HEAD_PERM)].reshape(d, N_HEADS * HEAD_DIM)


def _perm_heads_rows(w):
    n = w.shape[1]
    return w.reshape(N_HEADS, HEAD_DIM, n)[jnp.array(HEAD_PERM)].reshape(N_HEADS * HEAD_DIM, n)


def _block_diag(w, per):
    n, a, b = w.shape
    eye = jnp.eye(per, dtype=w.dtype)
    w4 = w.reshape(n // per, per, a, b)
    return jnp.einsum("ihab,hk->ihakb", w4, eye).reshape(n // per, per * a, per * b)


def _head_gain(g):
    return jnp.tile(g, LANE // HEAD_DIM).reshape(1, LANE)


def _w_in(w, q_off):
    return jnp.concatenate([w[:, :q_off], _perm_heads_cols(w[:, q_off:q_off + MIX_W]), w[:, q_off + MIX_W:]],
                           axis=1).astype(BF16)


def _w_out(w):
    return jnp.concatenate([w[:MIX_W], _perm_heads_rows(w[MIX_W:])], axis=0).astype(BF16)


def kernel(x, c, ctx, c_ctx, mod_w, mod_b, norm_g, ffn_up, ffn_conv_w, ffn_conv_b, ffn_down, ev_w_in, ev_w_out, rg_conv_w, rg_conv_b, rg_wa, rg_ba, rg_wx, rg_bx, rg_lam, ga_qn, ga_kn, od_w_in, od_w_out, s5_lam_re, s5_lam_im, s5_log_step, s5_b_re, s5_b_im, s5_c_re, s5_c_im, s5_d, s5_glu_w, s5_glu_b, wa_qn, wa_kn, wa_sink):
    cvec = jnp.concatenate([c, jnp.broadcast_to(c_ctx[None], (BATCH, D_MODEL))], axis=0)
    mod = _mod_call(cvec, mod_w, mod_b).reshape(DEPTH, 2, BATCH, N_MOD * D_MODEL)
    rope = _rope_table()
    gm = _block_diag(jnp.full((LANE // HEAD_DIM, HEAD_DIM, HEAD_DIM), 1.0 / HEAD_DIM, F32), LANE // HEAD_DIM)[0].astype(BF16)

    def ffn_args(layer):
        return (norm_g[layer, 1].reshape(1, D_MODEL), _cast_call(ffn_up, layer), ffn_conv_w[layer],
                ffn_conv_b[layer].reshape(1, 2 * FFN_HIDDEN), _cast_call(ffn_down, layer))

    h, u_tb, gate_tb, q, k, v = _in_call(
        (x, ctx), mod[0], norm_g[0, 0].reshape(1, D_MODEL), _w_in(ev_w_in[0], 2 * MIX_W), rope,
        _head_gain(ga_qn[0]), _head_gain(ga_kn[0]), gm, n_tb=2, first_layer=True)
    rg = []
    for d in range(2):
        rg.append([rg_conv_w[0], rg_conv_b[0].reshape(1, MIX_W),
                   _block_diag(rg_wa[0, d], 4).astype(BF16), rg_ba[0, d].reshape(1, MIX_W),
                   _block_diag(rg_wx[0, d], 4).astype(BF16), rg_bx[0, d].reshape(1, MIX_W),
                   rg_lam[0, d].reshape(1, MIX_W)])
    hb = _rg_call(u_tb, None, None, *rg[1], reverse=True)
    a_out = _rg_call(u_tb, gate_tb, hb, *rg[0], reverse=False)
    b_out = _gattn_call(q, k, v)
    h = _post_call(h, a_out, b_out, _w_out(ev_w_out[0]), mod[0], *ffn_args(0),
                   n_tiles=L_TOT // POST_T, out_batch_major=False)

    u_tb, q, k, v = _in_call(
        (h,), mod[1], norm_g[1, 0].reshape(1, D_MODEL), _w_in(od_w_in[0], MIX_W), rope,
        _head_gain(wa_qn[0]), _head_gain(wa_kn[0]), gm, n_tb=1, first_layer=False)
    n_state = S5_GROUPS * S5_STATE
    bt_re = s5_b_re[0].transpose(0, 3, 1, 2).reshape(2, S5_GROUP, n_state)
    bt_im = s5_b_im[0].transpose(0, 3, 1, 2).reshape(2, S5_GROUP, n_state)
    log_step = jnp.repeat(s5_log_step[0], S5_STATE, axis=-1).reshape(2, 1, n_state)
    abr, abi, bbr, bbi = _s5_disc_call(s5_lam_re[0].reshape(2, 1, n_state), s5_lam_im[0].reshape(2, 1, n_state),
                                       log_step, bt_re, bt_im)
    gpb = S5_GROUPS // S5_NBLK
    eye = jnp.eye(gpb, dtype=F32)

    def drive_w(bb):
        return jnp.einsum("hbgp,gk->bghkp", bb.reshape(S5_GROUP, S5_NBLK, gpb, S5_STATE), eye).reshape(
            S5_NBLK, gpb * S5_GROUP, S5_BLK_STATE)

    def read_w(cc):
        return jnp.einsum("bghp,gk->bgpkh", cc.reshape(S5_NBLK, gpb, S5_GROUP, S5_STATE), eye).reshape(
            S5_NBLK, S5_BLK_STATE, gpb * S5_GROUP)

    def blocks(re, im):
        row = jnp.concatenate([re.reshape(1, S5_NBLK, S5_BLK_STATE), im.reshape(1, S5_NBLK, S5_BLK_STATE)],
                              axis=2).reshape(1, S5_SW)
        return jnp.broadcast_to(row, (BATCH, S5_SW))

    s5 = []
    for d in range(2):
        wd = jnp.concatenate([drive_w(bbr[d]), drive_w(bbi[d])], axis=2).astype(BF16)
        wr = jnp.concatenate([read_w(s5_c_re[0, d]), -read_w(s5_c_im[0, d])], axis=1).astype(BF16)
        s5.append([wd, blocks(abr[d], abr[d]), blocks(-abi[d], abi[d]), wr])
    yb = _s5_call(u_tb, None, *s5[1], None, None, None, reverse=True)
    c_out = _s5_call(u_tb, yb, *s5[0], s5_d[0].reshape(1, MIX_W), s5_glu_w[0].astype(BF16),
                     s5_glu_b[0].reshape(1, MIX_W), reverse=False)
    sink = wa_sink[0][jnp.array(HEAD_PERM)].reshape(N_HEADS, 1, 1)
    d_out = _wattn_call(q, k, v, sink)
    return _post_call(h, c_out, d_out, _w_out(od_w_out[0]), mod[1], *ffn_args(1),
                      n_tiles=SEQ // POST_T, out_batch_major=True)
```

```python
import functools
import math

import jax
import jax.numpy as jnp
from jax import lax
from jax.experimental import pallas as pl
from jax.experimental.pallas import tpu as pltpu

F32 = jnp.float32
BF16 = jnp.bfloat16

D_MODEL = 1024
BATCH = 8
SEQ = 2048
CTX_LEN = 256
L_TOT = SEQ + CTX_LEN
DEPTH = 2
GRID_W = 64
HEAD_DIM = 64
ROPE_BASE = 10000.0
NORM_EPS = 1e-6
WINDOW = 128
N_MOD = 6
MIX_W = 512
N_HEADS = 8
N_KV = 2
KV_W = N_KV * HEAD_DIM
V_W = 2 * KV_W
QKV_W = MIX_W + KV_W + KV_W
Q_SCALE = HEAD_DIM ** -0.5 * math.log2(math.e)
RG_CONV = 4
RG_C = 8.0
S5_GROUP = 16
S5_GROUPS = 32
S5_STATE = 64
S5_NBLK = 4
S5_BLK_STATE = (S5_GROUPS // S5_NBLK) * S5_STATE
FFN_HIDDEN = 2816
FFN_CONV = 3

LANE = 128
BF16_ROWS = 16
VMEM_LIMIT = 56 * 1024 * 1024

IN_T = 128
RG_T = 128
S5_T = 64
POST_T = 128
TQ_G = 256
FFN_CH = 256
HEAD_PERM = (0, 4, 1, 5, 2, 6, 3, 7)


def _cparams(sem):
    return pltpu.CompilerParams(dimension_semantics=sem, vmem_limit_bytes=VMEM_LIMIT)


def _dot(a, b):
    return jnp.dot(a, b, preferred_element_type=F32)


def _gelu_tanh(x):
    return x * (0.5 * (1.0 + jnp.tanh(math.sqrt(2.0 / math.pi) * (x + 0.044715 * (x * x * x)))))


def _rms(x, g):
    ms = jnp.mean(x * x, axis=-1, keepdims=True)
    return x * lax.rsqrt(ms + NORM_EPS) * g


def _to_time_major(x):
    b, t, c = x.shape
    return jnp.swapaxes(x, 0, 1).reshape(t * b, c)


def _to_batch_major(x):
    r, c = x.shape
    return jnp.swapaxes(x.reshape(r // BATCH, BATCH, c), 0, 1)


def _per_sample(x, v, op):
    r, c = x.shape
    return op(x.reshape(r // BATCH, BATCH, c), v[None]).reshape(r, c)


def _full(shape, single=False):
    kw = dict(pipeline_mode=pl.Buffered(1)) if single else {}
    return pl.BlockSpec(shape, lambda *_: (0,) * len(shape), **kw)


MOD_TN = 1536


def _mod_kernel(c_ref, w_ref, b_ref, o_ref):
    c = c_ref[...]
    a = c * jax.nn.sigmoid(c)
    w = w_ref[0]
    ah = a.astype(BF16)
    al = (a - ah.astype(F32)).astype(BF16)
    wh = w.astype(BF16)
    wl = (w - wh.astype(F32)).astype(BF16)
    o_ref[0] = _dot(ah, wh) + _dot(ah, wl) + _dot(al, wh) + b_ref[0]


def _mod_call(cvec, mod_w, mod_b):
    n = N_MOD * D_MODEL
    return pl.pallas_call(
        _mod_kernel,
        out_shape=jax.ShapeDtypeStruct((DEPTH, 2 * BATCH, n), F32),
        grid=(DEPTH, n // MOD_TN),
        in_specs=[
            pl.BlockSpec((2 * BATCH, D_MODEL), lambda l, k: (0, 0)),
            pl.BlockSpec((1, D_MODEL, MOD_TN), lambda l, k: (l, 0, k)),
            pl.BlockSpec((1, 1, MOD_TN), lambda l, k: (l, 0, k)),
        ],
        out_specs=pl.BlockSpec((1, 2 * BATCH, MOD_TN), lambda l, k: (l, 0, k)),
        compiler_params=_cparams(("parallel", "parallel")),
        name="mod",
    )(cvec, mod_w, mod_b.reshape(DEPTH, 1, n))


def _mod_spec(col, n_latent_tiles):
    return pl.BlockSpec((1, BATCH, D_MODEL), lambda j: (jnp.where(j < n_latent_tiles, 0, 1), 0, col))


IN_ROWS = IN_T * BATCH
IN_NLAT = SEQ // IN_T
IN_N = L_TOT // IN_T
IN_SUB = 4
IN_SUB_T = IN_T // IN_SUB
IN_SUB_ROWS = IN_SUB_T * BATCH


def _in_kernel(*refs, n_tb, first_layer):
    if first_layer:
        x_ref, c_ref, sh_ref, sc_ref, g_ref, w_ref, rope_ref, gq_ref, gk_ref, gm_ref = refs[:10]
        h_out = refs[10]
        outs = refs[11:]
        is_latent = pl.program_id(0) < IN_NLAT
    else:
        h_ref, sh_ref, sc_ref, g_ref, w_ref, rope_ref, gq_ref, gk_ref, gm_ref = refs[:9]
        outs = refs[9:]
    q_ref, k_ref, v_ref = outs[n_tb:]
    off = n_tb * MIX_W
    gm = gm_ref[...]
    for s in range(IN_SUB):
        ts = slice(s * IN_SUB_T, (s + 1) * IN_SUB_T)
        rs = slice(s * IN_SUB_ROWS, (s + 1) * IN_SUB_ROWS)
        if first_layer:
            h = _to_time_major(jnp.where(is_latent, x_ref[:, ts, :], c_ref[:, ts, :]))
            h_out[rs, :] = h
        else:
            h = h_ref[rs, :]
        xn = _per_sample(_rms(h, g_ref[...]), 1.0 + sc_ref[0], jnp.multiply)
        xn = _per_sample(xn, sh_ref[0], jnp.add)
        proj = _dot(xn.astype(BF16), w_ref[...])
        for i in range(n_tb):
            outs[i][rs, :] = proj[:, i * MIX_W:(i + 1) * MIX_W]
        qkv = _to_batch_major(proj[:, off:off + QKV_W])
        cos = rope_ref[ts, 0:LANE][None]
        sin_up = rope_ref[ts, LANE:2 * LANE][None]
        sin_dn = rope_ref[ts, 2 * LANE:3 * LANE][None]
        for p in range(5):
            blk = qkv[:, :, p * LANE:(p + 1) * LANE].reshape(IN_SUB_ROWS, LANE)
            sq = blk * blk
            hi = sq.astype(BF16)
            lo = (sq - hi.astype(F32)).astype(BF16)
            ms = _dot(hi, gm) + _dot(lo, gm)
            g = gq_ref[...] if p < 4 else gk_ref[...]
            bn = blk * lax.rsqrt(ms + NORM_EPS) * g
            up = pltpu.roll(bn, LANE - HEAD_DIM // 4, 1).reshape(BATCH, IN_SUB_T, LANE)
            dn = pltpu.roll(bn, HEAD_DIM // 4, 1).reshape(BATCH, IN_SUB_T, LANE)
            ro = bn.reshape(BATCH, IN_SUB_T, LANE) * cos + up * sin_up + dn * sin_dn
            if p < 4:
                q_ref[:, ts, p * LANE:(p + 1) * LANE] = (ro * Q_SCALE).astype(BF16)
            else:
                k_ref[:, ts, :] = ro.astype(BF16)
        v = qkv[:, :, 5 * LANE:6 * LANE].astype(BF16)
        v_ref[:, ts, :] = jnp.concatenate([v, jnp.ones((BATCH, IN_SUB_T, LANE), BF16)], axis=2)


def _in_call(h_args, modt, g, w, rope, gq, gk, gm, n_tb, first_layer):
    n = w.shape[1]
    tb = lambda width: pl.BlockSpec((IN_ROWS, width), lambda j: (j, 0))
    bm = lambda width: pl.BlockSpec((BATCH, IN_T, width), lambda j: (0, j, 0))
    if first_layer:
        h_specs = [
            pl.BlockSpec((BATCH, IN_T, D_MODEL), lambda j: (0, jnp.minimum(j, IN_NLAT - 1), 0)),
            pl.BlockSpec((BATCH, IN_T, D_MODEL), lambda j: (0, jnp.maximum(j - IN_NLAT, 0), 0)),
        ]
        extra_shape = [jax.ShapeDtypeStruct((L_TOT * BATCH, D_MODEL), F32)]
        extra_spec = [tb(D_MODEL)]
    else:
        h_specs = [tb(D_MODEL)]
        extra_shape, extra_spec = [], []
    return pl.pallas_call(
        functools.partial(_in_kernel, n_tb=n_tb, first_layer=first_layer),
        out_shape=extra_shape + [jax.ShapeDtypeStruct((L_TOT * BATCH, MIX_W), F32)] * n_tb + [
            jax.ShapeDtypeStruct((BATCH, L_TOT, MIX_W), BF16),
            jax.ShapeDtypeStruct((BATCH, L_TOT, KV_W), BF16),
            jax.ShapeDtypeStruct((BATCH, L_TOT, V_W), BF16),
        ],
        grid=(IN_N,),
        in_specs=h_specs + [
            _mod_spec(0, IN_NLAT), _mod_spec(1, IN_NLAT),
            _full((1, D_MODEL)),
            _full((D_MODEL, n), single=True),
            pl.BlockSpec((IN_T, 3 * LANE), lambda j: (j, 0)),
            _full((1, LANE)), _full((1, LANE)), _full((LANE, LANE)),
        ],
        out_specs=extra_spec + [tb(MIX_W)] * n_tb + [bm(MIX_W), bm(KV_W), bm(V_W)],
        compiler_params=_cparams(("parallel",)),
        name="in_proj",
    )(*h_args, modt, modt, g, w, rope, gq, gk, gm)


def _chunk_of(i, reverse, n_latent, n_chunks):
    if reverse:
        return n_chunks - 1 - i
    return jnp.where(i < n_chunks - n_latent, i + n_latent, i - (n_chunks - n_latent))


RG_ROWS = RG_T * BATCH
RG_NC = L_TOT // RG_T
RG_NLAT = SEQ // RG_T


def _softplus(z):
    return jnp.maximum(z, 0.0) + jnp.log1p(jnp.exp(-jnp.abs(z)))


def _rg_kernel(*refs, reverse):
    if reverse:
        (u_ref, up_ref, un_ref, cw_ref, cb_ref, wa_ref, ba_ref, wx_ref, bx_ref, lam_ref,
         out_ref, xbuf, abuf, bbuf, hst) = refs
    else:
        (u_ref, up_ref, un_ref, gate_ref, hb_ref, cw_ref, cb_ref, wa_ref, ba_ref, wx_ref, bx_ref, lam_ref,
         out_ref, xbuf, abuf, bbuf, hst) = refs
    i = pl.program_id(0)
    c = _chunk_of(i, reverse, RG_NLAT, RG_NC)
    prev_zero = jnp.logical_or(c == 0, c == RG_NLAT)
    next_zero = jnp.logical_or(c == RG_NLAT - 1, c == RG_NC - 1)

    @pl.when(i == 0)
    def _():
        hst[...] = jnp.zeros_like(hst)

    xbuf[0:BATCH, :] = up_ref[...] * jnp.where(prev_zero, 0.0, 1.0)
    xbuf[BATCH:BATCH + RG_ROWS, :] = u_ref[...]
    xbuf[BATCH + RG_ROWS:3 * BATCH + RG_ROWS, :] = un_ref[...] * jnp.where(next_zero, 0.0, 1.0)
    uc = cb_ref[...]
    for k in range(RG_CONV):
        uc = uc + xbuf[k * BATCH:k * BATCH + RG_ROWS, :] * cw_ref[k:k + 1, :]

    ub = uc.astype(BF16)
    half = MIX_W // 2

    def gate(w_ref, b_ref):
        z = jnp.concatenate([_dot(ub[:, :half], w_ref[0]), _dot(ub[:, half:], w_ref[1])], axis=1)
        return jax.nn.sigmoid(z + b_ref[...])

    r = gate(wa_ref, ba_ref)
    ig = gate(wx_ref, bx_ref)
    log_a = (-RG_C) * r * _softplus(-lam_ref[...])
    a = jnp.exp(log_a)
    abuf[...] = a
    bbuf[...] = jnp.sqrt(-jnp.tanh(log_a) * (a * a + 1.0)) * (ig * uc)

    def step(s, h):
        t = (RG_T - 1 - s) if reverse else s
        r0 = pl.multiple_of(t * BATCH, BATCH)
        h = abuf[pl.ds(r0, BATCH), :] * h + bbuf[pl.ds(r0, BATCH), :]
        bbuf[pl.ds(r0, BATCH), :] = h
        return h

    hst[...] = lax.fori_loop(0, RG_T, step, hst[...], unroll=8)

    if reverse:
        out_ref[...] = bbuf[...]
    else:
        y = bbuf[...] + hb_ref[...]
        out_ref[...] = (y * _gelu_tanh(gate_ref[...])).astype(BF16)


def _rg_call(u, gate, hb, cw, cb, wa, ba, wx, bx, lam, reverse):
    def cidx(i):
        return _chunk_of(i, reverse, RG_NLAT, RG_NC)

    main = pl.BlockSpec((RG_ROWS, MIX_W), lambda i: (cidx(i), 0))
    prev = pl.BlockSpec((BATCH, MIX_W), lambda i: (jnp.maximum(cidx(i) * RG_T - 1, 0), 0))
    n_next = L_TOT // 2
    nxt = pl.BlockSpec((2 * BATCH, MIX_W), lambda i: (jnp.minimum((cidx(i) + 1) * (RG_T // 2), n_next - 1), 0))
    params = [cw, cb, wa, ba, wx, bx, lam]
    pspecs = [_full(p.shape) for p in params]
    if reverse:
        args = [u, u, u] + params
        specs = [main, prev, nxt] + pspecs
        out_dtype = F32
    else:
        args = [u, u, u, gate, hb] + params
        specs = [main, prev, nxt, main, main] + pspecs
        out_dtype = BF16
    return pl.pallas_call(
        functools.partial(_rg_kernel, reverse=reverse),
        out_shape=jax.ShapeDtypeStruct((L_TOT * BATCH, MIX_W), out_dtype),
        grid=(RG_NC,),
        in_specs=specs,
        out_specs=main,
        scratch_shapes=[
            pltpu.VMEM((RG_ROWS + 3 * BATCH, MIX_W), F32),
            pltpu.VMEM((RG_ROWS, MIX_W), F32),
            pltpu.VMEM((RG_ROWS, MIX_W), F32),
            pltpu.VMEM((BATCH, MIX_W), F32),
        ],
        compiler_params=_cparams(("arbitrary",)),
        name="rglru_bwd" if reverse else "rglru_fwd",
    )(*args)


def _head_rows(q_ref, tq):
    lane = lax.broadcasted_iota(jnp.int32, (tq, LANE), 1)
    low = lane < HEAD_DIM
    zero = jnp.zeros((tq, LANE), BF16)
    parts = []
    for p in range(MIX_W // LANE):
        qb = q_ref[0, :, p * LANE:(p + 1) * LANE]
        parts.append(jnp.where(low, qb, zero))
        parts.append(jnp.where(low, zero, qb))
    return parts, low


def _merge_heads(outs, low):
    return jnp.concatenate([jnp.where(low, outs[2 * p], outs[2 * p + 1]) for p in range(MIX_W // LANE)], axis=1)


def _qk(q, k):
    return lax.dot_general(q, k, (((1,), (1,)), ((), ())), preferred_element_type=F32)


def _gattn_kernel(q_ref, k_ref, v_ref, o_ref):
    j = pl.program_id(1)
    heads, low = _head_rows(q_ref, TQ_G)

    def attend(k0, nk):
        outs = []
        for q in heads:
            s = _qk(q, k_ref[0, k0:k0 + nk, :])
            m = jnp.max(s, axis=-1, keepdims=True)
            p = jnp.exp2(s - m).astype(BF16)
            acc = _dot(p, v_ref[0, k0:k0 + nk, :])
            outs.append(acc[:, :KV_W] / acc[:, KV_W:])
        o_ref[0] = _merge_heads(outs, low).astype(BF16)

    @pl.when(j < SEQ // TQ_G)
    def _():
        attend(0, L_TOT)

    @pl.when(j >= SEQ // TQ_G)
    def _():
        attend(SEQ, CTX_LEN)


def _gattn_call(q, k, v):
    return pl.pallas_call(
        _gattn_kernel,
        out_shape=jax.ShapeDtypeStruct((BATCH, L_TOT, MIX_W), BF16),
        grid=(BATCH, L_TOT // TQ_G),
        in_specs=[
            pl.BlockSpec((1, TQ_G, MIX_W), lambda b, j: (b, j, 0)),
            pl.BlockSpec((1, L_TOT, KV_W), lambda b, j: (b, 0, 0)),
            pl.BlockSpec((1, L_TOT, V_W), lambda b, j: (b, 0, 0)),
        ],
        out_specs=pl.BlockSpec((1, TQ_G, MIX_W), lambda b, j: (b, j, 0)),
        compiler_params=_cparams(("parallel", "arbitrary")),
        name="global_attn",
    )(q, k, v)


TQ_W = 2 * WINDOW
N_WBLK = SEQ // WINDOW
N_WSTEP = SEQ // TQ_W


def _wattn_kernel(q_ref, k_ref, v_ref, sink_ref, o_ref):
    i = pl.program_id(1)
    heads, low = _head_rows(q_ref, TQ_W)
    prev_start = pl.multiple_of(jnp.maximum(2 * i - 1, 0) * WINDOW, WINDOW)
    mid_start = pl.multiple_of(i * TQ_W, TQ_W)
    next_start = pl.multiple_of(jnp.minimum(2 * i + 2, N_WBLK - 1) * WINDOW, WINDOW)

    def rows(ref):
        return jnp.concatenate([ref[0, SEQ:L_TOT, :], ref[0, pl.ds(prev_start, WINDOW), :],
                                ref[0, pl.ds(mid_start, TQ_W), :], ref[0, pl.ds(next_start, WINDOW), :]], axis=0)

    nk = CTX_LEN + 2 * WINDOW + TQ_W
    r = lax.broadcasted_iota(jnp.int32, (TQ_W, nk), 0)
    col = lax.broadcasted_iota(jnp.int32, (TQ_W, nk), 1)
    c = col - CTX_LEN
    ninf = -jnp.inf
    pen_prev = jnp.where(i > 0, 0.0, ninf)
    pen_next = jnp.where(i < N_WSTEP - 1, 0.0, ninf)
    edge = jnp.where(c < WINDOW, pen_prev, jnp.where(c >= WINDOW + TQ_W, pen_next, 0.0))
    band = jnp.where(c >= r, jnp.where(c <= r + 2 * WINDOW, edge, ninf), ninf)
    bias = jnp.where(col < CTX_LEN, 0.0, band)
    kk = rows(k_ref)
    vv = rows(v_ref)
    outs = []
    for hd, q in enumerate(heads):
        s = _qk(q, kk) + bias
        sink = sink_ref[hd] * math.log2(math.e)
        m = jnp.maximum(jnp.max(s, axis=-1, keepdims=True), sink)
        p = jnp.exp2(s - m).astype(BF16)
        acc = _dot(p, vv)
        outs.append(acc[:, :KV_W] / (acc[:, KV_W:] + jnp.exp2(sink - m)))
    o_ref[0] = _merge_heads(outs, low).astype(BF16)


def _wattn_call(q, k, v, sink):
    return pl.pallas_call(
        _wattn_kernel,
        out_shape=jax.ShapeDtypeStruct((BATCH, SEQ, MIX_W), BF16),
        grid=(BATCH, N_WSTEP),
        in_specs=[
            pl.BlockSpec((1, TQ_W, MIX_W), lambda b, i: (b, i, 0)),
            pl.BlockSpec((1, L_TOT, KV_W), lambda b, i: (b, 0, 0)),
            pl.BlockSpec((1, L_TOT, V_W), lambda b, i: (b, 0, 0)),
            pl.BlockSpec((N_HEADS, 1, 1), lambda b, i: (0, 0, 0)),
        ],
        out_specs=pl.BlockSpec((1, TQ_W, MIX_W), lambda b, i: (b, i, 0)),
        compiler_params=_cparams(("parallel", "arbitrary")),
        name="window_attn",
    )(q, k, v, sink)


def _s5_disc_kernel(lr_ref, li_ref, ls_ref, br_ref, bi_ref, ar_ref, ai_ref, bbr_ref, bbi_ref):
    lr = lr_ref[0]
    li = li_ref[0]
    dt = jnp.exp(ls_ref[0])
    mag = jnp.exp(lr * dt)
    ang = li * dt
    abr = mag * jnp.cos(ang)
    abi = mag * jnp.sin(ang)
    den = lr * lr + li * li
    nr = abr - 1.0
    kr = (nr * lr + abi * li) / den
    ki = (abi * lr - nr * li) / den
    br = br_ref[0]
    bi = bi_ref[0]
    ar_ref[0] = abr
    ai_ref[0] = abi
    bbr_ref[0] = kr * br - ki * bi
    bbi_ref[0] = kr * bi + ki * br


def _s5_disc_call(lam_re, lam_im, log_step, bt_re, bt_im):
    n = S5_GROUPS * S5_STATE
    row = pl.BlockSpec((1, 1, n), lambda d: (d, 0, 0))
    mat = pl.BlockSpec((1, S5_GROUP, n), lambda d: (d, 0, 0))
    return pl.pallas_call(
        _s5_disc_kernel,
        out_shape=[jax.ShapeDtypeStruct((2, 1, n), F32)] * 2 + [jax.ShapeDtypeStruct((2, S5_GROUP, n), F32)] * 2,
        grid=(2,),
        in_specs=[row, row, row, mat, mat],
        out_specs=[row, row, mat, mat],
        compiler_params=_cparams(("parallel",)),
        name="s5_discretize",
    )(lam_re, lam_im, log_step, bt_re, bt_im)


S5_ROWS = S5_T * BATCH
S5_NC = L_TOT // S5_T
S5_NLAT = SEQ // S5_T
S5_SW = 2 * S5_BLK_STATE * S5_NBLK


def _s5_kernel(*refs, reverse):
    if reverse:
        u2_ref, un_ref, wd_ref, ar_ref, ai_ref, wr_ref, out_ref, buf0, buf1, xst = refs
    else:
        (u2_ref, un_ref, yb2_ref, wd_ref, ar_ref, ai_ref, wr_ref, dsk_ref, gw_ref, gb_ref,
         out_ref, buf0, buf1, xst) = refs
    nb = 2 * S5_BLK_STATE
    lo, hi = slice(0, S5_ROWS), slice(S5_ROWS, 2 * S5_ROWS)
    first, second = (hi, lo) if reverse else (lo, hi)

    def drive(u, buf):
        ub = u.astype(BF16)
        for k in range(S5_NBLK):
            buf[:, k * nb:(k + 1) * nb] = _dot(ub[:, k * LANE:(k + 1) * LANE], wd_ref[k])

    def scan(buf, x):
        for s in range(S5_T):
            t = (S5_T - 1 - s) if reverse else s
            rows = slice(t * BATCH, (t + 1) * BATCH)
            halves = []
            for k in range(S5_NBLK):
                halves.append(x[:, k * nb + S5_BLK_STATE:(k + 1) * nb])
                halves.append(x[:, k * nb:k * nb + S5_BLK_STATE])
            xsw = jnp.concatenate(halves, axis=1)
            x = ar_ref[...] * x + ai_ref[...] * xsw + buf[rows, :]
            buf[rows, :] = x
        return x

    def readout(buf, rows):
        y = jnp.concatenate(
            [_dot(buf[:, k * nb:(k + 1) * nb].astype(BF16), wr_ref[k]) for k in range(S5_NBLK)], axis=1)
        if reverse:
            out_ref[rows, :] = y
        else:
            y = u2_ref[rows, :] * dsk_ref[...] + y + yb2_ref[rows, :]
            z = _gelu_tanh(y)
            out_ref[rows, :] = (z * jax.nn.sigmoid(_dot(z.astype(BF16), gw_ref[...]) + gb_ref[...])).astype(BF16)

    @pl.when(pl.program_id(0) == 0)
    def _():
        xst[...] = jnp.zeros_like(xst)
        drive(u2_ref[first, :], buf0)

    x = xst[...]
    drive(u2_ref[second, :], buf1)
    x = scan(buf0, x)
    readout(buf0, first)
    drive(un_ref[...], buf0)
    x = scan(buf1, x)
    readout(buf1, second)
    xst[...] = x


def _s5_call(u, yb, wd, ar, ai, wr, dsk, gw, gb, reverse):
    def chunk(p):
        return _chunk_of(p, reverse, S5_NLAT, S5_NC)

    pair = pl.BlockSpec((2 * S5_ROWS, MIX_W), lambda k: (chunk(2 * k) // 2, 0))
    nxt = pl.BlockSpec((S5_ROWS, MIX_W), lambda k: (chunk(jnp.minimum(2 * k + 2, S5_NC - 1)), 0))
    if reverse:
        args = [u, u, wd, ar, ai, wr]
        specs = [pair, nxt] + [_full(a.shape) for a in args[2:]]
        out_dtype = F32
    else:
        args = [u, u, yb, wd, ar, ai, wr, dsk, gw, gb]
        specs = [pair, nxt, pair] + [_full(a.shape) for a in args[3:]]
        out_dtype = BF16
    return pl.pallas_call(
        functools.partial(_s5_kernel, reverse=reverse),
        out_shape=jax.ShapeDtypeStruct((L_TOT * BATCH, MIX_W), out_dtype),
        grid=(S5_NC // 2,),
        in_specs=specs,
        out_specs=pair,
        scratch_shapes=[
            pltpu.VMEM((S5_ROWS, S5_SW), F32),
            pltpu.VMEM((S5_ROWS, S5_SW), F32),
            pltpu.VMEM((BATCH, S5_SW), F32),
        ],
        compiler_params=_cparams(("arbitrary",)),
        name="s5_bwd" if reverse else "s5_fwd",
    )(*args)


POST_ROWS = POST_T * BATCH
POST_HALO_T = BF16_ROWS // BATCH
POST_HALO = POST_HALO_T * BATCH
POST_NLAT = SEQ // POST_T
FFN_NCH = FFN_HIDDEN // FFN_CH


def _post_kernel(h_ref, hp_ref, hn_ref, a_ref, ap_ref, an_ref, b_ref, bp_ref, bn_ref, wo_ref,
                 g2_ref, sh_ref, sc_ref, g5_ref, g_ref, wu_ref, cw_ref, cb_ref, wd_ref, o_ref, *, n_tiles, out_batch_major):
    j = pl.program_id(0)
    first = jnp.logical_or(j == 0, j == POST_NLAT)
    last = jnp.logical_or(j == POST_NLAT - 1, j == n_tiles - 1)
    gate2 = g2_ref[0]
    gain = g_ref[...]
    scale = 1.0 + sc_ref[0]
    shift = sh_ref[0]

    def mix(h, a, b_tm):
        m = _dot(a, wo_ref[0:MIX_W, :]) + _dot(b_tm.astype(BF16), wo_ref[MIX_W:2 * MIX_W, :])
        return h + _per_sample(m, gate2, jnp.multiply)

    def norm(x):
        return _per_sample(_per_sample(_rms(x, gain), scale, jnp.multiply), shift, jnp.add)

    def halo_b(ref, t0):
        x = jnp.swapaxes(ref[...].astype(F32), 0, 1)[t0:t0 + POST_HALO_T]
        return x.reshape(POST_HALO, MIX_W)

    h_mid = mix(h_ref[...], a_ref[...], _to_time_major(b_ref[...].astype(F32)))
    h_prev = mix(hp_ref[...], ap_ref[...], halo_b(bp_ref, BF16_ROWS - POST_HALO_T))
    h_next = mix(hn_ref[...], an_ref[...], halo_b(bn_ref, 0))
    xn = jnp.concatenate([
        (norm(h_prev) * jnp.where(first, 0.0, 1.0)).astype(BF16),
        norm(h_mid).astype(BF16),
        (norm(h_next) * jnp.where(last, 0.0, 1.0)).astype(BF16)], axis=0)

    acts = []
    for c in range(FFN_NCH):
        cv = slice(c * FFN_CH, (c + 1) * FFN_CH)
        cg = slice(FFN_HIDDEN + c * FFN_CH, FFN_HIDDEN + (c + 1) * FFN_CH)

        def conv(cols):
            hid = _dot(xn, wu_ref[:, cols])
            out = cb_ref[:, cols]
            for k in range(FFN_CONV):
                r0 = POST_HALO + (k - 1) * BATCH
                out = out + hid[r0:r0 + POST_ROWS] * cw_ref[k:k + 1, cols]
            return out

        val = conv(cv)
        gate = conv(cg)
        acts.append((val * (gate * jax.nn.sigmoid(gate))).astype(BF16))
    ffn = _dot(jnp.concatenate(acts, axis=1), wd_ref[...])
    out = h_mid + _per_sample(ffn, g5_ref[0], jnp.multiply)
    if out_batch_major:
        o_ref[...] = _to_batch_major(out)
    else:
        o_ref[...] = out


def _post_call(h, a_tb, b_out, w_out, modt, g, wu, cw, cb, wd, n_tiles, out_batch_major):
    per = POST_ROWS // POST_HALO
    n_halo = h.shape[0] // POST_HALO
    per_b = POST_T // BF16_ROWS
    n_halo_b = b_out.shape[1] // BF16_ROWS

    def main(width):
        return pl.BlockSpec((POST_ROWS, width), lambda j: (j, 0))

    def prev(width):
        return pl.BlockSpec((POST_HALO, width), lambda j: (jnp.maximum(j * per - 1, 0), 0))

    def nxt(width):
        return pl.BlockSpec((POST_HALO, width), lambda j: (jnp.minimum((j + 1) * per, n_halo - 1), 0))

    if out_batch_major:
        out_shape = jax.ShapeDtypeStruct((BATCH, n_tiles * POST_T, D_MODEL), F32)
        out_spec = pl.BlockSpec((BATCH, POST_T, D_MODEL), lambda j: (0, j, 0))
    else:
        out_shape = jax.ShapeDtypeStruct((n_tiles * POST_ROWS, D_MODEL), F32)
        out_spec = main(D_MODEL)
    return pl.pallas_call(
        functools.partial(_post_kernel, n_tiles=n_tiles, out_batch_major=out_batch_major),
        out_shape=out_shape,
        grid=(n_tiles,),
        in_specs=[
            main(D_MODEL), prev(D_MODEL), nxt(D_MODEL),
            main(MIX_W), prev(MIX_W), nxt(MIX_W),
            pl.BlockSpec((BATCH, POST_T, MIX_W), lambda j: (0, j, 0)),
            pl.BlockSpec((BATCH, BF16_ROWS, MIX_W), lambda j: (0, jnp.maximum(j * per_b - 1, 0), 0)),
            pl.BlockSpec((BATCH, BF16_ROWS, MIX_W), lambda j: (0, jnp.minimum((j + 1) * per_b, n_halo_b - 1), 0)),
            _full((2 * MIX_W, D_MODEL), single=True),
            _mod_spec(2, POST_NLAT), _mod_spec(3, POST_NLAT), _mod_spec(4, POST_NLAT), _mod_spec(5, POST_NLAT),
            _full((1, D_MODEL)),
            _full((D_MODEL, 2 * FFN_HIDDEN), single=True),
            _full((FFN_CONV, 2 * FFN_HIDDEN)),
            _full((1, 2 * FFN_HIDDEN)),
            _full((FFN_HIDDEN, D_MODEL), single=True),
        ],
        out_specs=out_spec,
        compiler_params=_cparams(("parallel",)),
        name="post_ffn",
    )(h, h, h, a_tb, a_tb, a_tb, b_out, b_out, b_out, w_out, modt, modt, modt, modt, g, wu, cw, cb, wd)


CAST_STEPS = 4


def _cast_kernel(w_ref, o_ref):
    o_ref[...] = w_ref[0].astype(BF16)


def _cast_call(w, layer):
    _, r, c = w.shape
    tr = r // CAST_STEPS
    return pl.pallas_call(
        _cast_kernel,
        out_shape=jax.ShapeDtypeStruct((r, c), BF16),
        grid=(CAST_STEPS,),
        in_specs=[pl.BlockSpec((1, tr, c), lambda i: (layer, i, 0))],
        out_specs=pl.BlockSpec((tr, c), lambda i: (i, 0)),
        compiler_params=_cparams(("parallel",)),
        name="cast_bf16",
    )(w)


def _rope_table():
    rows = SEQ // GRID_W
    row = jnp.repeat(jnp.arange(rows, dtype=F32), GRID_W)
    col = jnp.tile(jnp.arange(GRID_W, dtype=F32), rows)
    quarter = HEAD_DIM // 4
    inv_freq = ROPE_BASE ** (-jnp.arange(quarter, dtype=F32) / quarter)
    ang = jnp.stack([row[:, None] * inv_freq, col[:, None] * inv_freq], axis=1)
    cos = jnp.cos(ang)
    sin = jnp.sin(ang)
    zero = jnp.zeros_like(sin)

    def lanes(first, second):
        t = jnp.stack([first, second], axis=2).reshape(SEQ, HEAD_DIM)
        return jnp.tile(t, (1, LANE // HEAD_DIM))

    tab = jnp.concatenate([lanes(cos, cos), lanes(-sin, zero), lanes(zero, sin)], axis=1)
    ctx = jnp.concatenate([jnp.ones((CTX_LEN, LANE), F32), jnp.zeros((CTX_LEN, 2 * LANE), F32)], axis=1)
    return jnp.concatenate([tab, ctx], axis=0)


def _perm_heads_cols(w):
    d = w.shape[0]
    return w.reshape(d, N_HEADS, HEAD_DIM)[:, jnp.array(HEAD_PERM)].reshape(d, N_HEADS * HEAD_DIM)


def _perm_heads_rows(w):
    n = w.shape[1]
    return w.reshape(N_HEADS, HEAD_DIM, n)[jnp.array(HEAD_PERM)].reshape(N_HEADS * HEAD_DIM, n)


def _block_diag(w, per):
    n, a, b = w.shape
    eye = jnp.eye(per, dtype=w.dtype)
    w4 = w.reshape(n // per, per, a, b)
    return jnp.einsum("ihab,hk->ihakb", w4, eye).reshape(n // per, per * a, per * b)


def _head_gain(g):
    return jnp.tile(g, LANE // HEAD_DIM).reshape(1, LANE)


def _w_in(w, q_off):
    return jnp.concatenate([w[:, :q_off], _perm_heads_cols(w[:, q_off:q_off + MIX_W]), w[:, q_off + MIX_W:]],
                           axis=1).astype(BF16)


def _w_out(w):
    return jnp.concatenate([w[:MIX_W], _perm_heads_rows(w[MIX_W:])], axis=0).astype(BF16)


def kernel(x, c, ctx, c_ctx, mod_w, mod_b, norm_g, ffn_up, ffn_conv_w, ffn_conv_b, ffn_down, ev_w_in, ev_w_out, rg_conv_w, rg_conv_b, rg_wa, rg_ba, rg_wx, rg_bx, rg_lam, ga_qn, ga_kn, od_w_in, od_w_out, s5_lam_re, s5_lam_im, s5_log_step, s5_b_re, s5_b_im, s5_c_re, s5_c_im, s5_d, s5_glu_w, s5_glu_b, wa_qn, wa_kn, wa_sink):
    cvec = jnp.concatenate([c, jnp.broadcast_to(c_ctx[None], (BATCH, D_MODEL))], axis=0)
    mod = _mod_call(cvec, mod_w, mod_b).reshape(DEPTH, 2, BATCH, N_MOD * D_MODEL)
    rope = _rope_table()
    gm = _block_diag(jnp.full((LANE // HEAD_DIM, HEAD_DIM, HEAD_DIM), 1.0 / HEAD_DIM, F32), LANE // HEAD_DIM)[0].astype(BF16)

    def ffn_args(layer):
        return (norm_g[layer, 1].reshape(1, D_MODEL), _cast_call(ffn_up, layer), ffn_conv_w[layer],
                ffn_conv_b[layer].reshape(1, 2 * FFN_HIDDEN), _cast_call(ffn_down, layer))

    h, u_tb, gate_tb, q, k, v = _in_call(
        (x, ctx), mod[0], norm_g[0, 0].reshape(1, D_MODEL), _w_in(ev_w_in[0], 2 * MIX_W), rope,
        _head_gain(ga_qn[0]), _head_gain(ga_kn[0]), gm, n_tb=2, first_layer=True)
    rg = []
    for d in range(2):
        rg.append([rg_conv_w[0], rg_conv_b[0].reshape(1, MIX_W),
                   _block_diag(rg_wa[0, d], 4).astype(BF16), rg_ba[0, d].reshape(1, MIX_W),
                   _block_diag(rg_wx[0, d], 4).astype(BF16), rg_bx[0, d].reshape(1, MIX_W),
                   rg_lam[0, d].reshape(1, MIX_W)])
    hb = _rg_call(u_tb, None, None, *rg[1], reverse=True)
    a_out = _rg_call(u_tb, gate_tb, hb, *rg[0], reverse=False)
    b_out = _gattn_call(q, k, v)
    h = _post_call(h, a_out, b_out, _w_out(ev_w_out[0]), mod[0], *ffn_args(0),
                   n_tiles=L_TOT // POST_T, out_batch_major=False)

    u_tb, q, k, v = _in_call(
        (h,), mod[1], norm_g[1, 0].reshape(1, D_MODEL), _w_in(od_w_in[0], MIX_W), rope,
        _head_gain(wa_qn[0]), _head_gain(wa_kn[0]), gm, n_tb=1, first_layer=False)
    n_state = S5_GROUPS * S5_STATE
    bt_re = s5_b_re[0].transpose(0, 3, 1, 2).reshape(2, S5_GROUP, n_state)
    bt_im = s5_b_im[0].transpose(0, 3, 1, 2).reshape(2, S5_GROUP, n_state)
    log_step = jnp.repeat(s5_log_step[0], S5_STATE, axis=-1).reshape(2, 1, n_state)
    abr, abi, bbr, bbi = _s5_disc_call(s5_lam_re[0].reshape(2, 1, n_state), s5_lam_im[0].reshape(2, 1, n_state),
                                       log_step, bt_re, bt_im)
    gpb = S5_GROUPS // S5_NBLK
    eye = jnp.eye(gpb, dtype=F32)

    def drive_w(bb):
        return jnp.einsum("hbgp,gk->bghkp", bb.reshape(S5_GROUP, S5_NBLK, gpb, S5_STATE), eye).reshape(
            S5_NBLK, gpb * S5_GROUP, S5_BLK_STATE)

    def read_w(cc):
        return jnp.einsum("bghp,gk->bgpkh", cc.reshape(S5_NBLK, gpb, S5_GROUP, S5_STATE), eye).reshape(
            S5_NBLK, S5_BLK_STATE, gpb * S5_GROUP)

    def blocks(re, im):
        row = jnp.concatenate([re.reshape(1, S5_NBLK, S5_BLK_STATE), im.reshape(1, S5_NBLK, S5_BLK_STATE)],
                              axis=2).reshape(1, S5_SW)
        return jnp.broadcast_to(row, (BATCH, S5_SW))

    s5 = []
    for d in range(2):
        wd = jnp.concatenate([drive_w(bbr[d]), drive_w(bbi[d])], axis=2).astype(BF16)
        wr = jnp.concatenate([read_w(s5_c_re[0, d]), -read_w(s5_c_im[0, d])], axis=1).astype(BF16)
        s5.append([wd, blocks(abr[d], abr[d]), blocks(-abi[d], abi[d]), wr])
    yb = _s5_call(u_tb, None, *s5[1], None, None, None, reverse=True)
    c_out = _s5_call(u_tb, yb, *s5[0], s5_d[0].reshape(1, MIX_W), s5_glu_w[0].astype(BF16),
                     s5_glu_b[0].reshape(1, MIX_W), reverse=False)
    sink = wa_sink[0][jnp.array(HEAD_PERM)].reshape(N_HEADS, 1, 1)
    d_out = _wattn_call(q, k, v, sink)
    return _post_call(h, c_out, d_out, _w_out(od_w_out[0]), mod[1], *ffn_args(1),
                      n_tiles=SEQ // POST_T, out_batch_major=True)
```

```python
import functools
import math

import jax
import jax.numpy as jnp
from jax import lax
from jax.experimental import pallas as pl
from jax.experimental.pallas import tpu as pltpu

F32 = jnp.float32
BF16 = jnp.bfloat16

D_MODEL = 1024
BATCH = 8
SEQ = 2048
CTX_LEN = 256
L_TOT = SEQ + CTX_LEN
DEPTH = 2
GRID_W = 64
HEAD_DIM = 64
ROPE_BASE = 10000.0
NORM_EPS = 1e-6
WINDOW = 128
N_MOD = 6
MIX_W = 512
N_HEADS = 8
N_KV = 2
KV_W = N_KV * HEAD_DIM
V_W = 2 * KV_W
QKV_W = MIX_W + KV_W + KV_W
Q_SCALE = HEAD_DIM ** -0.5 * math.log2(math.e)
RG_CONV = 4
RG_C = 8.0
S5_GROUP = 16
S5_GROUPS = 32
S5_STATE = 64
S5_NBLK = 4
S5_BLK_STATE = (S5_GROUPS // S5_NBLK) * S5_STATE
FFN_HIDDEN = 2816
FFN_CONV = 3

LANE = 128
BF16_ROWS = 16
VMEM_LIMIT = 56 * 1024 * 1024

IN_T = 128
RG_T = 128
S5_T = 64
POST_T = 128
TQ_G = 256
FFN_CH = 256
HEAD_PERM = (0, 4, 1, 5, 2, 6, 3, 7)


def _cparams(sem):
    return pltpu.CompilerParams(dimension_semantics=sem, vmem_limit_bytes=VMEM_LIMIT)


def _dot(a, b):
    return jnp.dot(a, b, preferred_element_type=F32)


def _gelu_tanh(x):
    return x * (0.5 * (1.0 + jnp.tanh(math.sqrt(2.0 / math.pi) * (x + 0.044715 * (x * x * x)))))


def _rms(x, g):
    ms = jnp.mean(x * x, axis=-1, keepdims=True)
    return x * lax.rsqrt(ms + NORM_EPS) * g


def _to_time_major(x):
    b, t, c = x.shape
    return jnp.swapaxes(x, 0, 1).reshape(t * b, c)


def _to_batch_major(x):
    r, c = x.shape
    return jnp.swapaxes(x.reshape(r // BATCH, BATCH, c), 0, 1)


def _per_sample(x, v, op):
    r, c = x.shape
    return op(x.reshape(r // BATCH, BATCH, c), v[None]).reshape(r, c)


def _full(shape, single=False):
    kw = dict(pipeline_mode=pl.Buffered(1)) if single else {}
    return pl.BlockSpec(shape, lambda *_: (0,) * len(shape), **kw)


MOD_TN = 1536


def _mod_kernel(c_ref, w_ref, b_ref, o_ref):
    c = c_ref[...]
    a = c * jax.nn.sigmoid(c)
    w = w_ref[0]
    ah = a.astype(BF16)
    al = (a - ah.astype(F32)).astype(BF16)
    wh = w.astype(BF16)
    wl = (w - wh.astype(F32)).astype(BF16)
    o_ref[0] = _dot(ah, wh) + _dot(ah, wl) + _dot(al, wh) + b_ref[0]


def _mod_call(cvec, mod_w, mod_b):
    n = N_MOD * D_MODEL
    return pl.pallas_call(
        _mod_kernel,
        out_shape=jax.ShapeDtypeStruct((DEPTH, 2 * BATCH, n), F32),
        grid=(DEPTH, n // MOD_TN),
        in_specs=[
            pl.BlockSpec((2 * BATCH, D_MODEL), lambda l, k: (0, 0)),
            pl.BlockSpec((1, D_MODEL, MOD_TN), lambda l, k: (l, 0, k)),
            pl.BlockSpec((1, 1, MOD_TN), lambda l, k: (l, 0, k)),
        ],
        out_specs=pl.BlockSpec((1, 2 * BATCH, MOD_TN), lambda l, k: (l, 0, k)),
        compiler_params=_cparams(("parallel", "parallel")),
        name="mod",
    )(cvec, mod_w, mod_b.reshape(DEPTH, 1, n))


def _mod_spec(col, n_latent_tiles):
    return pl.BlockSpec((1, BATCH, D_MODEL), lambda j: (jnp.where(j < n_latent_tiles, 0, 1), 0, col))


IN_ROWS = IN_T * BATCH
IN_NLAT = SEQ // IN_T
IN_N = L_TOT // IN_T
IN_SUB = 4
CAST_STEPS = 16
assert CAST_STEPS <= IN_N
IN_SUB_T = IN_T // IN_SUB
IN_SUB_ROWS = IN_SUB_T * BATCH


def _in_kernel(fu_ref, fd_ref, *refs, n_tb, first_layer):
    fu_out, fd_out = refs[-2:]
    refs = refs[:-2]

    @pl.when(pl.program_id(0) < CAST_STEPS)
    def _():
        fu_out[...] = fu_ref[0].astype(BF16)
        fd_out[...] = fd_ref[0].astype(BF16)

    if first_layer:
        x_ref, c_ref, sh_ref, sc_ref, g_ref, w_ref, rope_ref, gq_ref, gk_ref, gm_ref = refs[:10]
        h_out = refs[10]
        outs = refs[11:]
        is_latent = pl.program_id(0) < IN_NLAT
    else:
        h_ref, sh_ref, sc_ref, g_ref, w_ref, rope_ref, gq_ref, gk_ref, gm_ref = refs[:9]
        outs = refs[9:]
    q_ref, k_ref, v_ref = outs[n_tb:]
    off = n_tb * MIX_W
    gm = gm_ref[...]
    for s in range(IN_SUB):
        ts = slice(s * IN_SUB_T, (s + 1) * IN_SUB_T)
        rs = slice(s * IN_SUB_ROWS, (s + 1) * IN_SUB_ROWS)
        if first_layer:
            h = _to_time_major(jnp.where(is_latent, x_ref[:, ts, :], c_ref[:, ts, :]))
            h_out[rs, :] = h
        else:
            h = h_ref[rs, :]
        xn = _per_sample(_rms(h, g_ref[...]), 1.0 + sc_ref[0], jnp.multiply)
        xn = _per_sample(xn, sh_ref[0], jnp.add)
        proj = _dot(xn.astype(BF16), w_ref[...])
        for i in range(n_tb):
            outs[i][rs, :] = proj[:, i * MIX_W:(i + 1) * MIX_W]
        qkv = _to_batch_major(proj[:, off:off + QKV_W])
        cos = rope_ref[ts, 0:LANE][None]
        sin_up = rope_ref[ts, LANE:2 * LANE][None]
        sin_dn = rope_ref[ts, 2 * LANE:3 * LANE][None]
        for p in range(5):
            blk = qkv[:, :, p * LANE:(p + 1) * LANE].reshape(IN_SUB_ROWS, LANE)
            sq = blk * blk
            hi = sq.astype(BF16)
            lo = (sq - hi.astype(F32)).astype(BF16)
            ms = _dot(hi, gm) + _dot(lo, gm)
            g = gq_ref[...] if p < 4 else gk_ref[...]
            bn = blk * lax.rsqrt(ms + NORM_EPS) * g
            up = pltpu.roll(bn, LANE - HEAD_DIM // 4, 1).reshape(BATCH, IN_SUB_T, LANE)
            dn = pltpu.roll(bn, HEAD_DIM // 4, 1).reshape(BATCH, IN_SUB_T, LANE)
            ro = bn.reshape(BATCH, IN_SUB_T, LANE) * cos + up * sin_up + dn * sin_dn
            if p < 4:
                q_ref[:, ts, p * LANE:(p + 1) * LANE] = (ro * Q_SCALE).astype(BF16)
            else:
                k_ref[:, ts, :] = ro.astype(BF16)
        v = qkv[:, :, 5 * LANE:6 * LANE].astype(BF16)
        v_ref[:, ts, :] = jnp.concatenate([v, jnp.ones((BATCH, IN_SUB_T, LANE), BF16)], axis=2)


def _in_call(h_args, modt, g, w, rope, gq, gk, gm, ffn_up, ffn_down, layer, n_tb, first_layer):
    n = w.shape[1]
    tb = lambda width: pl.BlockSpec((IN_ROWS, width), lambda j: (j, 0))
    bm = lambda width: pl.BlockSpec((BATCH, IN_T, width), lambda j: (0, j, 0))
    up_rows = D_MODEL // CAST_STEPS
    down_rows = FFN_HIDDEN // CAST_STEPS
    slab = lambda j: jnp.minimum(j, CAST_STEPS - 1)
    if first_layer:
        h_specs = [
            pl.BlockSpec((BATCH, IN_T, D_MODEL), lambda j: (0, jnp.minimum(j, IN_NLAT - 1), 0)),
            pl.BlockSpec((BATCH, IN_T, D_MODEL), lambda j: (0, jnp.maximum(j - IN_NLAT, 0), 0)),
        ]
        extra_shape = [jax.ShapeDtypeStruct((L_TOT * BATCH, D_MODEL), F32)]
        extra_spec = [tb(D_MODEL)]
    else:
        h_specs = [tb(D_MODEL)]
        extra_shape, extra_spec = [], []
    return pl.pallas_call(
        functools.partial(_in_kernel, n_tb=n_tb, first_layer=first_layer),
        out_shape=extra_shape + [jax.ShapeDtypeStruct((L_TOT * BATCH, MIX_W), F32)] * n_tb + [
            jax.ShapeDtypeStruct((BATCH, L_TOT, MIX_W), BF16),
            jax.ShapeDtypeStruct((BATCH, L_TOT, KV_W), BF16),
            jax.ShapeDtypeStruct((BATCH, L_TOT, V_W), BF16),
            jax.ShapeDtypeStruct((D_MODEL, 2 * FFN_HIDDEN), BF16),
            jax.ShapeDtypeStruct((FFN_HIDDEN, D_MODEL), BF16),
        ],
        grid=(IN_N,),
        in_specs=[
            pl.BlockSpec((1, up_rows, 2 * FFN_HIDDEN), lambda j: (layer, slab(j), 0)),
            pl.BlockSpec((1, down_rows, D_MODEL), lambda j: (layer, slab(j), 0)),
        ] + h_specs + [
            _mod_spec(0, IN_NLAT), _mod_spec(1, IN_NLAT),
            _full((1, D_MODEL)),
            _full((D_MODEL, n), single=True),
            pl.BlockSpec((IN_T, 3 * LANE), lambda j: (j, 0)),
            _full((1, LANE)), _full((1, LANE)), _full((LANE, LANE)),
        ],
        out_specs=extra_spec + [tb(MIX_W)] * n_tb + [bm(MIX_W), bm(KV_W), bm(V_W)] + [
            pl.BlockSpec((up_rows, 2 * FFN_HIDDEN), lambda j: (slab(j), 0)),
            pl.BlockSpec((down_rows, D_MODEL), lambda j: (slab(j), 0)),
        ],
        compiler_params=_cparams(("arbitrary",)),
        name="in_proj",
    )(ffn_up, ffn_down, *h_args, modt, modt, g, w, rope, gq, gk, gm)


def _chunk_of(i, reverse, n_latent, n_chunks):
    if reverse:
        return n_chunks - 1 - i
    return jnp.where(i < n_chunks - n_latent, i + n_latent, i - (n_chunks - n_latent))


RG_ROWS = RG_T * BATCH
RG_NC = L_TOT // RG_T
RG_NLAT = SEQ // RG_T


def _softplus(z):
    return jnp.maximum(z, 0.0) + jnp.log1p(jnp.exp(-jnp.abs(z)))


def _rg_kernel(*refs, reverse):
    if reverse:
        (u_ref, up_ref, un_ref, cw_ref, cb_ref, wa_ref, ba_ref, wx_ref, bx_ref, lam_ref,
         out_ref, xbuf, abuf, bbuf, hst) = refs
    else:
        (u_ref, up_ref, un_ref, gate_ref, hb_ref, cw_ref, cb_ref, wa_ref, ba_ref, wx_ref, bx_ref, lam_ref,
         out_ref, xbuf, abuf, bbuf, hst) = refs
    i = pl.program_id(0)
    c = _chunk_of(i, reverse, RG_NLAT, RG_NC)
    prev_zero = jnp.logical_or(c == 0, c == RG_NLAT)
    next_zero = jnp.logical_or(c == RG_NLAT - 1, c == RG_NC - 1)

    @pl.when(i == 0)
    def _():
        hst[...] = jnp.zeros_like(hst)

    xbuf[0:BATCH, :] = up_ref[...] * jnp.where(prev_zero, 0.0, 1.0)
    xbuf[BATCH:BATCH + RG_ROWS, :] = u_ref[...]
    xbuf[BATCH + RG_ROWS:3 * BATCH + RG_ROWS, :] = un_ref[...] * jnp.where(next_zero, 0.0, 1.0)
    uc = cb_ref[...]
    for k in range(RG_CONV):
        uc = uc + xbuf[k * BATCH:k * BATCH + RG_ROWS, :] * cw_ref[k:k + 1, :]

    ub = uc.astype(BF16)
    half = MIX_W // 2

    def gate(w_ref, b_ref):
        z = jnp.concatenate([_dot(ub[:, :half], w_ref[0]), _dot(ub[:, half:], w_ref[1])], axis=1)
        return jax.nn.sigmoid(z + b_ref[...])

    r = gate(wa_ref, ba_ref)
    ig = gate(wx_ref, bx_ref)
    log_a = (-RG_C) * r * _softplus(-lam_ref[...])
    a = jnp.exp(log_a)
    abuf[...] = a
    bbuf[...] = jnp.sqrt(-jnp.tanh(log_a) * (a * a + 1.0)) * (ig * uc)

    h = hst[...]
    for s in range(RG_T):
        t = (RG_T - 1 - s) if reverse else s
        rows = slice(t * BATCH, (t + 1) * BATCH)
        h = abuf[rows, :] * h + bbuf[rows, :]
        bbuf[rows, :] = h
    hst[...] = h

    if reverse:
        out_ref[...] = bbuf[...]
    else:
        y = bbuf[...] + hb_ref[...]
        out_ref[...] = (y * _gelu_tanh(gate_ref[...])).astype(BF16)


def _rg_call(u, gate, hb, cw, cb, wa, ba, wx, bx, lam, reverse):
    def cidx(i):
        return _chunk_of(i, reverse, RG_NLAT, RG_NC)

    main = pl.BlockSpec((RG_ROWS, MIX_W), lambda i: (cidx(i), 0))
    prev = pl.BlockSpec((BATCH, MIX_W), lambda i: (jnp.maximum(cidx(i) * RG_T - 1, 0), 0))
    n_next = L_TOT // 2
    nxt = pl.BlockSpec((2 * BATCH, MIX_W), lambda i: (jnp.minimum((cidx(i) + 1) * (RG_T // 2), n_next - 1), 0))
    params = [cw, cb, wa, ba, wx, bx, lam]
    pspecs = [_full(p.shape) for p in params]
    if reverse:
        args = [u, u, u] + params
        specs = [main, prev, nxt] + pspecs
        out_dtype = F32
    else:
        args = [u, u, u, gate, hb] + params
        specs = [main, prev, nxt, main, main] + pspecs
        out_dtype = BF16
    return pl.pallas_call(
        functools.partial(_rg_kernel, reverse=reverse),
        out_shape=jax.ShapeDtypeStruct((L_TOT * BATCH, MIX_W), out_dtype),
        grid=(RG_NC,),
        in_specs=specs,
        out_specs=main,
        scratch_shapes=[
            pltpu.VMEM((RG_ROWS + 3 * BATCH, MIX_W), F32),
            pltpu.VMEM((RG_ROWS, MIX_W), F32),
            pltpu.VMEM((RG_ROWS, MIX_W), F32),
            pltpu.VMEM((BATCH, MIX_W), F32),
        ],
        compiler_params=_cparams(("arbitrary",)),
        name="rglru_bwd" if reverse else "rglru_fwd",
    )(*args)


def _head_rows(q_ref, tq):
    lane = lax.broadcasted_iota(jnp.int32, (tq, LANE), 1)
    low = lane < HEAD_DIM
    zero = jnp.zeros((tq, LANE), BF16)
    parts = []
    for p in range(MIX_W // LANE):
        qb = q_ref[0, :, p * LANE:(p + 1) * LANE]
        parts.append(jnp.where(low, qb, zero))
        parts.append(jnp.where(low, zero, qb))
    return parts, low


def _merge_heads(outs, low):
    return jnp.concatenate([jnp.where(low, outs[2 * p], outs[2 * p + 1]) for p in range(MIX_W // LANE)], axis=1)


def _qk(q, k):
    return lax.dot_general(q, k, (((1,), (1,)), ((), ())), preferred_element_type=F32)


def _gattn_kernel(q_ref, k_ref, v_ref, o_ref):
    j = pl.program_id(1)
    heads, low = _head_rows(q_ref, TQ_G)

    def attend(k0, nk):
        outs = []
        for q in heads:
            s = _qk(q, k_ref[0, k0:k0 + nk, :])
            m = jnp.max(s, axis=-1, keepdims=True)
            p = jnp.exp2(s - m).astype(BF16)
            acc = _dot(p, v_ref[0, k0:k0 + nk, :])
            outs.append(acc[:, :KV_W] / acc[:, KV_W:])
        o_ref[0] = _merge_heads(outs, low).astype(BF16)

    @pl.when(j < SEQ // TQ_G)
    def _():
        attend(0, L_TOT)

    @pl.when(j >= SEQ // TQ_G)
    def _():
        attend(SEQ, CTX_LEN)


def _gattn_call(q, k, v):
    return pl.pallas_call(
        _gattn_kernel,
        out_shape=jax.ShapeDtypeStruct((BATCH, L_TOT, MIX_W), BF16),
        grid=(BATCH, L_TOT // TQ_G),
        in_specs=[
            pl.BlockSpec((1, TQ_G, MIX_W), lambda b, j: (b, j, 0)),
            pl.BlockSpec((1, L_TOT, KV_W), lambda b, j: (b, 0, 0)),
            pl.BlockSpec((1, L_TOT, V_W), lambda b, j: (b, 0, 0)),
        ],
        out_specs=pl.BlockSpec((1, TQ_G, MIX_W), lambda b, j: (b, j, 0)),
        compiler_params=_cparams(("parallel", "arbitrary")),
        name="global_attn",
    )(q, k, v)


TQ_W = 2 * WINDOW
N_WBLK = SEQ // WINDOW
N_WSTEP = SEQ // TQ_W


def _wattn_kernel(q_ref, k_ref, v_ref, sink_ref, o_ref):
    i = pl.program_id(1)
    heads, low = _head_rows(q_ref, TQ_W)
    prev_start = pl.multiple_of(jnp.maximum(2 * i - 1, 0) * WINDOW, WINDOW)
    mid_start = pl.multiple_of(i * TQ_W, TQ_W)
    next_start = pl.multiple_of(jnp.minimum(2 * i + 2, N_WBLK - 1) * WINDOW, WINDOW)

    def rows(ref):
        return jnp.concatenate([ref[0, SEQ:L_TOT, :], ref[0, pl.ds(prev_start, WINDOW), :],
                                ref[0, pl.ds(mid_start, TQ_W), :], ref[0, pl.ds(next_start, WINDOW), :]], axis=0)

    nk = CTX_LEN + 2 * WINDOW + TQ_W
    r = lax.broadcasted_iota(jnp.int32, (TQ_W, nk), 0)
    col = lax.broadcasted_iota(jnp.int32, (TQ_W, nk), 1)
    c = col - CTX_LEN
    ninf = -jnp.inf
    pen_prev = jnp.where(i > 0, 0.0, ninf)
    pen_next = jnp.where(i < N_WSTEP - 1, 0.0, ninf)
    edge = jnp.where(c < WINDOW, pen_prev, jnp.where(c >= WINDOW + TQ_W, pen_next, 0.0))
    band = jnp.where(c >= r, jnp.where(c <= r + 2 * WINDOW, edge, ninf), ninf)
    bias = jnp.where(col < CTX_LEN, 0.0, band)
    kk = rows(k_ref)
    vv = rows(v_ref)
    outs = []
    for hd, q in enumerate(heads):
        s = _qk(q, kk) + bias
        sink = sink_ref[hd] * math.log2(math.e)
        m = jnp.maximum(jnp.max(s, axis=-1, keepdims=True), sink)
        p = jnp.exp2(s - m).astype(BF16)
        acc = _dot(p, vv)
        outs.append(acc[:, :KV_W] / (acc[:, KV_W:] + jnp.exp2(sink - m)))
    o_ref[0] = _merge_heads(outs, low).astype(BF16)


def _wattn_call(q, k, v, sink):
    return pl.pallas_call(
        _wattn_kernel,
        out_shape=jax.ShapeDtypeStruct((BATCH, SEQ, MIX_W), BF16),
        grid=(BATCH, N_WSTEP),
        in_specs=[
            pl.BlockSpec((1, TQ_W, MIX_W), lambda b, i: (b, i, 0)),
            pl.BlockSpec((1, L_TOT, KV_W), lambda b, i: (b, 0, 0)),
            pl.BlockSpec((1, L_TOT, V_W), lambda b, i: (b, 0, 0)),
            pl.BlockSpec((N_HEADS, 1, 1), lambda b, i: (0, 0, 0)),
        ],
        out_specs=pl.BlockSpec((1, TQ_W, MIX_W), lambda b, i: (b, i, 0)),
        compiler_params=_cparams(("parallel", "arbitrary")),
        name="window_attn",
    )(q, k, v, sink)


def _s5_disc_kernel(lr_ref, li_ref, ls_ref, br_ref, bi_ref, ar_ref, ai_ref, bbr_ref, bbi_ref):
    lr = lr_ref[0]
    li = li_ref[0]
    dt = jnp.exp(ls_ref[0])
    mag = jnp.exp(lr * dt)
    ang = li * dt
    abr = mag * jnp.cos(ang)
    abi = mag * jnp.sin(ang)
    den = lr * lr + li * li
    nr = abr - 1.0
    kr = (nr * lr + abi * li) / den
    ki = (abi * lr - nr * li) / den
    br = br_ref[0]
    bi = bi_ref[0]
    ar_ref[0] = abr
    ai_ref[0] = abi
    bbr_ref[0] = kr * br - ki * bi
    bbi_ref[0] = kr * bi + ki * br


def _s5_disc_call(lam_re, lam_im, log_step, bt_re, bt_im):
    n = S5_GROUPS * S5_STATE
    row = pl.BlockSpec((1, 1, n), lambda d: (d, 0, 0))
    mat = pl.BlockSpec((1, S5_GROUP, n), lambda d: (d, 0, 0))
    return pl.pallas_call(
        _s5_disc_kernel,
        out_shape=[jax.ShapeDtypeStruct((2, 1, n), F32)] * 2 + [jax.ShapeDtypeStruct((2, S5_GROUP, n), F32)] * 2,
        grid=(2,),
        in_specs=[row, row, row, mat, mat],
        out_specs=[row, row, mat, mat],
        compiler_params=_cparams(("parallel",)),
        name="s5_discretize",
    )(lam_re, lam_im, log_step, bt_re, bt_im)


S5_ROWS = S5_T * BATCH
S5_NC = L_TOT // S5_T
S5_NLAT = SEQ // S5_T
S5_SW = 2 * S5_BLK_STATE * S5_NBLK


def _s5_kernel(*refs, reverse):
    if reverse:
        u2_ref, un_ref, wd_ref, ar_ref, ai_ref, wr_ref, out_ref, buf0, buf1, xst = refs
    else:
        (u2_ref, un_ref, yb2_ref, wd_ref, ar_ref, ai_ref, wr_ref, dsk_ref, gw_ref, gb_ref,
         out_ref, buf0, buf1, xst) = refs
    nb = 2 * S5_BLK_STATE
    lo, hi = slice(0, S5_ROWS), slice(S5_ROWS, 2 * S5_ROWS)
    first, second = (hi, lo) if reverse else (lo, hi)

    def drive(u, buf):
        ub = u.astype(BF16)
        for k in range(S5_NBLK):
            buf[:, k * nb:(k + 1) * nb] = _dot(ub[:, k * LANE:(k + 1) * LANE], wd_ref[k])

    def scan(buf, x):
        for s in range(S5_T):
            t = (S5_T - 1 - s) if reverse else s
            rows = slice(t * BATCH, (t + 1) * BATCH)
            halves = []
            for k in range(S5_NBLK):
                halves.append(x[:, k * nb + S5_BLK_STATE:(k + 1) * nb])
                halves.append(x[:, k * nb:k * nb + S5_BLK_STATE])
            xsw = jnp.concatenate(halves, axis=1)
            x = ar_ref[...] * x + ai_ref[...] * xsw + buf[rows, :]
            buf[rows, :] = x
        return x

    def readout(buf, rows):
        y = jnp.concatenate(
            [_dot(buf[:, k * nb:(k + 1) * nb].astype(BF16), wr_ref[k]) for k in range(S5_NBLK)], axis=1)
        if reverse:
            out_ref[rows, :] = y
        else:
            y = u2_ref[rows, :] * dsk_ref[...] + y + yb2_ref[rows, :]
            z = _gelu_tanh(y)
            out_ref[rows, :] = (z * jax.nn.sigmoid(_dot(z.astype(BF16), gw_ref[...]) + gb_ref[...])).astype(BF16)

    @pl.when(pl.program_id(0) == 0)
    def _():
        xst[...] = jnp.zeros_like(xst)
        drive(u2_ref[first, :], buf0)

    x = xst[...]
    drive(u2_ref[second, :], buf1)
    x = scan(buf0, x)
    readout(buf0, first)
    drive(un_ref[...], buf0)
    x = scan(buf1, x)
    readout(buf1, second)
    xst[...] = x


def _s5_call(u, yb, wd, ar, ai, wr, dsk, gw, gb, reverse):
    def chunk(p):
        return _chunk_of(p, reverse, S5_NLAT, S5_NC)

    pair = pl.BlockSpec((2 * S5_ROWS, MIX_W), lambda k: (chunk(2 * k) // 2, 0))
    nxt = pl.BlockSpec((S5_ROWS, MIX_W), lambda k: (chunk(jnp.minimum(2 * k + 2, S5_NC - 1)), 0))
    if reverse:
        args = [u, u, wd, ar, ai, wr]
        specs = [pair, nxt] + [_full(a.shape) for a in args[2:]]
        out_dtype = F32
    else:
        args = [u, u, yb, wd, ar, ai, wr, dsk, gw, gb]
        specs = [pair, nxt, pair] + [_full(a.shape) for a in args[3:]]
        out_dtype = BF16
    return pl.pallas_call(
        functools.partial(_s5_kernel, reverse=reverse),
        out_shape=jax.ShapeDtypeStruct((L_TOT * BATCH, MIX_W), out_dtype),
        grid=(S5_NC // 2,),
        in_specs=specs,
        out_specs=pair,
        scratch_shapes=[
            pltpu.VMEM((S5_ROWS, S5_SW), F32),
            pltpu.VMEM((S5_ROWS, S5_SW), F32),
            pltpu.VMEM((BATCH, S5_SW), F32),
        ],
        compiler_params=_cparams(("arbitrary",)),
        name="s5_bwd" if reverse else "s5_fwd",
    )(*args)


POST_ROWS = POST_T * BATCH
POST_HALO_T = BF16_ROWS // BATCH
POST_HALO = POST_HALO_T * BATCH
POST_NLAT = SEQ // POST_T
FFN_NCH = FFN_HIDDEN // FFN_CH


def _post_kernel(h_ref, hp_ref, hn_ref, a_ref, ap_ref, an_ref, b_ref, bp_ref, bn_ref, wo_ref,
                 g2_ref, sh_ref, sc_ref, g5_ref, g_ref, wu_ref, cw_ref, cb_ref, wd_ref, o_ref, *, n_tiles, out_batch_major):
    j = pl.program_id(0)
    first = jnp.logical_or(j == 0, j == POST_NLAT)
    last = jnp.logical_or(j == POST_NLAT - 1, j == n_tiles - 1)
    gate2 = g2_ref[0]
    gain = g_ref[...]
    scale = 1.0 + sc_ref[0]
    shift = sh_ref[0]

    def mix(h, a, b_tm):
        m = _dot(a, wo_ref[0:MIX_W, :]) + _dot(b_tm.astype(BF16), wo_ref[MIX_W:2 * MIX_W, :])
        return h + _per_sample(m, gate2, jnp.multiply)

    def norm(x):
        return _per_sample(_per_sample(_rms(x, gain), scale, jnp.multiply), shift, jnp.add)

    def halo_b(ref, t0):
        x = jnp.swapaxes(ref[...].astype(F32), 0, 1)[t0:t0 + POST_HALO_T]
        return x.reshape(POST_HALO, MIX_W)

    h_mid = mix(h_ref[...], a_ref[...], _to_time_major(b_ref[...].astype(F32)))
    h_prev = mix(hp_ref[...], ap_ref[...], halo_b(bp_ref, BF16_ROWS - POST_HALO_T))
    h_next = mix(hn_ref[...], an_ref[...], halo_b(bn_ref, 0))
    xn = jnp.concatenate([
        (norm(h_prev) * jnp.where(first, 0.0, 1.0)).astype(BF16),
        norm(h_mid).astype(BF16),
        (norm(h_next) * jnp.where(last, 0.0, 1.0)).astype(BF16)], axis=0)

    acts = []
    for c in range(FFN_NCH):
        cv = slice(c * FFN_CH, (c + 1) * FFN_CH)
        cg = slice(FFN_HIDDEN + c * FFN_CH, FFN_HIDDEN + (c + 1) * FFN_CH)

        def conv(cols):
            hid = _dot(xn, wu_ref[:, cols])
            out = cb_ref[:, cols]
            for k in range(FFN_CONV):
                r0 = POST_HALO + (k - 1) * BATCH
                out = out + hid[r0:r0 + POST_ROWS] * cw_ref[k:k + 1, cols]
            return out

        val = conv(cv)
        gate = conv(cg)
        acts.append((val * (gate * jax.nn.sigmoid(gate))).astype(BF16))
    ffn = _dot(jnp.concatenate(acts, axis=1), wd_ref[...])
    out = h_mid + _per_sample(ffn, g5_ref[0], jnp.multiply)
    if out_batch_major:
        o_ref[...] = _to_batch_major(out)
    else:
        o_ref[...] = out


def _post_call(h, a_tb, b_out, w_out, modt, g, wu, cw, cb, wd, n_tiles, out_batch_major):
    per = POST_ROWS // POST_HALO
    n_halo = h.shape[0] // POST_HALO
    per_b = POST_T // BF16_ROWS
    n_halo_b = b_out.shape[1] // BF16_ROWS

    def main(width):
        return pl.BlockSpec((POST_ROWS, width), lambda j: (j, 0))

    def prev(width):
        return pl.BlockSpec((POST_HALO, width), lambda j: (jnp.maximum(j * per - 1, 0), 0))

    def nxt(width):
        return pl.BlockSpec((POST_HALO, width), lambda j: (jnp.minimum((j + 1) * per, n_halo - 1), 0))

    if out_batch_major:
        out_shape = jax.ShapeDtypeStruct((BATCH, n_tiles * POST_T, D_MODEL), F32)
        out_spec = pl.BlockSpec((BATCH, POST_T, D_MODEL), lambda j: (0, j, 0))
    else:
        out_shape = jax.ShapeDtypeStruct((n_tiles * POST_ROWS, D_MODEL), F32)
        out_spec = main(D_MODEL)
    return pl.pallas_call(
        functools.partial(_post_kernel, n_tiles=n_tiles, out_batch_major=out_batch_major),
        out_shape=out_shape,
        grid=(n_tiles,),
        in_specs=[
            main(D_MODEL), prev(D_MODEL), nxt(D_MODEL),
            main(MIX_W), prev(MIX_W), nxt(MIX_W),
            pl.BlockSpec((BATCH, POST_T, MIX_W), lambda j: (0, j, 0)),
            pl.BlockSpec((BATCH, BF16_ROWS, MIX_W), lambda j: (0, jnp.maximum(j * per_b - 1, 0), 0)),
            pl.BlockSpec((BATCH, BF16_ROWS, MIX_W), lambda j: (0, jnp.minimum((j + 1) * per_b, n_halo_b - 1), 0)),
            _full((2 * MIX_W, D_MODEL), single=True),
            _mod_spec(2, POST_NLAT), _mod_spec(3, POST_NLAT), _mod_spec(4, POST_NLAT), _mod_spec(5, POST_NLAT),
            _full((1, D_MODEL)),
            _full((D_MODEL, 2 * FFN_HIDDEN), single=True),
            _full((FFN_CONV, 2 * FFN_HIDDEN)),
            _full((1, 2 * FFN_HIDDEN)),
            _full((FFN_HIDDEN, D_MODEL), single=True),
        ],
        out_specs=out_spec,
        compiler_params=_cparams(("parallel",)),
        name="post_ffn",
    )(h, h, h, a_tb, a_tb, a_tb, b_out, b_out, b_out, w_out, modt, modt, modt, modt, g, wu, cw, cb, wd)


def _rope_table():
    rows = SEQ // GRID_W
    row = jnp.repeat(jnp.arange(rows, dtype=F32), GRID_W)
    col = jnp.tile(jnp.arange(GRID_W, dtype=F32), rows)
    quarter = HEAD_DIM // 4
    inv_freq = ROPE_BASE ** (-jnp.arange(quarter, dtype=F32) / quarter)
    ang = jnp.stack([row[:, None] * inv_freq, col[:, None] * inv_freq], axis=1)
    cos = jnp.cos(ang)
    sin = jnp.sin(ang)
    zero = jnp.zeros_like(sin)

    def lanes(first, second):
        t = jnp.stack([first, second], axis=2).reshape(SEQ, HEAD_DIM)
        return jnp.tile(t, (1, LANE // HEAD_DIM))

    tab = jnp.concatenate([lanes(cos, cos), lanes(-sin, zero), lanes(zero, sin)], axis=1)
    ctx = jnp.concatenate([jnp.ones((CTX_LEN, LANE), F32), jnp.zeros((CTX_LEN, 2 * LANE), F32)], axis=1)
    return jnp.concatenate([tab, ctx], axis=0)


def _perm_heads_cols(w):
    d = w.shape[0]
    return w.reshape(d, N_HEADS, HEAD_DIM)[:, jnp.array(HEAD_PERM)].reshape(d, N_HEADS * HEAD_DIM)


def _perm_heads_rows(w):
    n = w.shape[1]
    return w.reshape(N_HEADS, HEAD_DIM, n)[jnp.array(HEAD_PERM)].reshape(N_HEADS * HEAD_DIM, n)


def _block_diag(w, per):
    n, a, b = w.shape
    eye = jnp.eye(per, dtype=w.dtype)
    w4 = w.reshape(n // per, per, a, b)
    return jnp.einsum("ihab,hk->ihakb", w4, eye).reshape(n // per, per * a, per * b)


def _head_gain(g):
    return jnp.tile(g, LANE // HEAD_DIM).reshape(1, LANE)


def _w_in(w, q_off):
    return jnp.concatenate([w[:, :q_off], _perm_heads_cols(w[:, q_off:q_off + MIX_W]), w[:, q_off + MIX_W:]],
                           axis=1).astype(BF16)


def _w_out(w):
    return jnp.concatenate([w[:MIX_W], _perm_heads_rows(w[MIX_W:])], axis=0).astype(BF16)


def kernel(x, c, ctx, c_ctx, mod_w, mod_b, norm_g, ffn_up, ffn_conv_w, ffn_conv_b, ffn_down, ev_w_in, ev_w_out, rg_conv_w, rg_conv_b, rg_wa, rg_ba, rg_wx, rg_bx, rg_lam, ga_qn, ga_kn, od_w_in, od_w_out, s5_lam_re, s5_lam_im, s5_log_step, s5_b_re, s5_b_im, s5_c_re, s5_c_im, s5_d, s5_glu_w, s5_glu_b, wa_qn, wa_kn, wa_sink):
    cvec = jnp.concatenate([c, jnp.broadcast_to(c_ctx[None], (BATCH, D_MODEL))], axis=0)
    mod = _mod_call(cvec, mod_w, mod_b).reshape(DEPTH, 2, BATCH, N_MOD * D_MODEL)
    rope = _rope_table()
    gm = _block_diag(jnp.full((LANE // HEAD_DIM, HEAD_DIM, HEAD_DIM), 1.0 / HEAD_DIM, F32), LANE // HEAD_DIM)[0].astype(BF16)

    def ffn_args(layer, wu, wd):
        return (norm_g[layer, 1].reshape(1, D_MODEL), wu, ffn_conv_w[layer],
                ffn_conv_b[layer].reshape(1, 2 * FFN_HIDDEN), wd)

    h, u_tb, gate_tb, q, k, v, wu, wd = _in_call(
        (x, ctx), mod[0], norm_g[0, 0].reshape(1, D_MODEL), _w_in(ev_w_in[0], 2 * MIX_W), rope,
        _head_gain(ga_qn[0]), _head_gain(ga_kn[0]), gm, ffn_up, ffn_down, 0, n_tb=2, first_layer=True)
    rg = []
    for d in range(2):
        rg.append([rg_conv_w[0], rg_conv_b[0].reshape(1, MIX_W),
                   _block_diag(rg_wa[0, d], 4).astype(BF16), rg_ba[0, d].reshape(1, MIX_W),
                   _block_diag(rg_wx[0, d], 4).astype(BF16), rg_bx[0, d].reshape(1, MIX_W),
                   rg_lam[0, d].reshape(1, MIX_W)])
    hb = _rg_call(u_tb, None, None, *rg[1], reverse=True)
    a_out = _rg_call(u_tb, gate_tb, hb, *rg[0], reverse=False)
    b_out = _gattn_call(q, k, v)
    h = _post_call(h, a_out, b_out, _w_out(ev_w_out[0]), mod[0], *ffn_args(0, wu, wd),
                   n_tiles=L_TOT // POST_T, out_batch_major=False)

    u_tb, q, k, v, wu, wd = _in_call(
        (h,), mod[1], norm_g[1, 0].reshape(1, D_MODEL), _w_in(od_w_in[0], MIX_W), rope,
        _head_gain(wa_qn[0]), _head_gain(wa_kn[0]), gm, ffn_up, ffn_down, 1, n_tb=1, first_layer=False)
    n_state = S5_GROUPS * S5_STATE
    bt_re = s5_b_re[0].transpose(0, 3, 1, 2).reshape(2, S5_GROUP, n_state)
    bt_im = s5_b_im[0].transpose(0, 3, 1, 2).reshape(2, S5_GROUP, n_state)
    log_step = jnp.repeat(s5_log_step[0], S5_STATE, axis=-1).reshape(2, 1, n_state)
    abr, abi, bbr, bbi = _s5_disc_call(s5_lam_re[0].reshape(2, 1, n_state), s5_lam_im[0].reshape(2, 1, n_state),
                                       log_step, bt_re, bt_im)
    gpb = S5_GROUPS // S5_NBLK
    eye = jnp.eye(gpb, dtype=F32)

    def drive_w(bb):
        return jnp.einsum("hbgp,gk->bghkp", bb.reshape(S5_GROUP, S5_NBLK, gpb, S5_STATE), eye).reshape(
            S5_NBLK, gpb * S5_GROUP, S5_BLK_STATE)

    def read_w(cc):
        return jnp.einsum("bghp,gk->bgpkh", cc.reshape(S5_NBLK, gpb, S5_GROUP, S5_STATE), eye).reshape(
            S5_NBLK, S5_BLK_STATE, gpb * S5_GROUP)

    def blocks(re, im):
        row = jnp.concatenate([re.reshape(1, S5_NBLK, S5_BLK_STATE), im.reshape(1, S5_NBLK, S5_BLK_STATE)],
                              axis=2).reshape(1, S5_SW)
        return jnp.broadcast_to(row, (BATCH, S5_SW))

    s5 = []
    for d in range(2):
        w_drive = jnp.concatenate([drive_w(bbr[d]), drive_w(bbi[d])], axis=2).astype(BF16)
        w_read = jnp.concatenate([read_w(s5_c_re[0, d]), -read_w(s5_c_im[0, d])], axis=1).astype(BF16)
        s5.append([w_drive, blocks(abr[d], abr[d]), blocks(-abi[d], abi[d]), w_read])
    yb = _s5_call(u_tb, None, *s5[1], None, None, None, reverse=True)
    c_out = _s5_call(u_tb, yb, *s5[0], s5_d[0].reshape(1, MIX_W), s5_glu_w[0].astype(BF16),
                     s5_glu_b[0].reshape(1, MIX_W), reverse=False)
    sink = wa_sink[0][jnp.array(HEAD_PERM)].reshape(N_HEADS, 1, 1)
    d_out = _wattn_call(q, k, v, sink)
    return _post_call(h, c_out, d_out, _w_out(od_w_out[0]), mod[1], *ffn_args(1, wu, wd),
                      n_tiles=SEQ // POST_T, out_batch_major=True)
```

```python
import functools
import math

import jax
import jax.numpy as jnp
from jax import lax
from jax.experimental import pallas as pl
from jax.experimental.pallas import tpu as pltpu

F32 = jnp.float32
BF16 = jnp.bfloat16

D_MODEL = 1024
BATCH = 8
SEQ = 2048
CTX_LEN = 256
L_TOT = SEQ + CTX_LEN
DEPTH = 2
GRID_W = 64
HEAD_DIM = 64
ROPE_BASE = 10000.0
NORM_EPS = 1e-6
WINDOW = 128
N_MOD = 6
MIX_W = 512
N_HEADS = 8
N_KV = 2
KV_W = N_KV * HEAD_DIM
V_W = 2 * KV_W
QKV_W = MIX_W + KV_W + KV_W
Q_SCALE = HEAD_DIM ** -0.5 * math.log2(math.e)
RG_CONV = 4
RG_C = 8.0
S5_GROUP = 16
S5_GROUPS = 32
S5_STATE = 64
S5_NBLK = 4
S5_BLK_STATE = (S5_GROUPS // S5_NBLK) * S5_STATE
FFN_HIDDEN = 2816
FFN_CONV = 3

LANE = 128
BF16_ROWS = 16
VMEM_LIMIT = 56 * 1024 * 1024

IN_T = 128
RG_T = 128
S5_T = 64
POST_T = 128
TQ_G = 256
FFN_CH = 256
HEAD_PERM = (0, 4, 1, 5, 2, 6, 3, 7)


def _cparams(sem):
    return pltpu.CompilerParams(dimension_semantics=sem, vmem_limit_bytes=VMEM_LIMIT)


def _dot(a, b):
    return jnp.dot(a, b, preferred_element_type=F32)


def _gelu_tanh(x):
    return x * (0.5 * (1.0 + jnp.tanh(math.sqrt(2.0 / math.pi) * (x + 0.044715 * (x * x * x)))))


def _rms(x, g):
    ms = jnp.mean(x * x, axis=-1, keepdims=True)
    return x * lax.rsqrt(ms + NORM_EPS) * g


def _to_time_major(x):
    b, t, c = x.shape
    return jnp.swapaxes(x, 0, 1).reshape(t * b, c)


def _to_batch_major(x):
    r, c = x.shape
    return jnp.swapaxes(x.reshape(r // BATCH, BATCH, c), 0, 1)


def _per_sample(x, v, op):
    r, c = x.shape
    return op(x.reshape(r // BATCH, BATCH, c), v[None]).reshape(r, c)


def _full(shape, single=False):
    kw = dict(pipeline_mode=pl.Buffered(1)) if single else {}
    return pl.BlockSpec(shape, lambda *_: (0,) * len(shape), **kw)


MOD_TN = 1536


def _mod_kernel(c_ref, w_ref, b_ref, o_ref):
    c = c_ref[...]
    a = c * jax.nn.sigmoid(c)
    w = w_ref[0]
    ah = a.astype(BF16)
    al = (a - ah.astype(F32)).astype(BF16)
    wh = w.astype(BF16)
    wl = (w - wh.astype(F32)).astype(BF16)
    o_ref[0] = _dot(ah, wh) + _dot(ah, wl) + _dot(al, wh) + b_ref[0]


def _mod_call(cvec, mod_w, mod_b):
    n = N_MOD * D_MODEL
    return pl.pallas_call(
        _mod_kernel,
        out_shape=jax.ShapeDtypeStruct((DEPTH, 2 * BATCH, n), F32),
        grid=(DEPTH, n // MOD_TN),
        in_specs=[
            pl.BlockSpec((2 * BATCH, D_MODEL), lambda l, k: (0, 0)),
            pl.BlockSpec((1, D_MODEL, MOD_TN), lambda l, k: (l, 0, k)),
            pl.BlockSpec((1, 1, MOD_TN), lambda l, k: (l, 0, k)),
        ],
        out_specs=pl.BlockSpec((1, 2 * BATCH, MOD_TN), lambda l, k: (l, 0, k)),
        compiler_params=_cparams(("parallel", "parallel")),
        name="mod",
    )(cvec, mod_w, mod_b.reshape(DEPTH, 1, n))


def _mod_spec(col, n_latent_tiles):
    return pl.BlockSpec((1, BATCH, D_MODEL), lambda j: (jnp.where(j < n_latent_tiles, 0, 1), 0, col))


IN_ROWS = IN_T * BATCH
IN_NLAT = SEQ // IN_T
IN_N = L_TOT // IN_T
IN_SUB = 4
CAST_STEPS = 16
assert CAST_STEPS <= IN_N
IN_SUB_T = IN_T // IN_SUB
IN_SUB_ROWS = IN_SUB_T * BATCH


def _in_kernel(fu_ref, fd_ref, *refs, n_tb, first_layer, layer):
    fu_out, fd_out = refs[-2:]
    refs = refs[:-2]

    @pl.when(pl.program_id(0) < CAST_STEPS)
    def _():
        fu_out[...] = fu_ref[0].astype(BF16)
        fd_out[...] = fd_ref[0].astype(BF16)

    if first_layer:
        x_ref, c_ref, sh_ref, sc_ref, g_ref, w_ref, rope_ref, gq_ref, gk_ref, gm_ref = refs[:10]
        h_out = refs[10]
        outs = refs[11:]
        is_latent = pl.program_id(0) < IN_NLAT
    else:
        h_ref, sh_ref, sc_ref, g_ref, w_ref, rope_ref, gq_ref, gk_ref, gm_ref = refs[:9]
        outs = refs[9:]
    q_ref, k_ref, v_ref = outs[n_tb:]
    off = n_tb * MIX_W
    gm = gm_ref[...]
    gq = jnp.concatenate([gq_ref[...]] * (LANE // HEAD_DIM), axis=1)
    gk = jnp.concatenate([gk_ref[...]] * (LANE // HEAD_DIM), axis=1)
    for s in range(IN_SUB):
        ts = slice(s * IN_SUB_T, (s + 1) * IN_SUB_T)
        rs = slice(s * IN_SUB_ROWS, (s + 1) * IN_SUB_ROWS)
        if first_layer:
            h = _to_time_major(jnp.where(is_latent, x_ref[:, ts, :], c_ref[:, ts, :]))
            h_out[rs, :] = h
        else:
            h = h_ref[rs, :]
        xn = _per_sample(_rms(h, g_ref[layer, 0:1, :]), 1.0 + sc_ref[0], jnp.multiply)
        xn = _per_sample(xn, sh_ref[0], jnp.add)
        proj = _dot(xn.astype(BF16), w_ref[...])
        for i in range(n_tb):
            outs[i][rs, :] = proj[:, i * MIX_W:(i + 1) * MIX_W]
        qkv = _to_batch_major(proj[:, off:off + QKV_W])
        cos = rope_ref[ts, 0:LANE][None]
        sin_up = rope_ref[ts, LANE:2 * LANE][None]
        sin_dn = rope_ref[ts, 2 * LANE:3 * LANE][None]
        for p in range(5):
            blk = qkv[:, :, p * LANE:(p + 1) * LANE].reshape(IN_SUB_ROWS, LANE)
            sq = blk * blk
            hi = sq.astype(BF16)
            lo = (sq - hi.astype(F32)).astype(BF16)
            ms = _dot(hi, gm) + _dot(lo, gm)
            g = gq if p < 4 else gk
            bn = blk * lax.rsqrt(ms + NORM_EPS) * g
            up = pltpu.roll(bn, LANE - HEAD_DIM // 4, 1).reshape(BATCH, IN_SUB_T, LANE)
            dn = pltpu.roll(bn, HEAD_DIM // 4, 1).reshape(BATCH, IN_SUB_T, LANE)
            ro = bn.reshape(BATCH, IN_SUB_T, LANE) * cos + up * sin_up + dn * sin_dn
            if p < 4:
                q_ref[:, ts, p * LANE:(p + 1) * LANE] = (ro * Q_SCALE).astype(BF16)
            else:
                k_ref[:, ts, :] = ro.astype(BF16)
        v = qkv[:, :, 5 * LANE:6 * LANE].astype(BF16)
        v_ref[:, ts, :] = jnp.concatenate([v, jnp.ones((BATCH, IN_SUB_T, LANE), BF16)], axis=2)


def _in_call(h_args, modt, g, w, rope, gq, gk, gm, ffn_up, ffn_down, layer, n_tb, first_layer):
    n = w.shape[1]
    tb = lambda width: pl.BlockSpec((IN_ROWS, width), lambda j: (j, 0))
    bm = lambda width: pl.BlockSpec((BATCH, IN_T, width), lambda j: (0, j, 0))
    up_rows = D_MODEL // CAST_STEPS
    down_rows = FFN_HIDDEN // CAST_STEPS
    slab = lambda j: jnp.minimum(j, CAST_STEPS - 1)
    if first_layer:
        h_specs = [
            pl.BlockSpec((BATCH, IN_T, D_MODEL), lambda j: (0, jnp.minimum(j, IN_NLAT - 1), 0)),
            pl.BlockSpec((BATCH, IN_T, D_MODEL), lambda j: (0, jnp.maximum(j - IN_NLAT, 0), 0)),
        ]
        extra_shape = [jax.ShapeDtypeStruct((L_TOT * BATCH, D_MODEL), F32)]
        extra_spec = [tb(D_MODEL)]
    else:
        h_specs = [tb(D_MODEL)]
        extra_shape, extra_spec = [], []
    return pl.pallas_call(
        functools.partial(_in_kernel, n_tb=n_tb, first_layer=first_layer, layer=layer),
        out_shape=extra_shape + [jax.ShapeDtypeStruct((L_TOT * BATCH, MIX_W), F32)] * n_tb + [
            jax.ShapeDtypeStruct((BATCH, L_TOT, MIX_W), BF16),
            jax.ShapeDtypeStruct((BATCH, L_TOT, KV_W), BF16),
            jax.ShapeDtypeStruct((BATCH, L_TOT, V_W), BF16),
            jax.ShapeDtypeStruct((D_MODEL, 2 * FFN_HIDDEN), BF16),
            jax.ShapeDtypeStruct((FFN_HIDDEN, D_MODEL), BF16),
        ],
        grid=(IN_N,),
        in_specs=[
            pl.BlockSpec((1, up_rows, 2 * FFN_HIDDEN), lambda j: (layer, slab(j), 0)),
            pl.BlockSpec((1, down_rows, D_MODEL), lambda j: (layer, slab(j), 0)),
        ] + h_specs + [
            _mod_spec(0, IN_NLAT), _mod_spec(1, IN_NLAT),
            _full(g.shape),
            _full((D_MODEL, n), single=True),
            pl.BlockSpec((IN_T, 3 * LANE), lambda j: (j, 0)),
            _full(gq.shape), _full(gk.shape), _full((LANE, LANE)),
        ],
        out_specs=extra_spec + [tb(MIX_W)] * n_tb + [bm(MIX_W), bm(KV_W), bm(V_W)] + [
            pl.BlockSpec((up_rows, 2 * FFN_HIDDEN), lambda j: (slab(j), 0)),
            pl.BlockSpec((down_rows, D_MODEL), lambda j: (slab(j), 0)),
        ],
        compiler_params=_cparams(("arbitrary",)),
        name="in_proj",
    )(ffn_up, ffn_down, *h_args, modt, modt, g, w, rope, gq, gk, gm)


def _chunk_of(i, reverse, n_latent, n_chunks):
    if reverse:
        return n_chunks - 1 - i
    return jnp.where(i < n_chunks - n_latent, i + n_latent, i - (n_chunks - n_latent))


RG_ROWS = RG_T * BATCH
RG_NC = L_TOT // RG_T
RG_NLAT = SEQ // RG_T


def _softplus(z):
    return jnp.maximum(z, 0.0) + jnp.log1p(jnp.exp(-jnp.abs(z)))


def _rg_kernel(*refs, reverse):
    if reverse:
        (u_ref, up_ref, un_ref, cw_ref, cb_ref, w_ref, ba_ref, bx_ref, lam_ref,
         out_ref, xbuf, abuf, bbuf, hst) = refs
    else:
        (u_ref, up_ref, un_ref, gate_ref, hb_ref, cw_ref, cb_ref, w_ref, ba_ref, bx_ref, lam_ref,
         out_ref, xbuf, abuf, bbuf, hst) = refs
    d = 1 if reverse else 0
    i = pl.program_id(0)
    c = _chunk_of(i, reverse, RG_NLAT, RG_NC)
    prev_zero = jnp.logical_or(c == 0, c == RG_NLAT)
    next_zero = jnp.logical_or(c == RG_NLAT - 1, c == RG_NC - 1)

    @pl.when(i == 0)
    def _():
        hst[...] = jnp.zeros_like(hst)

    xbuf[0:BATCH, :] = up_ref[...] * jnp.where(prev_zero, 0.0, 1.0)
    xbuf[BATCH:BATCH + RG_ROWS, :] = u_ref[...]
    xbuf[BATCH + RG_ROWS:3 * BATCH + RG_ROWS, :] = un_ref[...] * jnp.where(next_zero, 0.0, 1.0)
    uc = cb_ref[...]
    for k in range(RG_CONV):
        uc = uc + xbuf[k * BATCH:k * BATCH + RG_ROWS, :] * cw_ref[0, k:k + 1, :]

    ub = uc.astype(BF16)
    half = MIX_W // 2

    def gate(which, b_ref):
        w0 = which * 4 + d * 2
        z = jnp.concatenate([_dot(ub[:, :half], w_ref[w0]), _dot(ub[:, half:], w_ref[w0 + 1])], axis=1)
        return jax.nn.sigmoid(z + b_ref[0, d:d + 1, :])

    r = gate(0, ba_ref)
    ig = gate(1, bx_ref)
    log_a = (-RG_C) * r * _softplus(-lam_ref[0, d:d + 1, :])
    a = jnp.exp(log_a)
    abuf[...] = a
    bbuf[...] = jnp.sqrt(-jnp.tanh(log_a) * (a * a + 1.0)) * (ig * uc)

    h = hst[...]
    for s in range(RG_T):
        t = (RG_T - 1 - s) if reverse else s
        rows = slice(t * BATCH, (t + 1) * BATCH)
        h = abuf[rows, :] * h + bbuf[rows, :]
        bbuf[rows, :] = h
    hst[...] = h

    if reverse:
        out_ref[...] = bbuf[...]
    else:
        y = bbuf[...] + hb_ref[...]
        out_ref[...] = (y * _gelu_tanh(gate_ref[...])).astype(BF16)


def _rg_call(u, gate, hb, cw, cb, w_gates, ba, bx, lam, reverse):
    def cidx(i):
        return _chunk_of(i, reverse, RG_NLAT, RG_NC)

    main = pl.BlockSpec((RG_ROWS, MIX_W), lambda i: (cidx(i), 0))
    prev = pl.BlockSpec((BATCH, MIX_W), lambda i: (jnp.maximum(cidx(i) * RG_T - 1, 0), 0))
    n_next = L_TOT // 2
    nxt = pl.BlockSpec((2 * BATCH, MIX_W), lambda i: (jnp.minimum((cidx(i) + 1) * (RG_T // 2), n_next - 1), 0))
    params = [cw, cb, w_gates, ba, bx, lam]
    pspecs = [_full(p.shape) for p in params]
    if reverse:
        args = [u, u, u] + params
        specs = [main, prev, nxt] + pspecs
        out_dtype = F32
    else:
        args = [u, u, u, gate, hb] + params
        specs = [main, prev, nxt, main, main] + pspecs
        out_dtype = BF16
    return pl.pallas_call(
        functools.partial(_rg_kernel, reverse=reverse),
        out_shape=jax.ShapeDtypeStruct((L_TOT * BATCH, MIX_W), out_dtype),
        grid=(RG_NC,),
        in_specs=specs,
        out_specs=main,
        scratch_shapes=[
            pltpu.VMEM((RG_ROWS + 3 * BATCH, MIX_W), F32),
            pltpu.VMEM((RG_ROWS, MIX_W), F32),
            pltpu.VMEM((RG_ROWS, MIX_W), F32),
            pltpu.VMEM((BATCH, MIX_W), F32),
        ],
        compiler_params=_cparams(("arbitrary",)),
        name="rglru_bwd" if reverse else "rglru_fwd",
    )(*args)


def _head_rows(q_ref, tq):
    lane = lax.broadcasted_iota(jnp.int32, (tq, LANE), 1)
    low = lane < HEAD_DIM
    zero = jnp.zeros((tq, LANE), BF16)
    parts = []
    for p in range(MIX_W // LANE):
        qb = q_ref[0, :, p * LANE:(p + 1) * LANE]
        parts.append(jnp.where(low, qb, zero))
        parts.append(jnp.where(low, zero, qb))
    return parts, low


def _merge_heads(outs, low):
    return jnp.concatenate([jnp.where(low, outs[2 * p], outs[2 * p + 1]) for p in range(MIX_W // LANE)], axis=1)


def _qk(q, k):
    return lax.dot_general(q, k, (((1,), (1,)), ((), ())), preferred_element_type=F32)


def _gattn_kernel(q_ref, k_ref, v_ref, o_ref):
    j = pl.program_id(1)
    heads, low = _head_rows(q_ref, TQ_G)

    def attend(k0, nk):
        outs = []
        for q in heads:
            s = _qk(q, k_ref[0, k0:k0 + nk, :])
            m = jnp.max(s, axis=-1, keepdims=True)
            p = jnp.exp2(s - m).astype(BF16)
            acc = _dot(p, v_ref[0, k0:k0 + nk, :])
            outs.append(acc[:, :KV_W] / acc[:, KV_W:])
        o_ref[0] = _merge_heads(outs, low).astype(BF16)

    @pl.when(j < SEQ // TQ_G)
    def _():
        attend(0, L_TOT)

    @pl.when(j >= SEQ // TQ_G)
    def _():
        attend(SEQ, CTX_LEN)


def _gattn_call(q, k, v):
    return pl.pallas_call(
        _gattn_kernel,
        out_shape=jax.ShapeDtypeStruct((BATCH, L_TOT, MIX_W), BF16),
        grid=(BATCH, L_TOT // TQ_G),
        in_specs=[
            pl.BlockSpec((1, TQ_G, MIX_W), lambda b, j: (b, j, 0)),
            pl.BlockSpec((1, L_TOT, KV_W), lambda b, j: (b, 0, 0)),
            pl.BlockSpec((1, L_TOT, V_W), lambda b, j: (b, 0, 0)),
        ],
        out_specs=pl.BlockSpec((1, TQ_G, MIX_W), lambda b, j: (b, j, 0)),
        compiler_params=_cparams(("parallel", "arbitrary")),
        name="global_attn",
    )(q, k, v)


TQ_W = 2 * WINDOW
N_WBLK = SEQ // WINDOW
N_WSTEP = SEQ // TQ_W


def _wattn_kernel(q_ref, k_ref, v_ref, sink_ref, o_ref):
    i = pl.program_id(1)
    heads, low = _head_rows(q_ref, TQ_W)
    prev_start = pl.multiple_of(jnp.maximum(2 * i - 1, 0) * WINDOW, WINDOW)
    mid_start = pl.multiple_of(i * TQ_W, TQ_W)
    next_start = pl.multiple_of(jnp.minimum(2 * i + 2, N_WBLK - 1) * WINDOW, WINDOW)

    def rows(ref):
        return jnp.concatenate([ref[0, SEQ:L_TOT, :], ref[0, pl.ds(prev_start, WINDOW), :],
                                ref[0, pl.ds(mid_start, TQ_W), :], ref[0, pl.ds(next_start, WINDOW), :]], axis=0)

    nk = CTX_LEN + 2 * WINDOW + TQ_W
    r = lax.broadcasted_iota(jnp.int32, (TQ_W, nk), 0)
    col = lax.broadcasted_iota(jnp.int32, (TQ_W, nk), 1)
    c = col - CTX_LEN
    ninf = -jnp.inf
    pen_prev = jnp.where(i > 0, 0.0, ninf)
    pen_next = jnp.where(i < N_WSTEP - 1, 0.0, ninf)
    edge = jnp.where(c < WINDOW, pen_prev, jnp.where(c >= WINDOW + TQ_W, pen_next, 0.0))
    band = jnp.where(c >= r, jnp.where(c <= r + 2 * WINDOW, edge, ninf), ninf)
    bias = jnp.where(col < CTX_LEN, 0.0, band)
    kk = rows(k_ref)
    vv = rows(v_ref)
    outs = []
    for hd, q in enumerate(heads):
        s = _qk(q, kk) + bias
        sink = sink_ref[0, HEAD_PERM[hd]] * math.log2(math.e)
        m = jnp.maximum(jnp.max(s, axis=-1, keepdims=True), sink)
        p = jnp.exp2(s - m).astype(BF16)
        acc = _dot(p, vv)
        outs.append(acc[:, :KV_W] / (acc[:, KV_W:] + jnp.exp2(sink - m)))
    o_ref[0] = _merge_heads(outs, low).astype(BF16)


def _wattn_call(q, k, v, sink):
    return pl.pallas_call(
        _wattn_kernel,
        out_shape=jax.ShapeDtypeStruct((BATCH, SEQ, MIX_W), BF16),
        grid=(BATCH, N_WSTEP),
        in_specs=[
            pl.BlockSpec((1, TQ_W, MIX_W), lambda b, i: (b, i, 0)),
            pl.BlockSpec((1, L_TOT, KV_W), lambda b, i: (b, 0, 0)),
            pl.BlockSpec((1, L_TOT, V_W), lambda b, i: (b, 0, 0)),
            pl.BlockSpec(memory_space=pltpu.SMEM),
        ],
        out_specs=pl.BlockSpec((1, TQ_W, MIX_W), lambda b, i: (b, i, 0)),
        compiler_params=_cparams(("parallel", "arbitrary")),
        name="window_attn",
    )(q, k, v, sink)


def _s5_disc_kernel(lr_ref, li_ref, ls_ref, br_ref, bi_ref, ar_ref, ai_ref, bbr_ref, bbi_ref):
    lr = lr_ref[0]
    li = li_ref[0]
    dt = jnp.exp(ls_ref[0])
    mag = jnp.exp(lr * dt)
    ang = li * dt
    abr = mag * jnp.cos(ang)
    abi = mag * jnp.sin(ang)
    den = lr * lr + li * li
    nr = abr - 1.0
    kr = (nr * lr + abi * li) / den
    ki = (abi * lr - nr * li) / den
    br = br_ref[0]
    bi = bi_ref[0]
    ar_ref[0] = abr
    ai_ref[0] = abi
    bbr_ref[0] = kr * br - ki * bi
    bbi_ref[0] = kr * bi + ki * br


def _s5_disc_call(lam_re, lam_im, log_step, bt_re, bt_im):
    n = S5_GROUPS * S5_STATE
    row = pl.BlockSpec((1, 1, n), lambda d: (d, 0, 0))
    mat = pl.BlockSpec((1, S5_GROUP, n), lambda d: (d, 0, 0))
    return pl.pallas_call(
        _s5_disc_kernel,
        out_shape=[jax.ShapeDtypeStruct((2, 1, n), F32)] * 2 + [jax.ShapeDtypeStruct((2, S5_GROUP, n), F32)] * 2,
        grid=(2,),
        in_specs=[row, row, row, mat, mat],
        out_specs=[row, row, mat, mat],
        compiler_params=_cparams(("parallel",)),
        name="s5_discretize",
    )(lam_re, lam_im, log_step, bt_re, bt_im)


S5_ROWS = S5_T * BATCH
S5_NC = L_TOT // S5_T
S5_NLAT = SEQ // S5_T
S5_SW = 2 * S5_BLK_STATE * S5_NBLK


def _s5_kernel(*refs, reverse):
    if reverse:
        u2_ref, un_ref, wd_ref, ar_ref, ai_ref, wr_ref, out_ref, buf0, buf1, xst = refs
    else:
        (u2_ref, un_ref, yb2_ref, wd_ref, ar_ref, ai_ref, wr_ref, dsk_ref, gw_ref, gb_ref,
         out_ref, buf0, buf1, xst) = refs
    nb = 2 * S5_BLK_STATE
    lo, hi = slice(0, S5_ROWS), slice(S5_ROWS, 2 * S5_ROWS)
    first, second = (hi, lo) if reverse else (lo, hi)

    def drive(u, buf):
        ub = u.astype(BF16)
        for k in range(S5_NBLK):
            buf[:, k * nb:(k + 1) * nb] = _dot(ub[:, k * LANE:(k + 1) * LANE], wd_ref[0, k])

    def scan(buf, x):
        for s in range(S5_T):
            t = (S5_T - 1 - s) if reverse else s
            rows = slice(t * BATCH, (t + 1) * BATCH)
            halves = []
            for k in range(S5_NBLK):
                halves.append(x[:, k * nb + S5_BLK_STATE:(k + 1) * nb])
                halves.append(x[:, k * nb:k * nb + S5_BLK_STATE])
            xsw = jnp.concatenate(halves, axis=1)
            x = ar_ref[0] * x + ai_ref[0] * xsw + buf[rows, :]
            buf[rows, :] = x
        return x

    def readout(buf, rows):
        y = jnp.concatenate(
            [_dot(buf[:, k * nb:(k + 1) * nb].astype(BF16), wr_ref[0, k]) for k in range(S5_NBLK)], axis=1)
        if reverse:
            out_ref[rows, :] = y
        else:
            y = u2_ref[rows, :] * dsk_ref[...] + y + yb2_ref[rows, :]
            z = _gelu_tanh(y)
            out_ref[rows, :] = (z * jax.nn.sigmoid(_dot(z.astype(BF16), gw_ref[...]) + gb_ref[...])).astype(BF16)

    @pl.when(pl.program_id(0) == 0)
    def _():
        xst[...] = jnp.zeros_like(xst)
        drive(u2_ref[first, :], buf0)

    x = xst[...]
    drive(u2_ref[second, :], buf1)
    x = scan(buf0, x)
    readout(buf0, first)
    drive(un_ref[...], buf0)
    x = scan(buf1, x)
    readout(buf1, second)
    xst[...] = x


def _s5_call(u, yb, wd, ar, ai, wr, dsk, gw, gb, reverse):
    def chunk(p):
        return _chunk_of(p, reverse, S5_NLAT, S5_NC)

    pair = pl.BlockSpec((2 * S5_ROWS, MIX_W), lambda k: (chunk(2 * k) // 2, 0))
    nxt = pl.BlockSpec((S5_ROWS, MIX_W), lambda k: (chunk(jnp.minimum(2 * k + 2, S5_NC - 1)), 0))
    d = 1 if reverse else 0
    dir_specs = [pl.BlockSpec((1,) + a.shape[1:], lambda k, n=a.ndim: (d,) + (0,) * (n - 1)) for a in (wd, ar, ai, wr)]
    if reverse:
        args = [u, u, wd, ar, ai, wr]
        specs = [pair, nxt] + dir_specs
        out_dtype = F32
    else:
        args = [u, u, yb, wd, ar, ai, wr, dsk, gw, gb]
        specs = [pair, nxt, pair] + dir_specs + [_full(a.shape) for a in (dsk, gw, gb)]
        out_dtype = BF16
    return pl.pallas_call(
        functools.partial(_s5_kernel, reverse=reverse),
        out_shape=jax.ShapeDtypeStruct((L_TOT * BATCH, MIX_W), out_dtype),
        grid=(S5_NC // 2,),
        in_specs=specs,
        out_specs=pair,
        scratch_shapes=[
            pltpu.VMEM((S5_ROWS, S5_SW), F32),
            pltpu.VMEM((S5_ROWS, S5_SW), F32),
            pltpu.VMEM((BATCH, S5_SW), F32),
        ],
        compiler_params=_cparams(("arbitrary",)),
        name="s5_bwd" if reverse else "s5_fwd",
    )(*args)


POST_ROWS = POST_T * BATCH
POST_HALO_T = BF16_ROWS // BATCH
POST_HALO = POST_HALO_T * BATCH
POST_NLAT = SEQ // POST_T
FFN_NCH = FFN_HIDDEN // FFN_CH


def _post_kernel(h_ref, hp_ref, hn_ref, a_ref, ap_ref, an_ref, b_ref, bp_ref, bn_ref, wo_ref,
                 g2_ref, sh_ref, sc_ref, g5_ref, g_ref, wu_ref, cw_ref, cb_ref, wd_ref, o_ref, *,
                 n_tiles, out_batch_major, layer):
    j = pl.program_id(0)
    first = jnp.logical_or(j == 0, j == POST_NLAT)
    last = jnp.logical_or(j == POST_NLAT - 1, j == n_tiles - 1)
    gate2 = g2_ref[0]
    gain = g_ref[layer, 1:2, :]
    scale = 1.0 + sc_ref[0]
    shift = sh_ref[0]

    def mix(h, a, b_tm):
        m = _dot(a, wo_ref[0:MIX_W, :]) + _dot(b_tm.astype(BF16), wo_ref[MIX_W:2 * MIX_W, :])
        return h + _per_sample(m, gate2, jnp.multiply)

    def norm(x):
        return _per_sample(_per_sample(_rms(x, gain), scale, jnp.multiply), shift, jnp.add)

    def halo_b(ref, t0):
        x = jnp.swapaxes(ref[...].astype(F32), 0, 1)[t0:t0 + POST_HALO_T]
        return x.reshape(POST_HALO, MIX_W)

    h_mid = mix(h_ref[...], a_ref[...], _to_time_major(b_ref[...].astype(F32)))
    h_prev = mix(hp_ref[...], ap_ref[...], halo_b(bp_ref, BF16_ROWS - POST_HALO_T))
    h_next = mix(hn_ref[...], an_ref[...], halo_b(bn_ref, 0))
    xn = jnp.concatenate([
        (norm(h_prev) * jnp.where(first, 0.0, 1.0)).astype(BF16),
        norm(h_mid).astype(BF16),
        (norm(h_next) * jnp.where(last, 0.0, 1.0)).astype(BF16)], axis=0)

    acts = []
    for c in range(FFN_NCH):
        cv = slice(c * FFN_CH, (c + 1) * FFN_CH)
        cg = slice(FFN_HIDDEN + c * FFN_CH, FFN_HIDDEN + (c + 1) * FFN_CH)

        def conv(cols):
            hid = _dot(xn, wu_ref[:, cols])
            out = cb_ref[layer:layer + 1, cols]
            for k in range(FFN_CONV):
                r0 = POST_HALO + (k - 1) * BATCH
                out = out + hid[r0:r0 + POST_ROWS] * cw_ref[0, k:k + 1, cols]
            return out

        val = conv(cv)
        gate = conv(cg)
        acts.append((val * (gate * jax.nn.sigmoid(gate))).astype(BF16))
    ffn = _dot(jnp.concatenate(acts, axis=1), wd_ref[...])
    out = h_mid + _per_sample(ffn, g5_ref[0], jnp.multiply)
    if out_batch_major:
        o_ref[...] = _to_batch_major(out)
    else:
        o_ref[...] = out


def _post_call(h, a_tb, b_out, w_out, modt, g, wu, cw, cb, wd, layer, n_tiles, out_batch_major):
    per = POST_ROWS // POST_HALO
    n_halo = h.shape[0] // POST_HALO
    per_b = POST_T // BF16_ROWS
    n_halo_b = b_out.shape[1] // BF16_ROWS

    def main(width):
        return pl.BlockSpec((POST_ROWS, width), lambda j: (j, 0))

    def prev(width):
        return pl.BlockSpec((POST_HALO, width), lambda j: (jnp.maximum(j * per - 1, 0), 0))

    def nxt(width):
        return pl.BlockSpec((POST_HALO, width), lambda j: (jnp.minimum((j + 1) * per, n_halo - 1), 0))

    if out_batch_major:
        out_shape = jax.ShapeDtypeStruct((BATCH, n_tiles * POST_T, D_MODEL), F32)
        out_spec = pl.BlockSpec((BATCH, POST_T, D_MODEL), lambda j: (0, j, 0))
    else:
        out_shape = jax.ShapeDtypeStruct((n_tiles * POST_ROWS, D_MODEL), F32)
        out_spec = main(D_MODEL)
    return pl.pallas_call(
        functools.partial(_post_kernel, n_tiles=n_tiles, out_batch_major=out_batch_major, layer=layer),
        out_shape=out_shape,
        grid=(n_tiles,),
        in_specs=[
            main(D_MODEL), prev(D_MODEL), nxt(D_MODEL),
            main(MIX_W), prev(MIX_W), nxt(MIX_W),
            pl.BlockSpec((BATCH, POST_T, MIX_W), lambda j: (0, j, 0)),
            pl.BlockSpec((BATCH, BF16_ROWS, MIX_W), lambda j: (0, jnp.maximum(j * per_b - 1, 0), 0)),
            pl.BlockSpec((BATCH, BF16_ROWS, MIX_W), lambda j: (0, jnp.minimum((j + 1) * per_b, n_halo_b - 1), 0)),
            _full((2 * MIX_W, D_MODEL), single=True),
            _mod_spec(2, POST_NLAT), _mod_spec(3, POST_NLAT), _mod_spec(4, POST_NLAT), _mod_spec(5, POST_NLAT),
            _full(g.shape),
            _full((D_MODEL, 2 * FFN_HIDDEN), single=True),
            pl.BlockSpec((1, FFN_CONV, 2 * FFN_HIDDEN), lambda j: (layer, 0, 0)),
            _full(cb.shape),
            _full((FFN_HIDDEN, D_MODEL), single=True),
        ],
        out_specs=out_spec,
        compiler_params=_cparams(("parallel",)),
        name="post_ffn",
    )(h, h, h, a_tb, a_tb, a_tb, b_out, b_out, b_out, w_out, modt, modt, modt, modt, g, wu, cw, cb, wd)


def _rope_table():
    rows = SEQ // GRID_W
    row = jnp.repeat(jnp.arange(rows, dtype=F32), GRID_W)
    col = jnp.tile(jnp.arange(GRID_W, dtype=F32), rows)
    quarter = HEAD_DIM // 4
    inv_freq = ROPE_BASE ** (-jnp.arange(quarter, dtype=F32) / quarter)
    ang = jnp.stack([row[:, None] * inv_freq, col[:, None] * inv_freq], axis=1)
    cos = jnp.cos(ang)
    sin = jnp.sin(ang)
    zero = jnp.zeros_like(sin)

    def lanes(first, second):
        t = jnp.stack([first, second], axis=2).reshape(SEQ, HEAD_DIM)
        return jnp.tile(t, (1, LANE // HEAD_DIM))

    tab = jnp.concatenate([lanes(cos, cos), lanes(-sin, zero), lanes(zero, sin)], axis=1)
    ctx = jnp.concatenate([jnp.ones((CTX_LEN, LANE), F32), jnp.zeros((CTX_LEN, 2 * LANE), F32)], axis=1)
    return jnp.concatenate([tab, ctx], axis=0)


def _perm_heads_cols(w):
    d = w.shape[0]
    return w.reshape(d, N_HEADS, HEAD_DIM)[:, jnp.array(HEAD_PERM)].reshape(d, N_HEADS * HEAD_DIM)


def _perm_heads_rows(w):
    n = w.shape[1]
    return w.reshape(N_HEADS, HEAD_DIM, n)[jnp.array(HEAD_PERM)].reshape(N_HEADS * HEAD_DIM, n)


def _block_diag(w, per):
    n, a, b = w.shape
    eye = jnp.eye(per, dtype=w.dtype)
    w4 = w.reshape(n // per, per, a, b)
    return jnp.einsum("ihab,hk->ihakb", w4, eye).reshape(n // per, per * a, per * b)


def _w_in(w, q_off):
    return jnp.concatenate([w[:, :q_off], _perm_heads_cols(w[:, q_off:q_off + MIX_W]), w[:, q_off + MIX_W:]],
                           axis=1).astype(BF16)


def _w_out(w):
    return jnp.concatenate([w[:MIX_W], _perm_heads_rows(w[MIX_W:])], axis=0).astype(BF16)


def kernel(x, c, ctx, c_ctx, mod_w, mod_b, norm_g, ffn_up, ffn_conv_w, ffn_conv_b, ffn_down, ev_w_in, ev_w_out, rg_conv_w, rg_conv_b, rg_wa, rg_ba, rg_wx, rg_bx, rg_lam, ga_qn, ga_kn, od_w_in, od_w_out, s5_lam_re, s5_lam_im, s5_log_step, s5_b_re, s5_b_im, s5_c_re, s5_c_im, s5_d, s5_glu_w, s5_glu_b, wa_qn, wa_kn, wa_sink):
    cvec = jnp.concatenate([c, jnp.broadcast_to(c_ctx[None], (BATCH, D_MODEL))], axis=0)
    mod = _mod_call(cvec, mod_w, mod_b).reshape(DEPTH, 2, BATCH, N_MOD * D_MODEL)
    rope = _rope_table()
    gm = _block_diag(jnp.full((LANE // HEAD_DIM, HEAD_DIM, HEAD_DIM), 1.0 / HEAD_DIM, F32), LANE // HEAD_DIM)[0].astype(BF16)

    h, u_tb, gate_tb, q, k, v, wu, wd = _in_call(
        (x, ctx), mod[0], norm_g, _w_in(ev_w_in[0], 2 * MIX_W), rope,
        ga_qn, ga_kn, gm, ffn_up, ffn_down, 0, n_tb=2, first_layer=True)
    w_gates = _block_diag(jnp.stack([rg_wa[0], rg_wx[0]]).reshape(-1, HEAD_DIM, HEAD_DIM), 4).astype(BF16)
    rg = (rg_conv_w, rg_conv_b, w_gates, rg_ba, rg_bx, rg_lam)
    hb = _rg_call(u_tb, None, None, *rg, reverse=True)
    a_out = _rg_call(u_tb, gate_tb, hb, *rg, reverse=False)
    b_out = _gattn_call(q, k, v)
    h = _post_call(h, a_out, b_out, _w_out(ev_w_out[0]), mod[0], norm_g, wu, ffn_conv_w, ffn_conv_b, wd, 0,
                   n_tiles=L_TOT // POST_T, out_batch_major=False)

    u_tb, q, k, v, wu, wd = _in_call(
        (h,), mod[1], norm_g, _w_in(od_w_in[0], MIX_W), rope,
        wa_qn, wa_kn, gm, ffn_up, ffn_down, 1, n_tb=1, first_layer=False)
    n_state = S5_GROUPS * S5_STATE
    bt_re = s5_b_re[0].transpose(0, 3, 1, 2).reshape(2, S5_GROUP, n_state)
    bt_im = s5_b_im[0].transpose(0, 3, 1, 2).reshape(2, S5_GROUP, n_state)
    log_step = jnp.repeat(s5_log_step[0], S5_STATE, axis=-1).reshape(2, 1, n_state)
    abr, abi, bbr, bbi = _s5_disc_call(s5_lam_re[0].reshape(2, 1, n_state), s5_lam_im[0].reshape(2, 1, n_state),
                                       log_step, bt_re, bt_im)
    gpb = S5_GROUPS // S5_NBLK
    eye = jnp.eye(gpb, dtype=F32)
    bb = jnp.stack([bbr, bbi]).reshape(2, 2, S5_GROUP, S5_NBLK, gpb, S5_STATE)
    w_drive = jnp.einsum("rdhbgp,gk->dbghrkp", bb, eye).reshape(
        2, S5_NBLK, gpb * S5_GROUP, 2 * S5_BLK_STATE).astype(BF16)
    cc = jnp.stack([s5_c_re[0], -s5_c_im[0]]).reshape(2, 2, S5_NBLK, gpb, S5_GROUP, S5_STATE)
    w_read = jnp.einsum("rdbghp,gk->dbrgpkh", cc, eye).reshape(
        2, S5_NBLK, 2 * S5_BLK_STATE, gpb * S5_GROUP).astype(BF16)

    def state_rows(re, im):
        row = jnp.concatenate([re.reshape(2, S5_NBLK, S5_BLK_STATE), im.reshape(2, S5_NBLK, S5_BLK_STATE)],
                              axis=2).reshape(2, 1, S5_SW)
        return jnp.broadcast_to(row, (2, BATCH, S5_SW))

    s5 = (w_drive, state_rows(abr, abr), state_rows(-abi, abi), w_read)
    yb = _s5_call(u_tb, None, *s5, None, None, None, reverse=True)
    c_out = _s5_call(u_tb, yb, *s5, s5_d, s5_glu_w[0].astype(BF16), s5_glu_b, reverse=False)
    d_out = _wattn_call(q, k, v, wa_sink)
    return _post_call(h, c_out, d_out, _w_out(od_w_out[0]), mod[1], norm_g, wu, ffn_conv_w, ffn_conv_b, wd, 1,
                      n_tiles=SEQ // POST_T, out_batch_major=True)
```

```python
import functools
import math

import jax
import jax.numpy as jnp
from jax import lax
from jax.experimental import pallas as pl
from jax.experimental.pallas import tpu as pltpu

F32 = jnp.float32
BF16 = jnp.bfloat16

D_MODEL = 1024
BATCH = 8
SEQ = 2048
CTX_LEN = 256
L_TOT = SEQ + CTX_LEN
DEPTH = 2
GRID_W = 64
HEAD_DIM = 64
ROPE_BASE = 10000.0
NORM_EPS = 1e-6
WINDOW = 128
N_MOD = 6
MIX_W = 512
N_HEADS = 8
N_KV = 2
KV_W = N_KV * HEAD_DIM
V_W = 2 * KV_W
QKV_W = MIX_W + KV_W + KV_W
Q_SCALE = HEAD_DIM ** -0.5 * math.log2(math.e)
RG_CONV = 4
RG_C = 8.0
S5_GROUP = 16
S5_GROUPS = 32
S5_STATE = 64
S5_NBLK = 4
S5_BLK_STATE = (S5_GROUPS // S5_NBLK) * S5_STATE
FFN_HIDDEN = 2816
FFN_CONV = 3

LANE = 128
BF16_ROWS = 16
VMEM_LIMIT = 56 * 1024 * 1024

IN_T = 128
RG_T = 128
S5_T = 64
POST_T = 128
TQ_G = 256
FFN_CH = 256
HEAD_PERM = (0, 4, 1, 5, 2, 6, 3, 7)


def _cparams(sem):
    return pltpu.CompilerParams(dimension_semantics=sem, vmem_limit_bytes=VMEM_LIMIT)


def _dot(a, b):
    return jnp.dot(a, b, preferred_element_type=F32)


def _gelu_tanh(x):
    return x * (0.5 * (1.0 + jnp.tanh(math.sqrt(2.0 / math.pi) * (x + 0.044715 * (x * x * x)))))


def _rms_unit(x):
    ms = jnp.mean(x * x, axis=-1, keepdims=True)
    return x * lax.rsqrt(ms + NORM_EPS)


def _adaln(x, gain, scale, shift):
    y = _per_sample(_rms_unit(x), gain * (1.0 + scale), jnp.multiply)
    return _per_sample(y, shift, jnp.add)


def _to_time_major(x):
    b, t, c = x.shape
    return jnp.swapaxes(x, 0, 1).reshape(t * b, c)


def _to_batch_major(x):
    r, c = x.shape
    return jnp.swapaxes(x.reshape(r // BATCH, BATCH, c), 0, 1)


def _per_sample(x, v, op):
    r, c = x.shape
    return op(x.reshape(r // BATCH, BATCH, c), v[None]).reshape(r, c)


def _full(shape, single=False):
    kw = dict(pipeline_mode=pl.Buffered(1)) if single else {}
    return pl.BlockSpec(shape, lambda *_: (0,) * len(shape), **kw)


MOD_TN = 3072


def _mod_kernel(c_ref, w_ref, b_ref, o_ref):
    c = c_ref[...]
    a = c * jax.nn.sigmoid(c)
    w = w_ref[0]
    ah = a.astype(BF16)
    al = (a - ah.astype(F32)).astype(BF16)
    wh = w.astype(BF16)
    wl = (w - wh.astype(F32)).astype(BF16)
    o_ref[0] = _dot(ah, wh) + _dot(ah, wl) + _dot(al, wh) + b_ref[0]


def _mod_call(cvec, mod_w, mod_b):
    n = N_MOD * D_MODEL
    return pl.pallas_call(
        _mod_kernel,
        out_shape=jax.ShapeDtypeStruct((DEPTH, 2 * BATCH, n), F32),
        grid=(DEPTH, n // MOD_TN),
        in_specs=[
            pl.BlockSpec((2 * BATCH, D_MODEL), lambda l, k: (0, 0)),
            pl.BlockSpec((1, D_MODEL, MOD_TN), lambda l, k: (l, 0, k)),
            pl.BlockSpec((1, 1, MOD_TN), lambda l, k: (l, 0, k)),
        ],
        out_specs=pl.BlockSpec((1, 2 * BATCH, MOD_TN), lambda l, k: (l, 0, k)),
        compiler_params=_cparams(("parallel", "parallel")),
        name="mod",
    )(cvec, mod_w, mod_b.reshape(DEPTH, 1, n))


def _mod_spec(col, n_latent_tiles):
    return pl.BlockSpec((1, BATCH, D_MODEL), lambda j: (jnp.where(j < n_latent_tiles, 0, 1), 0, col))


IN_ROWS = IN_T * BATCH
IN_NLAT = SEQ // IN_T
IN_N = L_TOT // IN_T
IN_SUB = 4
CAST_STEPS = 16
assert CAST_STEPS <= IN_N
IN_SUB_T = IN_T // IN_SUB
IN_SUB_ROWS = IN_SUB_T * BATCH


def _in_kernel(fu_ref, fd_ref, *refs, n_tb, first_layer, layer):
    fu_out, fd_out = refs[-2:]
    refs = refs[:-2]

    @pl.when(pl.program_id(0) < CAST_STEPS)
    def _():
        fu_out[...] = fu_ref[0].astype(BF16)
        fd_out[...] = fd_ref[0].astype(BF16)

    if first_layer:
        x_ref, c_ref, sh_ref, sc_ref, g_ref, w_ref, rope_ref, gq_ref, gk_ref, gm_ref = refs[:10]
        h_out = refs[10]
        outs = refs[11:]
        is_latent = pl.program_id(0) < IN_NLAT
    else:
        h_ref, sh_ref, sc_ref, g_ref, w_ref, rope_ref, gq_ref, gk_ref, gm_ref = refs[:9]
        outs = refs[9:]
    q_ref, k_ref, v_ref = outs[n_tb:]
    off = n_tb * MIX_W
    gm = gm_ref[...]
    gq = jnp.concatenate([gq_ref[...]] * (LANE // HEAD_DIM), axis=1)
    gk = jnp.concatenate([gk_ref[...]] * (LANE // HEAD_DIM), axis=1)
    for s in range(IN_SUB):
        ts = slice(s * IN_SUB_T, (s + 1) * IN_SUB_T)
        rs = slice(s * IN_SUB_ROWS, (s + 1) * IN_SUB_ROWS)
        if first_layer:
            h = _to_time_major(jnp.where(is_latent, x_ref[:, ts, :], c_ref[:, ts, :]))
            h_out[rs, :] = h
        else:
            h = h_ref[rs, :]
        xn = _adaln(h, g_ref[layer, 0:1, :], sc_ref[0], sh_ref[0])
        proj = _dot(xn.astype(BF16), w_ref[...])
        for i in range(n_tb):
            outs[i][rs, :] = proj[:, i * MIX_W:(i + 1) * MIX_W]
        qkv = _to_batch_major(proj[:, off:off + QKV_W])
        cos = rope_ref[ts, 0:LANE][None]
        sin_up = rope_ref[ts, LANE:2 * LANE][None]
        sin_dn = rope_ref[ts, 2 * LANE:3 * LANE][None]
        for p in range(5):
            blk = qkv[:, :, p * LANE:(p + 1) * LANE].reshape(IN_SUB_ROWS, LANE)
            sq = blk * blk
            hi = sq.astype(BF16)
            lo = (sq - hi.astype(F32)).astype(BF16)
            ms = _dot(hi, gm) + _dot(lo, gm)
            g = gq if p < 4 else gk
            bn = blk * lax.rsqrt(ms + NORM_EPS) * g
            up = pltpu.roll(bn, LANE - HEAD_DIM // 4, 1).reshape(BATCH, IN_SUB_T, LANE)
            dn = pltpu.roll(bn, HEAD_DIM // 4, 1).reshape(BATCH, IN_SUB_T, LANE)
            ro = bn.reshape(BATCH, IN_SUB_T, LANE) * cos + up * sin_up + dn * sin_dn
            if p < 4:
                q_ref[:, ts, p * LANE:(p + 1) * LANE] = (ro * Q_SCALE).astype(BF16)
            else:
                k_ref[:, ts, :] = ro.astype(BF16)
        v = qkv[:, :, 5 * LANE:6 * LANE].astype(BF16)
        v_ref[:, ts, :] = jnp.concatenate([v, jnp.ones((BATCH, IN_SUB_T, LANE), BF16)], axis=2)


def _in_call(h_args, modt, g, w, rope, gq, gk, gm, ffn_up, ffn_down, layer, n_tb, first_layer):
    n = w.shape[1]
    tb = lambda width: pl.BlockSpec((IN_ROWS, width), lambda j: (j, 0))
    bm = lambda width: pl.BlockSpec((BATCH, IN_T, width), lambda j: (0, j, 0))
    up_rows = D_MODEL // CAST_STEPS
    down_rows = FFN_HIDDEN // CAST_STEPS
    slab = lambda j: jnp.minimum(j, CAST_STEPS - 1)
    if first_layer:
        h_specs = [
            pl.BlockSpec((BATCH, IN_T, D_MODEL), lambda j: (0, jnp.minimum(j, IN_NLAT - 1), 0)),
            pl.BlockSpec((BATCH, IN_T, D_MODEL), lambda j: (0, jnp.maximum(j - IN_NLAT, 0), 0)),
        ]
        extra_shape = [jax.ShapeDtypeStruct((L_TOT * BATCH, D_MODEL), F32)]
        extra_spec = [tb(D_MODEL)]
    else:
        h_specs = [tb(D_MODEL)]
        extra_shape, extra_spec = [], []
    return pl.pallas_call(
        functools.partial(_in_kernel, n_tb=n_tb, first_layer=first_layer, layer=layer),
        out_shape=extra_shape + [jax.ShapeDtypeStruct((L_TOT * BATCH, MIX_W), F32)] * n_tb + [
            jax.ShapeDtypeStruct((BATCH, L_TOT, MIX_W), BF16),
            jax.ShapeDtypeStruct((BATCH, L_TOT, KV_W), BF16),
            jax.ShapeDtypeStruct((BATCH, L_TOT, V_W), BF16),
            jax.ShapeDtypeStruct((D_MODEL, 2 * FFN_HIDDEN), BF16),
            jax.ShapeDtypeStruct((FFN_HIDDEN, D_MODEL), BF16),
        ],
        grid=(IN_N,),
        in_specs=[
            pl.BlockSpec((1, up_rows, 2 * FFN_HIDDEN), lambda j: (layer, slab(j), 0)),
            pl.BlockSpec((1, down_rows, D_MODEL), lambda j: (layer, slab(j), 0)),
        ] + h_specs + [
            _mod_spec(0, IN_NLAT), _mod_spec(1, IN_NLAT),
            _full(g.shape),
            _full((D_MODEL, n), single=True),
            pl.BlockSpec((IN_T, 3 * LANE), lambda j: (j, 0)),
            _full(gq.shape), _full(gk.shape), _full((LANE, LANE)),
        ],
        out_specs=extra_spec + [tb(MIX_W)] * n_tb + [bm(MIX_W), bm(KV_W), bm(V_W)] + [
            pl.BlockSpec((up_rows, 2 * FFN_HIDDEN), lambda j: (slab(j), 0)),
            pl.BlockSpec((down_rows, D_MODEL), lambda j: (slab(j), 0)),
        ],
        compiler_params=_cparams(("arbitrary",)),
        name="in_proj",
    )(ffn_up, ffn_down, *h_args, modt, modt, g, w, rope, gq, gk, gm)


def _chunk_of(i, reverse, n_latent, n_chunks):
    if reverse:
        return n_chunks - 1 - i
    return jnp.where(i < n_chunks - n_latent, i + n_latent, i - (n_chunks - n_latent))


RG_ROWS = RG_T * BATCH
RG_NC = L_TOT // RG_T
RG_NLAT = SEQ // RG_T


def _softplus(z):
    return jnp.maximum(z, 0.0) + jnp.log1p(jnp.exp(-jnp.abs(z)))


def _rg_kernel(*refs, reverse):
    if reverse:
        (u_ref, up_ref, un_ref, cw_ref, cb_ref, w_ref, ba_ref, bx_ref, lam_ref,
         out_ref, uc_out, xbuf, abuf, bbuf, hst) = refs
    else:
        uc_ref, gate_ref, hb_ref, w_ref, ba_ref, bx_ref, lam_ref, out_ref, abuf, bbuf, hst = refs
    d = 1 if reverse else 0
    i = pl.program_id(0)

    @pl.when(i == 0)
    def _():
        hst[...] = jnp.zeros_like(hst)

    if reverse:
        c = _chunk_of(i, reverse, RG_NLAT, RG_NC)
        prev_zero = jnp.logical_or(c == 0, c == RG_NLAT)
        next_zero = jnp.logical_or(c == RG_NLAT - 1, c == RG_NC - 1)
        xbuf[0:BATCH, :] = up_ref[...] * jnp.where(prev_zero, 0.0, 1.0)
        xbuf[BATCH:BATCH + RG_ROWS, :] = u_ref[...]
        xbuf[BATCH + RG_ROWS:3 * BATCH + RG_ROWS, :] = un_ref[...] * jnp.where(next_zero, 0.0, 1.0)
        uc = cb_ref[...]
        for k in range(RG_CONV):
            uc = uc + xbuf[k * BATCH:k * BATCH + RG_ROWS, :] * cw_ref[0, k:k + 1, :]
        uc_out[...] = uc
    else:
        uc = uc_ref[...]

    ub = uc.astype(BF16)
    half = MIX_W // 2

    def gate(which, b_ref):
        w0 = which * 4 + d * 2
        z = jnp.concatenate([_dot(ub[:, :half], w_ref[w0]), _dot(ub[:, half:], w_ref[w0 + 1])], axis=1)
        return jax.nn.sigmoid(z + b_ref[0, d:d + 1, :])

    r = gate(0, ba_ref)
    ig = gate(1, bx_ref)
    log_a = (-RG_C) * r * _softplus(-lam_ref[0, d:d + 1, :])
    a = jnp.exp(log_a)
    abuf[...] = a
    bbuf[...] = jnp.sqrt(-jnp.tanh(log_a) * (a * a + 1.0)) * (ig * uc)

    h = hst[...]
    for s in range(RG_T):
        t = (RG_T - 1 - s) if reverse else s
        rows = slice(t * BATCH, (t + 1) * BATCH)
        h = abuf[rows, :] * h + bbuf[rows, :]
        bbuf[rows, :] = h
    hst[...] = h

    if reverse:
        out_ref[...] = bbuf[...]
    else:
        y = bbuf[...] + hb_ref[...]
        out_ref[...] = (y * _gelu_tanh(gate_ref[...])).astype(BF16)


def _rg_call(u, gate, hb, cw, cb, w_gates, ba, bx, lam, reverse):
    def cidx(i):
        return _chunk_of(i, reverse, RG_NLAT, RG_NC)

    main = pl.BlockSpec((RG_ROWS, MIX_W), lambda i: (cidx(i), 0))
    gates = [w_gates, ba, bx, lam]
    scratch = [pltpu.VMEM((RG_ROWS, MIX_W), F32), pltpu.VMEM((RG_ROWS, MIX_W), F32), pltpu.VMEM((BATCH, MIX_W), F32)]
    rows = jax.ShapeDtypeStruct((L_TOT * BATCH, MIX_W), F32)
    if reverse:
        prev = pl.BlockSpec((BATCH, MIX_W), lambda i: (jnp.maximum(cidx(i) * RG_T - 1, 0), 0))
        n_next = L_TOT // 2
        nxt = pl.BlockSpec((2 * BATCH, MIX_W), lambda i: (jnp.minimum((cidx(i) + 1) * (RG_T // 2), n_next - 1), 0))
        args = [u, u, u, cw, cb] + gates
        specs = [main, prev, nxt] + [_full(p.shape) for p in args[3:]]
        out_shape, out_specs = [rows, rows], [main, main]
        scratch = [pltpu.VMEM((RG_ROWS + 3 * BATCH, MIX_W), F32)] + scratch
    else:
        args = [u, gate, hb] + gates
        specs = [main, main, main] + [_full(p.shape) for p in gates]
        out_shape, out_specs = jax.ShapeDtypeStruct((L_TOT * BATCH, MIX_W), BF16), main
    return pl.pallas_call(
        functools.partial(_rg_kernel, reverse=reverse),
        out_shape=out_shape,
        grid=(RG_NC,),
        in_specs=specs,
        out_specs=out_specs,
        scratch_shapes=scratch,
        compiler_params=_cparams(("arbitrary",)),
        name="rglru_bwd" if reverse else "rglru_fwd",
    )(*args)


def _head_rows(q_ref, tq):
    lane = lax.broadcasted_iota(jnp.int32, (tq, LANE), 1)
    low = lane < HEAD_DIM
    zero = jnp.zeros((tq, LANE), BF16)
    parts = []
    for p in range(MIX_W // LANE):
        qb = q_ref[0, :, p * LANE:(p + 1) * LANE]
        parts.append(jnp.where(low, qb, zero))
        parts.append(jnp.where(low, zero, qb))
    return parts, low


def _merge_heads(outs, low):
    return jnp.concatenate([jnp.where(low, outs[2 * p], outs[2 * p + 1]) for p in range(MIX_W // LANE)], axis=1)


def _qk(q, k):
    return lax.dot_general(q, k, (((1,), (1,)), ((), ())), preferred_element_type=F32)


def _gattn_kernel(q_ref, k_ref, v_ref, o_ref):
    j = pl.program_id(1)
    heads, low = _head_rows(q_ref, TQ_G)

    def attend(k0, nk):
        outs = []
        for q in heads:
            s = _qk(q, k_ref[0, k0:k0 + nk, :])
            m = jnp.max(s, axis=-1, keepdims=True)
            p = jnp.exp2(s - m).astype(BF16)
            acc = _dot(p, v_ref[0, k0:k0 + nk, :])
            outs.append(acc[:, :KV_W] / acc[:, KV_W:])
        o_ref[0] = _merge_heads(outs, low).astype(BF16)

    @pl.when(j < SEQ // TQ_G)
    def _():
        attend(0, L_TOT)

    @pl.when(j >= SEQ // TQ_G)
    def _():
        attend(SEQ, CTX_LEN)


def _gattn_call(q, k, v):
    return pl.pallas_call(
        _gattn_kernel,
        out_shape=jax.ShapeDtypeStruct((BATCH, L_TOT, MIX_W), BF16),
        grid=(BATCH, L_TOT // TQ_G),
        in_specs=[
            pl.BlockSpec((1, TQ_G, MIX_W), lambda b, j: (b, j, 0)),
            pl.BlockSpec((1, L_TOT, KV_W), lambda b, j: (b, 0, 0)),
            pl.BlockSpec((1, L_TOT, V_W), lambda b, j: (b, 0, 0)),
        ],
        out_specs=pl.BlockSpec((1, TQ_G, MIX_W), lambda b, j: (b, j, 0)),
        compiler_params=_cparams(("parallel", "arbitrary")),
        name="global_attn",
    )(q, k, v)


TQ_W = 2 * WINDOW
N_WBLK = SEQ // WINDOW
N_WSTEP = SEQ // TQ_W


def _wattn_kernel(q_ref, k_ref, v_ref, sink_ref, o_ref):
    i = pl.program_id(1)
    heads, low = _head_rows(q_ref, TQ_W)
    prev_start = pl.multiple_of(jnp.maximum(2 * i - 1, 0) * WINDOW, WINDOW)
    mid_start = pl.multiple_of(i * TQ_W, TQ_W)
    next_start = pl.multiple_of(jnp.minimum(2 * i + 2, N_WBLK - 1) * WINDOW, WINDOW)

    def rows(ref):
        return jnp.concatenate([ref[0, SEQ:L_TOT, :], ref[0, pl.ds(prev_start, WINDOW), :],
                                ref[0, pl.ds(mid_start, TQ_W), :], ref[0, pl.ds(next_start, WINDOW), :]], axis=0)

    nk = CTX_LEN + 2 * WINDOW + TQ_W
    r = lax.broadcasted_iota(jnp.int32, (TQ_W, nk), 0)
    col = lax.broadcasted_iota(jnp.int32, (TQ_W, nk), 1)
    c = col - CTX_LEN
    ninf = -jnp.inf
    pen_prev = jnp.where(i > 0, 0.0, ninf)
    pen_next = jnp.where(i < N_WSTEP - 1, 0.0, ninf)
    edge = jnp.where(c < WINDOW, pen_prev, jnp.where(c >= WINDOW + TQ_W, pen_next, 0.0))
    band = jnp.where(c >= r, jnp.where(c <= r + 2 * WINDOW, edge, ninf), ninf)
    bias = jnp.where(col < CTX_LEN, 0.0, band)
    kk = rows(k_ref)
    vv = rows(v_ref)
    outs = []
    for hd, q in enumerate(heads):
        s = _qk(q, kk) + bias
        sink = sink_ref[0, HEAD_PERM[hd]] * math.log2(math.e)
        m = jnp.maximum(jnp.max(s, axis=-1, keepdims=True), sink)
        p = jnp.exp2(s - m).astype(BF16)
        acc = _dot(p, vv)
        outs.append(acc[:, :KV_W] / (acc[:, KV_W:] + jnp.exp2(sink - m)))
    o_ref[0] = _merge_heads(outs, low).astype(BF16)


def _wattn_call(q, k, v, sink):
    return pl.pallas_call(
        _wattn_kernel,
        out_shape=jax.ShapeDtypeStruct((BATCH, SEQ, MIX_W), BF16),
        grid=(BATCH, N_WSTEP),
        in_specs=[
            pl.BlockSpec((1, TQ_W, MIX_W), lambda b, i: (b, i, 0)),
            pl.BlockSpec((1, L_TOT, KV_W), lambda b, i: (b, 0, 0)),
            pl.BlockSpec((1, L_TOT, V_W), lambda b, i: (b, 0, 0)),
            pl.BlockSpec(memory_space=pltpu.SMEM),
        ],
        out_specs=pl.BlockSpec((1, TQ_W, MIX_W), lambda b, i: (b, i, 0)),
        compiler_params=_cparams(("parallel", "arbitrary")),
        name="window_attn",
    )(q, k, v, sink)


def _s5_disc_kernel(lr_ref, li_ref, ls_ref, br_ref, bi_ref, ar_ref, ai_ref, bbr_ref, bbi_ref):
    lr = lr_ref[0]
    li = li_ref[0]
    dt = jnp.exp(ls_ref[0])
    mag = jnp.exp(lr * dt)
    ang = li * dt
    abr = mag * jnp.cos(ang)
    abi = mag * jnp.sin(ang)
    den = lr * lr + li * li
    nr = abr - 1.0
    kr = (nr * lr + abi * li) / den
    ki = (abi * lr - nr * li) / den
    br = br_ref[0]
    bi = bi_ref[0]
    ar_ref[0] = abr
    ai_ref[0] = abi
    bbr_ref[0] = kr * br - ki * bi
    bbi_ref[0] = kr * bi + ki * br


def _s5_disc_call(lam_re, lam_im, log_step, bt_re, bt_im):
    n = S5_GROUPS * S5_STATE
    row = pl.BlockSpec((1, 1, n), lambda d: (d, 0, 0))
    mat = pl.BlockSpec((1, S5_GROUP, n), lambda d: (d, 0, 0))
    return pl.pallas_call(
        _s5_disc_kernel,
        out_shape=[jax.ShapeDtypeStruct((2, 1, n), F32)] * 2 + [jax.ShapeDtypeStruct((2, S5_GROUP, n), F32)] * 2,
        grid=(2,),
        in_specs=[row, row, row, mat, mat],
        out_specs=[row, row, mat, mat],
        compiler_params=_cparams(("parallel",)),
        name="s5_discretize",
    )(lam_re, lam_im, log_step, bt_re, bt_im)


S5_ROWS = S5_T * BATCH
S5_NC = L_TOT // S5_T
S5_NLAT = SEQ // S5_T
S5_SW = 2 * S5_BLK_STATE * S5_NBLK


def _s5_kernel(*refs, reverse):
    if reverse:
        u2_ref, un_ref, wd_ref, ar_ref, ai_ref, wr_ref, out_ref, buf0, buf1, xst = refs
    else:
        (u2_ref, un_ref, yb2_ref, wd_ref, ar_ref, ai_ref, wr_ref, dsk_ref, gw_ref, gb_ref,
         out_ref, buf0, buf1, xst) = refs
    nb = 2 * S5_BLK_STATE
    lo, hi = slice(0, S5_ROWS), slice(S5_ROWS, 2 * S5_ROWS)
    first, second = (hi, lo) if reverse else (lo, hi)

    def drive(u, buf):
        ub = u.astype(BF16)
        for k in range(S5_NBLK):
            buf[:, k * nb:(k + 1) * nb] = _dot(ub[:, k * LANE:(k + 1) * LANE], wd_ref[0, k])

    def scan(buf, x):
        for s in range(S5_T):
            t = (S5_T - 1 - s) if reverse else s
            rows = slice(t * BATCH, (t + 1) * BATCH)
            halves = []
            for k in range(S5_NBLK):
                halves.append(x[:, k * nb + S5_BLK_STATE:(k + 1) * nb])
                halves.append(x[:, k * nb:k * nb + S5_BLK_STATE])
            xsw = jnp.concatenate(halves, axis=1)
            x = ar_ref[0] * x + ai_ref[0] * xsw + buf[rows, :]
            buf[rows, :] = x
        return x

    def readout(buf, rows):
        y = jnp.concatenate(
            [_dot(buf[:, k * nb:(k + 1) * nb].astype(BF16), wr_ref[0, k]) for k in range(S5_NBLK)], axis=1)
        if reverse:
            out_ref[rows, :] = y
        else:
            y = u2_ref[rows, :] * dsk_ref[...] + y + yb2_ref[rows, :]
            z = _gelu_tanh(y)
            out_ref[rows, :] = (z * jax.nn.sigmoid(_dot(z.astype(BF16), gw_ref[...]) + gb_ref[...])).astype(BF16)

    @pl.when(pl.program_id(0) == 0)
    def _():
        xst[...] = jnp.zeros_like(xst)
        drive(u2_ref[first, :], buf0)

    x = xst[...]
    drive(u2_ref[second, :], buf1)
    x = scan(buf0, x)
    readout(buf0, first)
    drive(un_ref[...], buf0)
    x = scan(buf1, x)
    readout(buf1, second)
    xst[...] = x


def _s5_call(u, yb, wd, ar, ai, wr, dsk, gw, gb, reverse):
    def chunk(p):
        return _chunk_of(p, reverse, S5_NLAT, S5_NC)

    pair = pl.BlockSpec((2 * S5_ROWS, MIX_W), lambda k: (chunk(2 * k) // 2, 0))
    nxt = pl.BlockSpec((S5_ROWS, MIX_W), lambda k: (chunk(jnp.minimum(2 * k + 2, S5_NC - 1)), 0))
    d = 1 if reverse else 0
    dir_specs = [pl.BlockSpec((1,) + a.shape[1:], lambda k, n=a.ndim: (d,) + (0,) * (n - 1)) for a in (wd, ar, ai, wr)]
    if reverse:
        args = [u, u, wd, ar, ai, wr]
        specs = [pair, nxt] + dir_specs
        out_dtype = F32
    else:
        args = [u, u, yb, wd, ar, ai, wr, dsk, gw, gb]
        specs = [pair, nxt, pair] + dir_specs + [_full(a.shape) for a in (dsk, gw, gb)]
        out_dtype = BF16
    return pl.pallas_call(
        functools.partial(_s5_kernel, reverse=reverse),
        out_shape=jax.ShapeDtypeStruct((L_TOT * BATCH, MIX_W), out_dtype),
        grid=(S5_NC // 2,),
        in_specs=specs,
        out_specs=pair,
        scratch_shapes=[
            pltpu.VMEM((S5_ROWS, S5_SW), F32),
            pltpu.VMEM((S5_ROWS, S5_SW), F32),
            pltpu.VMEM((BATCH, S5_SW), F32),
        ],
        compiler_params=_cparams(("arbitrary",)),
        name="s5_bwd" if reverse else "s5_fwd",
    )(*args)


POST_ROWS = POST_T * BATCH
POST_HALO_T = BF16_ROWS // BATCH
POST_HALO = POST_HALO_T * BATCH
POST_NLAT = SEQ // POST_T
FFN_NCH = FFN_HIDDEN // FFN_CH


def _post_kernel(h_ref, hp_ref, hn_ref, a_ref, ap_ref, an_ref, b_ref, bp_ref, bn_ref, wo_ref,
                 g2_ref, sh_ref, sc_ref, g5_ref, g_ref, wu_ref, cw_ref, cb_ref, wd_ref, o_ref, *,
                 n_tiles, out_batch_major, layer):
    j = pl.program_id(0)
    first = jnp.logical_or(j == 0, j == POST_NLAT)
    last = jnp.logical_or(j == POST_NLAT - 1, j == n_tiles - 1)
    gate2 = g2_ref[0]

    def mix(h, a, b_tm):
        m = _dot(a, wo_ref[0:MIX_W, :]) + _dot(b_tm.astype(BF16), wo_ref[MIX_W:2 * MIX_W, :])
        return h + _per_sample(m, gate2, jnp.multiply)

    def norm(x):
        return _adaln(x, g_ref[layer, 1:2, :], sc_ref[0], sh_ref[0])

    def halo_b(ref, t0):
        x = jnp.swapaxes(ref[...].astype(F32), 0, 1)[t0:t0 + POST_HALO_T]
        return x.reshape(POST_HALO, MIX_W)

    h_mid = mix(h_ref[...], a_ref[...], _to_time_major(b_ref[...].astype(F32)))
    h_prev = mix(hp_ref[...], ap_ref[...], halo_b(bp_ref, BF16_ROWS - POST_HALO_T))
    h_next = mix(hn_ref[...], an_ref[...], halo_b(bn_ref, 0))
    xn = jnp.concatenate([
        (norm(h_prev) * jnp.where(first, 0.0, 1.0)).astype(BF16),
        norm(h_mid).astype(BF16),
        (norm(h_next) * jnp.where(last, 0.0, 1.0)).astype(BF16)], axis=0)

    acts = []
    for c in range(FFN_NCH):
        cv = slice(c * FFN_CH, (c + 1) * FFN_CH)
        cg = slice(FFN_HIDDEN + c * FFN_CH, FFN_HIDDEN + (c + 1) * FFN_CH)

        def conv(cols):
            hid = _dot(xn, wu_ref[:, cols])
            out = cb_ref[layer:layer + 1, cols]
            for k in range(FFN_CONV):
                r0 = POST_HALO + (k - 1) * BATCH
                out = out + hid[r0:r0 + POST_ROWS] * cw_ref[0, k:k + 1, cols]
            return out

        val = conv(cv)
        gate = conv(cg)
        acts.append((val * (gate * jax.nn.sigmoid(gate))).astype(BF16))
    ffn = _dot(jnp.concatenate(acts, axis=1), wd_ref[...])
    out = h_mid + _per_sample(ffn, g5_ref[0], jnp.multiply)
    if out_batch_major:
        o_ref[...] = _to_batch_major(out)
    else:
        o_ref[...] = out


def _post_call(h, a_tb, b_out, w_out, modt, g, wu, cw, cb, wd, layer, n_tiles, out_batch_major):
    per = POST_ROWS // POST_HALO
    n_halo = h.shape[0] // POST_HALO
    per_b = POST_T // BF16_ROWS
    n_halo_b = b_out.shape[1] // BF16_ROWS

    def main(width):
        return pl.BlockSpec((POST_ROWS, width), lambda j: (j, 0))

    def prev(width):
        return pl.BlockSpec((POST_HALO, width), lambda j: (jnp.maximum(j * per - 1, 0), 0))

    def nxt(width):
        return pl.BlockSpec((POST_HALO, width), lambda j: (jnp.minimum((j + 1) * per, n_halo - 1), 0))

    if out_batch_major:
        out_shape = jax.ShapeDtypeStruct((BATCH, n_tiles * POST_T, D_MODEL), F32)
        out_spec = pl.BlockSpec((BATCH, POST_T, D_MODEL), lambda j: (0, j, 0))
    else:
        out_shape = jax.ShapeDtypeStruct((n_tiles * POST_ROWS, D_MODEL), F32)
        out_spec = main(D_MODEL)
    return pl.pallas_call(
        functools.partial(_post_kernel, n_tiles=n_tiles, out_batch_major=out_batch_major, layer=layer),
        out_shape=out_shape,
        grid=(n_tiles,),
        in_specs=[
            main(D_MODEL), prev(D_MODEL), nxt(D_MODEL),
            main(MIX_W), prev(MIX_W), nxt(MIX_W),
            pl.BlockSpec((BATCH, POST_T, MIX_W), lambda j: (0, j, 0)),
            pl.BlockSpec((BATCH, BF16_ROWS, MIX_W), lambda j: (0, jnp.maximum(j * per_b - 1, 0), 0)),
            pl.BlockSpec((BATCH, BF16_ROWS, MIX_W), lambda j: (0, jnp.minimum((j + 1) * per_b, n_halo_b - 1), 0)),
            _full((2 * MIX_W, D_MODEL), single=True),
            _mod_spec(2, POST_NLAT), _mod_spec(3, POST_NLAT), _mod_spec(4, POST_NLAT), _mod_spec(5, POST_NLAT),
            _full(g.shape),
            _full((D_MODEL, 2 * FFN_HIDDEN), single=True),
            pl.BlockSpec((1, FFN_CONV, 2 * FFN_HIDDEN), lambda j: (layer, 0, 0)),
            _full(cb.shape),
            _full((FFN_HIDDEN, D_MODEL), single=True),
        ],
        out_specs=out_spec,
        compiler_params=_cparams(("parallel",)),
        name="post_ffn",
    )(h, h, h, a_tb, a_tb, a_tb, b_out, b_out, b_out, w_out, modt, modt, modt, modt, g, wu, cw, cb, wd)


def _rope_table():
    rows = SEQ // GRID_W
    row = jnp.repeat(jnp.arange(rows, dtype=F32), GRID_W)
    col = jnp.tile(jnp.arange(GRID_W, dtype=F32), rows)
    quarter = HEAD_DIM // 4
    inv_freq = ROPE_BASE ** (-jnp.arange(quarter, dtype=F32) / quarter)
    ang = jnp.stack([row[:, None] * inv_freq, col[:, None] * inv_freq], axis=1)
    cos = jnp.cos(ang)
    sin = jnp.sin(ang)
    zero = jnp.zeros_like(sin)

    def lanes(first, second):
        t = jnp.stack([first, second], axis=2).reshape(SEQ, HEAD_DIM)
        return jnp.tile(t, (1, LANE // HEAD_DIM))

    tab = jnp.concatenate([lanes(cos, cos), lanes(-sin, zero), lanes(zero, sin)], axis=1)
    ctx = jnp.concatenate([jnp.ones((CTX_LEN, LANE), F32), jnp.zeros((CTX_LEN, 2 * LANE), F32)], axis=1)
    return jnp.concatenate([tab, ctx], axis=0)


def _perm_heads_cols(w):
    d = w.shape[0]
    return w.reshape(d, N_HEADS, HEAD_DIM)[:, jnp.array(HEAD_PERM)].reshape(d, N_HEADS * HEAD_DIM)


def _perm_heads_rows(w):
    n = w.shape[1]
    return w.reshape(N_HEADS, HEAD_DIM, n)[jnp.array(HEAD_PERM)].reshape(N_HEADS * HEAD_DIM, n)


def _block_diag(w, per):
    n, a, b = w.shape
    eye = jnp.eye(per, dtype=w.dtype)
    w4 = w.reshape(n // per, per, a, b)
    return jnp.einsum("ihab,hk->ihakb", w4, eye).reshape(n // per, per * a, per * b)


def _w_in(w, q_off):
    return jnp.concatenate([w[:, :q_off], _perm_heads_cols(w[:, q_off:q_off + MIX_W]), w[:, q_off + MIX_W:]],
                           axis=1).astype(BF16)


def _w_out(w):
    return jnp.concatenate([w[:MIX_W], _perm_heads_rows(w[MIX_W:])], axis=0).astype(BF16)


def kernel(x, c, ctx, c_ctx, mod_w, mod_b, norm_g, ffn_up, ffn_conv_w, ffn_conv_b, ffn_down, ev_w_in, ev_w_out, rg_conv_w, rg_conv_b, rg_wa, rg_ba, rg_wx, rg_bx, rg_lam, ga_qn, ga_kn, od_w_in, od_w_out, s5_lam_re, s5_lam_im, s5_log_step, s5_b_re, s5_b_im, s5_c_re, s5_c_im, s5_d, s5_glu_w, s5_glu_b, wa_qn, wa_kn, wa_sink):
    cvec = jnp.concatenate([c, jnp.broadcast_to(c_ctx[None], (BATCH, D_MODEL))], axis=0)
    mod = _mod_call(cvec, mod_w, mod_b).reshape(DEPTH, 2, BATCH, N_MOD * D_MODEL)
    rope = _rope_table()
    gm = _block_diag(jnp.full((LANE // HEAD_DIM, HEAD_DIM, HEAD_DIM), 1.0 / HEAD_DIM, F32), LANE // HEAD_DIM)[0].astype(BF16)

    h, u_tb, gate_tb, q, k, v, wu, wd = _in_call(
        (x, ctx), mod[0], norm_g, _w_in(ev_w_in[0], 2 * MIX_W), rope,
        ga_qn, ga_kn, gm, ffn_up, ffn_down, 0, n_tb=2, first_layer=True)
    w_gates = _block_diag(jnp.stack([rg_wa[0], rg_wx[0]]).reshape(-1, HEAD_DIM, HEAD_DIM), 4).astype(BF16)
    rg = (rg_conv_w, rg_conv_b, w_gates, rg_ba, rg_bx, rg_lam)
    hb, uc = _rg_call(u_tb, None, None, *rg, reverse=True)
    a_out = _rg_call(uc, gate_tb, hb, *rg, reverse=False)
    b_out = _gattn_call(q, k, v)
    h = _post_call(h, a_out, b_out, _w_out(ev_w_out[0]), mod[0], norm_g, wu, ffn_conv_w, ffn_conv_b, wd, 0,
                   n_tiles=L_TOT // POST_T, out_batch_major=False)

    u_tb, q, k, v, wu, wd = _in_call(
        (h,), mod[1], norm_g, _w_in(od_w_in[0], MIX_W), rope,
        wa_qn, wa_kn, gm, ffn_up, ffn_down, 1, n_tb=1, first_layer=False)
    n_state = S5_GROUPS * S5_STATE
    bt_re = s5_b_re[0].transpose(0, 3, 1, 2).reshape(2, S5_GROUP, n_state)
    bt_im = s5_b_im[0].transpose(0, 3, 1, 2).reshape(2, S5_GROUP, n_state)
    log_step = jnp.repeat(s5_log_step[0], S5_STATE, axis=-1).reshape(2, 1, n_state)
    abr, abi, bbr, bbi = _s5_disc_call(s5_lam_re[0].reshape(2, 1, n_state), s5_lam_im[0].reshape(2, 1, n_state),
                                       log_step, bt_re, bt_im)
    gpb = S5_GROUPS // S5_NBLK
    eye = jnp.eye(gpb, dtype=F32)
    bb = jnp.stack([bbr, bbi]).reshape(2, 2, S5_GROUP, S5_NBLK, gpb, S5_STATE)
    w_drive = jnp.einsum("rdhbgp,gk->dbghrkp", bb, eye).reshape(
        2, S5_NBLK, gpb * S5_GROUP, 2 * S5_BLK_STATE).astype(BF16)
    cc = jnp.stack([s5_c_re[0], -s5_c_im[0]]).reshape(2, 2, S5_NBLK, gpb, S5_GROUP, S5_STATE)
    w_read = jnp.einsum("rdbghp,gk->dbrgpkh", cc, eye).reshape(
        2, S5_NBLK, 2 * S5_BLK_STATE, gpb * S5_GROUP).astype(BF16)

    def state_rows(re, im):
        row = jnp.concatenate([re.reshape(2, S5_NBLK, S5_BLK_STATE), im.reshape(2, S5_NBLK, S5_BLK_STATE)],
                              axis=2).reshape(2, 1, S5_SW)
        return jnp.broadcast_to(row, (2, BATCH, S5_SW))

    s5 = (w_drive, state_rows(abr, abr), state_rows(-abi, abi), w_read)
    yb = _s5_call(u_tb, None, *s5, None, None, None, reverse=True)
    c_out = _s5_call(u_tb, yb, *s5, s5_d, s5_glu_w[0].astype(BF16), s5_glu_b, reverse=False)
    d_out = _wattn_call(q, k, v, wa_sink)
    return _post_call(h, c_out, d_out, _w_out(od_w_out[0]), mod[1], norm_g, wu, ffn_conv_w, ffn_conv_b, wd, 1,
                      n_tiles=SEQ // POST_T, out_batch_major=True)
```

```python
import functools
import math

import jax
import jax.numpy as jnp
from jax import lax
from jax.experimental import pallas as pl
from jax.experimental.pallas import tpu as pltpu

F32 = jnp.float32
BF16 = jnp.bfloat16

D_MODEL = 1024
BATCH = 8
SEQ = 2048
CTX_LEN = 256
L_TOT = SEQ + CTX_LEN
DEPTH = 2
GRID_W = 64
HEAD_DIM = 64
ROPE_BASE = 10000.0
NORM_EPS = 1e-6
WINDOW = 128
N_MOD = 6
MIX_W = 512
N_HEADS = 8
N_KV = 2
KV_W = N_KV * HEAD_DIM
V_W = 2 * KV_W
QKV_W = MIX_W + KV_W + KV_W
Q_SCALE = HEAD_DIM ** -0.5 * math.log2(math.e)
RG_CONV = 4
RG_C = 8.0
S5_GROUP = 16
S5_GROUPS = 32
S5_STATE = 64
S5_NBLK = 4
S5_BLK_STATE = (S5_GROUPS // S5_NBLK) * S5_STATE
FFN_HIDDEN = 2816
FFN_CONV = 3

LANE = 128
BF16_ROWS = 16
VMEM_LIMIT = 56 * 1024 * 1024

IN_T = 128
RG_T = 256
S5_T = 64
POST_T = 128
TQ_G = 256
FFN_CH = 256
HEAD_PERM = (0, 4, 1, 5, 2, 6, 3, 7)


def _cparams(sem):
    return pltpu.CompilerParams(dimension_semantics=sem, vmem_limit_bytes=VMEM_LIMIT)


def _dot(a, b):
    return jnp.dot(a, b, preferred_element_type=F32)


def _gelu_tanh(x):
    return x * (0.5 * (1.0 + jnp.tanh(math.sqrt(2.0 / math.pi) * (x + 0.044715 * (x * x * x)))))


def _rms_unit(x):
    ms = jnp.mean(x * x, axis=-1, keepdims=True)
    return x * lax.rsqrt(ms + NORM_EPS)


def _adaln(x, gain, scale, shift):
    y = _per_sample(_rms_unit(x), gain * (1.0 + scale), jnp.multiply)
    return _per_sample(y, shift, jnp.add)


def _to_time_major(x):
    b, t, c = x.shape
    return jnp.swapaxes(x, 0, 1).reshape(t * b, c)


def _to_batch_major(x):
    r, c = x.shape
    return jnp.swapaxes(x.reshape(r // BATCH, BATCH, c), 0, 1)


def _per_sample(x, v, op):
    r, c = x.shape
    return op(x.reshape(r // BATCH, BATCH, c), v[None]).reshape(r, c)


def _full(shape, single=False):
    kw = dict(pipeline_mode=pl.Buffered(1)) if single else {}
    return pl.BlockSpec(shape, lambda *_: (0,) * len(shape), **kw)


MOD_TN = 3072


def _mod_kernel(c_ref, w_ref, b_ref, o_ref):
    c = c_ref[...]
    a = c * jax.nn.sigmoid(c)
    w = w_ref[0]
    ah = a.astype(BF16)
    al = (a - ah.astype(F32)).astype(BF16)
    wh = w.astype(BF16)
    wl = (w - wh.astype(F32)).astype(BF16)
    o_ref[0] = _dot(ah, wh) + _dot(ah, wl) + _dot(al, wh) + b_ref[0]


def _mod_call(cvec, mod_w, mod_b):
    n = N_MOD * D_MODEL
    return pl.pallas_call(
        _mod_kernel,
        out_shape=jax.ShapeDtypeStruct((DEPTH, 2 * BATCH, n), F32),
        grid=(DEPTH, n // MOD_TN),
        in_specs=[
            pl.BlockSpec((2 * BATCH, D_MODEL), lambda l, k: (0, 0)),
            pl.BlockSpec((1, D_MODEL, MOD_TN), lambda l, k: (l, 0, k)),
            pl.BlockSpec((1, 1, MOD_TN), lambda l, k: (l, 0, k)),
        ],
        out_specs=pl.BlockSpec((1, 2 * BATCH, MOD_TN), lambda l, k: (l, 0, k)),
        compiler_params=_cparams(("parallel", "parallel")),
        name="mod",
    )(cvec, mod_w, mod_b.reshape(DEPTH, 1, n))


def _mod_spec(col, n_latent_tiles):
    return pl.BlockSpec((1, BATCH, D_MODEL), lambda j: (jnp.where(j < n_latent_tiles, 0, 1), 0, col))


IN_ROWS = IN_T * BATCH
IN_NLAT = SEQ // IN_T
IN_N = L_TOT // IN_T
IN_SUB = 2
CAST_STEPS = 16
assert CAST_STEPS <= IN_N
IN_SUB_T = IN_T // IN_SUB
IN_SUB_ROWS = IN_SUB_T * BATCH


def _in_kernel(fu_ref, fd_ref, *refs, n_tb, first_layer, layer):
    fu_out, fd_out = refs[-2:]
    refs = refs[:-2]

    @pl.when(pl.program_id(0) < CAST_STEPS)
    def _():
        fu_out[...] = fu_ref[0].astype(BF16)
        fd_out[...] = fd_ref[0].astype(BF16)

    if first_layer:
        x_ref, c_ref, sh_ref, sc_ref, g_ref, w_ref, rope_ref, gq_ref, gk_ref, gm_ref = refs[:10]
        h_out = refs[10]
        outs = refs[11:]
        is_latent = pl.program_id(0) < IN_NLAT
    else:
        h_ref, sh_ref, sc_ref, g_ref, w_ref, rope_ref, gq_ref, gk_ref, gm_ref = refs[:9]
        outs = refs[9:]
    q_ref, k_ref, v_ref = outs[n_tb:]
    off = n_tb * MIX_W
    gm = gm_ref[...]
    gq = jnp.concatenate([gq_ref[...]] * (LANE // HEAD_DIM), axis=1)
    gk = jnp.concatenate([gk_ref[...]] * (LANE // HEAD_DIM), axis=1)
    for s in range(IN_SUB):
        ts = slice(s * IN_SUB_T, (s + 1) * IN_SUB_T)
        rs = slice(s * IN_SUB_ROWS, (s + 1) * IN_SUB_ROWS)
        if first_layer:
            h = _to_time_major(jnp.where(is_latent, x_ref[:, ts, :], c_ref[:, ts, :]))
            h_out[rs, :] = h
        else:
            h = h_ref[rs, :]
        xn = _adaln(h, g_ref[layer, 0:1, :], sc_ref[0], sh_ref[0])
        proj = _dot(xn.astype(BF16), w_ref[...])
        for i in range(n_tb):
            outs[i][rs, :] = proj[:, i * MIX_W:(i + 1) * MIX_W]
        qkv = _to_batch_major(proj[:, off:off + QKV_W])
        cos = rope_ref[ts, 0:LANE][None]
        sin_up = rope_ref[ts, LANE:2 * LANE][None]
        sin_dn = rope_ref[ts, 2 * LANE:3 * LANE][None]
        for p in range(5):
            blk = qkv[:, :, p * LANE:(p + 1) * LANE].reshape(IN_SUB_ROWS, LANE)
            sq = blk * blk
            hi = sq.astype(BF16)
            lo = (sq - hi.astype(F32)).astype(BF16)
            ms = _dot(hi, gm) + _dot(lo, gm)
            g = gq if p < 4 else gk
            bn = blk * lax.rsqrt(ms + NORM_EPS) * g
            up = pltpu.roll(bn, LANE - HEAD_DIM // 4, 1).reshape(BATCH, IN_SUB_T, LANE)
            dn = pltpu.roll(bn, HEAD_DIM // 4, 1).reshape(BATCH, IN_SUB_T, LANE)
            ro = bn.reshape(BATCH, IN_SUB_T, LANE) * cos + up * sin_up + dn * sin_dn
            if p < 4:
                q_ref[:, ts, p * LANE:(p + 1) * LANE] = (ro * Q_SCALE).astype(BF16)
            else:
                k_ref[:, ts, :] = ro.astype(BF16)
        v = qkv[:, :, 5 * LANE:6 * LANE].astype(BF16)
        v_ref[:, ts, :] = jnp.concatenate([v, jnp.ones((BATCH, IN_SUB_T, LANE), BF16)], axis=2)


def _in_call(h_args, modt, g, w, rope, gq, gk, gm, ffn_up, ffn_down, layer, n_tb, first_layer):
    n = w.shape[1]
    tb = lambda width: pl.BlockSpec((IN_ROWS, width), lambda j: (j, 0))
    bm = lambda width: pl.BlockSpec((BATCH, IN_T, width), lambda j: (0, j, 0))
    up_rows = D_MODEL // CAST_STEPS
    down_rows = FFN_HIDDEN // CAST_STEPS
    slab = lambda j: jnp.minimum(j, CAST_STEPS - 1)
    if first_layer:
        h_specs = [
            pl.BlockSpec((BATCH, IN_T, D_MODEL), lambda j: (0, jnp.minimum(j, IN_NLAT - 1), 0)),
            pl.BlockSpec((BATCH, IN_T, D_MODEL), lambda j: (0, jnp.maximum(j - IN_NLAT, 0), 0)),
        ]
        extra_shape = [jax.ShapeDtypeStruct((L_TOT * BATCH, D_MODEL), F32)]
        extra_spec = [tb(D_MODEL)]
    else:
        h_specs = [tb(D_MODEL)]
        extra_shape, extra_spec = [], []
    return pl.pallas_call(
        functools.partial(_in_kernel, n_tb=n_tb, first_layer=first_layer, layer=layer),
        out_shape=extra_shape + [jax.ShapeDtypeStruct((L_TOT * BATCH, MIX_W), F32)] * n_tb + [
            jax.ShapeDtypeStruct((BATCH, L_TOT, MIX_W), BF16),
            jax.ShapeDtypeStruct((BATCH, L_TOT, KV_W), BF16),
            jax.ShapeDtypeStruct((BATCH, L_TOT, V_W), BF16),
            jax.ShapeDtypeStruct((D_MODEL, 2 * FFN_HIDDEN), BF16),
            jax.ShapeDtypeStruct((FFN_HIDDEN, D_MODEL), BF16),
        ],
        grid=(IN_N,),
        in_specs=[
            pl.BlockSpec((1, up_rows, 2 * FFN_HIDDEN), lambda j: (layer, slab(j), 0)),
            pl.BlockSpec((1, down_rows, D_MODEL), lambda j: (layer, slab(j), 0)),
        ] + h_specs + [
            _mod_spec(0, IN_NLAT), _mod_spec(1, IN_NLAT),
            _full(g.shape),
            _full((D_MODEL, n), single=True),
            pl.BlockSpec((IN_T, 3 * LANE), lambda j: (j, 0)),
            _full(gq.shape), _full(gk.shape), _full((LANE, LANE)),
        ],
        out_specs=extra_spec + [tb(MIX_W)] * n_tb + [bm(MIX_W), bm(KV_W), bm(V_W)] + [
            pl.BlockSpec((up_rows, 2 * FFN_HIDDEN), lambda j: (slab(j), 0)),
            pl.BlockSpec((down_rows, D_MODEL), lambda j: (slab(j), 0)),
        ],
        compiler_params=_cparams(("arbitrary",)),
        name="in_proj",
    )(ffn_up, ffn_down, *h_args, modt, modt, g, w, rope, gq, gk, gm)


def _chunk_of(i, reverse, n_latent, n_chunks):
    if reverse:
        return n_chunks - 1 - i
    return jnp.where(i < n_chunks - n_latent, i + n_latent, i - (n_chunks - n_latent))


RG_ROWS = RG_T * BATCH
RG_NC = L_TOT // RG_T
RG_NLAT = SEQ // RG_T


def _softplus(z):
    return jnp.maximum(z, 0.0) + jnp.log1p(jnp.exp(-jnp.abs(z)))


def _rg_kernel(*refs, reverse):
    if reverse:
        (u_ref, up_ref, un_ref, cw_ref, cb_ref, w_ref, ba_ref, bx_ref, lam_ref,
         out_ref, uc_out, xbuf, abuf, bbuf, hst) = refs
    else:
        uc_ref, gate_ref, hb_ref, w_ref, ba_ref, bx_ref, lam_ref, out_ref, abuf, bbuf, hst = refs
    d = 1 if reverse else 0
    i = pl.program_id(0)

    @pl.when(i == 0)
    def _():
        hst[...] = jnp.zeros_like(hst)

    if reverse:
        c = _chunk_of(i, reverse, RG_NLAT, RG_NC)
        prev_zero = jnp.logical_or(c == 0, c == RG_NLAT)
        next_zero = jnp.logical_or(c == RG_NLAT - 1, c == RG_NC - 1)
        xbuf[0:BATCH, :] = up_ref[...] * jnp.where(prev_zero, 0.0, 1.0)
        xbuf[BATCH:BATCH + RG_ROWS, :] = u_ref[...]
        xbuf[BATCH + RG_ROWS:3 * BATCH + RG_ROWS, :] = un_ref[...] * jnp.where(next_zero, 0.0, 1.0)
        uc = cb_ref[...]
        for k in range(RG_CONV):
            uc = uc + xbuf[k * BATCH:k * BATCH + RG_ROWS, :] * cw_ref[0, k:k + 1, :]
        uc_out[...] = uc
    else:
        uc = uc_ref[...]

    ub = uc.astype(BF16)
    half = MIX_W // 2

    def gate(which, b_ref):
        w0 = which * 4 + d * 2
        z = jnp.concatenate([_dot(ub[:, :half], w_ref[w0]), _dot(ub[:, half:], w_ref[w0 + 1])], axis=1)
        return jax.nn.sigmoid(z + b_ref[0, d:d + 1, :])

    r = gate(0, ba_ref)
    ig = gate(1, bx_ref)
    log_a = (-RG_C) * r * _softplus(-lam_ref[0, d:d + 1, :])
    a = jnp.exp(log_a)
    abuf[...] = a
    bbuf[...] = jnp.sqrt(-jnp.tanh(log_a) * (a * a + 1.0)) * (ig * uc)

    h = hst[...]
    for s in range(RG_T):
        t = (RG_T - 1 - s) if reverse else s
        rows = slice(t * BATCH, (t + 1) * BATCH)
        h = abuf[rows, :] * h + bbuf[rows, :]
        bbuf[rows, :] = h
    hst[...] = h

    if reverse:
        out_ref[...] = bbuf[...]
    else:
        y = bbuf[...] + hb_ref[...]
        out_ref[...] = (y * _gelu_tanh(gate_ref[...])).astype(BF16)


def _rg_call(u, gate, hb, cw, cb, w_gates, ba, bx, lam, reverse):
    def cidx(i):
        return _chunk_of(i, reverse, RG_NLAT, RG_NC)

    main = pl.BlockSpec((RG_ROWS, MIX_W), lambda i: (cidx(i), 0))
    gates = [w_gates, ba, bx, lam]
    scratch = [pltpu.VMEM((RG_ROWS, MIX_W), F32), pltpu.VMEM((RG_ROWS, MIX_W), F32), pltpu.VMEM((BATCH, MIX_W), F32)]
    rows = jax.ShapeDtypeStruct((L_TOT * BATCH, MIX_W), F32)
    if reverse:
        prev = pl.BlockSpec((BATCH, MIX_W), lambda i: (jnp.maximum(cidx(i) * RG_T - 1, 0), 0))
        n_next = L_TOT // 2
        nxt = pl.BlockSpec((2 * BATCH, MIX_W), lambda i: (jnp.minimum((cidx(i) + 1) * (RG_T // 2), n_next - 1), 0))
        args = [u, u, u, cw, cb] + gates
        specs = [main, prev, nxt] + [_full(p.shape) for p in args[3:]]
        out_shape, out_specs = [rows, rows], [main, main]
        scratch = [pltpu.VMEM((RG_ROWS + 3 * BATCH, MIX_W), F32)] + scratch
    else:
        args = [u, gate, hb] + gates
        specs = [main, main, main] + [_full(p.shape) for p in gates]
        out_shape, out_specs = jax.ShapeDtypeStruct((L_TOT * BATCH, MIX_W), BF16), main
    return pl.pallas_call(
        functools.partial(_rg_kernel, reverse=reverse),
        out_shape=out_shape,
        grid=(RG_NC,),
        in_specs=specs,
        out_specs=out_specs,
        scratch_shapes=scratch,
        compiler_params=_cparams(("arbitrary",)),
        name="rglru_bwd" if reverse else "rglru_fwd",
    )(*args)


def _head_rows(q_ref, tq):
    lane = lax.broadcasted_iota(jnp.int32, (tq, LANE), 1)
    low = lane < HEAD_DIM
    zero = jnp.zeros((tq, LANE), BF16)
    parts = []
    for p in range(MIX_W // LANE):
        qb = q_ref[0, :, p * LANE:(p + 1) * LANE]
        parts.append(jnp.where(low, qb, zero))
        parts.append(jnp.where(low, zero, qb))
    return parts, low


def _merge_heads(outs, low):
    return jnp.concatenate([jnp.where(low, outs[2 * p], outs[2 * p + 1]) for p in range(MIX_W // LANE)], axis=1)


def _qk(q, k):
    return lax.dot_general(q, k, (((1,), (1,)), ((), ())), preferred_element_type=F32)


def _gattn_kernel(q_ref, k_ref, v_ref, o_ref):
    j = pl.program_id(1)
    heads, low = _head_rows(q_ref, TQ_G)

    def attend(k0, nk):
        outs = []
        for q in heads:
            s = _qk(q, k_ref[0, k0:k0 + nk, :])
            m = jnp.max(s, axis=-1, keepdims=True)
            p = jnp.exp2(s - m).astype(BF16)
            acc = _dot(p, v_ref[0, k0:k0 + nk, :])
            outs.append(acc[:, :KV_W] / acc[:, KV_W:])
        o_ref[0] = _merge_heads(outs, low).astype(BF16)

    @pl.when(j < SEQ // TQ_G)
    def _():
        attend(0, L_TOT)

    @pl.when(j >= SEQ // TQ_G)
    def _():
        attend(SEQ, CTX_LEN)


def _gattn_call(q, k, v):
    return pl.pallas_call(
        _gattn_kernel,
        out_shape=jax.ShapeDtypeStruct((BATCH, L_TOT, MIX_W), BF16),
        grid=(BATCH, L_TOT // TQ_G),
        in_specs=[
            pl.BlockSpec((1, TQ_G, MIX_W), lambda b, j: (b, j, 0)),
            pl.BlockSpec((1, L_TOT, KV_W), lambda b, j: (b, 0, 0)),
            pl.BlockSpec((1, L_TOT, V_W), lambda b, j: (b, 0, 0)),
        ],
        out_specs=pl.BlockSpec((1, TQ_G, MIX_W), lambda b, j: (b, j, 0)),
        compiler_params=_cparams(("parallel", "arbitrary")),
        name="global_attn",
    )(q, k, v)


TQ_W = 2 * WINDOW
N_WBLK = SEQ // WINDOW
N_WSTEP = SEQ // TQ_W


def _wattn_kernel(q_ref, k_ref, v_ref, sink_ref, o_ref):
    i = pl.program_id(1)
    heads, low = _head_rows(q_ref, TQ_W)
    prev_start = pl.multiple_of(jnp.maximum(2 * i - 1, 0) * WINDOW, WINDOW)
    mid_start = pl.multiple_of(i * TQ_W, TQ_W)
    next_start = pl.multiple_of(jnp.minimum(2 * i + 2, N_WBLK - 1) * WINDOW, WINDOW)

    def rows(ref):
        return jnp.concatenate([ref[0, SEQ:L_TOT, :], ref[0, pl.ds(prev_start, WINDOW), :],
                                ref[0, pl.ds(mid_start, TQ_W), :], ref[0, pl.ds(next_start, WINDOW), :]], axis=0)

    nk = CTX_LEN + 2 * WINDOW + TQ_W
    r = lax.broadcasted_iota(jnp.int32, (TQ_W, nk), 0)
    col = lax.broadcasted_iota(jnp.int32, (TQ_W, nk), 1)
    c = col - CTX_LEN
    ninf = -jnp.inf
    pen_prev = jnp.where(i > 0, 0.0, ninf)
    pen_next = jnp.where(i < N_WSTEP - 1, 0.0, ninf)
    edge = jnp.where(c < WINDOW, pen_prev, jnp.where(c >= WINDOW + TQ_W, pen_next, 0.0))
    band = jnp.where(c >= r, jnp.where(c <= r + 2 * WINDOW, edge, ninf), ninf)
    bias = jnp.where(col < CTX_LEN, 0.0, band)
    kk = rows(k_ref)
    vv = rows(v_ref)
    outs = []
    for hd, q in enumerate(heads):
        s = _qk(q, kk) + bias
        sink = sink_ref[0, HEAD_PERM[hd]] * math.log2(math.e)
        m = jnp.maximum(jnp.max(s, axis=-1, keepdims=True), sink)
        p = jnp.exp2(s - m).astype(BF16)
        acc = _dot(p, vv)
        outs.append(acc[:, :KV_W] / (acc[:, KV_W:] + jnp.exp2(sink - m)))
    o_ref[0] = _merge_heads(outs, low).astype(BF16)


def _wattn_call(q, k, v, sink):
    return pl.pallas_call(
        _wattn_kernel,
        out_shape=jax.ShapeDtypeStruct((BATCH, SEQ, MIX_W), BF16),
        grid=(BATCH, N_WSTEP),
        in_specs=[
            pl.BlockSpec((1, TQ_W, MIX_W), lambda b, i: (b, i, 0)),
            pl.BlockSpec((1, L_TOT, KV_W), lambda b, i: (b, 0, 0)),
            pl.BlockSpec((1, L_TOT, V_W), lambda b, i: (b, 0, 0)),
            pl.BlockSpec(memory_space=pltpu.SMEM),
        ],
        out_specs=pl.BlockSpec((1, TQ_W, MIX_W), lambda b, i: (b, i, 0)),
        compiler_params=_cparams(("parallel", "arbitrary")),
        name="window_attn",
    )(q, k, v, sink)


def _s5_disc_kernel(lr_ref, li_ref, ls_ref, br_ref, bi_ref, ar_ref, ai_ref, bbr_ref, bbi_ref):
    lr = lr_ref[0]
    li = li_ref[0]
    dt = jnp.exp(ls_ref[0])
    mag = jnp.exp(lr * dt)
    ang = li * dt
    abr = mag * jnp.cos(ang)
    abi = mag * jnp.sin(ang)
    den = lr * lr + li * li
    nr = abr - 1.0
    kr = (nr * lr + abi * li) / den
    ki = (abi * lr - nr * li) / den
    br = br_ref[0]
    bi = bi_ref[0]
    ar_ref[0] = abr
    ai_ref[0] = abi
    bbr_ref[0] = kr * br - ki * bi
    bbi_ref[0] = kr * bi + ki * br


def _s5_disc_call(lam_re, lam_im, log_step, bt_re, bt_im):
    n = S5_GROUPS * S5_STATE
    row = pl.BlockSpec((1, 1, n), lambda d: (d, 0, 0))
    mat = pl.BlockSpec((1, S5_GROUP, n), lambda d: (d, 0, 0))
    return pl.pallas_call(
        _s5_disc_kernel,
        out_shape=[jax.ShapeDtypeStruct((2, 1, n), F32)] * 2 + [jax.ShapeDtypeStruct((2, S5_GROUP, n), F32)] * 2,
        grid=(2,),
        in_specs=[row, row, row, mat, mat],
        out_specs=[row, row, mat, mat],
        compiler_params=_cparams(("parallel",)),
        name="s5_discretize",
    )(lam_re, lam_im, log_step, bt_re, bt_im)


S5_ROWS = S5_T * BATCH
S5_NC = L_TOT // S5_T
S5_NLAT = SEQ // S5_T
S5_SW = 2 * S5_BLK_STATE * S5_NBLK


def _s5_kernel(*refs, reverse):
    if reverse:
        u2_ref, un_ref, wd_ref, ar_ref, ai_ref, wr_ref, out_ref, buf0, buf1, xst = refs
    else:
        (u2_ref, un_ref, yb2_ref, wd_ref, ar_ref, ai_ref, wr_ref, dsk_ref, gw_ref, gb_ref,
         out_ref, buf0, buf1, xst) = refs
    nb = 2 * S5_BLK_STATE
    lo, hi = slice(0, S5_ROWS), slice(S5_ROWS, 2 * S5_ROWS)
    first, second = (hi, lo) if reverse else (lo, hi)

    def drive(u, buf):
        ub = u.astype(BF16)
        for k in range(S5_NBLK):
            buf[:, k * nb:(k + 1) * nb] = _dot(ub[:, k * LANE:(k + 1) * LANE], wd_ref[0, k])

    def scan(buf, x):
        for s in range(S5_T):
            t = (S5_T - 1 - s) if reverse else s
            rows = slice(t * BATCH, (t + 1) * BATCH)
            halves = []
            for k in range(S5_NBLK):
                halves.append(x[:, k * nb + S5_BLK_STATE:(k + 1) * nb])
                halves.append(x[:, k * nb:k * nb + S5_BLK_STATE])
            xsw = jnp.concatenate(halves, axis=1)
            x = ar_ref[0] * x + ai_ref[0] * xsw + buf[rows, :]
            buf[rows, :] = x
        return x

    def readout(buf, rows):
        y = jnp.concatenate(
            [_dot(buf[:, k * nb:(k + 1) * nb].astype(BF16), wr_ref[0, k]) for k in range(S5_NBLK)], axis=1)
        if reverse:
            out_ref[rows, :] = y
        else:
            y = u2_ref[rows, :] * dsk_ref[...] + y + yb2_ref[rows, :]
            z = _gelu_tanh(y)
            out_ref[rows, :] = (z * jax.nn.sigmoid(_dot(z.astype(BF16), gw_ref[...]) + gb_ref[...])).astype(BF16)

    @pl.when(pl.program_id(0) == 0)
    def _():
        xst[...] = jnp.zeros_like(xst)
        drive(u2_ref[first, :], buf0)

    x = xst[...]
    drive(u2_ref[second, :], buf1)
    x = scan(buf0, x)
    readout(buf0, first)
    drive(un_ref[...], buf0)
    x = scan(buf1, x)
    readout(buf1, second)
    xst[...] = x


def _s5_call(u, yb, wd, ar, ai, wr, dsk, gw, gb, reverse):
    def chunk(p):
        return _chunk_of(p, reverse, S5_NLAT, S5_NC)

    pair = pl.BlockSpec((2 * S5_ROWS, MIX_W), lambda k: (chunk(2 * k) // 2, 0))
    nxt = pl.BlockSpec((S5_ROWS, MIX_W), lambda k: (chunk(jnp.minimum(2 * k + 2, S5_NC - 1)), 0))
    d = 1 if reverse else 0
    dir_specs = [pl.BlockSpec((1,) + a.shape[1:], lambda k, n=a.ndim: (d,) + (0,) * (n - 1)) for a in (wd, ar, ai, wr)]
    if reverse:
        args = [u, u, wd, ar, ai, wr]
        specs = [pair, nxt] + dir_specs
        out_dtype = F32
    else:
        args = [u, u, yb, wd, ar, ai, wr, dsk, gw, gb]
        specs = [pair, nxt, pair] + dir_specs + [_full(a.shape) for a in (dsk, gw, gb)]
        out_dtype = BF16
    return pl.pallas_call(
        functools.partial(_s5_kernel, reverse=reverse),
        out_shape=jax.ShapeDtypeStruct((L_TOT * BATCH, MIX_W), out_dtype),
        grid=(S5_NC // 2,),
        in_specs=specs,
        out_specs=pair,
        scratch_shapes=[
            pltpu.VMEM((S5_ROWS, S5_SW), F32),
            pltpu.VMEM((S5_ROWS, S5_SW), F32),
            pltpu.VMEM((BATCH, S5_SW), F32),
        ],
        compiler_params=_cparams(("arbitrary",)),
        name="s5_bwd" if reverse else "s5_fwd",
    )(*args)


POST_ROWS = POST_T * BATCH
POST_HALO_T = BF16_ROWS // BATCH
POST_HALO = POST_HALO_T * BATCH
POST_NLAT = SEQ // POST_T
FFN_NCH = FFN_HIDDEN // FFN_CH


def _post_kernel(h_ref, hp_ref, hn_ref, a_ref, ap_ref, an_ref, b_ref, bp_ref, bn_ref, wo_ref,
                 g2_ref, sh_ref, sc_ref, g5_ref, g_ref, wu_ref, cw_ref, cb_ref, wd_ref, o_ref, *,
                 n_tiles, out_batch_major, layer):
    j = pl.program_id(0)
    first = jnp.logical_or(j == 0, j == POST_NLAT)
    last = jnp.logical_or(j == POST_NLAT - 1, j == n_tiles - 1)
    gate2 = g2_ref[0]

    def mix(h, a, b_tm):
        m = _dot(a, wo_ref[0:MIX_W, :]) + _dot(b_tm.astype(BF16), wo_ref[MIX_W:2 * MIX_W, :])
        return h + _per_sample(m, gate2, jnp.multiply)

    def norm(x):
        return _adaln(x, g_ref[layer, 1:2, :], sc_ref[0], sh_ref[0])

    def halo_b(ref, t0):
        x = jnp.swapaxes(ref[...].astype(F32), 0, 1)[t0:t0 + POST_HALO_T]
        return x.reshape(POST_HALO, MIX_W)

    h_mid = mix(h_ref[...], a_ref[...], _to_time_major(b_ref[...].astype(F32)))
    h_prev = mix(hp_ref[...], ap_ref[...], halo_b(bp_ref, BF16_ROWS - POST_HALO_T))
    h_next = mix(hn_ref[...], an_ref[...], halo_b(bn_ref, 0))
    xn = jnp.concatenate([
        (norm(h_prev) * jnp.where(first, 0.0, 1.0)).astype(BF16),
        norm(h_mid).astype(BF16),
        (norm(h_next) * jnp.where(last, 0.0, 1.0)).astype(BF16)], axis=0)

    acts = []
    for c in range(FFN_NCH):
        cv = slice(c * FFN_CH, (c + 1) * FFN_CH)
        cg = slice(FFN_HIDDEN + c * FFN_CH, FFN_HIDDEN + (c + 1) * FFN_CH)

        def conv(cols):
            hid = _dot(xn, wu_ref[:, cols])
            out = cb_ref[layer:layer + 1, cols]
            for k in range(FFN_CONV):
                r0 = POST_HALO + (k - 1) * BATCH
                out = out + hid[r0:r0 + POST_ROWS] * cw_ref[0, k:k + 1, cols]
            return out

        val = conv(cv)
        gate = conv(cg)
        acts.append((val * (gate * jax.nn.sigmoid(gate))).astype(BF16))
    ffn = _dot(jnp.concatenate(acts, axis=1), wd_ref[...])
    out = h_mid + _per_sample(ffn, g5_ref[0], jnp.multiply)
    if out_batch_major:
        o_ref[...] = _to_batch_major(out)
    else:
        o_ref[...] = out


def _post_call(h, a_tb, b_out, w_out, modt, g, wu, cw, cb, wd, layer, n_tiles, out_batch_major):
    per = POST_ROWS // POST_HALO
    n_halo = h.shape[0] // POST_HALO
    per_b = POST_T // BF16_ROWS
    n_halo_b = b_out.shape[1] // BF16_ROWS

    def main(width):
        return pl.BlockSpec((POST_ROWS, width), lambda j: (j, 0))

    def prev(width):
        return pl.BlockSpec((POST_HALO, width), lambda j: (jnp.maximum(j * per - 1, 0), 0))

    def nxt(width):
        return pl.BlockSpec((POST_HALO, width), lambda j: (jnp.minimum((j + 1) * per, n_halo - 1), 0))

    if out_batch_major:
        out_shape = jax.ShapeDtypeStruct((BATCH, n_tiles * POST_T, D_MODEL), F32)
        out_spec = pl.BlockSpec((BATCH, POST_T, D_MODEL), lambda j: (0, j, 0))
    else:
        out_shape = jax.ShapeDtypeStruct((n_tiles * POST_ROWS, D_MODEL), F32)
        out_spec = main(D_MODEL)
    return pl.pallas_call(
        functools.partial(_post_kernel, n_tiles=n_tiles, out_batch_major=out_batch_major, layer=layer),
        out_shape=out_shape,
        grid=(n_tiles,),
        in_specs=[
            main(D_MODEL), prev(D_MODEL), nxt(D_MODEL),
            main(MIX_W), prev(MIX_W), nxt(MIX_W),
            pl.BlockSpec((BATCH, POST_T, MIX_W), lambda j: (0, j, 0)),
            pl.BlockSpec((BATCH, BF16_ROWS, MIX_W), lambda j: (0, jnp.maximum(j * per_b - 1, 0), 0)),
            pl.BlockSpec((BATCH, BF16_ROWS, MIX_W), lambda j: (0, jnp.minimum((j + 1) * per_b, n_halo_b - 1), 0)),
            _full((2 * MIX_W, D_MODEL), single=True),
            _mod_spec(2, POST_NLAT), _mod_spec(3, POST_NLAT), _mod_spec(4, POST_NLAT), _mod_spec(5, POST_NLAT),
            _full(g.shape),
            _full((D_MODEL, 2 * FFN_HIDDEN), single=True),
            pl.BlockSpec((1, FFN_CONV, 2 * FFN_HIDDEN), lambda j: (layer, 0, 0)),
            _full(cb.shape),
            _full((FFN_HIDDEN, D_MODEL), single=True),
        ],
        out_specs=out_spec,
        compiler_params=_cparams(("parallel",)),
        name="post_ffn",
    )(h, h, h, a_tb, a_tb, a_tb, b_out, b_out, b_out, w_out, modt, modt, modt, modt, g, wu, cw, cb, wd)


def _rope_table():
    rows = SEQ // GRID_W
    row = jnp.repeat(jnp.arange(rows, dtype=F32), GRID_W)
    col = jnp.tile(jnp.arange(GRID_W, dtype=F32), rows)
    quarter = HEAD_DIM // 4
    inv_freq = ROPE_BASE ** (-jnp.arange(quarter, dtype=F32) / quarter)
    ang = jnp.stack([row[:, None] * inv_freq, col[:, None] * inv_freq], axis=1)
    cos = jnp.cos(ang)
    sin = jnp.sin(ang)
    zero = jnp.zeros_like(sin)

    def lanes(first, second):
        t = jnp.stack([first, second], axis=2).reshape(SEQ, HEAD_DIM)
        return jnp.tile(t, (1, LANE // HEAD_DIM))

    tab = jnp.concatenate([lanes(cos, cos), lanes(-sin, zero), lanes(zero, sin)], axis=1)
    ctx = jnp.concatenate([jnp.ones((CTX_LEN, LANE), F32), jnp.zeros((CTX_LEN, 2 * LANE), F32)], axis=1)
    return jnp.concatenate([tab, ctx], axis=0)


def _perm_heads_cols(w):
    d = w.shape[0]
    return w.reshape(d, N_HEADS, HEAD_DIM)[:, jnp.array(HEAD_PERM)].reshape(d, N_HEADS * HEAD_DIM)


def _perm_heads_rows(w):
    n = w.shape[1]
    return w.reshape(N_HEADS, HEAD_DIM, n)[jnp.array(HEAD_PERM)].reshape(N_HEADS * HEAD_DIM, n)


def _block_diag(w, per):
    n, a, b = w.shape
    eye = jnp.eye(per, dtype=w.dtype)
    w4 = w.reshape(n // per, per, a, b)
    return jnp.einsum("ihab,hk->ihakb", w4, eye).reshape(n // per, per * a, per * b)


def _w_in(w, q_off):
    w = w.astype(BF16)
    return jnp.concatenate([w[:, :q_off], _perm_heads_cols(w[:, q_off:q_off + MIX_W]), w[:, q_off + MIX_W:]], axis=1)


def _w_out(w):
    w = w.astype(BF16)
    return jnp.concatenate([w[:MIX_W], _perm_heads_rows(w[MIX_W:])], axis=0)


def kernel(x, c, ctx, c_ctx, mod_w, mod_b, norm_g, ffn_up, ffn_conv_w, ffn_conv_b, ffn_down, ev_w_in, ev_w_out, rg_conv_w, rg_conv_b, rg_wa, rg_ba, rg_wx, rg_bx, rg_lam, ga_qn, ga_kn, od_w_in, od_w_out, s5_lam_re, s5_lam_im, s5_log_step, s5_b_re, s5_b_im, s5_c_re, s5_c_im, s5_d, s5_glu_w, s5_glu_b, wa_qn, wa_kn, wa_sink):
    cvec = jnp.concatenate([c, jnp.broadcast_to(c_ctx[None], (BATCH, D_MODEL))], axis=0)
    mod = _mod_call(cvec, mod_w, mod_b).reshape(DEPTH, 2, BATCH, N_MOD * D_MODEL)
    rope = _rope_table()
    gm = _block_diag(jnp.full((LANE // HEAD_DIM, HEAD_DIM, HEAD_DIM), 1.0 / HEAD_DIM, F32), LANE // HEAD_DIM)[0].astype(BF16)

    h, u_tb, gate_tb, q, k, v, wu, wd = _in_call(
        (x, ctx), mod[0], norm_g, _w_in(ev_w_in[0], 2 * MIX_W), rope,
        ga_qn, ga_kn, gm, ffn_up, ffn_down, 0, n_tb=2, first_layer=True)
    w_gates = _block_diag(jnp.stack([rg_wa[0], rg_wx[0]]).astype(BF16).reshape(-1, HEAD_DIM, HEAD_DIM), 4)
    rg = (rg_conv_w, rg_conv_b, w_gates, rg_ba, rg_bx, rg_lam)
    hb, uc = _rg_call(u_tb, None, None, *rg, reverse=True)
    a_out = _rg_call(uc, gate_tb, hb, *rg, reverse=False)
    b_out = _gattn_call(q, k, v)
    h = _post_call(h, a_out, b_out, _w_out(ev_w_out[0]), mod[0], norm_g, wu, ffn_conv_w, ffn_conv_b, wd, 0,
                   n_tiles=L_TOT // POST_T, out_batch_major=False)

    u_tb, q, k, v, wu, wd = _in_call(
        (h,), mod[1], norm_g, _w_in(od_w_in[0], MIX_W), rope,
        wa_qn, wa_kn, gm, ffn_up, ffn_down, 1, n_tb=1, first_layer=False)
    n_state = S5_GROUPS * S5_STATE
    bt_re = s5_b_re[0].transpose(0, 3, 1, 2).reshape(2, S5_GROUP, n_state)
    bt_im = s5_b_im[0].transpose(0, 3, 1, 2).reshape(2, S5_GROUP, n_state)
    log_step = jnp.repeat(s5_log_step[0], S5_STATE, axis=-1).reshape(2, 1, n_state)
    abr, abi, bbr, bbi = _s5_disc_call(s5_lam_re[0].reshape(2, 1, n_state), s5_lam_im[0].reshape(2, 1, n_state),
                                       log_step, bt_re, bt_im)
    gpb = S5_GROUPS // S5_NBLK
    eye = jnp.eye(gpb, dtype=BF16)
    bb = jnp.stack([bbr, bbi]).astype(BF16).reshape(2, 2, S5_GROUP, S5_NBLK, gpb, S5_STATE)
    w_drive = jnp.einsum("rdhbgp,gk->dbghrkp", bb, eye).reshape(
        2, S5_NBLK, gpb * S5_GROUP, 2 * S5_BLK_STATE)
    cc = jnp.stack([s5_c_re[0], -s5_c_im[0]]).astype(BF16).reshape(
        2, 2, S5_NBLK, gpb, S5_GROUP, S5_STATE)
    w_read = jnp.einsum("rdbghp,gk->dbrgpkh", cc, eye).reshape(
        2, S5_NBLK, 2 * S5_BLK_STATE, gpb * S5_GROUP)

    def state_rows(re, im):
        row = jnp.concatenate([re.reshape(2, S5_NBLK, S5_BLK_STATE), im.reshape(2, S5_NBLK, S5_BLK_STATE)],
                              axis=2).reshape(2, 1, S5_SW)
        return jnp.broadcast_to(row, (2, BATCH, S5_SW))

    s5 = (w_drive, state_rows(abr, abr), state_rows(-abi, abi), w_read)
    yb = _s5_call(u_tb, None, *s5, None, None, None, reverse=True)
    c_out = _s5_call(u_tb, yb, *s5, s5_d, s5_glu_w[0].astype(BF16), s5_glu_b, reverse=False)
    d_out = _wattn_call(q, k, v, wa_sink)
    return _post_call(h, c_out, d_out, _w_out(od_w_out[0]), mod[1], norm_g, wu, ffn_conv_w, ffn_conv_b, wd, 1,
                      n_tiles=SEQ // POST_T, out_batch_major=True)
```

```python
import functools
import math

import jax
import jax.numpy as jnp
from jax import lax
from jax.experimental import pallas as pl
from jax.experimental.pallas import tpu as pltpu

F32 = jnp.float32
BF16 = jnp.bfloat16

D_MODEL = 1024
BATCH = 8
SEQ = 2048
CTX_LEN = 256
L_TOT = SEQ + CTX_LEN
DEPTH = 2
GRID_W = 64
HEAD_DIM = 64
ROPE_BASE = 10000.0
NORM_EPS = 1e-6
WINDOW = 128
N_MOD = 6
MIX_W = 512
N_HEADS = 8
N_KV = 2
KV_W = N_KV * HEAD_DIM
V_W = 2 * KV_W
QKV_W = MIX_W + KV_W + KV_W
Q_SCALE = HEAD_DIM ** -0.5 * math.log2(math.e)
RG_CONV = 4
RG_C = 8.0
S5_GROUP = 16
S5_GROUPS = 32
S5_STATE = 64
S5_NBLK = 4
S5_BLK_STATE = (S5_GROUPS // S5_NBLK) * S5_STATE
FFN_HIDDEN = 2816
FFN_CONV = 3

LANE = 128
BF16_ROWS = 16
VMEM_LIMIT = 56 * 1024 * 1024

IN_T = 128
RG_T = 256
S5_T = 64
POST_T = 128
TQ_G = 256
FFN_CH = 256
HEAD_PERM = (0, 4, 1, 5, 2, 6, 3, 7)


def _cparams(sem):
    return pltpu.CompilerParams(dimension_semantics=sem, vmem_limit_bytes=VMEM_LIMIT)


def _dot(a, b):
    return jnp.dot(a, b, preferred_element_type=F32)


def _gelu_tanh(x):
    return x * (0.5 * (1.0 + jnp.tanh(math.sqrt(2.0 / math.pi) * (x + 0.044715 * (x * x * x)))))


def _rms_unit(x):
    ms = jnp.mean(x * x, axis=-1, keepdims=True)
    return x * lax.rsqrt(ms + NORM_EPS)


def _adaln(x, gain, scale, shift):
    y = _per_sample(_rms_unit(x), gain * (1.0 + scale), jnp.multiply)
    return _per_sample(y, shift, jnp.add)


def _to_time_major(x):
    b, t, c = x.shape
    return jnp.swapaxes(x, 0, 1).reshape(t * b, c)


def _to_batch_major(x):
    r, c = x.shape
    return jnp.swapaxes(x.reshape(r // BATCH, BATCH, c), 0, 1)


def _per_sample(x, v, op):
    r, c = x.shape
    return op(x.reshape(r // BATCH, BATCH, c), v[None]).reshape(r, c)


def _full(shape, single=False):
    kw = dict(pipeline_mode=pl.Buffered(1)) if single else {}
    return pl.BlockSpec(shape, lambda *_: (0,) * len(shape), **kw)


MOD_TN = 3072


def _mod_kernel(c_ref, wa_ref, wb_ref, b_ref, o_ref):
    c = c_ref[...]
    a = c * jax.nn.sigmoid(c)
    ah = a.astype(BF16)
    al = (a - ah.astype(F32)).astype(BF16)
    for i, w_ref in enumerate((wa_ref, wb_ref)):
        w = w_ref[0]
        wh = w.astype(BF16)
        wl = (w - wh.astype(F32)).astype(BF16)
        cols = slice(i * (MOD_TN // 2), (i + 1) * (MOD_TN // 2))
        o_ref[0, :, cols] = _dot(ah, wh) + _dot(ah, wl) + _dot(al, wh) + b_ref[0, :, cols]


def _mod_call(cvec, mod_w, mod_b):
    n = N_MOD * D_MODEL
    return pl.pallas_call(
        _mod_kernel,
        out_shape=jax.ShapeDtypeStruct((DEPTH, 2 * BATCH, n), F32),
        grid=(DEPTH, n // MOD_TN),
        in_specs=[
            pl.BlockSpec((2 * BATCH, D_MODEL), lambda l, k: (0, 0)),
            pl.BlockSpec((1, D_MODEL, MOD_TN // 2), lambda l, k: (l, 0, 2 * k)),
            pl.BlockSpec((1, D_MODEL, MOD_TN // 2), lambda l, k: (l, 0, 2 * k + 1)),
            pl.BlockSpec((1, 1, MOD_TN), lambda l, k: (l, 0, k)),
        ],
        out_specs=pl.BlockSpec((1, 2 * BATCH, MOD_TN), lambda l, k: (l, 0, k)),
        compiler_params=_cparams(("parallel", "parallel")),
        name="mod",
    )(cvec, mod_w, mod_w, mod_b.reshape(DEPTH, 1, n))


def _mod_spec(col, n_latent_tiles):
    return pl.BlockSpec((1, BATCH, D_MODEL), lambda j: (jnp.where(j < n_latent_tiles, 0, 1), 0, col))


IN_ROWS = IN_T * BATCH
IN_NLAT = SEQ // IN_T
IN_N = L_TOT // IN_T
IN_SUB = 2
CAST_STEPS = 16
assert CAST_STEPS <= IN_N
IN_SUB_T = IN_T // IN_SUB
IN_SUB_ROWS = IN_SUB_T * BATCH


def _in_kernel(fu_ref, fd_ref, *refs, n_tb, first_layer, layer):
    fu_out, fd_out = refs[-2:]
    refs = refs[:-2]

    @pl.when(pl.program_id(0) < CAST_STEPS)
    def _():
        fu_out[...] = fu_ref[0].astype(BF16)
        fd_out[...] = fd_ref[0].astype(BF16)

    if first_layer:
        x_ref, c_ref, sh_ref, sc_ref, g_ref, w_ref, rope_ref, gq_ref, gk_ref, gm_ref = refs[:10]
        h_out = refs[10]
        outs = refs[11:]
        is_latent = pl.program_id(0) < IN_NLAT
    else:
        h_ref, sh_ref, sc_ref, g_ref, w_ref, rope_ref, gq_ref, gk_ref, gm_ref = refs[:9]
        outs = refs[9:]
    q_ref, k_ref, v_ref = outs[n_tb:]
    off = n_tb * MIX_W
    gm = gm_ref[...]
    gq = jnp.concatenate([gq_ref[...]] * (LANE // HEAD_DIM), axis=1)
    gk = jnp.concatenate([gk_ref[...]] * (LANE // HEAD_DIM), axis=1)
    for s in range(IN_SUB):
        ts = slice(s * IN_SUB_T, (s + 1) * IN_SUB_T)
        rs = slice(s * IN_SUB_ROWS, (s + 1) * IN_SUB_ROWS)
        if first_layer:
            h = _to_time_major(jnp.where(is_latent, x_ref[:, ts, :], c_ref[:, ts, :]))
            h_out[rs, :] = h
        else:
            h = h_ref[rs, :]
        xn = _adaln(h, g_ref[layer, 0:1, :], sc_ref[0], sh_ref[0])
        proj = _dot(xn.astype(BF16), w_ref[...])
        for i in range(n_tb):
            outs[i][rs, :] = proj[:, i * MIX_W:(i + 1) * MIX_W]
        qkv = _to_batch_major(proj[:, off:off + QKV_W])
        cos = rope_ref[ts, 0:LANE][None]
        sin_up = rope_ref[ts, LANE:2 * LANE][None]
        sin_dn = rope_ref[ts, 2 * LANE:3 * LANE][None]
        for p in range(5):
            blk = qkv[:, :, p * LANE:(p + 1) * LANE].reshape(IN_SUB_ROWS, LANE)
            sq = blk * blk
            hi = sq.astype(BF16)
            lo = (sq - hi.astype(F32)).astype(BF16)
            ms = _dot(hi, gm) + _dot(lo, gm)
            g = gq if p < 4 else gk
            bn = blk * lax.rsqrt(ms + NORM_EPS) * g
            up = pltpu.roll(bn, LANE - HEAD_DIM // 4, 1).reshape(BATCH, IN_SUB_T, LANE)
            dn = pltpu.roll(bn, HEAD_DIM // 4, 1).reshape(BATCH, IN_SUB_T, LANE)
            ro = bn.reshape(BATCH, IN_SUB_T, LANE) * cos + up * sin_up + dn * sin_dn
            if p < 4:
                q_ref[:, ts, p * LANE:(p + 1) * LANE] = (ro * Q_SCALE).astype(BF16)
            else:
                k_ref[:, ts, :] = ro.astype(BF16)
        v = qkv[:, :, 5 * LANE:6 * LANE].astype(BF16)
        v_ref[:, ts, :] = jnp.concatenate([v, jnp.ones((BATCH, IN_SUB_T, LANE), BF16)], axis=2)


def _in_call(h_args, modt, g, w, rope, gq, gk, gm, ffn_up, ffn_down, layer, n_tb, first_layer):
    n = w.shape[1]
    tb = lambda width: pl.BlockSpec((IN_ROWS, width), lambda j: (j, 0))
    bm = lambda width: pl.BlockSpec((BATCH, IN_T, width), lambda j: (0, j, 0))
    up_rows = D_MODEL // CAST_STEPS
    down_rows = FFN_HIDDEN // CAST_STEPS
    slab = lambda j: jnp.minimum(j, CAST_STEPS - 1)
    if first_layer:
        h_specs = [
            pl.BlockSpec((BATCH, IN_T, D_MODEL), lambda j: (0, jnp.minimum(j, IN_NLAT - 1), 0)),
            pl.BlockSpec((BATCH, IN_T, D_MODEL), lambda j: (0, jnp.maximum(j - IN_NLAT, 0), 0)),
        ]
        extra_shape = [jax.ShapeDtypeStruct((L_TOT * BATCH, D_MODEL), F32)]
        extra_spec = [tb(D_MODEL)]
    else:
        h_specs = [tb(D_MODEL)]
        extra_shape, extra_spec = [], []
    return pl.pallas_call(
        functools.partial(_in_kernel, n_tb=n_tb, first_layer=first_layer, layer=layer),
        out_shape=extra_shape + [jax.ShapeDtypeStruct((L_TOT * BATCH, MIX_W), F32)] * n_tb + [
            jax.ShapeDtypeStruct((BATCH, L_TOT, MIX_W), BF16),
            jax.ShapeDtypeStruct((BATCH, L_TOT, KV_W), BF16),
            jax.ShapeDtypeStruct((BATCH, L_TOT, V_W), BF16),
            jax.ShapeDtypeStruct((D_MODEL, 2 * FFN_HIDDEN), BF16),
            jax.ShapeDtypeStruct((FFN_HIDDEN, D_MODEL), BF16),
        ],
        grid=(IN_N,),
        in_specs=[
            pl.BlockSpec((1, up_rows, 2 * FFN_HIDDEN), lambda j: (layer, slab(j), 0)),
            pl.BlockSpec((1, down_rows, D_MODEL), lambda j: (layer, slab(j), 0)),
        ] + h_specs + [
            _mod_spec(0, IN_NLAT), _mod_spec(1, IN_NLAT),
            _full(g.shape),
            _full((D_MODEL, n), single=True),
            pl.BlockSpec((IN_T, 3 * LANE), lambda j: (j, 0)),
            _full(gq.shape), _full(gk.shape), _full((LANE, LANE)),
        ],
        out_specs=extra_spec + [tb(MIX_W)] * n_tb + [bm(MIX_W), bm(KV_W), bm(V_W)] + [
            pl.BlockSpec((up_rows, 2 * FFN_HIDDEN), lambda j: (slab(j), 0)),
            pl.BlockSpec((down_rows, D_MODEL), lambda j: (slab(j), 0)),
        ],
        compiler_params=_cparams(("arbitrary",)),
        name="in_proj",
    )(ffn_up, ffn_down, *h_args, modt, modt, g, w, rope, gq, gk, gm)


def _chunk_of(i, reverse, n_latent, n_chunks):
    if reverse:
        return n_chunks - 1 - i
    return jnp.where(i < n_chunks - n_latent, i + n_latent, i - (n_chunks - n_latent))


RG_ROWS = RG_T * BATCH
RG_NC = L_TOT // RG_T
RG_NLAT = SEQ // RG_T


def _softplus(z):
    return jnp.maximum(z, 0.0) + jnp.log1p(jnp.exp(-jnp.abs(z)))


def _rg_kernel(*refs, reverse):
    if reverse:
        (u_ref, up_ref, un_ref, cw_ref, cb_ref, w_ref, ba_ref, bx_ref, lam_ref,
         out_ref, uc_out, xbuf, abuf, bbuf, hst) = refs
    else:
        uc_ref, gate_ref, hb_ref, w_ref, ba_ref, bx_ref, lam_ref, out_ref, abuf, bbuf, hst = refs
    d = 1 if reverse else 0
    i = pl.program_id(0)

    @pl.when(i == 0)
    def _():
        hst[...] = jnp.zeros_like(hst)

    if reverse:
        c = _chunk_of(i, reverse, RG_NLAT, RG_NC)
        prev_zero = jnp.logical_or(c == 0, c == RG_NLAT)
        next_zero = jnp.logical_or(c == RG_NLAT - 1, c == RG_NC - 1)
        xbuf[0:BATCH, :] = up_ref[...] * jnp.where(prev_zero, 0.0, 1.0)
        xbuf[BATCH:BATCH + RG_ROWS, :] = u_ref[...]
        xbuf[BATCH + RG_ROWS:3 * BATCH + RG_ROWS, :] = un_ref[...] * jnp.where(next_zero, 0.0, 1.0)
        uc = cb_ref[...]
        for k in range(RG_CONV):
            uc = uc + xbuf[k * BATCH:k * BATCH + RG_ROWS, :] * cw_ref[0, k:k + 1, :]
        uc_out[...] = uc
    else:
        uc = uc_ref[...]

    ub = uc.astype(BF16)
    half = MIX_W // 2

    def gate(which, b_ref):
        w0 = which * 4 + d * 2
        z = jnp.concatenate([_dot(ub[:, :half], w_ref[w0]), _dot(ub[:, half:], w_ref[w0 + 1])], axis=1)
        return jax.nn.sigmoid(z + b_ref[0, d:d + 1, :])

    r = gate(0, ba_ref)
    ig = gate(1, bx_ref)
    log_a = (-RG_C) * r * _softplus(-lam_ref[0, d:d + 1, :])
    a = jnp.exp(log_a)
    abuf[...] = a
    bbuf[...] = jnp.sqrt(-jnp.tanh(log_a) * (a * a + 1.0)) * (ig * uc)

    h = hst[...]
    for s in range(RG_T):
        t = (RG_T - 1 - s) if reverse else s
        rows = slice(t * BATCH, (t + 1) * BATCH)
        h = abuf[rows, :] * h + bbuf[rows, :]
        bbuf[rows, :] = h
    hst[...] = h

    if reverse:
        out_ref[...] = bbuf[...]
    else:
        y = bbuf[...] + hb_ref[...]
        out_ref[...] = (y * _gelu_tanh(gate_ref[...])).astype(BF16)


def _rg_call(u, gate, hb, cw, cb, w_gates, ba, bx, lam, reverse):
    def cidx(i):
        return _chunk_of(i, reverse, RG_NLAT, RG_NC)

    main = pl.BlockSpec((RG_ROWS, MIX_W), lambda i: (cidx(i), 0))
    gates = [w_gates, ba, bx, lam]
    scratch = [pltpu.VMEM((RG_ROWS, MIX_W), F32), pltpu.VMEM((RG_ROWS, MIX_W), F32), pltpu.VMEM((BATCH, MIX_W), F32)]
    rows = jax.ShapeDtypeStruct((L_TOT * BATCH, MIX_W), F32)
    if reverse:
        prev = pl.BlockSpec((BATCH, MIX_W), lambda i: (jnp.maximum(cidx(i) * RG_T - 1, 0), 0))
        n_next = L_TOT // 2
        nxt = pl.BlockSpec((2 * BATCH, MIX_W), lambda i: (jnp.minimum((cidx(i) + 1) * (RG_T // 2), n_next - 1), 0))
        args = [u, u, u, cw, cb] + gates
        specs = [main, prev, nxt] + [_full(p.shape) for p in args[3:]]
        out_shape, out_specs = [rows, rows], [main, main]
        scratch = [pltpu.VMEM((RG_ROWS + 3 * BATCH, MIX_W), F32)] + scratch
    else:
        args = [u, gate, hb] + gates
        specs = [main, main, main] + [_full(p.shape) for p in gates]
        out_shape, out_specs = jax.ShapeDtypeStruct((L_TOT * BATCH, MIX_W), BF16), main
    return pl.pallas_call(
        functools.partial(_rg_kernel, reverse=reverse),
        out_shape=out_shape,
        grid=(RG_NC,),
        in_specs=specs,
        out_specs=out_specs,
        scratch_shapes=scratch,
        compiler_params=_cparams(("arbitrary",)),
        name="rglru_bwd" if reverse else "rglru_fwd",
    )(*args)


def _head_rows(q_ref, tq):
    lane = lax.broadcasted_iota(jnp.int32, (tq, LANE), 1)
    low = lane < HEAD_DIM
    zero = jnp.zeros((tq, LANE), BF16)
    parts = []
    for p in range(MIX_W // LANE):
        qb = q_ref[0, :, p * LANE:(p + 1) * LANE]
        parts.append(jnp.where(low, qb, zero))
        parts.append(jnp.where(low, zero, qb))
    return parts, low


def _merge_heads(outs, low):
    return jnp.concatenate([jnp.where(low, outs[2 * p], outs[2 * p + 1]) for p in range(MIX_W // LANE)], axis=1)


def _qk(q, k):
    return lax.dot_general(q, k, (((1,), (1,)), ((), ())), preferred_element_type=F32)


def _gattn_kernel(q_ref, k_ref, v_ref, o_ref):
    j = pl.program_id(1)
    heads, low = _head_rows(q_ref, TQ_G)

    def attend(k0, nk):
        outs = []
        for q in heads:
            s = _qk(q, k_ref[0, k0:k0 + nk, :])
            m = jnp.max(s, axis=-1, keepdims=True)
            p = jnp.exp2(s - m).astype(BF16)
            acc = _dot(p, v_ref[0, k0:k0 + nk, :])
            outs.append(acc[:, :KV_W] / acc[:, KV_W:])
        o_ref[0] = _merge_heads(outs, low).astype(BF16)

    @pl.when(j < SEQ // TQ_G)
    def _():
        attend(0, L_TOT)

    @pl.when(j >= SEQ // TQ_G)
    def _():
        attend(SEQ, CTX_LEN)


def _gattn_call(q, k, v):
    return pl.pallas_call(
        _gattn_kernel,
        out_shape=jax.ShapeDtypeStruct((BATCH, L_TOT, MIX_W), BF16),
        grid=(BATCH, L_TOT // TQ_G),
        in_specs=[
            pl.BlockSpec((1, TQ_G, MIX_W), lambda b, j: (b, j, 0)),
            pl.BlockSpec((1, L_TOT, KV_W), lambda b, j: (b, 0, 0)),
            pl.BlockSpec((1, L_TOT, V_W), lambda b, j: (b, 0, 0)),
        ],
        out_specs=pl.BlockSpec((1, TQ_G, MIX_W), lambda b, j: (b, j, 0)),
        compiler_params=_cparams(("parallel", "arbitrary")),
        name="global_attn",
    )(q, k, v)


TQ_W = 2 * WINDOW
N_WBLK = SEQ // WINDOW
N_WSTEP = SEQ // TQ_W


def _wattn_kernel(q_ref, k_ref, v_ref, sink_ref, o_ref):
    i = pl.program_id(1)
    heads, low = _head_rows(q_ref, TQ_W)
    prev_start = pl.multiple_of(jnp.maximum(2 * i - 1, 0) * WINDOW, WINDOW)
    mid_start = pl.multiple_of(i * TQ_W, TQ_W)
    next_start = pl.multiple_of(jnp.minimum(2 * i + 2, N_WBLK - 1) * WINDOW, WINDOW)

    def rows(ref):
        return jnp.concatenate([ref[0, SEQ:L_TOT, :], ref[0, pl.ds(prev_start, WINDOW), :],
                                ref[0, pl.ds(mid_start, TQ_W), :], ref[0, pl.ds(next_start, WINDOW), :]], axis=0)

    nk = CTX_LEN + 2 * WINDOW + TQ_W
    r = lax.broadcasted_iota(jnp.int32, (TQ_W, nk), 0)
    col = lax.broadcasted_iota(jnp.int32, (TQ_W, nk), 1)
    c = col - CTX_LEN
    ninf = -jnp.inf
    pen_prev = jnp.where(i > 0, 0.0, ninf)
    pen_next = jnp.where(i < N_WSTEP - 1, 0.0, ninf)
    edge = jnp.where(c < WINDOW, pen_prev, jnp.where(c >= WINDOW + TQ_W, pen_next, 0.0))
    band = jnp.where(c >= r, jnp.where(c <= r + 2 * WINDOW, edge, ninf), ninf)
    bias = jnp.where(col < CTX_LEN, 0.0, band)
    kk = rows(k_ref)
    vv = rows(v_ref)
    outs = []
    for hd, q in enumerate(heads):
        s = _qk(q, kk) + bias
        sink = sink_ref[0, HEAD_PERM[hd]] * math.log2(math.e)
        m = jnp.maximum(jnp.max(s, axis=-1, keepdims=True), sink)
        p = jnp.exp2(s - m).astype(BF16)
        acc = _dot(p, vv)
        outs.append(acc[:, :KV_W] / (acc[:, KV_W:] + jnp.exp2(sink - m)))
    o_ref[0] = _merge_heads(outs, low).astype(BF16)


def _wattn_call(q, k, v, sink):
    return pl.pallas_call(
        _wattn_kernel,
        out_shape=jax.ShapeDtypeStruct((BATCH, SEQ, MIX_W), BF16),
        grid=(BATCH, N_WSTEP),
        in_specs=[
            pl.BlockSpec((1, TQ_W, MIX_W), lambda b, i: (b, i, 0)),
            pl.BlockSpec((1, L_TOT, KV_W), lambda b, i: (b, 0, 0)),
            pl.BlockSpec((1, L_TOT, V_W), lambda b, i: (b, 0, 0)),
            pl.BlockSpec(memory_space=pltpu.SMEM),
        ],
        out_specs=pl.BlockSpec((1, TQ_W, MIX_W), lambda b, i: (b, i, 0)),
        compiler_params=_cparams(("parallel", "arbitrary")),
        name="window_attn",
    )(q, k, v, sink)


def _s5_disc_kernel(lr_ref, li_ref, ls_ref, br_ref, bi_ref, ar_ref, ai_ref, bbr_ref, bbi_ref):
    lr = lr_ref[0]
    li = li_ref[0]
    dt = jnp.exp(ls_ref[0])
    mag = jnp.exp(lr * dt)
    ang = li * dt
    abr = mag * jnp.cos(ang)
    abi = mag * jnp.sin(ang)
    den = lr * lr + li * li
    nr = abr - 1.0
    kr = (nr * lr + abi * li) / den
    ki = (abi * lr - nr * li) / den
    br = br_ref[0]
    bi = bi_ref[0]
    ar_ref[0] = abr
    ai_ref[0] = abi
    bbr_ref[0] = kr * br - ki * bi
    bbi_ref[0] = kr * bi + ki * br


def _s5_disc_call(lam_re, lam_im, log_step, bt_re, bt_im):
    n = S5_GROUPS * S5_STATE
    row = pl.BlockSpec((1, 1, n), lambda d: (d, 0, 0))
    mat = pl.BlockSpec((1, S5_GROUP, n), lambda d: (d, 0, 0))
    return pl.pallas_call(
        _s5_disc_kernel,
        out_shape=[jax.ShapeDtypeStruct((2, 1, n), F32)] * 2 + [jax.ShapeDtypeStruct((2, S5_GROUP, n), F32)] * 2,
        grid=(2,),
        in_specs=[row, row, row, mat, mat],
        out_specs=[row, row, mat, mat],
        compiler_params=_cparams(("parallel",)),
        name="s5_discretize",
    )(lam_re, lam_im, log_step, bt_re, bt_im)


S5_ROWS = S5_T * BATCH
S5_NC = L_TOT // S5_T
S5_NLAT = SEQ // S5_T
S5_SW = 2 * S5_BLK_STATE * S5_NBLK


def _s5_kernel(*refs, reverse):
    if reverse:
        u2_ref, un_ref, wd_ref, ar_ref, ai_ref, wr_ref, out_ref, buf0, buf1, xst = refs
    else:
        (u2_ref, un_ref, yb2_ref, wd_ref, ar_ref, ai_ref, wr_ref, dsk_ref, gw_ref, gb_ref,
         out_ref, buf0, buf1, xst) = refs
    nb = 2 * S5_BLK_STATE
    lo, hi = slice(0, S5_ROWS), slice(S5_ROWS, 2 * S5_ROWS)
    first, second = (hi, lo) if reverse else (lo, hi)

    def drive(u, buf):
        ub = u.astype(BF16)
        for k in range(S5_NBLK):
            buf[:, k * nb:(k + 1) * nb] = _dot(ub[:, k * LANE:(k + 1) * LANE], wd_ref[0, k])

    def scan(buf, x):
        for s in range(S5_T):
            t = (S5_T - 1 - s) if reverse else s
            rows = slice(t * BATCH, (t + 1) * BATCH)
            halves = []
            for k in range(S5_NBLK):
                halves.append(x[:, k * nb + S5_BLK_STATE:(k + 1) * nb])
                halves.append(x[:, k * nb:k * nb + S5_BLK_STATE])
            xsw = jnp.concatenate(halves, axis=1)
            x = ar_ref[0] * x + ai_ref[0] * xsw + buf[rows, :]
            buf[rows, :] = x
        return x

    def readout(buf, rows):
        y = jnp.concatenate(
            [_dot(buf[:, k * nb:(k + 1) * nb].astype(BF16), wr_ref[0, k]) for k in range(S5_NBLK)], axis=1)
        if reverse:
            out_ref[rows, :] = y
        else:
            y = u2_ref[rows, :] * dsk_ref[...] + y + yb2_ref[rows, :]
            z = _gelu_tanh(y)
            out_ref[rows, :] = (z * jax.nn.sigmoid(_dot(z.astype(BF16), gw_ref[...]) + gb_ref[...])).astype(BF16)

    @pl.when(pl.program_id(0) == 0)
    def _():
        xst[...] = jnp.zeros_like(xst)
        drive(u2_ref[first, :], buf0)

    x = xst[...]
    drive(u2_ref[second, :], buf1)
    x = scan(buf0, x)
    readout(buf0, first)
    drive(un_ref[...], buf0)
    x = scan(buf1, x)
    readout(buf1, second)
    xst[...] = x


def _s5_call(u, yb, wd, ar, ai, wr, dsk, gw, gb, reverse):
    def chunk(p):
        return _chunk_of(p, reverse, S5_NLAT, S5_NC)

    pair = pl.BlockSpec((2 * S5_ROWS, MIX_W), lambda k: (chunk(2 * k) // 2, 0))
    nxt = pl.BlockSpec((S5_ROWS, MIX_W), lambda k: (chunk(jnp.minimum(2 * k + 2, S5_NC - 1)), 0))
    d = 1 if reverse else 0
    dir_specs = [pl.BlockSpec((1,) + a.shape[1:], lambda k, n=a.ndim: (d,) + (0,) * (n - 1)) for a in (wd, ar, ai, wr)]
    if reverse:
        args = [u, u, wd, ar, ai, wr]
        specs = [pair, nxt] + dir_specs
        out_dtype = F32
    else:
        args = [u, u, yb, wd, ar, ai, wr, dsk, gw, gb]
        specs = [pair, nxt, pair] + dir_specs + [_full(a.shape) for a in (dsk, gw, gb)]
        out_dtype = BF16
    return pl.pallas_call(
        functools.partial(_s5_kernel, reverse=reverse),
        out_shape=jax.ShapeDtypeStruct((L_TOT * BATCH, MIX_W), out_dtype),
        grid=(S5_NC // 2,),
        in_specs=specs,
        out_specs=pair,
        scratch_shapes=[
            pltpu.VMEM((S5_ROWS, S5_SW), F32),
            pltpu.VMEM((S5_ROWS, S5_SW), F32),
            pltpu.VMEM((BATCH, S5_SW), F32),
        ],
        compiler_params=_cparams(("arbitrary",)),
        name="s5_bwd" if reverse else "s5_fwd",
    )(*args)


POST_ROWS = POST_T * BATCH
POST_HALO_T = BF16_ROWS // BATCH
POST_HALO = POST_HALO_T * BATCH
POST_NLAT = SEQ // POST_T
FFN_NCH = FFN_HIDDEN // FFN_CH


def _post_kernel(h_ref, hp_ref, hn_ref, a_ref, ap_ref, an_ref, b_ref, bp_ref, bn_ref, wo_ref,
                 g2_ref, sh_ref, sc_ref, g5_ref, g_ref, wu_ref, cw_ref, cb_ref, wd_ref, o_ref, *,
                 n_tiles, out_batch_major, layer):
    j = pl.program_id(0)
    first = jnp.logical_or(j == 0, j == POST_NLAT)
    last = jnp.logical_or(j == POST_NLAT - 1, j == n_tiles - 1)
    gate2 = g2_ref[0]

    def mix(h, a, b_tm):
        m = _dot(a, wo_ref[0:MIX_W, :]) + _dot(b_tm.astype(BF16), wo_ref[MIX_W:2 * MIX_W, :])
        return h + _per_sample(m, gate2, jnp.multiply)

    def norm(x):
        return _adaln(x, g_ref[layer, 1:2, :], sc_ref[0], sh_ref[0])

    def halo_b(ref, t0):
        x = jnp.swapaxes(ref[...].astype(F32), 0, 1)[t0:t0 + POST_HALO_T]
        return x.reshape(POST_HALO, MIX_W)

    h_mid = mix(h_ref[...], a_ref[...], _to_time_major(b_ref[...].astype(F32)))
    h_prev = mix(hp_ref[...], ap_ref[...], halo_b(bp_ref, BF16_ROWS - POST_HALO_T))
    h_next = mix(hn_ref[...], an_ref[...], halo_b(bn_ref, 0))
    xn = jnp.concatenate([
        (norm(h_prev) * jnp.where(first, 0.0, 1.0)).astype(BF16),
        norm(h_mid).astype(BF16),
        (norm(h_next) * jnp.where(last, 0.0, 1.0)).astype(BF16)], axis=0)

    acts = []
    for c in range(FFN_NCH):
        cv = slice(c * FFN_CH, (c + 1) * FFN_CH)
        cg = slice(FFN_HIDDEN + c * FFN_CH, FFN_HIDDEN + (c + 1) * FFN_CH)

        def conv(cols):
            hid = _dot(xn, wu_ref[:, cols])
            out = cb_ref[layer:layer + 1, cols]
            for k in range(FFN_CONV):
                r0 = POST_HALO + (k - 1) * BATCH
                out = out + hid[r0:r0 + POST_ROWS] * cw_ref[0, k:k + 1, cols]
            return out

        val = conv(cv)
        gate = conv(cg)
        acts.append((val * (gate * jax.nn.sigmoid(gate))).astype(BF16))
    ffn = _dot(jnp.concatenate(acts, axis=1), wd_ref[...])
    out = h_mid + _per_sample(ffn, g5_ref[0], jnp.multiply)
    if out_batch_major:
        o_ref[...] = _to_batch_major(out)
    else:
        o_ref[...] = out


def _post_call(h, a_tb, b_out, w_out, modt, g, wu, cw, cb, wd, layer, n_tiles, out_batch_major):
    per = POST_ROWS // POST_HALO
    n_halo = h.shape[0] // POST_HALO
    per_b = POST_T // BF16_ROWS
    n_halo_b = b_out.shape[1] // BF16_ROWS

    def main(width):
        return pl.BlockSpec((POST_ROWS, width), lambda j: (j, 0))

    def prev(width):
        return pl.BlockSpec((POST_HALO, width), lambda j: (jnp.maximum(j * per - 1, 0), 0))

    def nxt(width):
        return pl.BlockSpec((POST_HALO, width), lambda j: (jnp.minimum((j + 1) * per, n_halo - 1), 0))

    if out_batch_major:
        out_shape = jax.ShapeDtypeStruct((BATCH, n_tiles * POST_T, D_MODEL), F32)
        out_spec = pl.BlockSpec((BATCH, POST_T, D_MODEL), lambda j: (0, j, 0))
    else:
        out_shape = jax.ShapeDtypeStruct((n_tiles * POST_ROWS, D_MODEL), F32)
        out_spec = main(D_MODEL)
    return pl.pallas_call(
        functools.partial(_post_kernel, n_tiles=n_tiles, out_batch_major=out_batch_major, layer=layer),
        out_shape=out_shape,
        grid=(n_tiles,),
        in_specs=[
            main(D_MODEL), prev(D_MODEL), nxt(D_MODEL),
            main(MIX_W), prev(MIX_W), nxt(MIX_W),
            pl.BlockSpec((BATCH, POST_T, MIX_W), lambda j: (0, j, 0)),
            pl.BlockSpec((BATCH, BF16_ROWS, MIX_W), lambda j: (0, jnp.maximum(j * per_b - 1, 0), 0)),
            pl.BlockSpec((BATCH, BF16_ROWS, MIX_W), lambda j: (0, jnp.minimum((j + 1) * per_b, n_halo_b - 1), 0)),
            _full((2 * MIX_W, D_MODEL), single=True),
            _mod_spec(2, POST_NLAT), _mod_spec(3, POST_NLAT), _mod_spec(4, POST_NLAT), _mod_spec(5, POST_NLAT),
            _full(g.shape),
            _full((D_MODEL, 2 * FFN_HIDDEN), single=True),
            pl.BlockSpec((1, FFN_CONV, 2 * FFN_HIDDEN), lambda j: (layer, 0, 0)),
            _full(cb.shape),
            _full((FFN_HIDDEN, D_MODEL), single=True),
        ],
        out_specs=out_spec,
        compiler_params=_cparams(("parallel",)),
        name="post_ffn",
    )(h, h, h, a_tb, a_tb, a_tb, b_out, b_out, b_out, w_out, modt, modt, modt, modt, g, wu, cw, cb, wd)


def _rope_table():
    rows = SEQ // GRID_W
    row = jnp.repeat(jnp.arange(rows, dtype=F32), GRID_W)
    col = jnp.tile(jnp.arange(GRID_W, dtype=F32), rows)
    quarter = HEAD_DIM // 4
    inv_freq = ROPE_BASE ** (-jnp.arange(quarter, dtype=F32) / quarter)
    ang = jnp.stack([row[:, None] * inv_freq, col[:, None] * inv_freq], axis=1)
    cos = jnp.cos(ang)
    sin = jnp.sin(ang)
    zero = jnp.zeros_like(sin)

    def lanes(first, second):
        t = jnp.stack([first, second], axis=2).reshape(SEQ, HEAD_DIM)
        return jnp.tile(t, (1, LANE // HEAD_DIM))

    tab = jnp.concatenate([lanes(cos, cos), lanes(-sin, zero), lanes(zero, sin)], axis=1)
    ctx = jnp.concatenate([jnp.ones((CTX_LEN, LANE), F32), jnp.zeros((CTX_LEN, 2 * LANE), F32)], axis=1)
    return jnp.concatenate([tab, ctx], axis=0)


def _perm_heads_cols(w):
    return jnp.concatenate([w[:, h * HEAD_DIM:(h + 1) * HEAD_DIM] for h in HEAD_PERM], axis=1)


def _perm_heads_rows(w):
    return jnp.concatenate([w[h * HEAD_DIM:(h + 1) * HEAD_DIM] for h in HEAD_PERM], axis=0)


def _block_diag(w, per):
    n, a, b = w.shape
    eye = jnp.eye(per, dtype=w.dtype)
    w4 = w.reshape(n // per, per, a, b)
    return jnp.einsum("ihab,hk->ihakb", w4, eye).reshape(n // per, per * a, per * b)


def _w_in(w, q_off):
    w = w.astype(BF16)
    return jnp.concatenate([w[:, :q_off], _perm_heads_cols(w[:, q_off:q_off + MIX_W]), w[:, q_off + MIX_W:]], axis=1)


def _w_out(w):
    w = w.astype(BF16)
    return jnp.concatenate([w[:MIX_W], _perm_heads_rows(w[MIX_W:])], axis=0)


def kernel(x, c, ctx, c_ctx, mod_w, mod_b, norm_g, ffn_up, ffn_conv_w, ffn_conv_b, ffn_down, ev_w_in, ev_w_out, rg_conv_w, rg_conv_b, rg_wa, rg_ba, rg_wx, rg_bx, rg_lam, ga_qn, ga_kn, od_w_in, od_w_out, s5_lam_re, s5_lam_im, s5_log_step, s5_b_re, s5_b_im, s5_c_re, s5_c_im, s5_d, s5_glu_w, s5_glu_b, wa_qn, wa_kn, wa_sink):
    cvec = jnp.concatenate([c, jnp.broadcast_to(c_ctx[None], (BATCH, D_MODEL))], axis=0)
    mod = _mod_call(cvec, mod_w, mod_b).reshape(DEPTH, 2, BATCH, N_MOD * D_MODEL)
    rope = _rope_table()
    gm = _block_diag(jnp.full((LANE // HEAD_DIM, HEAD_DIM, HEAD_DIM), 1.0 / HEAD_DIM, F32), LANE // HEAD_DIM)[0].astype(BF16)

    h, u_tb, gate_tb, q, k, v, wu, wd = _in_call(
        (x, ctx), mod[0], norm_g, _w_in(ev_w_in[0], 2 * MIX_W), rope,
        ga_qn, ga_kn, gm, ffn_up, ffn_down, 0, n_tb=2, first_layer=True)
    w_gates = _block_diag(jnp.stack([rg_wa[0], rg_wx[0]]).astype(BF16).reshape(-1, HEAD_DIM, HEAD_DIM), 4)
    rg = (rg_conv_w, rg_conv_b, w_gates, rg_ba, rg_bx, rg_lam)
    hb, uc = _rg_call(u_tb, None, None, *rg, reverse=True)
    a_out = _rg_call(uc, gate_tb, hb, *rg, reverse=False)
    b_out = _gattn_call(q, k, v)
    h = _post_call(h, a_out, b_out, _w_out(ev_w_out[0]), mod[0], norm_g, wu, ffn_conv_w, ffn_conv_b, wd, 0,
                   n_tiles=L_TOT // POST_T, out_batch_major=False)

    u_tb, q, k, v, wu, wd = _in_call(
        (h,), mod[1], norm_g, _w_in(od_w_in[0], MIX_W), rope,
        wa_qn, wa_kn, gm, ffn_up, ffn_down, 1, n_tb=1, first_layer=False)
    n_state = S5_GROUPS * S5_STATE
    bt_re = s5_b_re[0].transpose(0, 3, 1, 2).reshape(2, S5_GROUP, n_state)
    bt_im = s5_b_im[0].transpose(0, 3, 1, 2).reshape(2, S5_GROUP, n_state)
    log_step = jnp.repeat(s5_log_step[0], S5_STATE, axis=-1).reshape(2, 1, n_state)
    abr, abi, bbr, bbi = _s5_disc_call(s5_lam_re[0].reshape(2, 1, n_state), s5_lam_im[0].reshape(2, 1, n_state),
                                       log_step, bt_re, bt_im)
    gpb = S5_GROUPS // S5_NBLK
    eye = jnp.eye(gpb, dtype=BF16)
    bb = jnp.stack([bbr, bbi]).astype(BF16).reshape(2, 2, S5_GROUP, S5_NBLK, gpb, S5_STATE)
    w_drive = jnp.einsum("rdhbgp,gk->dbghrkp", bb, eye).reshape(
        2, S5_NBLK, gpb * S5_GROUP, 2 * S5_BLK_STATE)
    cc = jnp.stack([s5_c_re[0], -s5_c_im[0]]).astype(BF16).reshape(
        2, 2, S5_NBLK, gpb, S5_GROUP, S5_STATE)
    w_read = jnp.einsum("rdbghp,gk->dbrgpkh", cc, eye).reshape(
        2, S5_NBLK, 2 * S5_BLK_STATE, gpb * S5_GROUP)

    def state_rows(re, im):
        row = jnp.concatenate([re.reshape(2, S5_NBLK, S5_BLK_STATE), im.reshape(2, S5_NBLK, S5_BLK_STATE)],
                              axis=2).reshape(2, 1, S5_SW)
        return jnp.broadcast_to(row, (2, BATCH, S5_SW))

    s5 = (w_drive, state_rows(abr, abr), state_rows(-abi, abi), w_read)
    yb = _s5_call(u_tb, None, *s5, None, None, None, reverse=True)
    c_out = _s5_call(u_tb, yb, *s5, s5_d, s5_glu_w[0].astype(BF16), s5_glu_b, reverse=False)
    d_out = _wattn_call(q, k, v, wa_sink)
    return _post_call(h, c_out, d_out, _w_out(od_w_out[0]), mod[1], norm_g, wu, ffn_conv_w, ffn_conv_b, wd, 1,
                      n_tiles=SEQ // POST_T, out_batch_major=True)
```

```python
import functools
import math

import jax
import jax.numpy as jnp
from jax import lax
from jax.experimental import pallas as pl
from jax.experimental.pallas import tpu as pltpu

F32 = jnp.float32
BF16 = jnp.bfloat16

D_MODEL = 1024
BATCH = 8
SEQ = 2048
CTX_LEN = 256
L_TOT = SEQ + CTX_LEN
DEPTH = 2
GRID_W = 64
HEAD_DIM = 64
ROPE_BASE = 10000.0
NORM_EPS = 1e-6
WINDOW = 128
N_MOD = 6
MIX_W = 512
N_HEADS = 8
N_KV = 2
KV_W = N_KV * HEAD_DIM
V_W = 2 * KV_W
QKV_W = MIX_W + KV_W + KV_W
Q_SCALE = HEAD_DIM ** -0.5 * math.log2(math.e)
RG_CONV = 4
RG_C = 8.0
S5_GROUP = 16
S5_GROUPS = 32
S5_STATE = 64
S5_NBLK = 4
S5_BLK_STATE = (S5_GROUPS // S5_NBLK) * S5_STATE
FFN_HIDDEN = 2816
FFN_CONV = 3

LANE = 128
BF16_ROWS = 16
VMEM_LIMIT = 56 * 1024 * 1024

IN_T = 128
RG_T = 256
S5_T = 64
POST_T = 128
TQ_G = 256
G_NB = 4
W_NB = 8
FFN_CH = 256
HEAD_PERM = (0, 4, 1, 5, 2, 6, 3, 7)


def _cparams(sem):
    return pltpu.CompilerParams(dimension_semantics=sem, vmem_limit_bytes=VMEM_LIMIT)


def _dot(a, b):
    return jnp.dot(a, b, preferred_element_type=F32)


def _gelu_tanh(x):
    return x * (0.5 * (1.0 + jnp.tanh(math.sqrt(2.0 / math.pi) * (x + 0.044715 * (x * x * x)))))


def _rms_unit(x):
    ms = jnp.mean(x * x, axis=-1, keepdims=True)
    return x * lax.rsqrt(ms + NORM_EPS)


def _adaln(x, gain, scale, shift):
    y = _per_sample(_rms_unit(x), gain * (1.0 + scale), jnp.multiply)
    return _per_sample(y, shift, jnp.add)


def _to_time_major(x):
    b, t, c = x.shape
    return jnp.swapaxes(x, 0, 1).reshape(t * b, c)


def _to_batch_major(x):
    r, c = x.shape
    return jnp.swapaxes(x.reshape(r // BATCH, BATCH, c), 0, 1)


def _per_sample(x, v, op):
    r, c = x.shape
    return op(x.reshape(r // BATCH, BATCH, c), v[None]).reshape(r, c)


def _full(shape, single=False):
    kw = dict(pipeline_mode=pl.Buffered(1)) if single else {}
    return pl.BlockSpec(shape, lambda *_: (0,) * len(shape), **kw)


MOD_TN = 3072


def _mod_kernel(c_ref, wa_ref, wb_ref, b_ref, o_ref):
    c = c_ref[...]
    a = c * jax.nn.sigmoid(c)
    ah = a.astype(BF16)
    al = (a - ah.astype(F32)).astype(BF16)
    for i, w_ref in enumerate((wa_ref, wb_ref)):
        w = w_ref[0]
        wh = w.astype(BF16)
        wl = (w - wh.astype(F32)).astype(BF16)
        cols = slice(i * (MOD_TN // 2), (i + 1) * (MOD_TN // 2))
        o_ref[0, :, cols] = _dot(ah, wh) + _dot(ah, wl) + _dot(al, wh) + b_ref[0, :, cols]


def _mod_call(cvec, mod_w, mod_b):
    n = N_MOD * D_MODEL
    return pl.pallas_call(
        _mod_kernel,
        out_shape=jax.ShapeDtypeStruct((DEPTH, 2 * BATCH, n), F32),
        grid=(DEPTH, n // MOD_TN),
        in_specs=[
            pl.BlockSpec((2 * BATCH, D_MODEL), lambda l, k: (0, 0)),
            pl.BlockSpec((1, D_MODEL, MOD_TN // 2), lambda l, k: (l, 0, 2 * k)),
            pl.BlockSpec((1, D_MODEL, MOD_TN // 2), lambda l, k: (l, 0, 2 * k + 1)),
            pl.BlockSpec((1, 1, MOD_TN), lambda l, k: (l, 0, k)),
        ],
        out_specs=pl.BlockSpec((1, 2 * BATCH, MOD_TN), lambda l, k: (l, 0, k)),
        compiler_params=_cparams(("parallel", "parallel")),
        name="mod",
    )(cvec, mod_w, mod_w, mod_b.reshape(DEPTH, 1, n))


def _mod_spec(col, n_latent_tiles):
    return pl.BlockSpec((1, BATCH, D_MODEL), lambda j: (jnp.where(j < n_latent_tiles, 0, 1), 0, col))


IN_ROWS = IN_T * BATCH
IN_NLAT = SEQ // IN_T
IN_N = L_TOT // IN_T
IN_SUB = 2
CAST_STEPS = 16
assert CAST_STEPS <= IN_N
IN_SUB_T = IN_T // IN_SUB
IN_SUB_ROWS = IN_SUB_T * BATCH


def _in_kernel(fu_ref, fd_ref, *refs, n_tb, first_layer, layer):
    fu_out, fd_out = refs[-2:]
    refs = refs[:-2]

    @pl.when(pl.program_id(0) < CAST_STEPS)
    def _():
        fu_out[...] = fu_ref[0].astype(BF16)
        fd_out[...] = fd_ref[0].astype(BF16)

    if first_layer:
        x_ref, c_ref, sh_ref, sc_ref, g_ref, w_ref, rope_ref, gq_ref, gk_ref, gm_ref = refs[:10]
        h_out = refs[10]
        outs = refs[11:]
        is_latent = pl.program_id(0) < IN_NLAT
    else:
        h_ref, sh_ref, sc_ref, g_ref, w_ref, rope_ref, gq_ref, gk_ref, gm_ref = refs[:9]
        outs = refs[9:]
    q_ref, k_ref, v_ref = outs[n_tb:]
    off = n_tb * MIX_W
    gm = gm_ref[...]
    gq = jnp.concatenate([gq_ref[...]] * (LANE // HEAD_DIM), axis=1)
    gk = jnp.concatenate([gk_ref[...]] * (LANE // HEAD_DIM), axis=1)
    for s in range(IN_SUB):
        ts = slice(s * IN_SUB_T, (s + 1) * IN_SUB_T)
        rs = slice(s * IN_SUB_ROWS, (s + 1) * IN_SUB_ROWS)
        if first_layer:
            h = _to_time_major(jnp.where(is_latent, x_ref[:, ts, :], c_ref[:, ts, :]))
            h_out[rs, :] = h
        else:
            h = h_ref[rs, :]
        xn = _adaln(h, g_ref[layer, 0:1, :], sc_ref[0], sh_ref[0])
        proj = _dot(xn.astype(BF16), w_ref[...])
        for i in range(n_tb):
            outs[i][rs, :] = proj[:, i * MIX_W:(i + 1) * MIX_W]
        qkv = _to_batch_major(proj[:, off:off + QKV_W])
        cos = rope_ref[ts, 0:LANE][None]
        sin_up = rope_ref[ts, LANE:2 * LANE][None]
        sin_dn = rope_ref[ts, 2 * LANE:3 * LANE][None]
        for p in range(5):
            blk = qkv[:, :, p * LANE:(p + 1) * LANE].reshape(IN_SUB_ROWS, LANE)
            sq = blk * blk
            hi = sq.astype(BF16)
            lo = (sq - hi.astype(F32)).astype(BF16)
            ms = _dot(hi, gm) + _dot(lo, gm)
            g = gq if p < 4 else gk
            bn = blk * lax.rsqrt(ms + NORM_EPS) * g
            up = pltpu.roll(bn, LANE - HEAD_DIM // 4, 1).reshape(BATCH, IN_SUB_T, LANE)
            dn = pltpu.roll(bn, HEAD_DIM // 4, 1).reshape(BATCH, IN_SUB_T, LANE)
            ro = bn.reshape(BATCH, IN_SUB_T, LANE) * cos + up * sin_up + dn * sin_dn
            if p < 4:
                q_ref[:, ts, p * LANE:(p + 1) * LANE] = (ro * Q_SCALE).astype(BF16)
            else:
                k_ref[:, ts, :] = ro.astype(BF16)
        v = qkv[:, :, 5 * LANE:6 * LANE].astype(BF16)
        v_ref[:, ts, :] = jnp.concatenate([v, jnp.ones((BATCH, IN_SUB_T, LANE), BF16)], axis=2)


def _in_call(h_args, modt, g, w, rope, gq, gk, gm, ffn_up, ffn_down, layer, n_tb, first_layer):
    n = w.shape[1]
    tb = lambda width: pl.BlockSpec((IN_ROWS, width), lambda j: (j, 0))
    bm = lambda width: pl.BlockSpec((BATCH, IN_T, width), lambda j: (0, j, 0))
    up_rows = D_MODEL // CAST_STEPS
    down_rows = FFN_HIDDEN // CAST_STEPS
    slab = lambda j: jnp.minimum(j, CAST_STEPS - 1)
    if first_layer:
        h_specs = [
            pl.BlockSpec((BATCH, IN_T, D_MODEL), lambda j: (0, jnp.minimum(j, IN_NLAT - 1), 0)),
            pl.BlockSpec((BATCH, IN_T, D_MODEL), lambda j: (0, jnp.maximum(j - IN_NLAT, 0), 0)),
        ]
        extra_shape = [jax.ShapeDtypeStruct((L_TOT * BATCH, D_MODEL), F32)]
        extra_spec = [tb(D_MODEL)]
    else:
        h_specs = [tb(D_MODEL)]
        extra_shape, extra_spec = [], []
    return pl.pallas_call(
        functools.partial(_in_kernel, n_tb=n_tb, first_layer=first_layer, layer=layer),
        out_shape=extra_shape + [jax.ShapeDtypeStruct((L_TOT * BATCH, MIX_W), F32)] * n_tb + [
            jax.ShapeDtypeStruct((BATCH, L_TOT, MIX_W), BF16),
            jax.ShapeDtypeStruct((BATCH, L_TOT, KV_W), BF16),
            jax.ShapeDtypeStruct((BATCH, L_TOT, V_W), BF16),
            jax.ShapeDtypeStruct((D_MODEL, 2 * FFN_HIDDEN), BF16),
            jax.ShapeDtypeStruct((FFN_HIDDEN, D_MODEL), BF16),
        ],
        grid=(IN_N,),
        in_specs=[
            pl.BlockSpec((1, up_rows, 2 * FFN_HIDDEN), lambda j: (layer, slab(j), 0)),
            pl.BlockSpec((1, down_rows, D_MODEL), lambda j: (layer, slab(j), 0)),
        ] + h_specs + [
            _mod_spec(0, IN_NLAT), _mod_spec(1, IN_NLAT),
            _full(g.shape),
            _full((D_MODEL, n), single=True),
            pl.BlockSpec((IN_T, 3 * LANE), lambda j: (j, 0)),
            _full(gq.shape), _full(gk.shape), _full((LANE, LANE)),
        ],
        out_specs=extra_spec + [tb(MIX_W)] * n_tb + [bm(MIX_W), bm(KV_W), bm(V_W)] + [
            pl.BlockSpec((up_rows, 2 * FFN_HIDDEN), lambda j: (slab(j), 0)),
            pl.BlockSpec((down_rows, D_MODEL), lambda j: (slab(j), 0)),
        ],
        compiler_params=_cparams(("arbitrary",)),
        name="in_proj",
    )(ffn_up, ffn_down, *h_args, modt, modt, g, w, rope, gq, gk, gm)


def _chunk_of(i, reverse, n_latent, n_chunks):
    if reverse:
        return n_chunks - 1 - i
    return jnp.where(i < n_chunks - n_latent, i + n_latent, i - (n_chunks - n_latent))


RG_ROWS = RG_T * BATCH
RG_NC = L_TOT // RG_T
RG_NLAT = SEQ // RG_T


def _softplus(z):
    return jnp.maximum(z, 0.0) + jnp.log1p(jnp.exp(-jnp.abs(z)))


def _rg_kernel(*refs, reverse):
    if reverse:
        (u_ref, up_ref, un_ref, cw_ref, cb_ref, w_ref, ba_ref, bx_ref, lam_ref,
         out_ref, uc_out, xbuf, abuf, bbuf, hst) = refs
    else:
        uc_ref, gate_ref, hb_ref, w_ref, ba_ref, bx_ref, lam_ref, out_ref, abuf, bbuf, hst = refs
    d = 1 if reverse else 0
    i = pl.program_id(0)

    @pl.when(i == 0)
    def _():
        hst[...] = jnp.zeros_like(hst)

    if reverse:
        c = _chunk_of(i, reverse, RG_NLAT, RG_NC)
        prev_zero = jnp.logical_or(c == 0, c == RG_NLAT)
        next_zero = jnp.logical_or(c == RG_NLAT - 1, c == RG_NC - 1)
        xbuf[0:BATCH, :] = up_ref[...] * jnp.where(prev_zero, 0.0, 1.0)
        xbuf[BATCH:BATCH + RG_ROWS, :] = u_ref[...]
        xbuf[BATCH + RG_ROWS:3 * BATCH + RG_ROWS, :] = un_ref[...] * jnp.where(next_zero, 0.0, 1.0)
        uc = cb_ref[...]
        for k in range(RG_CONV):
            uc = uc + xbuf[k * BATCH:k * BATCH + RG_ROWS, :] * cw_ref[0, k:k + 1, :]
        uc_out[...] = uc
    else:
        uc = uc_ref[...]

    ub = uc.astype(BF16)
    half = MIX_W // 2

    def gate(which, b_ref):
        w0 = which * 4 + d * 2
        z = jnp.concatenate([_dot(ub[:, :half], w_ref[w0]), _dot(ub[:, half:], w_ref[w0 + 1])], axis=1)
        return jax.nn.sigmoid(z + b_ref[0, d:d + 1, :])

    r = gate(0, ba_ref)
    ig = gate(1, bx_ref)
    log_a = (-RG_C) * r * _softplus(-lam_ref[0, d:d + 1, :])
    a = jnp.exp(log_a)
    abuf[...] = a
    bbuf[...] = jnp.sqrt(-jnp.tanh(log_a) * (a * a + 1.0)) * (ig * uc)

    h = hst[...]
    for s in range(RG_T):
        t = (RG_T - 1 - s) if reverse else s
        rows = slice(t * BATCH, (t + 1) * BATCH)
        h = abuf[rows, :] * h + bbuf[rows, :]
        bbuf[rows, :] = h
    hst[...] = h

    if reverse:
        out_ref[...] = bbuf[...]
    else:
        y = bbuf[...] + hb_ref[...]
        out_ref[...] = (y * _gelu_tanh(gate_ref[...])).astype(BF16)


def _rg_call(u, gate, hb, cw, cb, w_gates, ba, bx, lam, reverse):
    def cidx(i):
        return _chunk_of(i, reverse, RG_NLAT, RG_NC)

    main = pl.BlockSpec((RG_ROWS, MIX_W), lambda i: (cidx(i), 0))
    gates = [w_gates, ba, bx, lam]
    scratch = [pltpu.VMEM((RG_ROWS, MIX_W), F32), pltpu.VMEM((RG_ROWS, MIX_W), F32), pltpu.VMEM((BATCH, MIX_W), F32)]
    rows = jax.ShapeDtypeStruct((L_TOT * BATCH, MIX_W), F32)
    if reverse:
        prev = pl.BlockSpec((BATCH, MIX_W), lambda i: (jnp.maximum(cidx(i) * RG_T - 1, 0), 0))
        n_next = L_TOT // 2
        nxt = pl.BlockSpec((2 * BATCH, MIX_W), lambda i: (jnp.minimum((cidx(i) + 1) * (RG_T // 2), n_next - 1), 0))
        args = [u, u, u, cw, cb] + gates
        specs = [main, prev, nxt] + [_full(p.shape) for p in args[3:]]
        out_shape, out_specs = [rows, rows], [main, main]
        scratch = [pltpu.VMEM((RG_ROWS + 3 * BATCH, MIX_W), F32)] + scratch
    else:
        args = [u, gate, hb] + gates
        specs = [main, main, main] + [_full(p.shape) for p in gates]
        out_shape, out_specs = jax.ShapeDtypeStruct((L_TOT * BATCH, MIX_W), BF16), main
    return pl.pallas_call(
        functools.partial(_rg_kernel, reverse=reverse),
        out_shape=out_shape,
        grid=(RG_NC,),
        in_specs=specs,
        out_specs=out_specs,
        scratch_shapes=scratch,
        compiler_params=_cparams(("arbitrary",)),
        name="rglru_bwd" if reverse else "rglru_fwd",
    )(*args)


def _head_rows(q_ref, tq, b):
    lane = lax.broadcasted_iota(jnp.int32, (tq, LANE), 1)
    low = lane < HEAD_DIM
    zero = jnp.zeros((tq, LANE), BF16)
    parts = []
    for p in range(MIX_W // LANE):
        qb = q_ref[b, :, p * LANE:(p + 1) * LANE]
        parts.append(jnp.where(low, qb, zero))
        parts.append(jnp.where(low, zero, qb))
    return parts, low


def _merge_heads(outs, low):
    return jnp.concatenate([jnp.where(low, outs[2 * p], outs[2 * p + 1]) for p in range(MIX_W // LANE)], axis=1)


def _qk(q, k):
    return lax.dot_general(q, k, (((1,), (1,)), ((), ())), preferred_element_type=F32)


def _gattn_kernel(q_ref, k_ref, v_ref, o_ref):
    j = pl.program_id(1)

    def attend(k0, nk):
        for b in range(G_NB):
            heads, low = _head_rows(q_ref, TQ_G, b)
            outs = []
            for q in heads:
                s = _qk(q, k_ref[b, k0:k0 + nk, :])
                m = jnp.max(s, axis=-1, keepdims=True)
                p = jnp.exp2(s - m).astype(BF16)
                acc = _dot(p, v_ref[b, k0:k0 + nk, :])
                outs.append(acc[:, :KV_W] / acc[:, KV_W:])
            o_ref[b] = _merge_heads(outs, low).astype(BF16)

    @pl.when(j < SEQ // TQ_G)
    def _():
        attend(0, L_TOT)

    @pl.when(j >= SEQ // TQ_G)
    def _():
        attend(SEQ, CTX_LEN)


def _gattn_call(q, k, v):
    return pl.pallas_call(
        _gattn_kernel,
        out_shape=jax.ShapeDtypeStruct((BATCH, L_TOT, MIX_W), BF16),
        grid=(BATCH // G_NB, L_TOT // TQ_G),
        in_specs=[
            pl.BlockSpec((G_NB, TQ_G, MIX_W), lambda b, j: (b, j, 0)),
            pl.BlockSpec((G_NB, L_TOT, KV_W), lambda b, j: (b, 0, 0)),
            pl.BlockSpec((G_NB, L_TOT, V_W), lambda b, j: (b, 0, 0)),
        ],
        out_specs=pl.BlockSpec((G_NB, TQ_G, MIX_W), lambda b, j: (b, j, 0)),
        compiler_params=_cparams(("parallel", "arbitrary")),
        name="global_attn",
    )(q, k, v)


TQ_W = 2 * WINDOW
N_WBLK = SEQ // WINDOW
N_WSTEP = SEQ // TQ_W


def _wattn_kernel(q_ref, k_ref, v_ref, sink_ref, o_ref):
    i = pl.program_id(1)
    prev_start = pl.multiple_of(jnp.maximum(2 * i - 1, 0) * WINDOW, WINDOW)
    mid_start = pl.multiple_of(i * TQ_W, TQ_W)
    next_start = pl.multiple_of(jnp.minimum(2 * i + 2, N_WBLK - 1) * WINDOW, WINDOW)

    def rows(ref, b):
        return jnp.concatenate([ref[b, SEQ:L_TOT, :], ref[b, pl.ds(prev_start, WINDOW), :],
                                ref[b, pl.ds(mid_start, TQ_W), :], ref[b, pl.ds(next_start, WINDOW), :]], axis=0)

    nk = CTX_LEN + 2 * WINDOW + TQ_W
    r = lax.broadcasted_iota(jnp.int32, (TQ_W, nk), 0)
    col = lax.broadcasted_iota(jnp.int32, (TQ_W, nk), 1)
    c = col - CTX_LEN
    ninf = -jnp.inf
    pen_prev = jnp.where(i > 0, 0.0, ninf)
    pen_next = jnp.where(i < N_WSTEP - 1, 0.0, ninf)
    edge = jnp.where(c < WINDOW, pen_prev, jnp.where(c >= WINDOW + TQ_W, pen_next, 0.0))
    band = jnp.where(c >= r, jnp.where(c <= r + 2 * WINDOW, edge, ninf), ninf)
    bias = jnp.where(col < CTX_LEN, 0.0, band)
    for b in range(W_NB):
        heads, low = _head_rows(q_ref, TQ_W, b)
        kk = rows(k_ref, b)
        vv = rows(v_ref, b)
        outs = []
        for hd, q in enumerate(heads):
            s = _qk(q, kk) + bias
            sink = sink_ref[0, HEAD_PERM[hd]] * math.log2(math.e)
            m = jnp.maximum(jnp.max(s, axis=-1, keepdims=True), sink)
            p = jnp.exp2(s - m).astype(BF16)
            acc = _dot(p, vv)
            outs.append(acc[:, :KV_W] / (acc[:, KV_W:] + jnp.exp2(sink - m)))
        o_ref[b] = _merge_heads(outs, low).astype(BF16)


def _wattn_call(q, k, v, sink):
    return pl.pallas_call(
        _wattn_kernel,
        out_shape=jax.ShapeDtypeStruct((BATCH, SEQ, MIX_W), BF16),
        grid=(BATCH // W_NB, N_WSTEP),
        in_specs=[
            pl.BlockSpec((W_NB, TQ_W, MIX_W), lambda b, i: (b, i, 0)),
            pl.BlockSpec((W_NB, L_TOT, KV_W), lambda b, i: (b, 0, 0)),
            pl.BlockSpec((W_NB, L_TOT, V_W), lambda b, i: (b, 0, 0)),
            pl.BlockSpec(memory_space=pltpu.SMEM),
        ],
        out_specs=pl.BlockSpec((W_NB, TQ_W, MIX_W), lambda b, i: (b, i, 0)),
        compiler_params=_cparams(("parallel", "arbitrary")),
        name="window_attn",
    )(q, k, v, sink)


def _s5_disc_kernel(lr_ref, li_ref, ls_ref, br_ref, bi_ref, ar_ref, ai_ref, bbr_ref, bbi_ref):
    lr = lr_ref[0]
    li = li_ref[0]
    dt = jnp.exp(ls_ref[0])
    mag = jnp.exp(lr * dt)
    ang = li * dt
    abr = mag * jnp.cos(ang)
    abi = mag * jnp.sin(ang)
    den = lr * lr + li * li
    nr = abr - 1.0
    kr = (nr * lr + abi * li) / den
    ki = (abi * lr - nr * li) / den
    br = br_ref[0]
    bi = bi_ref[0]
    ar_ref[0] = abr
    ai_ref[0] = abi
    bbr_ref[0] = kr * br - ki * bi
    bbi_ref[0] = kr * bi + ki * br


def _s5_disc_call(lam_re, lam_im, log_step, bt_re, bt_im):
    n = S5_GROUPS * S5_STATE
    row = pl.BlockSpec((1, 1, n), lambda d: (d, 0, 0))
    mat = pl.BlockSpec((1, S5_GROUP, n), lambda d: (d, 0, 0))
    return pl.pallas_call(
        _s5_disc_kernel,
        out_shape=[jax.ShapeDtypeStruct((2, 1, n), F32)] * 2 + [jax.ShapeDtypeStruct((2, S5_GROUP, n), F32)] * 2,
        grid=(2,),
        in_specs=[row, row, row, mat, mat],
        out_specs=[row, row, mat, mat],
        compiler_params=_cparams(("parallel",)),
        name="s5_discretize",
    )(lam_re, lam_im, log_step, bt_re, bt_im)


S5_ROWS = S5_T * BATCH
S5_NC = L_TOT // S5_T
S5_NLAT = SEQ // S5_T
S5_SW = 2 * S5_BLK_STATE * S5_NBLK


def _s5_kernel(*refs, reverse):
    if reverse:
        u2_ref, un_ref, wd_ref, ar_ref, ai_ref, wr_ref, out_ref, buf0, buf1, xst = refs
    else:
        (u2_ref, un_ref, yb2_ref, wd_ref, ar_ref, ai_ref, wr_ref, dsk_ref, gw_ref, gb_ref,
         out_ref, buf0, buf1, xst) = refs
    nb = 2 * S5_BLK_STATE
    lo, hi = slice(0, S5_ROWS), slice(S5_ROWS, 2 * S5_ROWS)
    first, second = (hi, lo) if reverse else (lo, hi)

    def drive(u, buf):
        ub = u.astype(BF16)
        for k in range(S5_NBLK):
            buf[:, k * nb:(k + 1) * nb] = _dot(ub[:, k * LANE:(k + 1) * LANE], wd_ref[0, k])

    def scan(buf, x):
        for s in range(S5_T):
            t = (S5_T - 1 - s) if reverse else s
            rows = slice(t * BATCH, (t + 1) * BATCH)
            halves = []
            for k in range(S5_NBLK):
                halves.append(x[:, k * nb + S5_BLK_STATE:(k + 1) * nb])
                halves.append(x[:, k * nb:k * nb + S5_BLK_STATE])
            xsw = jnp.concatenate(halves, axis=1)
            x = ar_ref[0] * x + ai_ref[0] * xsw + buf[rows, :]
            buf[rows, :] = x
        return x

    def readout(buf, rows):
        y = jnp.concatenate(
            [_dot(buf[:, k * nb:(k + 1) * nb].astype(BF16), wr_ref[0, k]) for k in range(S5_NBLK)], axis=1)
        if reverse:
            out_ref[rows, :] = y
        else:
            y = u2_ref[rows, :] * dsk_ref[...] + y + yb2_ref[rows, :]
            z = _gelu_tanh(y)
            out_ref[rows, :] = (z * jax.nn.sigmoid(_dot(z.astype(BF16), gw_ref[...]) + gb_ref[...])).astype(BF16)

    @pl.when(pl.program_id(0) == 0)
    def _():
        xst[...] = jnp.zeros_like(xst)
        drive(u2_ref[first, :], buf0)

    x = xst[...]
    drive(u2_ref[second, :], buf1)
    x = scan(buf0, x)
    readout(buf0, first)
    drive(un_ref[...], buf0)
    x = scan(buf1, x)
    readout(buf1, second)
    xst[...] = x


def _s5_call(u, yb, wd, ar, ai, wr, dsk, gw, gb, reverse):
    def chunk(p):
        return _chunk_of(p, reverse, S5_NLAT, S5_NC)

    pair = pl.BlockSpec((2 * S5_ROWS, MIX_W), lambda k: (chunk(2 * k) // 2, 0))
    nxt = pl.BlockSpec((S5_ROWS, MIX_W), lambda k: (chunk(jnp.minimum(2 * k + 2, S5_NC - 1)), 0))
    d = 1 if reverse else 0
    dir_specs = [pl.BlockSpec((1,) + a.shape[1:], lambda k, n=a.ndim: (d,) + (0,) * (n - 1)) for a in (wd, ar, ai, wr)]
    if reverse:
        args = [u, u, wd, ar, ai, wr]
        specs = [pair, nxt] + dir_specs
        out_dtype = F32
    else:
        args = [u, u, yb, wd, ar, ai, wr, dsk, gw, gb]
        specs = [pair, nxt, pair] + dir_specs + [_full(a.shape) for a in (dsk, gw, gb)]
        out_dtype = BF16
    return pl.pallas_call(
        functools.partial(_s5_kernel, reverse=reverse),
        out_shape=jax.ShapeDtypeStruct((L_TOT * BATCH, MIX_W), out_dtype),
        grid=(S5_NC // 2,),
        in_specs=specs,
        out_specs=pair,
        scratch_shapes=[
            pltpu.VMEM((S5_ROWS, S5_SW), F32),
            pltpu.VMEM((S5_ROWS, S5_SW), F32),
            pltpu.VMEM((BATCH, S5_SW), F32),
        ],
        compiler_params=_cparams(("arbitrary",)),
        name="s5_bwd" if reverse else "s5_fwd",
    )(*args)


POST_ROWS = POST_T * BATCH
POST_HALO_T = BF16_ROWS // BATCH
POST_HALO = POST_HALO_T * BATCH
POST_NLAT = SEQ // POST_T
FFN_NCH = FFN_HIDDEN // FFN_CH


def _post_kernel(h_ref, hp_ref, hn_ref, a_ref, ap_ref, an_ref, b_ref, bp_ref, bn_ref, wo_ref,
                 g2_ref, sh_ref, sc_ref, g5_ref, g_ref, wu_ref, cw_ref, cb_ref, wd_ref, o_ref, *,
                 n_tiles, out_batch_major, layer):
    j = pl.program_id(0)
    first = jnp.logical_or(j == 0, j == POST_NLAT)
    last = jnp.logical_or(j == POST_NLAT - 1, j == n_tiles - 1)
    gate2 = g2_ref[0]

    def mix(h, a, b_tm):
        m = _dot(a, wo_ref[0:MIX_W, :]) + _dot(b_tm.astype(BF16), wo_ref[MIX_W:2 * MIX_W, :])
        return h + _per_sample(m, gate2, jnp.multiply)

    def norm(x):
        return _adaln(x, g_ref[layer, 1:2, :], sc_ref[0], sh_ref[0])

    def halo_b(ref, t0):
        x = jnp.swapaxes(ref[...].astype(F32), 0, 1)[t0:t0 + POST_HALO_T]
        return x.reshape(POST_HALO, MIX_W)

    h_mid = mix(h_ref[...], a_ref[...], _to_time_major(b_ref[...].astype(F32)))
    h_prev = mix(hp_ref[...], ap_ref[...], halo_b(bp_ref, BF16_ROWS - POST_HALO_T))
    h_next = mix(hn_ref[...], an_ref[...], halo_b(bn_ref, 0))
    xn = jnp.concatenate([
        (norm(h_prev) * jnp.where(first, 0.0, 1.0)).astype(BF16),
        norm(h_mid).astype(BF16),
        (norm(h_next) * jnp.where(last, 0.0, 1.0)).astype(BF16)], axis=0)

    acts = []
    for c in range(FFN_NCH):
        cv = slice(c * FFN_CH, (c + 1) * FFN_CH)
        cg = slice(FFN_HIDDEN + c * FFN_CH, FFN_HIDDEN + (c + 1) * FFN_CH)

        def conv(cols):
            hid = _dot(xn, wu_ref[:, cols])
            out = cb_ref[layer:layer + 1, cols]
            for k in range(FFN_CONV):
                r0 = POST_HALO + (k - 1) * BATCH
                out = out + hid[r0:r0 + POST_ROWS] * cw_ref[0, k:k + 1, cols]
            return out

        val = conv(cv)
        gate = conv(cg)
        acts.append((val * (gate * jax.nn.sigmoid(gate))).astype(BF16))
    ffn = _dot(jnp.concatenate(acts, axis=1), wd_ref[...])
    out = h_mid + _per_sample(ffn, g5_ref[0], jnp.multiply)
    if out_batch_major:
        o_ref[...] = _to_batch_major(out)
    else:
        o_ref[...] = out


def _post_call(h, a_tb, b_out, w_out, modt, g, wu, cw, cb, wd, layer, n_tiles, out_batch_major):
    per = POST_ROWS // POST_HALO
    n_halo = h.shape[0] // POST_HALO
    per_b = POST_T // BF16_ROWS
    n_halo_b = b_out.shape[1] // BF16_ROWS

    def main(width):
        return pl.BlockSpec((POST_ROWS, width), lambda j: (j, 0))

    def prev(width):
        return pl.BlockSpec((POST_HALO, width), lambda j: (jnp.maximum(j * per - 1, 0), 0))

    def nxt(width):
        return pl.BlockSpec((POST_HALO, width), lambda j: (jnp.minimum((j + 1) * per, n_halo - 1), 0))

    if out_batch_major:
        out_shape = jax.ShapeDtypeStruct((BATCH, n_tiles * POST_T, D_MODEL), F32)
        out_spec = pl.BlockSpec((BATCH, POST_T, D_MODEL), lambda j: (0, j, 0))
    else:
        out_shape = jax.ShapeDtypeStruct((n_tiles * POST_ROWS, D_MODEL), F32)
        out_spec = main(D_MODEL)
    return pl.pallas_call(
        functools.partial(_post_kernel, n_tiles=n_tiles, out_batch_major=out_batch_major, layer=layer),
        out_shape=out_shape,
        grid=(n_tiles,),
        in_specs=[
            main(D_MODEL), prev(D_MODEL), nxt(D_MODEL),
            main(MIX_W), prev(MIX_W), nxt(MIX_W),
            pl.BlockSpec((BATCH, POST_T, MIX_W), lambda j: (0, j, 0)),
            pl.BlockSpec((BATCH, BF16_ROWS, MIX_W), lambda j: (0, jnp.maximum(j * per_b - 1, 0), 0)),
            pl.BlockSpec((BATCH, BF16_ROWS, MIX_W), lambda j: (0, jnp.minimum((j + 1) * per_b, n_halo_b - 1), 0)),
            _full((2 * MIX_W, D_MODEL), single=True),
            _mod_spec(2, POST_NLAT), _mod_spec(3, POST_NLAT), _mod_spec(4, POST_NLAT), _mod_spec(5, POST_NLAT),
            _full(g.shape),
            _full((D_MODEL, 2 * FFN_HIDDEN), single=True),
            pl.BlockSpec((1, FFN_CONV, 2 * FFN_HIDDEN), lambda j: (layer, 0, 0)),
            _full(cb.shape),
            _full((FFN_HIDDEN, D_MODEL), single=True),
        ],
        out_specs=out_spec,
        compiler_params=_cparams(("parallel",)),
        name="post_ffn",
    )(h, h, h, a_tb, a_tb, a_tb, b_out, b_out, b_out, w_out, modt, modt, modt, modt, g, wu, cw, cb, wd)


def _rope_table():
    rows = SEQ // GRID_W
    row = jnp.repeat(jnp.arange(rows, dtype=F32), GRID_W)
    col = jnp.tile(jnp.arange(GRID_W, dtype=F32), rows)
    quarter = HEAD_DIM // 4
    inv_freq = ROPE_BASE ** (-jnp.arange(quarter, dtype=F32) / quarter)
    ang = jnp.stack([row[:, None] * inv_freq, col[:, None] * inv_freq], axis=1)
    cos = jnp.cos(ang)
    sin = jnp.sin(ang)
    zero = jnp.zeros_like(sin)

    def lanes(first, second):
        t = jnp.stack([first, second], axis=2).reshape(SEQ, HEAD_DIM)
        return jnp.tile(t, (1, LANE // HEAD_DIM))

    tab = jnp.concatenate([lanes(cos, cos), lanes(-sin, zero), lanes(zero, sin)], axis=1)
    ctx = jnp.concatenate([jnp.ones((CTX_LEN, LANE), F32), jnp.zeros((CTX_LEN, 2 * LANE), F32)], axis=1)
    return jnp.concatenate([tab, ctx], axis=0)


def _perm_heads_cols(w):
    return jnp.concatenate([w[:, h * HEAD_DIM:(h + 1) * HEAD_DIM] for h in HEAD_PERM], axis=1)


def _perm_heads_rows(w):
    return jnp.concatenate([w[h * HEAD_DIM:(h + 1) * HEAD_DIM] for h in HEAD_PERM], axis=0)


def _block_diag(w, per):
    n, a, b = w.shape
    eye = jnp.eye(per, dtype=w.dtype)
    w4 = w.reshape(n // per, per, a, b)
    return jnp.einsum("ihab,hk->ihakb", w4, eye).reshape(n // per, per * a, per * b)


def _w_in(w, q_off):
    w = w.astype(BF16)
    return jnp.concatenate([w[:, :q_off], _perm_heads_cols(w[:, q_off:q_off + MIX_W]), w[:, q_off + MIX_W:]], axis=1)


def _w_out(w):
    w = w.astype(BF16)
    return jnp.concatenate([w[:MIX_W], _perm_heads_rows(w[MIX_W:])], axis=0)


def kernel(x, c, ctx, c_ctx, mod_w, mod_b, norm_g, ffn_up, ffn_conv_w, ffn_conv_b, ffn_down, ev_w_in, ev_w_out, rg_conv_w, rg_conv_b, rg_wa, rg_ba, rg_wx, rg_bx, rg_lam, ga_qn, ga_kn, od_w_in, od_w_out, s5_lam_re, s5_lam_im, s5_log_step, s5_b_re, s5_b_im, s5_c_re, s5_c_im, s5_d, s5_glu_w, s5_glu_b, wa_qn, wa_kn, wa_sink):
    cvec = jnp.concatenate([c, jnp.broadcast_to(c_ctx[None], (BATCH, D_MODEL))], axis=0)
    mod = _mod_call(cvec, mod_w, mod_b).reshape(DEPTH, 2, BATCH, N_MOD * D_MODEL)
    rope = _rope_table()
    gm = _block_diag(jnp.full((LANE // HEAD_DIM, HEAD_DIM, HEAD_DIM), 1.0 / HEAD_DIM, F32), LANE // HEAD_DIM)[0].astype(BF16)

    h, u_tb, gate_tb, q, k, v, wu, wd = _in_call(
        (x, ctx), mod[0], norm_g, _w_in(ev_w_in[0], 2 * MIX_W), rope,
        ga_qn, ga_kn, gm, ffn_up, ffn_down, 0, n_tb=2, first_layer=True)
    w_gates = _block_diag(jnp.stack([rg_wa[0], rg_wx[0]]).astype(BF16).reshape(-1, HEAD_DIM, HEAD_DIM), 4)
    rg = (rg_conv_w, rg_conv_b, w_gates, rg_ba, rg_bx, rg_lam)
    hb, uc = _rg_call(u_tb, None, None, *rg, reverse=True)
    a_out = _rg_call(uc, gate_tb, hb, *rg, reverse=False)
    b_out = _gattn_call(q, k, v)
    h = _post_call(h, a_out, b_out, _w_out(ev_w_out[0]), mod[0], norm_g, wu, ffn_conv_w, ffn_conv_b, wd, 0,
                   n_tiles=L_TOT // POST_T, out_batch_major=False)

    u_tb, q, k, v, wu, wd = _in_call(
        (h,), mod[1], norm_g, _w_in(od_w_in[0], MIX_W), rope,
        wa_qn, wa_kn, gm, ffn_up, ffn_down, 1, n_tb=1, first_layer=False)
    n_state = S5_GROUPS * S5_STATE
    bt_re = s5_b_re[0].transpose(0, 3, 1, 2).reshape(2, S5_GROUP, n_state)
    bt_im = s5_b_im[0].transpose(0, 3, 1, 2).reshape(2, S5_GROUP, n_state)
    log_step = jnp.repeat(s5_log_step[0], S5_STATE, axis=-1).reshape(2, 1, n_state)
    abr, abi, bbr, bbi = _s5_disc_call(s5_lam_re[0].reshape(2, 1, n_state), s5_lam_im[0].reshape(2, 1, n_state),
                                       log_step, bt_re, bt_im)
    gpb = S5_GROUPS // S5_NBLK
    eye = jnp.eye(gpb, dtype=BF16)
    bb = jnp.stack([bbr, bbi]).astype(BF16).reshape(2, 2, S5_GROUP, S5_NBLK, gpb, S5_STATE)
    w_drive = jnp.einsum("rdhbgp,gk->dbghrkp", bb, eye).reshape(
        2, S5_NBLK, gpb * S5_GROUP, 2 * S5_BLK_STATE)
    cc = jnp.stack([s5_c_re[0], -s5_c_im[0]]).astype(BF16).reshape(
        2, 2, S5_NBLK, gpb, S5_GROUP, S5_STATE)
    w_read = jnp.einsum("rdbghp,gk->dbrgpkh", cc, eye).reshape(
        2, S5_NBLK, 2 * S5_BLK_STATE, gpb * S5_GROUP)

    def state_rows(re, im):
        row = jnp.concatenate([re.reshape(2, S5_NBLK, S5_BLK_STATE), im.reshape(2, S5_NBLK, S5_BLK_STATE)],
                              axis=2).reshape(2, 1, S5_SW)
        return jnp.broadcast_to(row, (2, BATCH, S5_SW))

    s5 = (w_drive, state_rows(abr, abr), state_rows(-abi, abi), w_read)
    yb = _s5_call(u_tb, None, *s5, None, None, None, reverse=True)
    c_out = _s5_call(u_tb, yb, *s5, s5_d, s5_glu_w[0].astype(BF16), s5_glu_b, reverse=False)
    d_out = _wattn_call(q, k, v, wa_sink)
    return _post_call(h, c_out, d_out, _w_out(od_w_out[0]), mod[1], norm_g, wu, ffn_conv_w, ffn_conv_b, wd, 1,
                      n_tiles=SEQ // POST_T, out_batch_major=True)
```

```python
import functools
import math

import jax
import jax.numpy as jnp
from jax import lax
from jax.experimental import pallas as pl
from jax.experimental.pallas import tpu as pltpu

F32 = jnp.float32
BF16 = jnp.bfloat16

D_MODEL = 1024
BATCH = 8
SEQ = 2048
CTX_LEN = 256
L_TOT = SEQ + CTX_LEN
DEPTH = 2
GRID_W = 64
HEAD_DIM = 64
ROPE_BASE = 10000.0
NORM_EPS = 1e-6
WINDOW = 128
N_MOD = 6
MIX_W = 512
N_HEADS = 8
N_KV = 2
KV_W = N_KV * HEAD_DIM
V_W = 2 * KV_W
QKV_W = MIX_W + KV_W + KV_W
Q_SCALE = HEAD_DIM ** -0.5 * math.log2(math.e)
RG_CONV = 4
RG_C = 8.0
S5_GROUP = 16
S5_GROUPS = 32
S5_STATE = 64
S5_NBLK = 4
S5_BLK_STATE = (S5_GROUPS // S5_NBLK) * S5_STATE
FFN_HIDDEN = 2816
FFN_CONV = 3

LANE = 128
BF16_ROWS = 16
VMEM_LIMIT = 56 * 1024 * 1024

IN_T = 128
RG_T = 256
S5_T = 64
POST_T = 128
TQ_G = 256
G_NB = 8
W_NB = 8
FFN_CH = 256
HEAD_PERM = (0, 4, 1, 5, 2, 6, 3, 7)


def _cparams(sem):
    return pltpu.CompilerParams(dimension_semantics=sem, vmem_limit_bytes=VMEM_LIMIT)


def _dot(a, b):
    return jnp.dot(a, b, preferred_element_type=F32)


def _gelu_tanh(x):
    return x * (0.5 * (1.0 + jnp.tanh(math.sqrt(2.0 / math.pi) * (x + 0.044715 * (x * x * x)))))


def _rms_unit(x):
    ms = jnp.mean(x * x, axis=-1, keepdims=True)
    return x * lax.rsqrt(ms + NORM_EPS)


def _adaln(x, gain, scale, shift):
    y = _per_sample(_rms_unit(x), gain * (1.0 + scale), jnp.multiply)
    return _per_sample(y, shift, jnp.add)


def _to_time_major(x):
    b, t, c = x.shape
    return jnp.swapaxes(x, 0, 1).reshape(t * b, c)


def _to_batch_major(x):
    r, c = x.shape
    return jnp.swapaxes(x.reshape(r // BATCH, BATCH, c), 0, 1)


def _per_sample(x, v, op):
    r, c = x.shape
    return op(x.reshape(r // BATCH, BATCH, c), v[None]).reshape(r, c)


def _full(shape, single=False):
    kw = dict(pipeline_mode=pl.Buffered(1)) if single else {}
    return pl.BlockSpec(shape, lambda *_: (0,) * len(shape), **kw)


MOD_TN = 3072


def _mod_kernel(c_ref, wa_ref, wb_ref, b_ref, o_ref):
    c = c_ref[...]
    a = c * jax.nn.sigmoid(c)
    ah = a.astype(BF16)
    al = (a - ah.astype(F32)).astype(BF16)
    for i, w_ref in enumerate((wa_ref, wb_ref)):
        w = w_ref[0]
        wh = w.astype(BF16)
        wl = (w - wh.astype(F32)).astype(BF16)
        cols = slice(i * (MOD_TN // 2), (i + 1) * (MOD_TN // 2))
        o_ref[0, :, cols] = _dot(ah, wh) + _dot(ah, wl) + _dot(al, wh) + b_ref[0, :, cols]


def _mod_call(cvec, mod_w, mod_b):
    n = N_MOD * D_MODEL
    return pl.pallas_call(
        _mod_kernel,
        out_shape=jax.ShapeDtypeStruct((DEPTH, 2 * BATCH, n), F32),
        grid=(DEPTH, n // MOD_TN),
        in_specs=[
            pl.BlockSpec((2 * BATCH, D_MODEL), lambda l, k: (0, 0)),
            pl.BlockSpec((1, D_MODEL, MOD_TN // 2), lambda l, k: (l, 0, 2 * k)),
            pl.BlockSpec((1, D_MODEL, MOD_TN // 2), lambda l, k: (l, 0, 2 * k + 1)),
            pl.BlockSpec((1, 1, MOD_TN), lambda l, k: (l, 0, k)),
        ],
        out_specs=pl.BlockSpec((1, 2 * BATCH, MOD_TN), lambda l, k: (l, 0, k)),
        compiler_params=_cparams(("parallel", "parallel")),
        name="mod",
    )(cvec, mod_w, mod_w, mod_b.reshape(DEPTH, 1, n))


def _mod_spec(col, n_latent_tiles):
    return pl.BlockSpec((1, BATCH, D_MODEL), lambda j: (jnp.where(j < n_latent_tiles, 0, 1), 0, col))


IN_ROWS = IN_T * BATCH
IN_NLAT = SEQ // IN_T
IN_N = L_TOT // IN_T
IN_SUB = 2
CAST_STEPS = 16
assert CAST_STEPS <= IN_N
IN_SUB_T = IN_T // IN_SUB
IN_SUB_ROWS = IN_SUB_T * BATCH


def _in_kernel(fu_ref, fd_ref, *refs, n_tb, first_layer, layer):
    fu_out, fd_out = refs[-2:]
    refs = refs[:-2]

    @pl.when(pl.program_id(0) < CAST_STEPS)
    def _():
        fu_out[...] = fu_ref[0].astype(BF16)
        fd_out[...] = fd_ref[0].astype(BF16)

    if first_layer:
        x_ref, c_ref, sh_ref, sc_ref, g_ref, w_ref, rope_ref, gq_ref, gk_ref, gm_ref = refs[:10]
        h_out = refs[10]
        outs = refs[11:]
        is_latent = pl.program_id(0) < IN_NLAT
    else:
        h_ref, sh_ref, sc_ref, g_ref, w_ref, rope_ref, gq_ref, gk_ref, gm_ref = refs[:9]
        outs = refs[9:]
    q_ref, k_ref, v_ref = outs[n_tb:]
    off = n_tb * MIX_W
    gm = gm_ref[...]
    gq = jnp.concatenate([gq_ref[...]] * (LANE // HEAD_DIM), axis=1)
    gk = jnp.concatenate([gk_ref[...]] * (LANE // HEAD_DIM), axis=1)
    for s in range(IN_SUB):
        ts = slice(s * IN_SUB_T, (s + 1) * IN_SUB_T)
        rs = slice(s * IN_SUB_ROWS, (s + 1) * IN_SUB_ROWS)
        if first_layer:
            h = _to_time_major(jnp.where(is_latent, x_ref[:, ts, :], c_ref[:, ts, :]))
            h_out[rs, :] = h
        else:
            h = h_ref[rs, :]
        xn = _adaln(h, g_ref[layer, 0:1, :], sc_ref[0], sh_ref[0])
        proj = _dot(xn.astype(BF16), w_ref[...])
        for i in range(n_tb):
            outs[i][rs, :] = proj[:, i * MIX_W:(i + 1) * MIX_W]
        qkv = _to_batch_major(proj[:, off:off + QKV_W])
        cos = rope_ref[ts, 0:LANE][None]
        sin_up = rope_ref[ts, LANE:2 * LANE][None]
        sin_dn = rope_ref[ts, 2 * LANE:3 * LANE][None]
        for p in range(5):
            blk = qkv[:, :, p * LANE:(p + 1) * LANE].reshape(IN_SUB_ROWS, LANE)
            sq = blk * blk
            hi = sq.astype(BF16)
            lo = (sq - hi.astype(F32)).astype(BF16)
            ms = _dot(hi, gm) + _dot(lo, gm)
            g = gq if p < 4 else gk
            bn = blk * lax.rsqrt(ms + NORM_EPS) * g
            up = pltpu.roll(bn, LANE - HEAD_DIM // 4, 1).reshape(BATCH, IN_SUB_T, LANE)
            dn = pltpu.roll(bn, HEAD_DIM // 4, 1).reshape(BATCH, IN_SUB_T, LANE)
            ro = bn.reshape(BATCH, IN_SUB_T, LANE) * cos + up * sin_up + dn * sin_dn
            if p < 4:
                q_ref[:, ts, p * LANE:(p + 1) * LANE] = (ro * Q_SCALE).astype(BF16)
            else:
                k_ref[:, ts, :] = ro.astype(BF16)
        v = qkv[:, :, 5 * LANE:6 * LANE].astype(BF16)
        v_ref[:, ts, :] = jnp.concatenate([v, jnp.ones((BATCH, IN_SUB_T, LANE), BF16)], axis=2)


def _in_call(h_args, modt, g, w, rope, gq, gk, gm, ffn_up, ffn_down, layer, n_tb, first_layer):
    n = w.shape[1]
    tb = lambda width: pl.BlockSpec((IN_ROWS, width), lambda j: (j, 0))
    bm = lambda width: pl.BlockSpec((BATCH, IN_T, width), lambda j: (0, j, 0))
    up_rows = D_MODEL // CAST_STEPS
    down_rows = FFN_HIDDEN // CAST_STEPS
    slab = lambda j: jnp.minimum(j, CAST_STEPS - 1)
    if first_layer:
        h_specs = [
            pl.BlockSpec((BATCH, IN_T, D_MODEL), lambda j: (0, jnp.minimum(j, IN_NLAT - 1), 0)),
            pl.BlockSpec((BATCH, IN_T, D_MODEL), lambda j: (0, jnp.maximum(j - IN_NLAT, 0), 0)),
        ]
        extra_shape = [jax.ShapeDtypeStruct((L_TOT * BATCH, D_MODEL), F32)]
        extra_spec = [tb(D_MODEL)]
    else:
        h_specs = [tb(D_MODEL)]
        extra_shape, extra_spec = [], []
    return pl.pallas_call(
        functools.partial(_in_kernel, n_tb=n_tb, first_layer=first_layer, layer=layer),
        out_shape=extra_shape + [jax.ShapeDtypeStruct((L_TOT * BATCH, MIX_W), F32)] * n_tb + [
            jax.ShapeDtypeStruct((BATCH, L_TOT, MIX_W), BF16),
            jax.ShapeDtypeStruct((BATCH, L_TOT, KV_W), BF16),
            jax.ShapeDtypeStruct((BATCH, L_TOT, V_W), BF16),
            jax.ShapeDtypeStruct((D_MODEL, 2 * FFN_HIDDEN), BF16),
            jax.ShapeDtypeStruct((FFN_HIDDEN, D_MODEL), BF16),
        ],
        grid=(IN_N,),
        in_specs=[
            pl.BlockSpec((1, up_rows, 2 * FFN_HIDDEN), lambda j: (layer, slab(j), 0)),
            pl.BlockSpec((1, down_rows, D_MODEL), lambda j: (layer, slab(j), 0)),
        ] + h_specs + [
            _mod_spec(0, IN_NLAT), _mod_spec(1, IN_NLAT),
            _full(g.shape),
            _full((D_MODEL, n), single=True),
            pl.BlockSpec((IN_T, 3 * LANE), lambda j: (j, 0)),
            _full(gq.shape), _full(gk.shape), _full((LANE, LANE)),
        ],
        out_specs=extra_spec + [tb(MIX_W)] * n_tb + [bm(MIX_W), bm(KV_W), bm(V_W)] + [
            pl.BlockSpec((up_rows, 2 * FFN_HIDDEN), lambda j: (slab(j), 0)),
            pl.BlockSpec((down_rows, D_MODEL), lambda j: (slab(j), 0)),
        ],
        compiler_params=_cparams(("arbitrary",)),
        name="in_proj",
    )(ffn_up, ffn_down, *h_args, modt, modt, g, w, rope, gq, gk, gm)


def _chunk_of(i, reverse, n_latent, n_chunks):
    if reverse:
        return n_chunks - 1 - i
    return jnp.where(i < n_chunks - n_latent, i + n_latent, i - (n_chunks - n_latent))


RG_ROWS = RG_T * BATCH
RG_NC = L_TOT // RG_T
RG_NLAT = SEQ // RG_T


def _softplus(z):
    return jnp.maximum(z, 0.0) + jnp.log1p(jnp.exp(-jnp.abs(z)))


def _rg_kernel(*refs, reverse):
    if reverse:
        (u_ref, up_ref, un_ref, cw_ref, cb_ref, w_ref, ba_ref, bx_ref, lam_ref,
         out_ref, uc_out, xbuf, abuf, bbuf, hst) = refs
    else:
        uc_ref, gate_ref, hb_ref, w_ref, ba_ref, bx_ref, lam_ref, out_ref, abuf, bbuf, hst = refs
    d = 1 if reverse else 0
    i = pl.program_id(0)

    @pl.when(i == 0)
    def _():
        hst[...] = jnp.zeros_like(hst)

    if reverse:
        c = _chunk_of(i, reverse, RG_NLAT, RG_NC)
        prev_zero = jnp.logical_or(c == 0, c == RG_NLAT)
        next_zero = jnp.logical_or(c == RG_NLAT - 1, c == RG_NC - 1)
        xbuf[0:BATCH, :] = up_ref[...] * jnp.where(prev_zero, 0.0, 1.0)
        xbuf[BATCH:BATCH + RG_ROWS, :] = u_ref[...]
        xbuf[BATCH + RG_ROWS:3 * BATCH + RG_ROWS, :] = un_ref[...] * jnp.where(next_zero, 0.0, 1.0)
        uc = cb_ref[...]
        for k in range(RG_CONV):
            uc = uc + xbuf[k * BATCH:k * BATCH + RG_ROWS, :] * cw_ref[0, k:k + 1, :]
        uc_out[...] = uc
    else:
        uc = uc_ref[...]

    ub = uc.astype(BF16)
    half = MIX_W // 2

    def gate(which, b_ref):
        w0 = which * 4 + d * 2
        z = jnp.concatenate([_dot(ub[:, :half], w_ref[w0]), _dot(ub[:, half:], w_ref[w0 + 1])], axis=1)
        return jax.nn.sigmoid(z + b_ref[0, d:d + 1, :])

    r = gate(0, ba_ref)
    ig = gate(1, bx_ref)
    log_a = (-RG_C) * r * _softplus(-lam_ref[0, d:d + 1, :])
    a = jnp.exp(log_a)
    abuf[...] = a
    bbuf[...] = jnp.sqrt(-jnp.tanh(log_a) * (a * a + 1.0)) * (ig * uc)

    h = hst[...]
    for s in range(RG_T):
        t = (RG_T - 1 - s) if reverse else s
        rows = slice(t * BATCH, (t + 1) * BATCH)
        h = abuf[rows, :] * h + bbuf[rows, :]
        bbuf[rows, :] = h
    hst[...] = h

    if reverse:
        out_ref[...] = bbuf[...]
    else:
        y = bbuf[...] + hb_ref[...]
        out_ref[...] = (y * _gelu_tanh(gate_ref[...])).astype(BF16)


def _rg_call(u, gate, hb, cw, cb, w_gates, ba, bx, lam, reverse):
    def cidx(i):
        return _chunk_of(i, reverse, RG_NLAT, RG_NC)

    main = pl.BlockSpec((RG_ROWS, MIX_W), lambda i: (cidx(i), 0))
    gates = [w_gates, ba, bx, lam]
    scratch = [pltpu.VMEM((RG_ROWS, MIX_W), F32), pltpu.VMEM((RG_ROWS, MIX_W), F32), pltpu.VMEM((BATCH, MIX_W), F32)]
    rows = jax.ShapeDtypeStruct((L_TOT * BATCH, MIX_W), F32)
    if reverse:
        prev = pl.BlockSpec((BATCH, MIX_W), lambda i: (jnp.maximum(cidx(i) * RG_T - 1, 0), 0))
        n_next = L_TOT // 2
        nxt = pl.BlockSpec((2 * BATCH, MIX_W), lambda i: (jnp.minimum((cidx(i) + 1) * (RG_T // 2), n_next - 1), 0))
        args = [u, u, u, cw, cb] + gates
        specs = [main, prev, nxt] + [_full(p.shape) for p in args[3:]]
        out_shape, out_specs = [rows, rows], [main, main]
        scratch = [pltpu.VMEM((RG_ROWS + 3 * BATCH, MIX_W), F32)] + scratch
    else:
        args = [u, gate, hb] + gates
        specs = [main, main, main] + [_full(p.shape) for p in gates]
        out_shape, out_specs = jax.ShapeDtypeStruct((L_TOT * BATCH, MIX_W), BF16), main
    return pl.pallas_call(
        functools.partial(_rg_kernel, reverse=reverse),
        out_shape=out_shape,
        grid=(RG_NC,),
        in_specs=specs,
        out_specs=out_specs,
        scratch_shapes=scratch,
        compiler_params=_cparams(("arbitrary",)),
        name="rglru_bwd" if reverse else "rglru_fwd",
    )(*args)


def _head_rows(q_ref, tq, b):
    lane = lax.broadcasted_iota(jnp.int32, (tq, LANE), 1)
    low = lane < HEAD_DIM
    zero = jnp.zeros((tq, LANE), BF16)
    parts = []
    for p in range(MIX_W // LANE):
        qb = q_ref[b, :, p * LANE:(p + 1) * LANE]
        parts.append(jnp.where(low, qb, zero))
        parts.append(jnp.where(low, zero, qb))
    return parts, low


def _merge_heads(outs, low):
    return jnp.concatenate([jnp.where(low, outs[2 * p], outs[2 * p + 1]) for p in range(MIX_W // LANE)], axis=1)


def _qk(q, k):
    return lax.dot_general(q, k, (((1,), (1,)), ((), ())), preferred_element_type=F32)


def _gattn_kernel(q_ref, k_ref, v_ref, o_ref):
    j = pl.program_id(1)

    def attend(k0, nk):
        for b in range(G_NB):
            heads, low = _head_rows(q_ref, TQ_G, b)
            outs = []
            for q in heads:
                s = _qk(q, k_ref[b, k0:k0 + nk, :])
                m = jnp.max(s, axis=-1, keepdims=True)
                p = jnp.exp2(s - m).astype(BF16)
                acc = _dot(p, v_ref[b, k0:k0 + nk, :])
                outs.append(acc[:, :KV_W] / acc[:, KV_W:])
            o_ref[b] = _merge_heads(outs, low).astype(BF16)

    @pl.when(j < SEQ // TQ_G)
    def _():
        attend(0, L_TOT)

    @pl.when(j >= SEQ // TQ_G)
    def _():
        attend(SEQ, CTX_LEN)


def _gattn_call(q, k, v):
    return pl.pallas_call(
        _gattn_kernel,
        out_shape=jax.ShapeDtypeStruct((BATCH, L_TOT, MIX_W), BF16),
        grid=(BATCH // G_NB, L_TOT // TQ_G),
        in_specs=[
            pl.BlockSpec((G_NB, TQ_G, MIX_W), lambda b, j: (b, j, 0)),
            pl.BlockSpec((G_NB, L_TOT, KV_W), lambda b, j: (b, 0, 0)),
            pl.BlockSpec((G_NB, L_TOT, V_W), lambda b, j: (b, 0, 0)),
        ],
        out_specs=pl.BlockSpec((G_NB, TQ_G, MIX_W), lambda b, j: (b, j, 0)),
        compiler_params=_cparams(("parallel", "arbitrary")),
        name="global_attn",
    )(q, k, v)


TQ_W = 2 * WINDOW
N_WBLK = SEQ // WINDOW
N_WSTEP = SEQ // TQ_W


def _wattn_kernel(q_ref, k_ref, v_ref, sink_ref, o_ref):
    i = pl.program_id(1)
    prev_start = pl.multiple_of(jnp.maximum(2 * i - 1, 0) * WINDOW, WINDOW)
    mid_start = pl.multiple_of(i * TQ_W, TQ_W)
    next_start = pl.multiple_of(jnp.minimum(2 * i + 2, N_WBLK - 1) * WINDOW, WINDOW)

    def rows(ref, b):
        return jnp.concatenate([ref[b, SEQ:L_TOT, :], ref[b, pl.ds(prev_start, WINDOW), :],
                                ref[b, pl.ds(mid_start, TQ_W), :], ref[b, pl.ds(next_start, WINDOW), :]], axis=0)

    nk = CTX_LEN + 2 * WINDOW + TQ_W
    r = lax.broadcasted_iota(jnp.int32, (TQ_W, nk), 0)
    col = lax.broadcasted_iota(jnp.int32, (TQ_W, nk), 1)
    c = col - CTX_LEN
    ninf = -jnp.inf
    pen_prev = jnp.where(i > 0, 0.0, ninf)
    pen_next = jnp.where(i < N_WSTEP - 1, 0.0, ninf)
    edge = jnp.where(c < WINDOW, pen_prev, jnp.where(c >= WINDOW + TQ_W, pen_next, 0.0))
    band = jnp.where(c >= r, jnp.where(c <= r + 2 * WINDOW, edge, ninf), ninf)
    bias = jnp.where(col < CTX_LEN, 0.0, band)
    for b in range(W_NB):
        heads, low = _head_rows(q_ref, TQ_W, b)
        kk = rows(k_ref, b)
        vv = rows(v_ref, b)
        outs = []
        for hd, q in enumerate(heads):
            s = _qk(q, kk) + bias
            sink = sink_ref[0, HEAD_PERM[hd]] * math.log2(math.e)
            m = jnp.maximum(jnp.max(s, axis=-1, keepdims=True), sink)
            p = jnp.exp2(s - m).astype(BF16)
            acc = _dot(p, vv)
            outs.append(acc[:, :KV_W] / (acc[:, KV_W:] + jnp.exp2(sink - m)))
        o_ref[b] = _merge_heads(outs, low).astype(BF16)


def _wattn_call(q, k, v, sink):
    return pl.pallas_call(
        _wattn_kernel,
        out_shape=jax.ShapeDtypeStruct((BATCH, SEQ, MIX_W), BF16),
        grid=(BATCH // W_NB, N_WSTEP),
        in_specs=[
            pl.BlockSpec((W_NB, TQ_W, MIX_W), lambda b, i: (b, i, 0)),
            pl.BlockSpec((W_NB, L_TOT, KV_W), lambda b, i: (b, 0, 0)),
            pl.BlockSpec((W_NB, L_TOT, V_W), lambda b, i: (b, 0, 0)),
            pl.BlockSpec(memory_space=pltpu.SMEM),
        ],
        out_specs=pl.BlockSpec((W_NB, TQ_W, MIX_W), lambda b, i: (b, i, 0)),
        compiler_params=_cparams(("parallel", "arbitrary")),
        name="window_attn",
    )(q, k, v, sink)


def _s5_disc_kernel(lr_ref, li_ref, ls_ref, br_ref, bi_ref, ar_ref, ai_ref, bbr_ref, bbi_ref):
    lr = lr_ref[0]
    li = li_ref[0]
    dt = jnp.exp(ls_ref[0])
    mag = jnp.exp(lr * dt)
    ang = li * dt
    abr = mag * jnp.cos(ang)
    abi = mag * jnp.sin(ang)
    den = lr * lr + li * li
    nr = abr - 1.0
    kr = (nr * lr + abi * li) / den
    ki = (abi * lr - nr * li) / den
    br = br_ref[0]
    bi = bi_ref[0]
    ar_ref[0] = abr
    ai_ref[0] = abi
    bbr_ref[0] = kr * br - ki * bi
    bbi_ref[0] = kr * bi + ki * br


def _s5_disc_call(lam_re, lam_im, log_step, bt_re, bt_im):
    n = S5_GROUPS * S5_STATE
    row = pl.BlockSpec((1, 1, n), lambda d: (d, 0, 0))
    mat = pl.BlockSpec((1, S5_GROUP, n), lambda d: (d, 0, 0))
    return pl.pallas_call(
        _s5_disc_kernel,
        out_shape=[jax.ShapeDtypeStruct((2, 1, n), F32)] * 2 + [jax.ShapeDtypeStruct((2, S5_GROUP, n), F32)] * 2,
        grid=(2,),
        in_specs=[row, row, row, mat, mat],
        out_specs=[row, row, mat, mat],
        compiler_params=_cparams(("parallel",)),
        name="s5_discretize",
    )(lam_re, lam_im, log_step, bt_re, bt_im)


S5_ROWS = S5_T * BATCH
S5_NC = L_TOT // S5_T
S5_NLAT = SEQ // S5_T
S5_SW = 2 * S5_BLK_STATE * S5_NBLK


def _s5_kernel(*refs, reverse):
    if reverse:
        u2_ref, un_ref, wd_ref, ar_ref, ai_ref, wr_ref, out_ref, buf0, buf1, xst = refs
    else:
        (u2_ref, un_ref, yb2_ref, wd_ref, ar_ref, ai_ref, wr_ref, dsk_ref, gw_ref, gb_ref,
         out_ref, buf0, buf1, xst) = refs
    nb = 2 * S5_BLK_STATE
    lo, hi = slice(0, S5_ROWS), slice(S5_ROWS, 2 * S5_ROWS)
    first, second = (hi, lo) if reverse else (lo, hi)

    def drive(u, buf):
        ub = u.astype(BF16)
        for k in range(S5_NBLK):
            buf[:, k * nb:(k + 1) * nb] = _dot(ub[:, k * LANE:(k + 1) * LANE], wd_ref[0, k])

    def scan(buf, x):
        for s in range(S5_T):
            t = (S5_T - 1 - s) if reverse else s
            rows = slice(t * BATCH, (t + 1) * BATCH)
            halves = []
            for k in range(S5_NBLK):
                halves.append(x[:, k * nb + S5_BLK_STATE:(k + 1) * nb])
                halves.append(x[:, k * nb:k * nb + S5_BLK_STATE])
            xsw = jnp.concatenate(halves, axis=1)
            x = ar_ref[0] * x + ai_ref[0] * xsw + buf[rows, :]
            buf[rows, :] = x
        return x

    def readout(buf, rows):
        y = jnp.concatenate(
            [_dot(buf[:, k * nb:(k + 1) * nb].astype(BF16), wr_ref[0, k]) for k in range(S5_NBLK)], axis=1)
        if reverse:
            out_ref[rows, :] = y
        else:
            y = u2_ref[rows, :] * dsk_ref[...] + y + yb2_ref[rows, :]
            z = _gelu_tanh(y)
            out_ref[rows, :] = (z * jax.nn.sigmoid(_dot(z.astype(BF16), gw_ref[...]) + gb_ref[...])).astype(BF16)

    @pl.when(pl.program_id(0) == 0)
    def _():
        xst[...] = jnp.zeros_like(xst)
        drive(u2_ref[first, :], buf0)

    x = xst[...]
    drive(u2_ref[second, :], buf1)
    x = scan(buf0, x)
    readout(buf0, first)
    drive(un_ref[...], buf0)
    x = scan(buf1, x)
    readout(buf1, second)
    xst[...] = x


def _s5_call(u, yb, wd, ar, ai, wr, dsk, gw, gb, reverse):
    def chunk(p):
        return _chunk_of(p, reverse, S5_NLAT, S5_NC)

    pair = pl.BlockSpec((2 * S5_ROWS, MIX_W), lambda k: (chunk(2 * k) // 2, 0))
    nxt = pl.BlockSpec((S5_ROWS, MIX_W), lambda k: (chunk(jnp.minimum(2 * k + 2, S5_NC - 1)), 0))
    d = 1 if reverse else 0
    dir_specs = [pl.BlockSpec((1,) + a.shape[1:], lambda k, n=a.ndim: (d,) + (0,) * (n - 1)) for a in (wd, ar, ai, wr)]
    if reverse:
        args = [u, u, wd, ar, ai, wr]
        specs = [pair, nxt] + dir_specs
        out_dtype = F32
    else:
        args = [u, u, yb, wd, ar, ai, wr, dsk, gw, gb]
        specs = [pair, nxt, pair] + dir_specs + [_full(a.shape) for a in (dsk, gw, gb)]
        out_dtype = BF16
    return pl.pallas_call(
        functools.partial(_s5_kernel, reverse=reverse),
        out_shape=jax.ShapeDtypeStruct((L_TOT * BATCH, MIX_W), out_dtype),
        grid=(S5_NC // 2,),
        in_specs=specs,
        out_specs=pair,
        scratch_shapes=[
            pltpu.VMEM((S5_ROWS, S5_SW), F32),
            pltpu.VMEM((S5_ROWS, S5_SW), F32),
            pltpu.VMEM((BATCH, S5_SW), F32),
        ],
        compiler_params=_cparams(("arbitrary",)),
        name="s5_bwd" if reverse else "s5_fwd",
    )(*args)


POST_ROWS = POST_T * BATCH
POST_HALO_T = BF16_ROWS // BATCH
POST_HALO = POST_HALO_T * BATCH
POST_NLAT = SEQ // POST_T
FFN_NCH = FFN_HIDDEN // FFN_CH


def _post_kernel(h_ref, hp_ref, hn_ref, a_ref, ap_ref, an_ref, b_ref, bp_ref, bn_ref, wo_ref,
                 g2_ref, sh_ref, sc_ref, g5_ref, g_ref, wu_ref, cw_ref, cb_ref, wd_ref, o_ref, *,
                 n_tiles, out_batch_major, layer):
    j = pl.program_id(0)
    first = jnp.logical_or(j == 0, j == POST_NLAT)
    last = jnp.logical_or(j == POST_NLAT - 1, j == n_tiles - 1)
    gate2 = g2_ref[0]

    def mix(h, a, b_tm):
        m = _dot(a, wo_ref[0:MIX_W, :]) + _dot(b_tm.astype(BF16), wo_ref[MIX_W:2 * MIX_W, :])
        return h + _per_sample(m, gate2, jnp.multiply)

    def norm(x):
        return _adaln(x, g_ref[layer, 1:2, :], sc_ref[0], sh_ref[0])

    def halo_b(ref, t0):
        x = jnp.swapaxes(ref[...].astype(F32), 0, 1)[t0:t0 + POST_HALO_T]
        return x.reshape(POST_HALO, MIX_W)

    h_mid = mix(h_ref[...], a_ref[...], _to_time_major(b_ref[...].astype(F32)))
    h_prev = mix(hp_ref[...], ap_ref[...], halo_b(bp_ref, BF16_ROWS - POST_HALO_T))
    h_next = mix(hn_ref[...], an_ref[...], halo_b(bn_ref, 0))
    xn = jnp.concatenate([
        (norm(h_prev) * jnp.where(first, 0.0, 1.0)).astype(BF16),
        norm(h_mid).astype(BF16),
        (norm(h_next) * jnp.where(last, 0.0, 1.0)).astype(BF16)], axis=0)

    acts = []
    for c in range(FFN_NCH):
        cv = slice(c * FFN_CH, (c + 1) * FFN_CH)
        cg = slice(FFN_HIDDEN + c * FFN_CH, FFN_HIDDEN + (c + 1) * FFN_CH)

        def conv(cols):
            hid = _dot(xn, wu_ref[:, cols])
            out = cb_ref[layer:layer + 1, cols]
            for k in range(FFN_CONV):
                r0 = POST_HALO + (k - 1) * BATCH
                out = out + hid[r0:r0 + POST_ROWS] * cw_ref[0, k:k + 1, cols]
            return out

        val = conv(cv)
        gate = conv(cg)
        acts.append((val * (gate * jax.nn.sigmoid(gate))).astype(BF16))
    ffn = _dot(jnp.concatenate(acts, axis=1), wd_ref[...])
    out = h_mid + _per_sample(ffn, g5_ref[0], jnp.multiply)
    if out_batch_major:
        o_ref[...] = _to_batch_major(out)
    else:
        o_ref[...] = out


def _post_call(h, a_tb, b_out, w_out, modt, g, wu, cw, cb, wd, layer, n_tiles, out_batch_major):
    per = POST_ROWS // POST_HALO
    n_halo = h.shape[0] // POST_HALO
    per_b = POST_T // BF16_ROWS
    n_halo_b = b_out.shape[1] // BF16_ROWS

    def main(width):
        return pl.BlockSpec((POST_ROWS, width), lambda j: (j, 0))

    def prev(width):
        return pl.BlockSpec((POST_HALO, width), lambda j: (jnp.maximum(j * per - 1, 0), 0))

    def nxt(width):
        return pl.BlockSpec((POST_HALO, width), lambda j: (jnp.minimum((j + 1) * per, n_halo - 1), 0))

    if out_batch_major:
        out_shape = jax.ShapeDtypeStruct((BATCH, n_tiles * POST_T, D_MODEL), F32)
        out_spec = pl.BlockSpec((BATCH, POST_T, D_MODEL), lambda j: (0, j, 0))
    else:
        out_shape = jax.ShapeDtypeStruct((n_tiles * POST_ROWS, D_MODEL), F32)
        out_spec = main(D_MODEL)
    return pl.pallas_call(
        functools.partial(_post_kernel, n_tiles=n_tiles, out_batch_major=out_batch_major, layer=layer),
        out_shape=out_shape,
        grid=(n_tiles,),
        in_specs=[
            main(D_MODEL), prev(D_MODEL), nxt(D_MODEL),
            main(MIX_W), prev(MIX_W), nxt(MIX_W),
            pl.BlockSpec((BATCH, POST_T, MIX_W), lambda j: (0, j, 0)),
            pl.BlockSpec((BATCH, BF16_ROWS, MIX_W), lambda j: (0, jnp.maximum(j * per_b - 1, 0), 0)),
            pl.BlockSpec((BATCH, BF16_ROWS, MIX_W), lambda j: (0, jnp.minimum((j + 1) * per_b, n_halo_b - 1), 0)),
            _full((2 * MIX_W, D_MODEL), single=True),
            _mod_spec(2, POST_NLAT), _mod_spec(3, POST_NLAT), _mod_spec(4, POST_NLAT), _mod_spec(5, POST_NLAT),
            _full(g.shape),
            _full((D_MODEL, 2 * FFN_HIDDEN), single=True),
            pl.BlockSpec((1, FFN_CONV, 2 * FFN_HIDDEN), lambda j: (layer, 0, 0)),
            _full(cb.shape),
            _full((FFN_HIDDEN, D_MODEL), single=True),
        ],
        out_specs=out_spec,
        compiler_params=_cparams(("parallel",)),
        name="post_ffn",
    )(h, h, h, a_tb, a_tb, a_tb, b_out, b_out, b_out, w_out, modt, modt, modt, modt, g, wu, cw, cb, wd)


def _rope_table():
    rows = SEQ // GRID_W
    row = jnp.repeat(jnp.arange(rows, dtype=F32), GRID_W)
    col = jnp.tile(jnp.arange(GRID_W, dtype=F32), rows)
    quarter = HEAD_DIM // 4
    inv_freq = ROPE_BASE ** (-jnp.arange(quarter, dtype=F32) / quarter)
    ang = jnp.stack([row[:, None] * inv_freq, col[:, None] * inv_freq], axis=1)
    cos = jnp.cos(ang)
    sin = jnp.sin(ang)
    zero = jnp.zeros_like(sin)

    def lanes(first, second):
        t = jnp.stack([first, second], axis=2).reshape(SEQ, HEAD_DIM)
        return jnp.tile(t, (1, LANE // HEAD_DIM))

    tab = jnp.concatenate([lanes(cos, cos), lanes(-sin, zero), lanes(zero, sin)], axis=1)
    ctx = jnp.concatenate([jnp.ones((CTX_LEN, LANE), F32), jnp.zeros((CTX_LEN, 2 * LANE), F32)], axis=1)
    return jnp.concatenate([tab, ctx], axis=0)


def _perm_heads_cols(w):
    return jnp.concatenate([w[:, h * HEAD_DIM:(h + 1) * HEAD_DIM] for h in HEAD_PERM], axis=1)


def _perm_heads_rows(w):
    return jnp.concatenate([w[h * HEAD_DIM:(h + 1) * HEAD_DIM] for h in HEAD_PERM], axis=0)


def _block_diag(w, per):
    n, a, b = w.shape
    eye = jnp.eye(per, dtype=w.dtype)
    w4 = w.reshape(n // per, per, a, b)
    return jnp.einsum("ihab,hk->ihakb", w4, eye).reshape(n // per, per * a, per * b)


def _w_in(w, q_off):
    w = w.astype(BF16)
    return jnp.concatenate([w[:, :q_off], _perm_heads_cols(w[:, q_off:q_off + MIX_W]), w[:, q_off + MIX_W:]], axis=1)


def _w_out(w):
    w = w.astype(BF16)
    return jnp.concatenate([w[:MIX_W], _perm_heads_rows(w[MIX_W:])], axis=0)


def kernel(x, c, ctx, c_ctx, mod_w, mod_b, norm_g, ffn_up, ffn_conv_w, ffn_conv_b, ffn_down, ev_w_in, ev_w_out, rg_conv_w, rg_conv_b, rg_wa, rg_ba, rg_wx, rg_bx, rg_lam, ga_qn, ga_kn, od_w_in, od_w_out, s5_lam_re, s5_lam_im, s5_log_step, s5_b_re, s5_b_im, s5_c_re, s5_c_im, s5_d, s5_glu_w, s5_glu_b, wa_qn, wa_kn, wa_sink):
    cvec = jnp.concatenate([c, jnp.broadcast_to(c_ctx[None], (BATCH, D_MODEL))], axis=0)
    mod = _mod_call(cvec, mod_w, mod_b).reshape(DEPTH, 2, BATCH, N_MOD * D_MODEL)
    rope = _rope_table()
    gm = _block_diag(jnp.full((LANE // HEAD_DIM, HEAD_DIM, HEAD_DIM), 1.0 / HEAD_DIM, F32), LANE // HEAD_DIM)[0].astype(BF16)

    h, u_tb, gate_tb, q, k, v, wu, wd = _in_call(
        (x, ctx), mod[0], norm_g, _w_in(ev_w_in[0], 2 * MIX_W), rope,
        ga_qn, ga_kn, gm, ffn_up, ffn_down, 0, n_tb=2, first_layer=True)
    w_gates = _block_diag(jnp.stack([rg_wa[0], rg_wx[0]]).astype(BF16).reshape(-1, HEAD_DIM, HEAD_DIM), 4)
    rg = (rg_conv_w, rg_conv_b, w_gates, rg_ba, rg_bx, rg_lam)
    hb, uc = _rg_call(u_tb, None, None, *rg, reverse=True)
    a_out = _rg_call(uc, gate_tb, hb, *rg, reverse=False)
    b_out = _gattn_call(q, k, v)
    h = _post_call(h, a_out, b_out, _w_out(ev_w_out[0]), mod[0], norm_g, wu, ffn_conv_w, ffn_conv_b, wd, 0,
                   n_tiles=L_TOT // POST_T, out_batch_major=False)

    u_tb, q, k, v, wu, wd = _in_call(
        (h,), mod[1], norm_g, _w_in(od_w_in[0], MIX_W), rope,
        wa_qn, wa_kn, gm, ffn_up, ffn_down, 1, n_tb=1, first_layer=False)
    n_state = S5_GROUPS * S5_STATE
    bt_re = s5_b_re[0].transpose(0, 3, 1, 2).reshape(2, S5_GROUP, n_state)
    bt_im = s5_b_im[0].transpose(0, 3, 1, 2).reshape(2, S5_GROUP, n_state)
    log_step = jnp.repeat(s5_log_step[0], S5_STATE, axis=-1).reshape(2, 1, n_state)
    abr, abi, bbr, bbi = _s5_disc_call(s5_lam_re[0].reshape(2, 1, n_state), s5_lam_im[0].reshape(2, 1, n_state),
                                       log_step, bt_re, bt_im)
    gpb = S5_GROUPS // S5_NBLK
    eye = jnp.eye(gpb, dtype=BF16)
    bb = jnp.stack([bbr, bbi]).astype(BF16).reshape(2, 2, S5_GROUP, S5_NBLK, gpb, S5_STATE)
    w_drive = jnp.einsum("rdhbgp,gk->dbghrkp", bb, eye).reshape(
        2, S5_NBLK, gpb * S5_GROUP, 2 * S5_BLK_STATE)
    cc = jnp.stack([s5_c_re[0], -s5_c_im[0]]).astype(BF16).reshape(
        2, 2, S5_NBLK, gpb, S5_GROUP, S5_STATE)
    w_read = jnp.einsum("rdbghp,gk->dbrgpkh", cc, eye).reshape(
        2, S5_NBLK, 2 * S5_BLK_STATE, gpb * S5_GROUP)

    def state_rows(re, im):
        row = jnp.concatenate([re.reshape(2, S5_NBLK, S5_BLK_STATE), im.reshape(2, S5_NBLK, S5_BLK_STATE)],
                              axis=2).reshape(2, 1, S5_SW)
        return jnp.broadcast_to(row, (2, BATCH, S5_SW))

    s5 = (w_drive, state_rows(abr, abr), state_rows(-abi, abi), w_read)
    yb = _s5_call(u_tb, None, *s5, None, None, None, reverse=True)
    c_out = _s5_call(u_tb, yb, *s5, s5_d, s5_glu_w[0].astype(BF16), s5_glu_b, reverse=False)
    d_out = _wattn_call(q, k, v, wa_sink)
    return _post_call(h, c_out, d_out, _w_out(od_w_out[0]), mod[1], norm_g, wu, ffn_conv_w, ffn_conv_b, wd, 1,
                      n_tiles=SEQ // POST_T, out_batch_major=True)
```

```python
import functools
import math

import jax
import jax.numpy as jnp
from jax import lax
from jax.experimental import pallas as pl
from jax.experimental.pallas import tpu as pltpu

F32 = jnp.float32
BF16 = jnp.bfloat16

D_MODEL = 1024
BATCH = 8
SEQ = 2048
CTX_LEN = 256
L_TOT = SEQ + CTX_LEN
DEPTH = 2
GRID_W = 64
HEAD_DIM = 64
ROPE_BASE = 10000.0
NORM_EPS = 1e-6
WINDOW = 128
N_MOD = 6
MIX_W = 512
N_HEADS = 8
N_KV = 2
KV_W = N_KV * HEAD_DIM
V_W = 2 * KV_W
QKV_W = MIX_W + KV_W + KV_W
Q_SCALE = HEAD_DIM ** -0.5 * math.log2(math.e)
RG_CONV = 4
RG_C = 8.0
S5_GROUP = 16
S5_GROUPS = 32
S5_STATE = 64
S5_NBLK = 4
S5_BLK_STATE = (S5_GROUPS // S5_NBLK) * S5_STATE
FFN_HIDDEN = 2816
FFN_CONV = 3

LANE = 128
BF16_ROWS = 16
VMEM_LIMIT = 56 * 1024 * 1024

IN_T = 128
RG_T = 128
S5_T = 64
POST_T = 128
TQ_G = 256
G_NB = 2
W_NB = 8
FFN_CH = 256
HEAD_PERM = (0, 4, 1, 5, 2, 6, 3, 7)


def _cparams(sem):
    return pltpu.CompilerParams(dimension_semantics=sem, vmem_limit_bytes=VMEM_LIMIT)


def _dot(a, b):
    return jnp.dot(a, b, preferred_element_type=F32)


def _gelu_tanh(x):
    return x * (0.5 * (1.0 + jnp.tanh(math.sqrt(2.0 / math.pi) * (x + 0.044715 * (x * x * x)))))


def _rms_unit(x):
    ms = jnp.mean(x * x, axis=-1, keepdims=True)
    return x * lax.rsqrt(ms + NORM_EPS)


def _adaln(x, gain, scale, shift):
    y = _per_sample(_rms_unit(x), gain * (1.0 + scale), jnp.multiply)
    return _per_sample(y, shift, jnp.add)


def _to_time_major(x):
    b, t, c = x.shape
    return jnp.swapaxes(x, 0, 1).reshape(t * b, c)


def _to_batch_major(x):
    r, c = x.shape
    return jnp.swapaxes(x.reshape(r // BATCH, BATCH, c), 0, 1)


def _per_sample(x, v, op):
    r, c = x.shape
    return op(x.reshape(r // BATCH, BATCH, c), v[None]).reshape(r, c)


def _full(shape, single=False):
    kw = dict(pipeline_mode=pl.Buffered(1)) if single else {}
    return pl.BlockSpec(shape, lambda *_: (0,) * len(shape), **kw)


MOD_TN = 3072


def _mod_kernel(c_ref, wa_ref, wb_ref, b_ref, o_ref):
    c = c_ref[...]
    a = c * jax.nn.sigmoid(c)
    ah = a.astype(BF16)
    al = (a - ah.astype(F32)).astype(BF16)
    for i, w_ref in enumerate((wa_ref, wb_ref)):
        w = w_ref[0]
        wh = w.astype(BF16)
        wl = (w - wh.astype(F32)).astype(BF16)
        cols = slice(i * (MOD_TN // 2), (i + 1) * (MOD_TN // 2))
        o_ref[0, :, cols] = _dot(ah, wh) + _dot(ah, wl) + _dot(al, wh) + b_ref[0, :, cols]


def _mod_call(cvec, mod_w, mod_b):
    n = N_MOD * D_MODEL
    return pl.pallas_call(
        _mod_kernel,
        out_shape=jax.ShapeDtypeStruct((DEPTH, 2 * BATCH, n), F32),
        grid=(DEPTH, n // MOD_TN),
        in_specs=[
            pl.BlockSpec((2 * BATCH, D_MODEL), lambda l, k: (0, 0)),
            pl.BlockSpec((1, D_MODEL, MOD_TN // 2), lambda l, k: (l, 0, 2 * k)),
            pl.BlockSpec((1, D_MODEL, MOD_TN // 2), lambda l, k: (l, 0, 2 * k + 1)),
            pl.BlockSpec((1, 1, MOD_TN), lambda l, k: (l, 0, k)),
        ],
        out_specs=pl.BlockSpec((1, 2 * BATCH, MOD_TN), lambda l, k: (l, 0, k)),
        compiler_params=_cparams(("parallel", "parallel")),
        name="mod",
    )(cvec, mod_w, mod_w, mod_b.reshape(DEPTH, 1, n))


def _mod_spec(col, n_latent_tiles):
    return pl.BlockSpec((1, BATCH, D_MODEL), lambda j: (jnp.where(j < n_latent_tiles, 0, 1), 0, col))


IN_ROWS = IN_T * BATCH
IN_NLAT = SEQ // IN_T
IN_N = L_TOT // IN_T
IN_SUB = 2
CAST_STEPS = 16
assert CAST_STEPS <= IN_N
IN_SUB_T = IN_T // IN_SUB
IN_SUB_ROWS = IN_SUB_T * BATCH


def _in_kernel(fu_ref, fd_ref, *refs, n_tb, first_layer, layer):
    fu_out, fd_out = refs[-2:]
    refs = refs[:-2]

    @pl.when(pl.program_id(0) < CAST_STEPS)
    def _():
        fu_out[...] = fu_ref[0].astype(BF16)
        fd_out[...] = fd_ref[0].astype(BF16)

    if first_layer:
        x_ref, c_ref, sh_ref, sc_ref, g_ref, w_ref, rope_ref, gq_ref, gk_ref, gm_ref = refs[:10]
        h_out = refs[10]
        outs = refs[11:]
        is_latent = pl.program_id(0) < IN_NLAT
    else:
        h_ref, sh_ref, sc_ref, g_ref, w_ref, rope_ref, gq_ref, gk_ref, gm_ref = refs[:9]
        outs = refs[9:]
    q_ref, k_ref, v_ref = outs[n_tb:]
    off = n_tb * MIX_W
    gm = gm_ref[...]
    gq = jnp.concatenate([gq_ref[...]] * (LANE // HEAD_DIM), axis=1)
    gk = jnp.concatenate([gk_ref[...]] * (LANE // HEAD_DIM), axis=1)
    for s in range(IN_SUB):
        ts = slice(s * IN_SUB_T, (s + 1) * IN_SUB_T)
        rs = slice(s * IN_SUB_ROWS, (s + 1) * IN_SUB_ROWS)
        if first_layer:
            h = _to_time_major(jnp.where(is_latent, x_ref[:, ts, :], c_ref[:, ts, :]))
            h_out[rs, :] = h
        else:
            h = h_ref[rs, :]
        xn = _adaln(h, g_ref[layer, 0:1, :], sc_ref[0], sh_ref[0])
        proj = _dot(xn.astype(BF16), w_ref[...])
        for i in range(n_tb):
            outs[i][rs, :] = proj[:, i * MIX_W:(i + 1) * MIX_W]
        qkv = _to_batch_major(proj[:, off:off + QKV_W])
        cos = rope_ref[ts, 0:LANE][None]
        sin_up = rope_ref[ts, LANE:2 * LANE][None]
        sin_dn = rope_ref[ts, 2 * LANE:3 * LANE][None]
        for p in range(5):
            blk = qkv[:, :, p * LANE:(p + 1) * LANE].reshape(IN_SUB_ROWS, LANE)
            sq = blk * blk
            hi = sq.astype(BF16)
            lo = (sq - hi.astype(F32)).astype(BF16)
            ms = _dot(hi, gm) + _dot(lo, gm)
            g = gq if p < 4 else gk
            bn = blk * lax.rsqrt(ms + NORM_EPS) * g
            up = pltpu.roll(bn, LANE - HEAD_DIM // 4, 1).reshape(BATCH, IN_SUB_T, LANE)
            dn = pltpu.roll(bn, HEAD_DIM // 4, 1).reshape(BATCH, IN_SUB_T, LANE)
            ro = bn.reshape(BATCH, IN_SUB_T, LANE) * cos + up * sin_up + dn * sin_dn
            if p < 4:
                q_ref[:, ts, p * LANE:(p + 1) * LANE] = (ro * Q_SCALE).astype(BF16)
            else:
                k_ref[:, ts, :] = ro.astype(BF16)
        v = qkv[:, :, 5 * LANE:6 * LANE].astype(BF16)
        v_ref[:, ts, :] = jnp.concatenate([v, jnp.ones((BATCH, IN_SUB_T, LANE), BF16)], axis=2)


def _in_call(h_args, modt, g, w, rope, gq, gk, gm, ffn_up, ffn_down, layer, n_tb, first_layer):
    n = w.shape[1]
    tb = lambda width: pl.BlockSpec((IN_ROWS, width), lambda j: (j, 0))
    bm = lambda width: pl.BlockSpec((BATCH, IN_T, width), lambda j: (0, j, 0))
    up_rows = D_MODEL // CAST_STEPS
    down_rows = FFN_HIDDEN // CAST_STEPS
    slab = lambda j: jnp.minimum(j, CAST_STEPS - 1)
    if first_layer:
        h_specs = [
            pl.BlockSpec((BATCH, IN_T, D_MODEL), lambda j: (0, jnp.minimum(j, IN_NLAT - 1), 0)),
            pl.BlockSpec((BATCH, IN_T, D_MODEL), lambda j: (0, jnp.maximum(j - IN_NLAT, 0), 0)),
        ]
        extra_shape = [jax.ShapeDtypeStruct((L_TOT * BATCH, D_MODEL), F32)]
        extra_spec = [tb(D_MODEL)]
    else:
        h_specs = [tb(D_MODEL)]
        extra_shape, extra_spec = [], []
    return pl.pallas_call(
        functools.partial(_in_kernel, n_tb=n_tb, first_layer=first_layer, layer=layer),
        out_shape=extra_shape + [jax.ShapeDtypeStruct((L_TOT * BATCH, MIX_W), F32)] * n_tb + [
            jax.ShapeDtypeStruct((BATCH, L_TOT, MIX_W), BF16),
            jax.ShapeDtypeStruct((BATCH, L_TOT, KV_W), BF16),
            jax.ShapeDtypeStruct((BATCH, L_TOT, V_W), BF16),
            jax.ShapeDtypeStruct((D_MODEL, 2 * FFN_HIDDEN), BF16),
            jax.ShapeDtypeStruct((FFN_HIDDEN, D_MODEL), BF16),
        ],
        grid=(IN_N,),
        in_specs=[
            pl.BlockSpec((1, up_rows, 2 * FFN_HIDDEN), lambda j: (layer, slab(j), 0)),
            pl.BlockSpec((1, down_rows, D_MODEL), lambda j: (layer, slab(j), 0)),
        ] + h_specs + [
            _mod_spec(0, IN_NLAT), _mod_spec(1, IN_NLAT),
            _full(g.shape),
            _full((D_MODEL, n), single=True),
            pl.BlockSpec((IN_T, 3 * LANE), lambda j: (j, 0)),
            _full(gq.shape), _full(gk.shape), _full((LANE, LANE)),
        ],
        out_specs=extra_spec + [tb(MIX_W)] * n_tb + [bm(MIX_W), bm(KV_W), bm(V_W)] + [
            pl.BlockSpec((up_rows, 2 * FFN_HIDDEN), lambda j: (slab(j), 0)),
            pl.BlockSpec((down_rows, D_MODEL), lambda j: (slab(j), 0)),
        ],
        compiler_params=_cparams(("arbitrary",)),
        name="in_proj",
    )(ffn_up, ffn_down, *h_args, modt, modt, g, w, rope, gq, gk, gm)


def _chunk_of(i, reverse, n_latent, n_chunks):
    if reverse:
        return n_chunks - 1 - i
    return jnp.where(i < n_chunks - n_latent, i + n_latent, i - (n_chunks - n_latent))


RG_ROWS = RG_T * BATCH
RG_NC = L_TOT // RG_T
RG_NLAT = SEQ // RG_T


def _softplus(z):
    return jnp.maximum(z, 0.0) + jnp.log1p(jnp.exp(-jnp.abs(z)))


def _rg_chunk(i, refs, reverse):
    if reverse:
        (u_ref, up_ref, un_ref, cw_ref, cb_ref, w_ref, ba_ref, bx_ref, lam_ref,
         out_ref, uc_out, xbuf, abuf, bbuf, hst) = refs
    else:
        uc_ref, gate_ref, hb_ref, w_ref, ba_ref, bx_ref, lam_ref, out_ref, abuf, bbuf, hst = refs
    d = 1 if reverse else 0

    if reverse:
        c = _chunk_of(i, reverse, RG_NLAT, RG_NC)
        prev_zero = jnp.logical_or(c == 0, c == RG_NLAT)
        next_zero = jnp.logical_or(c == RG_NLAT - 1, c == RG_NC - 1)
        xbuf[0:BATCH, :] = up_ref[...] * jnp.where(prev_zero, 0.0, 1.0)
        xbuf[BATCH:BATCH + RG_ROWS, :] = u_ref[...]
        xbuf[BATCH + RG_ROWS:3 * BATCH + RG_ROWS, :] = un_ref[...] * jnp.where(next_zero, 0.0, 1.0)
        uc = cb_ref[...]
        for k in range(RG_CONV):
            uc = uc + xbuf[k * BATCH:k * BATCH + RG_ROWS, :] * cw_ref[0, k:k + 1, :]
        uc_out[...] = uc
    else:
        uc = uc_ref[...]

    ub = uc.astype(BF16)
    half = MIX_W // 2

    def gate(which, b_ref):
        w0 = which * 4 + d * 2
        z = jnp.concatenate([_dot(ub[:, :half], w_ref[w0]), _dot(ub[:, half:], w_ref[w0 + 1])], axis=1)
        return jax.nn.sigmoid(z + b_ref[0, d:d + 1, :])

    r = gate(0, ba_ref)
    ig = gate(1, bx_ref)
    log_a = (-RG_C) * r * _softplus(-lam_ref[0, d:d + 1, :])
    a = jnp.exp(log_a)
    abuf[...] = a
    bbuf[...] = jnp.sqrt(-jnp.tanh(log_a) * (a * a + 1.0)) * (ig * uc)

    h = hst[...]
    for s in range(RG_T):
        t = (RG_T - 1 - s) if reverse else s
        rows = slice(t * BATCH, (t + 1) * BATCH)
        h = abuf[rows, :] * h + bbuf[rows, :]
        bbuf[rows, :] = h
    hst[...] = h

    if reverse:
        out_ref[...] = bbuf[...]
    else:
        y = bbuf[...] + hb_ref[...]
        out_ref[...] = (y * _gelu_tanh(gate_ref[...])).astype(BF16)


def _head_rows(q_ref, tq, b):
    lane = lax.broadcasted_iota(jnp.int32, (tq, LANE), 1)
    low = lane < HEAD_DIM
    zero = jnp.zeros((tq, LANE), BF16)
    parts = []
    for p in range(MIX_W // LANE):
        qb = q_ref[b, :, p * LANE:(p + 1) * LANE]
        parts.append(jnp.where(low, qb, zero))
        parts.append(jnp.where(low, zero, qb))
    return parts, low


def _merge_heads(outs, low):
    return jnp.concatenate([jnp.where(low, outs[2 * p], outs[2 * p + 1]) for p in range(MIX_W // LANE)], axis=1)


def _qk(q, k):
    return lax.dot_general(q, k, (((1,), (1,)), ((), ())), preferred_element_type=F32)


G_TILES = L_TOT // TQ_G
G_HALF = BATCH // 2
assert (G_HALF // G_NB) * G_TILES == RG_NC


def _attend(q_ref, k_ref, v_ref, o_ref, k0, nk):
    for b in range(G_NB):
        heads, low = _head_rows(q_ref, TQ_G, b)
        outs = []
        for q in heads:
            s = _qk(q, k_ref[b, k0:k0 + nk, :])
            m = jnp.max(s, axis=-1, keepdims=True)
            p = jnp.exp2(s - m).astype(BF16)
            acc = _dot(p, v_ref[b, k0:k0 + nk, :])
            outs.append(acc[:, :KV_W] / acc[:, KV_W:])
        o_ref[b] = _merge_heads(outs, low).astype(BF16)


def _mix0_kernel(*refs, reverse):
    q_ref, k_ref, v_ref = refs[:3]
    n_rg_in = 9 if reverse else 7
    n_rg_out = 2 if reverse else 1
    rg_in = refs[3:3 + n_rg_in]
    o_ref = refs[3 + n_rg_in]
    rg_rest = refs[4 + n_rg_in:]
    assert len(rg_rest) == n_rg_out + (4 if reverse else 3)
    i = pl.program_id(0)
    j = lax.rem(i, G_TILES)

    @pl.when(i == 0)
    def _():
        rg_rest[-1][...] = jnp.zeros_like(rg_rest[-1])

    def step(k0, nk):
        _rg_chunk(i, rg_in + rg_rest, reverse)
        _attend(q_ref, k_ref, v_ref, o_ref, k0, nk)

    @pl.when(j < SEQ // TQ_G)
    def _():
        step(0, L_TOT)

    @pl.when(j >= SEQ // TQ_G)
    def _():
        step(SEQ, CTX_LEN)


def _mix0_call(q, k, v, u, gate, hb, cw, cb, w_gates, ba, bx, lam, reverse):
    g0 = 0 if reverse else G_HALF // G_NB

    def cidx(i):
        return _chunk_of(i, reverse, RG_NLAT, RG_NC)

    attn_specs = [
        pl.BlockSpec((G_NB, TQ_G, MIX_W), lambda i: (g0 + i // G_TILES, lax.rem(i, G_TILES), 0)),
        pl.BlockSpec((G_NB, L_TOT, KV_W), lambda i: (g0 + i // G_TILES, 0, 0)),
        pl.BlockSpec((G_NB, L_TOT, V_W), lambda i: (g0 + i // G_TILES, 0, 0)),
    ]
    attn_out = jax.ShapeDtypeStruct((G_HALF, L_TOT, MIX_W), BF16)
    attn_out_spec = pl.BlockSpec((G_NB, TQ_G, MIX_W), lambda i: (i // G_TILES, lax.rem(i, G_TILES), 0))
    main = pl.BlockSpec((RG_ROWS, MIX_W), lambda i: (cidx(i), 0))
    gates = [w_gates, ba, bx, lam]
    scratch = [pltpu.VMEM((RG_ROWS, MIX_W), F32), pltpu.VMEM((RG_ROWS, MIX_W), F32), pltpu.VMEM((BATCH, MIX_W), F32)]
    rows = jax.ShapeDtypeStruct((L_TOT * BATCH, MIX_W), F32)
    if reverse:
        prev = pl.BlockSpec((BATCH, MIX_W), lambda i: (jnp.maximum(cidx(i) * RG_T - 1, 0), 0))
        n_next = L_TOT // 2
        nxt = pl.BlockSpec((2 * BATCH, MIX_W), lambda i: (jnp.minimum((cidx(i) + 1) * (RG_T // 2), n_next - 1), 0))
        args = [u, u, u, cw, cb] + gates
        specs = [main, prev, nxt] + [_full(p.shape) for p in args[3:]]
        out_shape, out_specs = [attn_out, rows, rows], [attn_out_spec, main, main]
        scratch = [pltpu.VMEM((RG_ROWS + 3 * BATCH, MIX_W), F32)] + scratch
    else:
        args = [u, gate, hb] + gates
        specs = [main, main, main] + [_full(p.shape) for p in gates]
        out_shape = [attn_out, jax.ShapeDtypeStruct((L_TOT * BATCH, MIX_W), BF16)]
        out_specs = [attn_out_spec, main]
    return pl.pallas_call(
        functools.partial(_mix0_kernel, reverse=reverse),
        out_shape=out_shape,
        grid=(RG_NC,),
        in_specs=attn_specs + specs,
        out_specs=out_specs,
        scratch_shapes=scratch,
        compiler_params=_cparams(("arbitrary",)),
        name="mix0_bwd" if reverse else "mix0_fwd",
    )(q, k, v, *args)


TQ_W = 2 * WINDOW
N_WBLK = SEQ // WINDOW
N_WSTEP = SEQ // TQ_W


def _wattn_kernel(q_ref, k_ref, v_ref, sink_ref, o_ref):
    i = pl.program_id(1)
    prev_start = pl.multiple_of(jnp.maximum(2 * i - 1, 0) * WINDOW, WINDOW)
    mid_start = pl.multiple_of(i * TQ_W, TQ_W)
    next_start = pl.multiple_of(jnp.minimum(2 * i + 2, N_WBLK - 1) * WINDOW, WINDOW)

    def rows(ref, b):
        return jnp.concatenate([ref[b, SEQ:L_TOT, :], ref[b, pl.ds(prev_start, WINDOW), :],
                                ref[b, pl.ds(mid_start, TQ_W), :], ref[b, pl.ds(next_start, WINDOW), :]], axis=0)

    nk = CTX_LEN + 2 * WINDOW + TQ_W
    r = lax.broadcasted_iota(jnp.int32, (TQ_W, nk), 0)
    col = lax.broadcasted_iota(jnp.int32, (TQ_W, nk), 1)
    c = col - CTX_LEN
    ninf = -jnp.inf
    pen_prev = jnp.where(i > 0, 0.0, ninf)
    pen_next = jnp.where(i < N_WSTEP - 1, 0.0, ninf)
    edge = jnp.where(c < WINDOW, pen_prev, jnp.where(c >= WINDOW + TQ_W, pen_next, 0.0))
    band = jnp.where(c >= r, jnp.where(c <= r + 2 * WINDOW, edge, ninf), ninf)
    bias = jnp.where(col < CTX_LEN, 0.0, band)
    for b in range(W_NB):
        heads, low = _head_rows(q_ref, TQ_W, b)
        kk = rows(k_ref, b)
        vv = rows(v_ref, b)
        outs = []
        for hd, q in enumerate(heads):
            s = _qk(q, kk) + bias
            sink = sink_ref[0, HEAD_PERM[hd]] * math.log2(math.e)
            m = jnp.maximum(jnp.max(s, axis=-1, keepdims=True), sink)
            p = jnp.exp2(s - m).astype(BF16)
            acc = _dot(p, vv)
            outs.append(acc[:, :KV_W] / (acc[:, KV_W:] + jnp.exp2(sink - m)))
        o_ref[b] = _merge_heads(outs, low).astype(BF16)


def _wattn_call(q, k, v, sink):
    return pl.pallas_call(
        _wattn_kernel,
        out_shape=jax.ShapeDtypeStruct((BATCH, SEQ, MIX_W), BF16),
        grid=(BATCH // W_NB, N_WSTEP),
        in_specs=[
            pl.BlockSpec((W_NB, TQ_W, MIX_W), lambda b, i: (b, i, 0)),
            pl.BlockSpec((W_NB, L_TOT, KV_W), lambda b, i: (b, 0, 0)),
            pl.BlockSpec((W_NB, L_TOT, V_W), lambda b, i: (b, 0, 0)),
            pl.BlockSpec(memory_space=pltpu.SMEM),
        ],
        out_specs=pl.BlockSpec((W_NB, TQ_W, MIX_W), lambda b, i: (b, i, 0)),
        compiler_params=_cparams(("parallel", "arbitrary")),
        name="window_attn",
    )(q, k, v, sink)


def _s5_disc_kernel(lr_ref, li_ref, ls_ref, br_ref, bi_ref, ar_ref, ai_ref, bbr_ref, bbi_ref):
    lr = lr_ref[0]
    li = li_ref[0]
    dt = jnp.exp(ls_ref[0])
    mag = jnp.exp(lr * dt)
    ang = li * dt
    abr = mag * jnp.cos(ang)
    abi = mag * jnp.sin(ang)
    den = lr * lr + li * li
    nr = abr - 1.0
    kr = (nr * lr + abi * li) / den
    ki = (abi * lr - nr * li) / den
    br = br_ref[0]
    bi = bi_ref[0]
    ar_ref[0] = abr
    ai_ref[0] = abi
    bbr_ref[0] = kr * br - ki * bi
    bbi_ref[0] = kr * bi + ki * br


def _s5_disc_call(lam_re, lam_im, log_step, bt_re, bt_im):
    n = S5_GROUPS * S5_STATE
    row = pl.BlockSpec((1, 1, n), lambda d: (d, 0, 0))
    mat = pl.BlockSpec((1, S5_GROUP, n), lambda d: (d, 0, 0))
    return pl.pallas_call(
        _s5_disc_kernel,
        out_shape=[jax.ShapeDtypeStruct((2, 1, n), F32)] * 2 + [jax.ShapeDtypeStruct((2, S5_GROUP, n), F32)] * 2,
        grid=(2,),
        in_specs=[row, row, row, mat, mat],
        out_specs=[row, row, mat, mat],
        compiler_params=_cparams(("parallel",)),
        name="s5_discretize",
    )(lam_re, lam_im, log_step, bt_re, bt_im)


S5_ROWS = S5_T * BATCH
S5_NC = L_TOT // S5_T
S5_NLAT = SEQ // S5_T
S5_SW = 2 * S5_BLK_STATE * S5_NBLK


def _s5_kernel(*refs, reverse):
    if reverse:
        u2_ref, un_ref, wd_ref, ar_ref, ai_ref, wr_ref, out_ref, buf0, buf1, xst = refs
    else:
        (u2_ref, un_ref, yb2_ref, wd_ref, ar_ref, ai_ref, wr_ref, dsk_ref, gw_ref, gb_ref,
         out_ref, buf0, buf1, xst) = refs
    nb = 2 * S5_BLK_STATE
    lo, hi = slice(0, S5_ROWS), slice(S5_ROWS, 2 * S5_ROWS)
    first, second = (hi, lo) if reverse else (lo, hi)

    def drive(u, buf):
        ub = u.astype(BF16)
        for k in range(S5_NBLK):
            buf[:, k * nb:(k + 1) * nb] = _dot(ub[:, k * LANE:(k + 1) * LANE], wd_ref[0, k])

    def scan(buf, x):
        for s in range(S5_T):
            t = (S5_T - 1 - s) if reverse else s
            rows = slice(t * BATCH, (t + 1) * BATCH)
            halves = []
            for k in range(S5_NBLK):
                halves.append(x[:, k * nb + S5_BLK_STATE:(k + 1) * nb])
                halves.append(x[:, k * nb:k * nb + S5_BLK_STATE])
            xsw = jnp.concatenate(halves, axis=1)
            x = ar_ref[0] * x + ai_ref[0] * xsw + buf[rows, :]
            buf[rows, :] = x
        return x

    def readout(buf, rows):
        y = jnp.concatenate(
            [_dot(buf[:, k * nb:(k + 1) * nb].astype(BF16), wr_ref[0, k]) for k in range(S5_NBLK)], axis=1)
        if reverse:
            out_ref[rows, :] = y
        else:
            y = u2_ref[rows, :] * dsk_ref[...] + y + yb2_ref[rows, :]
            z = _gelu_tanh(y)
            out_ref[rows, :] = (z * jax.nn.sigmoid(_dot(z.astype(BF16), gw_ref[...]) + gb_ref[...])).astype(BF16)

    @pl.when(pl.program_id(0) == 0)
    def _():
        xst[...] = jnp.zeros_like(xst)
        drive(u2_ref[first, :], buf0)

    x = xst[...]
    drive(u2_ref[second, :], buf1)
    x = scan(buf0, x)
    readout(buf0, first)
    drive(un_ref[...], buf0)
    x = scan(buf1, x)
    readout(buf1, second)
    xst[...] = x


def _s5_call(u, yb, wd, ar, ai, wr, dsk, gw, gb, reverse):
    def chunk(p):
        return _chunk_of(p, reverse, S5_NLAT, S5_NC)

    pair = pl.BlockSpec((2 * S5_ROWS, MIX_W), lambda k: (chunk(2 * k) // 2, 0))
    nxt = pl.BlockSpec((S5_ROWS, MIX_W), lambda k: (chunk(jnp.minimum(2 * k + 2, S5_NC - 1)), 0))
    d = 1 if reverse else 0
    dir_specs = [pl.BlockSpec((1,) + a.shape[1:], lambda k, n=a.ndim: (d,) + (0,) * (n - 1)) for a in (wd, ar, ai, wr)]
    if reverse:
        args = [u, u, wd, ar, ai, wr]
        specs = [pair, nxt] + dir_specs
        out_dtype = F32
    else:
        args = [u, u, yb, wd, ar, ai, wr, dsk, gw, gb]
        specs = [pair, nxt, pair] + dir_specs + [_full(a.shape) for a in (dsk, gw, gb)]
        out_dtype = BF16
    return pl.pallas_call(
        functools.partial(_s5_kernel, reverse=reverse),
        out_shape=jax.ShapeDtypeStruct((L_TOT * BATCH, MIX_W), out_dtype),
        grid=(S5_NC // 2,),
        in_specs=specs,
        out_specs=pair,
        scratch_shapes=[
            pltpu.VMEM((S5_ROWS, S5_SW), F32),
            pltpu.VMEM((S5_ROWS, S5_SW), F32),
            pltpu.VMEM((BATCH, S5_SW), F32),
        ],
        compiler_params=_cparams(("arbitrary",)),
        name="s5_bwd" if reverse else "s5_fwd",
    )(*args)


POST_ROWS = POST_T * BATCH
POST_HALO_T = BF16_ROWS // BATCH
POST_HALO = POST_HALO_T * BATCH
POST_NLAT = SEQ // POST_T
FFN_NCH = FFN_HIDDEN // FFN_CH


def _post_kernel(*refs, n_tiles, out_batch_major, layer, n_b):
    h_ref, hp_ref, hn_ref, a_ref, ap_ref, an_ref = refs[:6]
    b_refs, bp_refs, bn_refs = (refs[6 + k * n_b:6 + (k + 1) * n_b] for k in range(3))
    (wo_ref, g2_ref, sh_ref, sc_ref, g5_ref, g_ref, wu_ref, cw_ref, cb_ref, wd_ref, o_ref) = refs[6 + 3 * n_b:]

    def samples(parts):
        return jnp.concatenate([r[...] for r in parts], axis=0).astype(F32)

    j = pl.program_id(0)
    first = jnp.logical_or(j == 0, j == POST_NLAT)
    last = jnp.logical_or(j == POST_NLAT - 1, j == n_tiles - 1)
    gate2 = g2_ref[0]

    def mix(h, a, b_tm):
        m = _dot(a, wo_ref[0:MIX_W, :]) + _dot(b_tm.astype(BF16), wo_ref[MIX_W:2 * MIX_W, :])
        return h + _per_sample(m, gate2, jnp.multiply)

    def norm(x):
        return _adaln(x, g_ref[layer, 1:2, :], sc_ref[0], sh_ref[0])

    def halo_b(parts, t0):
        x = jnp.swapaxes(samples(parts), 0, 1)[t0:t0 + POST_HALO_T]
        return x.reshape(POST_HALO, MIX_W)

    h_mid = mix(h_ref[...], a_ref[...], _to_time_major(samples(b_refs)))
    h_prev = mix(hp_ref[...], ap_ref[...], halo_b(bp_refs, BF16_ROWS - POST_HALO_T))
    h_next = mix(hn_ref[...], an_ref[...], halo_b(bn_refs, 0))
    xn = jnp.concatenate([
        (norm(h_prev) * jnp.where(first, 0.0, 1.0)).astype(BF16),
        norm(h_mid).astype(BF16),
        (norm(h_next) * jnp.where(last, 0.0, 1.0)).astype(BF16)], axis=0)

    acts = []
    for c in range(FFN_NCH):
        cv = slice(c * FFN_CH, (c + 1) * FFN_CH)
        cg = slice(FFN_HIDDEN + c * FFN_CH, FFN_HIDDEN + (c + 1) * FFN_CH)

        def conv(cols):
            hid = _dot(xn, wu_ref[:, cols])
            out = cb_ref[layer:layer + 1, cols]
            for k in range(FFN_CONV):
                r0 = POST_HALO + (k - 1) * BATCH
                out = out + hid[r0:r0 + POST_ROWS] * cw_ref[0, k:k + 1, cols]
            return out

        val = conv(cv)
        gate = conv(cg)
        acts.append((val * (gate * jax.nn.sigmoid(gate))).astype(BF16))
    ffn = _dot(jnp.concatenate(acts, axis=1), wd_ref[...])
    out = h_mid + _per_sample(ffn, g5_ref[0], jnp.multiply)
    if out_batch_major:
        o_ref[...] = _to_batch_major(out)
    else:
        o_ref[...] = out


def _post_call(h, a_tb, b_outs, w_out, modt, g, wu, cw, cb, wd, layer, n_tiles, out_batch_major):
    per = POST_ROWS // POST_HALO
    n_halo = h.shape[0] // POST_HALO
    per_b = POST_T // BF16_ROWS
    n_halo_b = b_outs[0].shape[1] // BF16_ROWS
    b_main = [pl.BlockSpec((b.shape[0], POST_T, MIX_W), lambda j: (0, j, 0)) for b in b_outs]
    b_prev = [pl.BlockSpec((b.shape[0], BF16_ROWS, MIX_W), lambda j: (0, jnp.maximum(j * per_b - 1, 0), 0))
              for b in b_outs]
    b_next = [pl.BlockSpec((b.shape[0], BF16_ROWS, MIX_W), lambda j: (0, jnp.minimum((j + 1) * per_b, n_halo_b - 1), 0))
              for b in b_outs]

    def main(width):
        return pl.BlockSpec((POST_ROWS, width), lambda j: (j, 0))

    def prev(width):
        return pl.BlockSpec((POST_HALO, width), lambda j: (jnp.maximum(j * per - 1, 0), 0))

    def nxt(width):
        return pl.BlockSpec((POST_HALO, width), lambda j: (jnp.minimum((j + 1) * per, n_halo - 1), 0))

    if out_batch_major:
        out_shape = jax.ShapeDtypeStruct((BATCH, n_tiles * POST_T, D_MODEL), F32)
        out_spec = pl.BlockSpec((BATCH, POST_T, D_MODEL), lambda j: (0, j, 0))
    else:
        out_shape = jax.ShapeDtypeStruct((n_tiles * POST_ROWS, D_MODEL), F32)
        out_spec = main(D_MODEL)
    return pl.pallas_call(
        functools.partial(_post_kernel, n_tiles=n_tiles, out_batch_major=out_batch_major, layer=layer,
                          n_b=len(b_outs)),
        out_shape=out_shape,
        grid=(n_tiles,),
        in_specs=[
            main(D_MODEL), prev(D_MODEL), nxt(D_MODEL),
            main(MIX_W), prev(MIX_W), nxt(MIX_W),
        ] + b_main + b_prev + b_next + [
            _full((2 * MIX_W, D_MODEL), single=True),
            _mod_spec(2, POST_NLAT), _mod_spec(3, POST_NLAT), _mod_spec(4, POST_NLAT), _mod_spec(5, POST_NLAT),
            _full(g.shape),
            _full((D_MODEL, 2 * FFN_HIDDEN), single=True),
            pl.BlockSpec((1, FFN_CONV, 2 * FFN_HIDDEN), lambda j: (layer, 0, 0)),
            _full(cb.shape),
            _full((FFN_HIDDEN, D_MODEL), single=True),
        ],
        out_specs=out_spec,
        compiler_params=_cparams(("parallel",)),
        name="post_ffn",
    )(h, h, h, a_tb, a_tb, a_tb, *b_outs, *b_outs, *b_outs, w_out, modt, modt, modt, modt, g, wu, cw, cb, wd)


def _rope_table():
    rows = SEQ // GRID_W
    row = jnp.repeat(jnp.arange(rows, dtype=F32), GRID_W)
    col = jnp.tile(jnp.arange(GRID_W, dtype=F32), rows)
    quarter = HEAD_DIM // 4
    inv_freq = ROPE_BASE ** (-jnp.arange(quarter, dtype=F32) / quarter)
    ang = jnp.stack([row[:, None] * inv_freq, col[:, None] * inv_freq], axis=1)
    cos = jnp.cos(ang)
    sin = jnp.sin(ang)
    zero = jnp.zeros_like(sin)

    def lanes(first, second):
        t = jnp.stack([first, second], axis=2).reshape(SEQ, HEAD_DIM)
        return jnp.tile(t, (1, LANE // HEAD_DIM))

    tab = jnp.concatenate([lanes(cos, cos), lanes(-sin, zero), lanes(zero, sin)], axis=1)
    ctx = jnp.concatenate([jnp.ones((CTX_LEN, LANE), F32), jnp.zeros((CTX_LEN, 2 * LANE), F32)], axis=1)
    return jnp.concatenate([tab, ctx], axis=0)


def _perm_heads_cols(w):
    return jnp.concatenate([w[:, h * HEAD_DIM:(h + 1) * HEAD_DIM] for h in HEAD_PERM], axis=1)


def _perm_heads_rows(w):
    return jnp.concatenate([w[h * HEAD_DIM:(h + 1) * HEAD_DIM] for h in HEAD_PERM], axis=0)


def _block_diag(w, per):
    n, a, b = w.shape
    eye = jnp.eye(per, dtype=w.dtype)
    w4 = w.reshape(n // per, per, a, b)
    return jnp.einsum("ihab,hk->ihakb", w4, eye).reshape(n // per, per * a, per * b)


def _w_in(w, q_off):
    w = w.astype(BF16)
    return jnp.concatenate([w[:, :q_off], _perm_heads_cols(w[:, q_off:q_off + MIX_W]), w[:, q_off + MIX_W:]], axis=1)


def _w_out(w):
    w = w.astype(BF16)
    return jnp.concatenate([w[:MIX_W], _perm_heads_rows(w[MIX_W:])], axis=0)


def kernel(x, c, ctx, c_ctx, mod_w, mod_b, norm_g, ffn_up, ffn_conv_w, ffn_conv_b, ffn_down, ev_w_in, ev_w_out, rg_conv_w, rg_conv_b, rg_wa, rg_ba, rg_wx, rg_bx, rg_lam, ga_qn, ga_kn, od_w_in, od_w_out, s5_lam_re, s5_lam_im, s5_log_step, s5_b_re, s5_b_im, s5_c_re, s5_c_im, s5_d, s5_glu_w, s5_glu_b, wa_qn, wa_kn, wa_sink):
    cvec = jnp.concatenate([c, jnp.broadcast_to(c_ctx[None], (BATCH, D_MODEL))], axis=0)
    mod = _mod_call(cvec, mod_w, mod_b).reshape(DEPTH, 2, BATCH, N_MOD * D_MODEL)
    rope = _rope_table()
    gm = _block_diag(jnp.full((LANE // HEAD_DIM, HEAD_DIM, HEAD_DIM), 1.0 / HEAD_DIM, F32), LANE // HEAD_DIM)[0].astype(BF16)

    h, u_tb, gate_tb, q, k, v, wu, wd = _in_call(
        (x, ctx), mod[0], norm_g, _w_in(ev_w_in[0], 2 * MIX_W), rope,
        ga_qn, ga_kn, gm, ffn_up, ffn_down, 0, n_tb=2, first_layer=True)
    w_gates = _block_diag(jnp.stack([rg_wa[0], rg_wx[0]]).astype(BF16).reshape(-1, HEAD_DIM, HEAD_DIM), 4)
    rg = (rg_conv_w, rg_conv_b, w_gates, rg_ba, rg_bx, rg_lam)
    b_lo, hb, uc = _mix0_call(q, k, v, u_tb, None, None, *rg, reverse=True)
    b_hi, a_out = _mix0_call(q, k, v, uc, gate_tb, hb, *rg, reverse=False)
    h = _post_call(h, a_out, [b_lo, b_hi], _w_out(ev_w_out[0]), mod[0], norm_g, wu, ffn_conv_w, ffn_conv_b, wd, 0,
                   n_tiles=L_TOT // POST_T, out_batch_major=False)

    u_tb, q, k, v, wu, wd = _in_call(
        (h,), mod[1], norm_g, _w_in(od_w_in[0], MIX_W), rope,
        wa_qn, wa_kn, gm, ffn_up, ffn_down, 1, n_tb=1, first_layer=False)
    n_state = S5_GROUPS * S5_STATE
    bt_re = s5_b_re[0].transpose(0, 3, 1, 2).reshape(2, S5_GROUP, n_state)
    bt_im = s5_b_im[0].transpose(0, 3, 1, 2).reshape(2, S5_GROUP, n_state)
    log_step = jnp.repeat(s5_log_step[0], S5_STATE, axis=-1).reshape(2, 1, n_state)
    abr, abi, bbr, bbi = _s5_disc_call(s5_lam_re[0].reshape(2, 1, n_state), s5_lam_im[0].reshape(2, 1, n_state),
                                       log_step, bt_re, bt_im)
    gpb = S5_GROUPS // S5_NBLK
    eye = jnp.eye(gpb, dtype=BF16)
    bb = jnp.stack([bbr, bbi]).astype(BF16).reshape(2, 2, S5_GROUP, S5_NBLK, gpb, S5_STATE)
    w_drive = jnp.einsum("rdhbgp,gk->dbghrkp", bb, eye).reshape(
        2, S5_NBLK, gpb * S5_GROUP, 2 * S5_BLK_STATE)
    cc = jnp.stack([s5_c_re[0], -s5_c_im[0]]).astype(BF16).reshape(
        2, 2, S5_NBLK, gpb, S5_GROUP, S5_STATE)
    w_read = jnp.einsum("rdbghp,gk->dbrgpkh", cc, eye).reshape(
        2, S5_NBLK, 2 * S5_BLK_STATE, gpb * S5_GROUP)

    def state_rows(re, im):
        row = jnp.concatenate([re.reshape(2, S5_NBLK, S5_BLK_STATE), im.reshape(2, S5_NBLK, S5_BLK_STATE)],
                              axis=2).reshape(2, 1, S5_SW)
        return jnp.broadcast_to(row, (2, BATCH, S5_SW))

    s5 = (w_drive, state_rows(abr, abr), state_rows(-abi, abi), w_read)
    yb = _s5_call(u_tb, None, *s5, None, None, None, reverse=True)
    c_out = _s5_call(u_tb, yb, *s5, s5_d, s5_glu_w[0].astype(BF16), s5_glu_b, reverse=False)
    d_out = _wattn_call(q, k, v, wa_sink)
    return _post_call(h, c_out, [d_out], _w_out(od_w_out[0]), mod[1], norm_g, wu, ffn_conv_w, ffn_conv_b, wd, 1,
                      n_tiles=SEQ // POST_T, out_batch_major=True)
```

```python
import functools
import math

import jax
import jax.numpy as jnp
from jax import lax
from jax.experimental import pallas as pl
from jax.experimental.pallas import tpu as pltpu

F32 = jnp.float32
BF16 = jnp.bfloat16

D_MODEL = 1024
BATCH = 8
SEQ = 2048
CTX_LEN = 256
L_TOT = SEQ + CTX_LEN
DEPTH = 2
GRID_W = 64
HEAD_DIM = 64
ROPE_BASE = 10000.0
NORM_EPS = 1e-6
WINDOW = 128
N_MOD = 6
MIX_W = 512
N_HEADS = 8
N_KV = 2
KV_W = N_KV * HEAD_DIM
V_W = 2 * KV_W
QKV_W = MIX_W + KV_W + KV_W
Q_SCALE = HEAD_DIM ** -0.5 * math.log2(math.e)
RG_CONV = 4
RG_C = 8.0
S5_GROUP = 16
S5_GROUPS = 32
S5_STATE = 64
S5_NBLK = 4
S5_BLK_STATE = (S5_GROUPS // S5_NBLK) * S5_STATE
FFN_HIDDEN = 2816
FFN_CONV = 3

LANE = 128
BF16_ROWS = 16
VMEM_LIMIT = 56 * 1024 * 1024

IN_T = 128
RG_T = 128
S5_T = 64
POST_T = 128
TQ_G = 256
G_NB = 4
W_NB = 8
FFN_CH = 256
HEAD_PERM = (0, 4, 1, 5, 2, 6, 3, 7)


def _cparams(sem):
    return pltpu.CompilerParams(dimension_semantics=sem, vmem_limit_bytes=VMEM_LIMIT)


def _dot(a, b):
    return jnp.dot(a, b, preferred_element_type=F32)


def _gelu_tanh(x):
    return x * (0.5 * (1.0 + jnp.tanh(math.sqrt(2.0 / math.pi) * (x + 0.044715 * (x * x * x)))))


def _rms_unit(x):
    ms = jnp.mean(x * x, axis=-1, keepdims=True)
    return x * lax.rsqrt(ms + NORM_EPS)


def _adaln(x, gain, scale, shift):
    y = _per_sample(_rms_unit(x), gain * (1.0 + scale), jnp.multiply)
    return _per_sample(y, shift, jnp.add)


def _to_time_major(x):
    b, t, c = x.shape
    return jnp.swapaxes(x, 0, 1).reshape(t * b, c)


def _to_batch_major(x):
    r, c = x.shape
    return jnp.swapaxes(x.reshape(r // BATCH, BATCH, c), 0, 1)


def _per_sample(x, v, op):
    r, c = x.shape
    return op(x.reshape(r // BATCH, BATCH, c), v[None]).reshape(r, c)


def _full(shape, single=False):
    kw = dict(pipeline_mode=pl.Buffered(1)) if single else {}
    return pl.BlockSpec(shape, lambda *_: (0,) * len(shape), **kw)


MOD_TN = 3072


def _mod_kernel(c_ref, wa_ref, wb_ref, b_ref, o_ref):
    c = c_ref[...]
    a = c * jax.nn.sigmoid(c)
    ah = a.astype(BF16)
    al = (a - ah.astype(F32)).astype(BF16)
    for i, w_ref in enumerate((wa_ref, wb_ref)):
        w = w_ref[0]
        wh = w.astype(BF16)
        wl = (w - wh.astype(F32)).astype(BF16)
        cols = slice(i * (MOD_TN // 2), (i + 1) * (MOD_TN // 2))
        o_ref[0, :, cols] = _dot(ah, wh) + _dot(ah, wl) + _dot(al, wh) + b_ref[0, :, cols]


def _mod_call(cvec, mod_w, mod_b):
    n = N_MOD * D_MODEL
    return pl.pallas_call(
        _mod_kernel,
        out_shape=jax.ShapeDtypeStruct((DEPTH, 2 * BATCH, n), F32),
        grid=(DEPTH, n // MOD_TN),
        in_specs=[
            pl.BlockSpec((2 * BATCH, D_MODEL), lambda l, k: (0, 0)),
            pl.BlockSpec((1, D_MODEL, MOD_TN // 2), lambda l, k: (l, 0, 2 * k)),
            pl.BlockSpec((1, D_MODEL, MOD_TN // 2), lambda l, k: (l, 0, 2 * k + 1)),
            pl.BlockSpec((1, 1, MOD_TN), lambda l, k: (l, 0, k)),
        ],
        out_specs=pl.BlockSpec((1, 2 * BATCH, MOD_TN), lambda l, k: (l, 0, k)),
        compiler_params=_cparams(("parallel", "parallel")),
        name="mod",
    )(cvec, mod_w, mod_w, mod_b.reshape(DEPTH, 1, n))


def _mod_spec(col, n_latent_tiles):
    return pl.BlockSpec((1, BATCH, D_MODEL), lambda j: (jnp.where(j < n_latent_tiles, 0, 1), 0, col))


IN_ROWS = IN_T * BATCH
IN_NLAT = SEQ // IN_T
IN_N = L_TOT // IN_T
IN_SUB = 2
CAST_STEPS = 16
assert CAST_STEPS <= IN_N
IN_SUB_T = IN_T // IN_SUB
IN_SUB_ROWS = IN_SUB_T * BATCH


def _in_kernel(fu_ref, fd_ref, *refs, n_tb, first_layer, layer):
    fu_out, fd_out = refs[-2:]
    refs = refs[:-2]

    @pl.when(pl.program_id(0) < CAST_STEPS)
    def _():
        fu_out[...] = fu_ref[0].astype(BF16)
        fd_out[...] = fd_ref[0].astype(BF16)

    if first_layer:
        x_ref, c_ref, sh_ref, sc_ref, g_ref, w_ref, rope_ref, gq_ref, gk_ref, gm_ref = refs[:10]
        h_out = refs[10]
        outs = refs[11:]
        is_latent = pl.program_id(0) < IN_NLAT
    else:
        h_ref, sh_ref, sc_ref, g_ref, w_ref, rope_ref, gq_ref, gk_ref, gm_ref = refs[:9]
        outs = refs[9:]
    q_ref, k_ref, v_ref = outs[n_tb:]
    off = n_tb * MIX_W
    gm = gm_ref[...]
    gq = jnp.concatenate([gq_ref[...]] * (LANE // HEAD_DIM), axis=1)
    gk = jnp.concatenate([gk_ref[...]] * (LANE // HEAD_DIM), axis=1)
    for s in range(IN_SUB):
        ts = slice(s * IN_SUB_T, (s + 1) * IN_SUB_T)
        rs = slice(s * IN_SUB_ROWS, (s + 1) * IN_SUB_ROWS)
        if first_layer:
            h = _to_time_major(jnp.where(is_latent, x_ref[:, ts, :], c_ref[:, ts, :]))
            h_out[rs, :] = h
        else:
            h = h_ref[rs, :]
        xn = _adaln(h, g_ref[layer, 0:1, :], sc_ref[0], sh_ref[0])
        proj = _dot(xn.astype(BF16), w_ref[...])
        for i in range(n_tb):
            outs[i][rs, :] = proj[:, i * MIX_W:(i + 1) * MIX_W]
        qkv = _to_batch_major(proj[:, off:off + QKV_W])
        cos = rope_ref[ts, 0:LANE][None]
        sin_up = rope_ref[ts, LANE:2 * LANE][None]
        sin_dn = rope_ref[ts, 2 * LANE:3 * LANE][None]
        for p in range(5):
            blk = qkv[:, :, p * LANE:(p + 1) * LANE].reshape(IN_SUB_ROWS, LANE)
            sq = blk * blk
            hi = sq.astype(BF16)
            lo = (sq - hi.astype(F32)).astype(BF16)
            ms = _dot(hi, gm) + _dot(lo, gm)
            g = gq if p < 4 else gk
            bn = blk * lax.rsqrt(ms + NORM_EPS) * g
            up = pltpu.roll(bn, LANE - HEAD_DIM // 4, 1).reshape(BATCH, IN_SUB_T, LANE)
            dn = pltpu.roll(bn, HEAD_DIM // 4, 1).reshape(BATCH, IN_SUB_T, LANE)
            ro = bn.reshape(BATCH, IN_SUB_T, LANE) * cos + up * sin_up + dn * sin_dn
            if p < 4:
                q_ref[:, ts, p * LANE:(p + 1) * LANE] = (ro * Q_SCALE).astype(BF16)
            else:
                k_ref[:, ts, :] = ro.astype(BF16)
        v = qkv[:, :, 5 * LANE:6 * LANE].astype(BF16)
        v_ref[:, ts, :] = jnp.concatenate([v, jnp.ones((BATCH, IN_SUB_T, LANE), BF16)], axis=2)


def _in_call(h_args, modt, g, w, rope, gq, gk, gm, ffn_up, ffn_down, layer, n_tb, first_layer):
    n = w.shape[1]
    tb = lambda width: pl.BlockSpec((IN_ROWS, width), lambda j: (j, 0))
    bm = lambda width: pl.BlockSpec((BATCH, IN_T, width), lambda j: (0, j, 0))
    up_rows = D_MODEL // CAST_STEPS
    down_rows = FFN_HIDDEN // CAST_STEPS
    slab = lambda j: jnp.minimum(j, CAST_STEPS - 1)
    if first_layer:
        h_specs = [
            pl.BlockSpec((BATCH, IN_T, D_MODEL), lambda j: (0, jnp.minimum(j, IN_NLAT - 1), 0)),
            pl.BlockSpec((BATCH, IN_T, D_MODEL), lambda j: (0, jnp.maximum(j - IN_NLAT, 0), 0)),
        ]
        extra_shape = [jax.ShapeDtypeStruct((L_TOT * BATCH, D_MODEL), F32)]
        extra_spec = [tb(D_MODEL)]
    else:
        h_specs = [tb(D_MODEL)]
        extra_shape, extra_spec = [], []
    return pl.pallas_call(
        functools.partial(_in_kernel, n_tb=n_tb, first_layer=first_layer, layer=layer),
        out_shape=extra_shape + [jax.ShapeDtypeStruct((L_TOT * BATCH, MIX_W), F32)] * n_tb + [
            jax.ShapeDtypeStruct((BATCH, L_TOT, MIX_W), BF16),
            jax.ShapeDtypeStruct((BATCH, L_TOT, KV_W), BF16),
            jax.ShapeDtypeStruct((BATCH, L_TOT, V_W), BF16),
            jax.ShapeDtypeStruct((D_MODEL, 2 * FFN_HIDDEN), BF16),
            jax.ShapeDtypeStruct((FFN_HIDDEN, D_MODEL), BF16),
        ],
        grid=(IN_N,),
        in_specs=[
            pl.BlockSpec((1, up_rows, 2 * FFN_HIDDEN), lambda j: (layer, slab(j), 0)),
            pl.BlockSpec((1, down_rows, D_MODEL), lambda j: (layer, slab(j), 0)),
        ] + h_specs + [
            _mod_spec(0, IN_NLAT), _mod_spec(1, IN_NLAT),
            _full(g.shape),
            _full((D_MODEL, n), single=True),
            pl.BlockSpec((IN_T, 3 * LANE), lambda j: (j, 0)),
            _full(gq.shape), _full(gk.shape), _full((LANE, LANE)),
        ],
        out_specs=extra_spec + [tb(MIX_W)] * n_tb + [bm(MIX_W), bm(KV_W), bm(V_W)] + [
            pl.BlockSpec((up_rows, 2 * FFN_HIDDEN), lambda j: (slab(j), 0)),
            pl.BlockSpec((down_rows, D_MODEL), lambda j: (slab(j), 0)),
        ],
        compiler_params=_cparams(("arbitrary",)),
        name="in_proj",
    )(ffn_up, ffn_down, *h_args, modt, modt, g, w, rope, gq, gk, gm)


def _chunk_of(i, reverse, n_latent, n_chunks):
    if reverse:
        return n_chunks - 1 - i
    return jnp.where(i < n_chunks - n_latent, i + n_latent, i - (n_chunks - n_latent))


RG_ROWS = RG_T * BATCH
RG_NC = L_TOT // RG_T
RG_NLAT = SEQ // RG_T


def _softplus(z):
    return jnp.maximum(z, 0.0) + jnp.log1p(jnp.exp(-jnp.abs(z)))


def _rg_kernel(*refs, reverse):
    i = pl.program_id(0)

    @pl.when(i == 0)
    def _():
        refs[-1][...] = jnp.zeros_like(refs[-1])

    _rg_chunk(i, refs, reverse)


def _rg_chunk(i, refs, reverse):
    if reverse:
        (u_ref, up_ref, un_ref, cw_ref, cb_ref, w_ref, ba_ref, bx_ref, lam_ref,
         out_ref, uc_out, xbuf, abuf, bbuf, hst) = refs
    else:
        uc_ref, gate_ref, hb_ref, w_ref, ba_ref, bx_ref, lam_ref, out_ref, abuf, bbuf, hst = refs
    d = 1 if reverse else 0

    if reverse:
        c = _chunk_of(i, reverse, RG_NLAT, RG_NC)
        prev_zero = jnp.logical_or(c == 0, c == RG_NLAT)
        next_zero = jnp.logical_or(c == RG_NLAT - 1, c == RG_NC - 1)
        xbuf[0:BATCH, :] = up_ref[...] * jnp.where(prev_zero, 0.0, 1.0)
        xbuf[BATCH:BATCH + RG_ROWS, :] = u_ref[...]
        xbuf[BATCH + RG_ROWS:3 * BATCH + RG_ROWS, :] = un_ref[...] * jnp.where(next_zero, 0.0, 1.0)
        uc = cb_ref[...]
        for k in range(RG_CONV):
            uc = uc + xbuf[k * BATCH:k * BATCH + RG_ROWS, :] * cw_ref[0, k:k + 1, :]
        uc_out[...] = uc
    else:
        uc = uc_ref[...]

    ub = uc.astype(BF16)
    half = MIX_W // 2

    def gate(which, b_ref):
        w0 = which * 4 + d * 2
        z = jnp.concatenate([_dot(ub[:, :half], w_ref[w0]), _dot(ub[:, half:], w_ref[w0 + 1])], axis=1)
        return jax.nn.sigmoid(z + b_ref[0, d:d + 1, :])

    r = gate(0, ba_ref)
    ig = gate(1, bx_ref)
    log_a = (-RG_C) * r * _softplus(-lam_ref[0, d:d + 1, :])
    a = jnp.exp(log_a)
    abuf[...] = a
    bbuf[...] = jnp.sqrt(-jnp.tanh(log_a) * (a * a + 1.0)) * (ig * uc)

    h = hst[...]
    for s in range(RG_T):
        t = (RG_T - 1 - s) if reverse else s
        rows = slice(t * BATCH, (t + 1) * BATCH)
        h = abuf[rows, :] * h + bbuf[rows, :]
        bbuf[rows, :] = h
    hst[...] = h

    if reverse:
        out_ref[...] = bbuf[...]
    else:
        y = bbuf[...] + hb_ref[...]
        out_ref[...] = (y * _gelu_tanh(gate_ref[...])).astype(BF16)


def _rg_call(u, gate, hb, cw, cb, w_gates, ba, bx, lam, reverse):
    def cidx(i):
        return _chunk_of(i, reverse, RG_NLAT, RG_NC)

    main = pl.BlockSpec((RG_ROWS, MIX_W), lambda i: (cidx(i), 0))
    gates = [w_gates, ba, bx, lam]
    scratch = [pltpu.VMEM((RG_ROWS, MIX_W), F32), pltpu.VMEM((RG_ROWS, MIX_W), F32), pltpu.VMEM((BATCH, MIX_W), F32)]
    rows = jax.ShapeDtypeStruct((L_TOT * BATCH, MIX_W), F32)
    if reverse:
        prev = pl.BlockSpec((BATCH, MIX_W), lambda i: (jnp.maximum(cidx(i) * RG_T - 1, 0), 0))
        n_next = L_TOT // 2
        nxt = pl.BlockSpec((2 * BATCH, MIX_W), lambda i: (jnp.minimum((cidx(i) + 1) * (RG_T // 2), n_next - 1), 0))
        args = [u, u, u, cw, cb] + gates
        specs = [main, prev, nxt] + [_full(p.shape) for p in args[3:]]
        out_shape, out_specs = [rows, rows], [main, main]
        scratch = [pltpu.VMEM((RG_ROWS + 3 * BATCH, MIX_W), F32)] + scratch
    else:
        args = [u, gate, hb] + gates
        specs = [main, main, main] + [_full(p.shape) for p in gates]
        out_shape, out_specs = jax.ShapeDtypeStruct((L_TOT * BATCH, MIX_W), BF16), main
    return pl.pallas_call(
        functools.partial(_rg_kernel, reverse=reverse),
        out_shape=out_shape,
        grid=(RG_NC,),
        in_specs=specs,
        out_specs=out_specs,
        scratch_shapes=scratch,
        compiler_params=_cparams(("arbitrary",)),
        name="rglru_bwd" if reverse else "rglru_fwd",
    )(*args)


def _head_rows(q_ref, tq, b):
    lane = lax.broadcasted_iota(jnp.int32, (tq, LANE), 1)
    low = lane < HEAD_DIM
    zero = jnp.zeros((tq, LANE), BF16)
    parts = []
    for p in range(MIX_W // LANE):
        qb = q_ref[b, :, p * LANE:(p + 1) * LANE]
        parts.append(jnp.where(low, qb, zero))
        parts.append(jnp.where(low, zero, qb))
    return parts, low


def _merge_heads(outs, low):
    return jnp.concatenate([jnp.where(low, outs[2 * p], outs[2 * p + 1]) for p in range(MIX_W // LANE)], axis=1)


def _qk(q, k):
    return lax.dot_general(q, k, (((1,), (1,)), ((), ())), preferred_element_type=F32)


def _gattn_kernel(q_ref, k_ref, v_ref, *rest):
    rg_in, o_ref, rg_rest = rest[:9], rest[9], rest[10:]
    i = pl.program_id(0)
    j = lax.rem(i, G_TILES)

    @pl.when(i == 0)
    def _():
        rg_rest[-1][...] = jnp.zeros_like(rg_rest[-1])

    def attend(k0, nk):
        _rg_chunk(i, rg_in + rg_rest, True)
        for b in range(G_NB):
            heads, low = _head_rows(q_ref, TQ_G, b)
            outs = []
            for q in heads:
                s = _qk(q, k_ref[b, k0:k0 + nk, :])
                m = jnp.max(s, axis=-1, keepdims=True)
                p = jnp.exp2(s - m).astype(BF16)
                acc = _dot(p, v_ref[b, k0:k0 + nk, :])
                outs.append(acc[:, :KV_W] / acc[:, KV_W:])
            o_ref[b] = _merge_heads(outs, low).astype(BF16)

    @pl.when(j < SEQ // TQ_G)
    def _():
        attend(0, L_TOT)

    @pl.when(j >= SEQ // TQ_G)
    def _():
        attend(SEQ, CTX_LEN)


G_TILES = L_TOT // TQ_G
assert (BATCH // G_NB) * G_TILES == RG_NC


def _gattn_call(q, k, v, u, cw, cb, w_gates, ba, bx, lam):
    def cidx(i):
        return _chunk_of(i, True, RG_NLAT, RG_NC)

    main = pl.BlockSpec((RG_ROWS, MIX_W), lambda i: (cidx(i), 0))
    prev = pl.BlockSpec((BATCH, MIX_W), lambda i: (jnp.maximum(cidx(i) * RG_T - 1, 0), 0))
    n_next = L_TOT // 2
    nxt = pl.BlockSpec((2 * BATCH, MIX_W), lambda i: (jnp.minimum((cidx(i) + 1) * (RG_T // 2), n_next - 1), 0))
    params = [cw, cb, w_gates, ba, bx, lam]
    rows = jax.ShapeDtypeStruct((L_TOT * BATCH, MIX_W), F32)
    attn_spec = pl.BlockSpec((G_NB, TQ_G, MIX_W), lambda i: (i // G_TILES, lax.rem(i, G_TILES), 0))
    return pl.pallas_call(
        _gattn_kernel,
        out_shape=[jax.ShapeDtypeStruct((BATCH, L_TOT, MIX_W), BF16), rows, rows],
        grid=(RG_NC,),
        in_specs=[
            attn_spec,
            pl.BlockSpec((G_NB, L_TOT, KV_W), lambda i: (i // G_TILES, 0, 0)),
            pl.BlockSpec((G_NB, L_TOT, V_W), lambda i: (i // G_TILES, 0, 0)),
            main, prev, nxt,
        ] + [_full(p.shape) for p in params],
        out_specs=[attn_spec, main, main],
        scratch_shapes=[
            pltpu.VMEM((RG_ROWS + 3 * BATCH, MIX_W), F32),
            pltpu.VMEM((RG_ROWS, MIX_W), F32),
            pltpu.VMEM((RG_ROWS, MIX_W), F32),
            pltpu.VMEM((BATCH, MIX_W), F32),
        ],
        compiler_params=_cparams(("arbitrary",)),
        name="global_attn_rglru_bwd",
    )(q, k, v, u, u, u, *params)


TQ_W = 2 * WINDOW
N_WBLK = SEQ // WINDOW
N_WSTEP = SEQ // TQ_W


def _wattn_kernel(q_ref, k_ref, v_ref, sink_ref, o_ref):
    i = pl.program_id(1)
    prev_start = pl.multiple_of(jnp.maximum(2 * i - 1, 0) * WINDOW, WINDOW)
    mid_start = pl.multiple_of(i * TQ_W, TQ_W)
    next_start = pl.multiple_of(jnp.minimum(2 * i + 2, N_WBLK - 1) * WINDOW, WINDOW)

    def rows(ref, b):
        return jnp.concatenate([ref[b, SEQ:L_TOT, :], ref[b, pl.ds(prev_start, WINDOW), :],
                                ref[b, pl.ds(mid_start, TQ_W), :], ref[b, pl.ds(next_start, WINDOW), :]], axis=0)

    nk = CTX_LEN + 2 * WINDOW + TQ_W
    r = lax.broadcasted_iota(jnp.int32, (TQ_W, nk), 0)
    col = lax.broadcasted_iota(jnp.int32, (TQ_W, nk), 1)
    c = col - CTX_LEN
    ninf = -jnp.inf
    pen_prev = jnp.where(i > 0, 0.0, ninf)
    pen_next = jnp.where(i < N_WSTEP - 1, 0.0, ninf)
    edge = jnp.where(c < WINDOW, pen_prev, jnp.where(c >= WINDOW + TQ_W, pen_next, 0.0))
    band = jnp.where(c >= r, jnp.where(c <= r + 2 * WINDOW, edge, ninf), ninf)
    bias = jnp.where(col < CTX_LEN, 0.0, band)
    for b in range(W_NB):
        heads, low = _head_rows(q_ref, TQ_W, b)
        kk = rows(k_ref, b)
        vv = rows(v_ref, b)
        outs = []
        for hd, q in enumerate(heads):
            s = _qk(q, kk) + bias
            sink = sink_ref[0, HEAD_PERM[hd]] * math.log2(math.e)
            m = jnp.maximum(jnp.max(s, axis=-1, keepdims=True), sink)
            p = jnp.exp2(s - m).astype(BF16)
            acc = _dot(p, vv)
            outs.append(acc[:, :KV_W] / (acc[:, KV_W:] + jnp.exp2(sink - m)))
        o_ref[b] = _merge_heads(outs, low).astype(BF16)


def _wattn_call(q, k, v, sink):
    return pl.pallas_call(
        _wattn_kernel,
        out_shape=jax.ShapeDtypeStruct((BATCH, SEQ, MIX_W), BF16),
        grid=(BATCH // W_NB, N_WSTEP),
        in_specs=[
            pl.BlockSpec((W_NB, TQ_W, MIX_W), lambda b, i: (b, i, 0)),
            pl.BlockSpec((W_NB, L_TOT, KV_W), lambda b, i: (b, 0, 0)),
            pl.BlockSpec((W_NB, L_TOT, V_W), lambda b, i: (b, 0, 0)),
            pl.BlockSpec(memory_space=pltpu.SMEM),
        ],
        out_specs=pl.BlockSpec((W_NB, TQ_W, MIX_W), lambda b, i: (b, i, 0)),
        compiler_params=_cparams(("parallel", "arbitrary")),
        name="window_attn",
    )(q, k, v, sink)


def _s5_disc_kernel(lr_ref, li_ref, ls_ref, br_ref, bi_ref, ar_ref, ai_ref, bbr_ref, bbi_ref):
    lr = lr_ref[0]
    li = li_ref[0]
    dt = jnp.exp(ls_ref[0])
    mag = jnp.exp(lr * dt)
    ang = li * dt
    abr = mag * jnp.cos(ang)
    abi = mag * jnp.sin(ang)
    den = lr * lr + li * li
    nr = abr - 1.0
    kr = (nr * lr + abi * li) / den
    ki = (abi * lr - nr * li) / den
    br = br_ref[0]
    bi = bi_ref[0]
    ar_ref[0] = abr
    ai_ref[0] = abi
    bbr_ref[0] = kr * br - ki * bi
    bbi_ref[0] = kr * bi + ki * br


def _s5_disc_call(lam_re, lam_im, log_step, bt_re, bt_im):
    n = S5_GROUPS * S5_STATE
    row = pl.BlockSpec((1, 1, n), lambda d: (d, 0, 0))
    mat = pl.BlockSpec((1, S5_GROUP, n), lambda d: (d, 0, 0))
    return pl.pallas_call(
        _s5_disc_kernel,
        out_shape=[jax.ShapeDtypeStruct((2, 1, n), F32)] * 2 + [jax.ShapeDtypeStruct((2, S5_GROUP, n), F32)] * 2,
        grid=(2,),
        in_specs=[row, row, row, mat, mat],
        out_specs=[row, row, mat, mat],
        compiler_params=_cparams(("parallel",)),
        name="s5_discretize",
    )(lam_re, lam_im, log_step, bt_re, bt_im)


S5_ROWS = S5_T * BATCH
S5_NC = L_TOT // S5_T
S5_NLAT = SEQ // S5_T
S5_SW = 2 * S5_BLK_STATE * S5_NBLK


def _s5_kernel(*refs, reverse):
    if reverse:
        u2_ref, un_ref, wd_ref, ar_ref, ai_ref, wr_ref, out_ref, buf0, buf1, xst = refs
    else:
        (u2_ref, un_ref, yb2_ref, wd_ref, ar_ref, ai_ref, wr_ref, dsk_ref, gw_ref, gb_ref,
         out_ref, buf0, buf1, xst) = refs
    nb = 2 * S5_BLK_STATE
    lo, hi = slice(0, S5_ROWS), slice(S5_ROWS, 2 * S5_ROWS)
    first, second = (hi, lo) if reverse else (lo, hi)

    def drive(u, buf):
        ub = u.astype(BF16)
        for k in range(S5_NBLK):
            buf[:, k * nb:(k + 1) * nb] = _dot(ub[:, k * LANE:(k + 1) * LANE], wd_ref[0, k])

    def scan(buf, x):
        for s in range(S5_T):
            t = (S5_T - 1 - s) if reverse else s
            rows = slice(t * BATCH, (t + 1) * BATCH)
            halves = []
            for k in range(S5_NBLK):
                halves.append(x[:, k * nb + S5_BLK_STATE:(k + 1) * nb])
                halves.append(x[:, k * nb:k * nb + S5_BLK_STATE])
            xsw = jnp.concatenate(halves, axis=1)
            x = ar_ref[0] * x + ai_ref[0] * xsw + buf[rows, :]
            buf[rows, :] = x
        return x

    def readout(buf, rows):
        y = jnp.concatenate(
            [_dot(buf[:, k * nb:(k + 1) * nb].astype(BF16), wr_ref[0, k]) for k in range(S5_NBLK)], axis=1)
        if reverse:
            out_ref[rows, :] = y
        else:
            y = u2_ref[rows, :] * dsk_ref[...] + y + yb2_ref[rows, :]
            z = _gelu_tanh(y)
            out_ref[rows, :] = (z * jax.nn.sigmoid(_dot(z.astype(BF16), gw_ref[...]) + gb_ref[...])).astype(BF16)

    @pl.when(pl.program_id(0) == 0)
    def _():
        xst[...] = jnp.zeros_like(xst)
        drive(u2_ref[first, :], buf0)

    x = xst[...]
    drive(u2_ref[second, :], buf1)
    x = scan(buf0, x)
    readout(buf0, first)
    drive(un_ref[...], buf0)
    x = scan(buf1, x)
    readout(buf1, second)
    xst[...] = x


def _s5_call(u, yb, wd, ar, ai, wr, dsk, gw, gb, reverse):
    def chunk(p):
        return _chunk_of(p, reverse, S5_NLAT, S5_NC)

    pair = pl.BlockSpec((2 * S5_ROWS, MIX_W), lambda k: (chunk(2 * k) // 2, 0))
    nxt = pl.BlockSpec((S5_ROWS, MIX_W), lambda k: (chunk(jnp.minimum(2 * k + 2, S5_NC - 1)), 0))
    d = 1 if reverse else 0
    dir_specs = [pl.BlockSpec((1,) + a.shape[1:], lambda k, n=a.ndim: (d,) + (0,) * (n - 1)) for a in (wd, ar, ai, wr)]
    if reverse:
        args = [u, u, wd, ar, ai, wr]
        specs = [pair, nxt] + dir_specs
        out_dtype = F32
    else:
        args = [u, u, yb, wd, ar, ai, wr, dsk, gw, gb]
        specs = [pair, nxt, pair] + dir_specs + [_full(a.shape) for a in (dsk, gw, gb)]
        out_dtype = BF16
    return pl.pallas_call(
        functools.partial(_s5_kernel, reverse=reverse),
        out_shape=jax.ShapeDtypeStruct((L_TOT * BATCH, MIX_W), out_dtype),
        grid=(S5_NC // 2,),
        in_specs=specs,
        out_specs=pair,
        scratch_shapes=[
            pltpu.VMEM((S5_ROWS, S5_SW), F32),
            pltpu.VMEM((S5_ROWS, S5_SW), F32),
            pltpu.VMEM((BATCH, S5_SW), F32),
        ],
        compiler_params=_cparams(("arbitrary",)),
        name="s5_bwd" if reverse else "s5_fwd",
    )(*args)


POST_ROWS = POST_T * BATCH
POST_HALO_T = BF16_ROWS // BATCH
POST_HALO = POST_HALO_T * BATCH
POST_NLAT = SEQ // POST_T
FFN_NCH = FFN_HIDDEN // FFN_CH


def _post_kernel(h_ref, hp_ref, hn_ref, a_ref, ap_ref, an_ref, b_ref, bp_ref, bn_ref, wo_ref,
                 g2_ref, sh_ref, sc_ref, g5_ref, g_ref, wu_ref, cw_ref, cb_ref, wd_ref, o_ref, *,
                 n_tiles, out_batch_major, layer):
    j = pl.program_id(0)
    first = jnp.logical_or(j == 0, j == POST_NLAT)
    last = jnp.logical_or(j == POST_NLAT - 1, j == n_tiles - 1)
    gate2 = g2_ref[0]

    def mix(h, a, b_tm):
        m = _dot(a, wo_ref[0:MIX_W, :]) + _dot(b_tm.astype(BF16), wo_ref[MIX_W:2 * MIX_W, :])
        return h + _per_sample(m, gate2, jnp.multiply)

    def norm(x):
        return _adaln(x, g_ref[layer, 1:2, :], sc_ref[0], sh_ref[0])

    def halo_b(ref, t0):
        x = jnp.swapaxes(ref[...].astype(F32), 0, 1)[t0:t0 + POST_HALO_T]
        return x.reshape(POST_HALO, MIX_W)

    h_mid = mix(h_ref[...], a_ref[...], _to_time_major(b_ref[...].astype(F32)))
    h_prev = mix(hp_ref[...], ap_ref[...], halo_b(bp_ref, BF16_ROWS - POST_HALO_T))
    h_next = mix(hn_ref[...], an_ref[...], halo_b(bn_ref, 0))
    xn = jnp.concatenate([
        (norm(h_prev) * jnp.where(first, 0.0, 1.0)).astype(BF16),
        norm(h_mid).astype(BF16),
        (norm(h_next) * jnp.where(last, 0.0, 1.0)).astype(BF16)], axis=0)

    acts = []
    for c in range(FFN_NCH):
        cv = slice(c * FFN_CH, (c + 1) * FFN_CH)
        cg = slice(FFN_HIDDEN + c * FFN_CH, FFN_HIDDEN + (c + 1) * FFN_CH)

        def conv(cols):
            hid = _dot(xn, wu_ref[:, cols])
            out = cb_ref[layer:layer + 1, cols]
            for k in range(FFN_CONV):
                r0 = POST_HALO + (k - 1) * BATCH
                out = out + hid[r0:r0 + POST_ROWS] * cw_ref[0, k:k + 1, cols]
            return out

        val = conv(cv)
        gate = conv(cg)
        acts.append((val * (gate * jax.nn.sigmoid(gate))).astype(BF16))
    ffn = _dot(jnp.concatenate(acts, axis=1), wd_ref[...])
    out = h_mid + _per_sample(ffn, g5_ref[0], jnp.multiply)
    if out_batch_major:
        o_ref[...] = _to_batch_major(out)
    else:
        o_ref[...] = out


def _post_call(h, a_tb, b_out, w_out, modt, g, wu, cw, cb, wd, layer, n_tiles, out_batch_major):
    per = POST_ROWS // POST_HALO
    n_halo = h.shape[0] // POST_HALO
    per_b = POST_T // BF16_ROWS
    n_halo_b = b_out.shape[1] // BF16_ROWS

    def main(width):
        return pl.BlockSpec((POST_ROWS, width), lambda j: (j, 0))

    def prev(width):
        return pl.BlockSpec((POST_HALO, width), lambda j: (jnp.maximum(j * per - 1, 0), 0))

    def nxt(width):
        return pl.BlockSpec((POST_HALO, width), lambda j: (jnp.minimum((j + 1) * per, n_halo - 1), 0))

    if out_batch_major:
        out_shape = jax.ShapeDtypeStruct((BATCH, n_tiles * POST_T, D_MODEL), F32)
        out_spec = pl.BlockSpec((BATCH, POST_T, D_MODEL), lambda j: (0, j, 0))
    else:
        out_shape = jax.ShapeDtypeStruct((n_tiles * POST_ROWS, D_MODEL), F32)
        out_spec = main(D_MODEL)
    return pl.pallas_call(
        functools.partial(_post_kernel, n_tiles=n_tiles, out_batch_major=out_batch_major, layer=layer),
        out_shape=out_shape,
        grid=(n_tiles,),
        in_specs=[
            main(D_MODEL), prev(D_MODEL), nxt(D_MODEL),
            main(MIX_W), prev(MIX_W), nxt(MIX_W),
            pl.BlockSpec((BATCH, POST_T, MIX_W), lambda j: (0, j, 0)),
            pl.BlockSpec((BATCH, BF16_ROWS, MIX_W), lambda j: (0, jnp.maximum(j * per_b - 1, 0), 0)),
            pl.BlockSpec((BATCH, BF16_ROWS, MIX_W), lambda j: (0, jnp.minimum((j + 1) * per_b, n_halo_b - 1), 0)),
            _full((2 * MIX_W, D_MODEL), single=True),
            _mod_spec(2, POST_NLAT), _mod_spec(3, POST_NLAT), _mod_spec(4, POST_NLAT), _mod_spec(5, POST_NLAT),
            _full(g.shape),
            _full((D_MODEL, 2 * FFN_HIDDEN), single=True),
            pl.BlockSpec((1, FFN_CONV, 2 * FFN_HIDDEN), lambda j: (layer, 0, 0)),
            _full(cb.shape),
            _full((FFN_HIDDEN, D_MODEL), single=True),
        ],
        out_specs=out_spec,
        compiler_params=_cparams(("parallel",)),
        name="post_ffn",
    )(h, h, h, a_tb, a_tb, a_tb, b_out, b_out, b_out, w_out, modt, modt, modt, modt, g, wu, cw, cb, wd)


def _rope_table():
    rows = SEQ // GRID_W
    row = jnp.repeat(jnp.arange(rows, dtype=F32), GRID_W)
    col = jnp.tile(jnp.arange(GRID_W, dtype=F32), rows)
    quarter = HEAD_DIM // 4
    inv_freq = ROPE_BASE ** (-jnp.arange(quarter, dtype=F32) / quarter)
    ang = jnp.stack([row[:, None] * inv_freq, col[:, None] * inv_freq], axis=1)
    cos = jnp.cos(ang)
    sin = jnp.sin(ang)
    zero = jnp.zeros_like(sin)

    def lanes(first, second):
        t = jnp.stack([first, second], axis=2).reshape(SEQ, HEAD_DIM)
        return jnp.tile(t, (1, LANE // HEAD_DIM))

    tab = jnp.concatenate([lanes(cos, cos), lanes(-sin, zero), lanes(zero, sin)], axis=1)
    ctx = jnp.concatenate([jnp.ones((CTX_LEN, LANE), F32), jnp.zeros((CTX_LEN, 2 * LANE), F32)], axis=1)
    return jnp.concatenate([tab, ctx], axis=0)


def _perm_heads_cols(w):
    return jnp.concatenate([w[:, h * HEAD_DIM:(h + 1) * HEAD_DIM] for h in HEAD_PERM], axis=1)


def _perm_heads_rows(w):
    return jnp.concatenate([w[h * HEAD_DIM:(h + 1) * HEAD_DIM] for h in HEAD_PERM], axis=0)


def _block_diag(w, per):
    n, a, b = w.shape
    eye = jnp.eye(per, dtype=w.dtype)
    w4 = w.reshape(n // per, per, a, b)
    return jnp.einsum("ihab,hk->ihakb", w4, eye).reshape(n // per, per * a, per * b)


def _w_in(w, q_off):
    w = w.astype(BF16)
    return jnp.concatenate([w[:, :q_off], _perm_heads_cols(w[:, q_off:q_off + MIX_W]), w[:, q_off + MIX_W:]], axis=1)


def _w_out(w):
    w = w.astype(BF16)
    return jnp.concatenate([w[:MIX_W], _perm_heads_rows(w[MIX_W:])], axis=0)


def kernel(x, c, ctx, c_ctx, mod_w, mod_b, norm_g, ffn_up, ffn_conv_w, ffn_conv_b, ffn_down, ev_w_in, ev_w_out, rg_conv_w, rg_conv_b, rg_wa, rg_ba, rg_wx, rg_bx, rg_lam, ga_qn, ga_kn, od_w_in, od_w_out, s5_lam_re, s5_lam_im, s5_log_step, s5_b_re, s5_b_im, s5_c_re, s5_c_im, s5_d, s5_glu_w, s5_glu_b, wa_qn, wa_kn, wa_sink):
    cvec = jnp.concatenate([c, jnp.broadcast_to(c_ctx[None], (BATCH, D_MODEL))], axis=0)
    mod = _mod_call(cvec, mod_w, mod_b).reshape(DEPTH, 2, BATCH, N_MOD * D_MODEL)
    rope = _rope_table()
    gm = _block_diag(jnp.full((LANE // HEAD_DIM, HEAD_DIM, HEAD_DIM), 1.0 / HEAD_DIM, F32), LANE // HEAD_DIM)[0].astype(BF16)

    h, u_tb, gate_tb, q, k, v, wu, wd = _in_call(
        (x, ctx), mod[0], norm_g, _w_in(ev_w_in[0], 2 * MIX_W), rope,
        ga_qn, ga_kn, gm, ffn_up, ffn_down, 0, n_tb=2, first_layer=True)
    w_gates = _block_diag(jnp.stack([rg_wa[0], rg_wx[0]]).astype(BF16).reshape(-1, HEAD_DIM, HEAD_DIM), 4)
    rg = (rg_conv_w, rg_conv_b, w_gates, rg_ba, rg_bx, rg_lam)
    b_out, hb, uc = _gattn_call(q, k, v, u_tb, *rg)
    a_out = _rg_call(uc, gate_tb, hb, *rg, reverse=False)
    h = _post_call(h, a_out, b_out, _w_out(ev_w_out[0]), mod[0], norm_g, wu, ffn_conv_w, ffn_conv_b, wd, 0,
                   n_tiles=L_TOT // POST_T, out_batch_major=False)

    u_tb, q, k, v, wu, wd = _in_call(
        (h,), mod[1], norm_g, _w_in(od_w_in[0], MIX_W), rope,
        wa_qn, wa_kn, gm, ffn_up, ffn_down, 1, n_tb=1, first_layer=False)
    n_state = S5_GROUPS * S5_STATE
    bt_re = s5_b_re[0].transpose(0, 3, 1, 2).reshape(2, S5_GROUP, n_state)
    bt_im = s5_b_im[0].transpose(0, 3, 1, 2).reshape(2, S5_GROUP, n_state)
    log_step = jnp.repeat(s5_log_step[0], S5_STATE, axis=-1).reshape(2, 1, n_state)
    abr, abi, bbr, bbi = _s5_disc_call(s5_lam_re[0].reshape(2, 1, n_state), s5_lam_im[0].reshape(2, 1, n_state),
                                       log_step, bt_re, bt_im)
    gpb = S5_GROUPS // S5_NBLK
    eye = jnp.eye(gpb, dtype=BF16)
    bb = jnp.stack([bbr, bbi]).astype(BF16).reshape(2, 2, S5_GROUP, S5_NBLK, gpb, S5_STATE)
    w_drive = jnp.einsum("rdhbgp,gk->dbghrkp", bb, eye).reshape(
        2, S5_NBLK, gpb * S5_GROUP, 2 * S5_BLK_STATE)
    cc = jnp.stack([s5_c_re[0], -s5_c_im[0]]).astype(BF16).reshape(
        2, 2, S5_NBLK, gpb, S5_GROUP, S5_STATE)
    w_read = jnp.einsum("rdbghp,gk->dbrgpkh", cc, eye).reshape(
        2, S5_NBLK, 2 * S5_BLK_STATE, gpb * S5_GROUP)

    def state_rows(re, im):
        row = jnp.concatenate([re.reshape(2, S5_NBLK, S5_BLK_STATE), im.reshape(2, S5_NBLK, S5_BLK_STATE)],
                              axis=2).reshape(2, 1, S5_SW)
        return jnp.broadcast_to(row, (2, BATCH, S5_SW))

    s5 = (w_drive, state_rows(abr, abr), state_rows(-abi, abi), w_read)
    yb = _s5_call(u_tb, None, *s5, None, None, None, reverse=True)
    c_out = _s5_call(u_tb, yb, *s5, s5_d, s5_glu_w[0].astype(BF16), s5_glu_b, reverse=False)
    d_out = _wattn_call(q, k, v, wa_sink)
    return _post_call(h, c_out, d_out, _w_out(od_w_out[0]), mod[1], norm_g, wu, ffn_conv_w, ffn_conv_b, wd, 1,
                      n_tiles=SEQ // POST_T, out_batch_major=True)
```

```python
import functools
import math

import jax
import jax.numpy as jnp
from jax import lax
from jax.experimental import pallas as pl
from jax.experimental.pallas import tpu as pltpu

F32 = jnp.float32
BF16 = jnp.bfloat16

D_MODEL = 1024
BATCH = 8
SEQ = 2048
CTX_LEN = 256
L_TOT = SEQ + CTX_LEN
DEPTH = 2
GRID_W = 64
HEAD_DIM = 64
ROPE_BASE = 10000.0
NORM_EPS = 1e-6
WINDOW = 128
N_MOD = 6
MIX_W = 512
N_HEADS = 8
N_KV = 2
KV_W = N_KV * HEAD_DIM
V_W = 2 * KV_W
QKV_W = MIX_W + KV_W + KV_W
Q_SCALE = HEAD_DIM ** -0.5 * math.log2(math.e)
RG_CONV = 4
RG_C = 8.0
S5_GROUP = 16
S5_GROUPS = 32
S5_STATE = 64
S5_NBLK = 4
S5_BLK_STATE = (S5_GROUPS // S5_NBLK) * S5_STATE
FFN_HIDDEN = 2816
FFN_CONV = 3

LANE = 128
BF16_ROWS = 16
VMEM_LIMIT = 56 * 1024 * 1024

IN_T = 128
RG_T = 128
S5_T = 64
POST_T = 128
TQ_G = 256
G_NB = 4
W_NB = 8
FFN_CH = 256
HEAD_PERM = (0, 4, 1, 5, 2, 6, 3, 7)


def _cparams(sem):
    return pltpu.CompilerParams(dimension_semantics=sem, vmem_limit_bytes=VMEM_LIMIT)


def _dot(a, b):
    return jnp.dot(a, b, preferred_element_type=F32)


def _gelu_tanh(x):
    return x * (0.5 * (1.0 + jnp.tanh(math.sqrt(2.0 / math.pi) * (x + 0.044715 * (x * x * x)))))


def _rms_unit(x):
    ms = jnp.mean(x * x, axis=-1, keepdims=True)
    return x * lax.rsqrt(ms + NORM_EPS)


def _adaln(x, gain, scale, shift):
    y = _per_sample(_rms_unit(x), gain * (1.0 + scale), jnp.multiply)
    return _per_sample(y, shift, jnp.add)


def _to_time_major(x):
    b, t, c = x.shape
    return jnp.swapaxes(x, 0, 1).reshape(t * b, c)


def _to_batch_major(x):
    r, c = x.shape
    return jnp.swapaxes(x.reshape(r // BATCH, BATCH, c), 0, 1)


def _per_sample(x, v, op):
    r, c = x.shape
    return op(x.reshape(r // BATCH, BATCH, c), v[None]).reshape(r, c)


def _full(shape, single=False):
    kw = dict(pipeline_mode=pl.Buffered(1)) if single else {}
    return pl.BlockSpec(shape, lambda *_: (0,) * len(shape), **kw)


MOD_TN = 3072


def _mod_kernel(c_ref, wa_ref, wb_ref, b_ref, o_ref):
    c = c_ref[...]
    a = c * jax.nn.sigmoid(c)
    ah = a.astype(BF16)
    al = (a - ah.astype(F32)).astype(BF16)
    for i, w_ref in enumerate((wa_ref, wb_ref)):
        w = w_ref[0]
        wh = w.astype(BF16)
        wl = (w - wh.astype(F32)).astype(BF16)
        cols = slice(i * (MOD_TN // 2), (i + 1) * (MOD_TN // 2))
        o_ref[0, :, cols] = _dot(ah, wh) + _dot(ah, wl) + _dot(al, wh) + b_ref[0, :, cols]


def _mod_call(cvec, mod_w, mod_b):
    n = N_MOD * D_MODEL
    return pl.pallas_call(
        _mod_kernel,
        out_shape=jax.ShapeDtypeStruct((DEPTH, 2 * BATCH, n), F32),
        grid=(DEPTH, n // MOD_TN),
        in_specs=[
            pl.BlockSpec((2 * BATCH, D_MODEL), lambda l, k: (0, 0)),
            pl.BlockSpec((1, D_MODEL, MOD_TN // 2), lambda l, k: (l, 0, 2 * k)),
            pl.BlockSpec((1, D_MODEL, MOD_TN // 2), lambda l, k: (l, 0, 2 * k + 1)),
            pl.BlockSpec((1, 1, MOD_TN), lambda l, k: (l, 0, k)),
        ],
        out_specs=pl.BlockSpec((1, 2 * BATCH, MOD_TN), lambda l, k: (l, 0, k)),
        compiler_params=_cparams(("parallel", "parallel")),
        name="mod",
    )(cvec, mod_w, mod_w, mod_b.reshape(DEPTH, 1, n))


def _mod_spec(col, n_latent_tiles):
    return pl.BlockSpec((1, BATCH, D_MODEL), lambda j: (jnp.where(j < n_latent_tiles, 0, 1), 0, col))


IN_ROWS = IN_T * BATCH
IN_NLAT = SEQ // IN_T
IN_N = L_TOT // IN_T
IN_SUB = 2
CAST_STEPS = 16
assert CAST_STEPS <= IN_N
IN_SUB_T = IN_T // IN_SUB
IN_SUB_ROWS = IN_SUB_T * BATCH


def _in_kernel(fu_ref, fd_ref, *refs, n_tb, first_layer, layer):
    fu_out, fd_out = refs[-2:]
    refs = refs[:-2]

    @pl.when(pl.program_id(0) < CAST_STEPS)
    def _():
        fu_out[...] = fu_ref[0].astype(BF16)
        fd_out[...] = fd_ref[0].astype(BF16)

    if first_layer:
        x_ref, c_ref, sh_ref, sc_ref, g_ref, w_ref, rope_ref, gq_ref, gk_ref, gm_ref = refs[:10]
        h_out = refs[10]
        outs = refs[11:]
        is_latent = pl.program_id(0) < IN_NLAT
    else:
        h_ref, sh_ref, sc_ref, g_ref, w_ref, rope_ref, gq_ref, gk_ref, gm_ref = refs[:9]
        outs = refs[9:]
    q_ref, k_ref, v_ref = outs[n_tb:]
    off = n_tb * MIX_W
    gm = gm_ref[...]
    gq = jnp.concatenate([gq_ref[...]] * (LANE // HEAD_DIM), axis=1)
    gk = jnp.concatenate([gk_ref[...]] * (LANE // HEAD_DIM), axis=1)
    for s in range(IN_SUB):
        ts = slice(s * IN_SUB_T, (s + 1) * IN_SUB_T)
        rs = slice(s * IN_SUB_ROWS, (s + 1) * IN_SUB_ROWS)
        if first_layer:
            h = _to_time_major(jnp.where(is_latent, x_ref[:, ts, :], c_ref[:, ts, :]))
            h_out[rs, :] = h
        else:
            h = h_ref[rs, :]
        xn = _adaln(h, g_ref[layer, 0:1, :], sc_ref[0], sh_ref[0])
        proj = _dot(xn.astype(BF16), w_ref[...])
        for i in range(n_tb):
            outs[i][rs, :] = proj[:, i * MIX_W:(i + 1) * MIX_W]
        qkv = _to_batch_major(proj[:, off:off + QKV_W])
        cos = rope_ref[ts, 0:LANE][None]
        sin_up = rope_ref[ts, LANE:2 * LANE][None]
        sin_dn = rope_ref[ts, 2 * LANE:3 * LANE][None]
        for p in range(5):
            blk = qkv[:, :, p * LANE:(p + 1) * LANE].reshape(IN_SUB_ROWS, LANE)
            sq = blk * blk
            hi = sq.astype(BF16)
            lo = (sq - hi.astype(F32)).astype(BF16)
            ms = _dot(hi, gm) + _dot(lo, gm)
            g = gq if p < 4 else gk
            bn = blk * lax.rsqrt(ms + NORM_EPS) * g
            up = pltpu.roll(bn, LANE - HEAD_DIM // 4, 1).reshape(BATCH, IN_SUB_T, LANE)
            dn = pltpu.roll(bn, HEAD_DIM // 4, 1).reshape(BATCH, IN_SUB_T, LANE)
            ro = bn.reshape(BATCH, IN_SUB_T, LANE) * cos + up * sin_up + dn * sin_dn
            if p < 4:
                q_ref[:, ts, p * LANE:(p + 1) * LANE] = (ro * Q_SCALE).astype(BF16)
            else:
                k_ref[:, ts, :] = ro.astype(BF16)
        v = qkv[:, :, 5 * LANE:6 * LANE].astype(BF16)
        v_ref[:, ts, :] = jnp.concatenate([v, jnp.ones((BATCH, IN_SUB_T, LANE), BF16)], axis=2)


def _in_call(h_args, modt, g, w, rope, gq, gk, gm, ffn_up, ffn_down, layer, n_tb, first_layer):
    n = w.shape[1]
    tb = lambda width: pl.BlockSpec((IN_ROWS, width), lambda j: (j, 0))
    bm = lambda width: pl.BlockSpec((BATCH, IN_T, width), lambda j: (0, j, 0))
    up_rows = D_MODEL // CAST_STEPS
    down_rows = FFN_HIDDEN // CAST_STEPS
    slab = lambda j: jnp.minimum(j, CAST_STEPS - 1)
    if first_layer:
        h_specs = [
            pl.BlockSpec((BATCH, IN_T, D_MODEL), lambda j: (0, jnp.minimum(j, IN_NLAT - 1), 0)),
            pl.BlockSpec((BATCH, IN_T, D_MODEL), lambda j: (0, jnp.maximum(j - IN_NLAT, 0), 0)),
        ]
        extra_shape = [jax.ShapeDtypeStruct((L_TOT * BATCH, D_MODEL), F32)]
        extra_spec = [tb(D_MODEL)]
    else:
        h_specs = [tb(D_MODEL)]
        extra_shape, extra_spec = [], []
    return pl.pallas_call(
        functools.partial(_in_kernel, n_tb=n_tb, first_layer=first_layer, layer=layer),
        out_shape=extra_shape + [jax.ShapeDtypeStruct((L_TOT * BATCH, MIX_W), F32)] * n_tb + [
            jax.ShapeDtypeStruct((BATCH, L_TOT, MIX_W), BF16),
            jax.ShapeDtypeStruct((BATCH, L_TOT, KV_W), BF16),
            jax.ShapeDtypeStruct((BATCH, L_TOT, V_W), BF16),
            jax.ShapeDtypeStruct((D_MODEL, 2 * FFN_HIDDEN), BF16),
            jax.ShapeDtypeStruct((FFN_HIDDEN, D_MODEL), BF16),
        ],
        grid=(IN_N,),
        in_specs=[
            pl.BlockSpec((1, up_rows, 2 * FFN_HIDDEN), lambda j: (layer, slab(j), 0)),
            pl.BlockSpec((1, down_rows, D_MODEL), lambda j: (layer, slab(j), 0)),
        ] + h_specs + [
            _mod_spec(0, IN_NLAT), _mod_spec(1, IN_NLAT),
            _full(g.shape),
            _full((D_MODEL, n), single=True),
            pl.BlockSpec((IN_T, 3 * LANE), lambda j: (j, 0)),
            _full(gq.shape), _full(gk.shape), _full((LANE, LANE)),
        ],
        out_specs=extra_spec + [tb(MIX_W)] * n_tb + [bm(MIX_W), bm(KV_W), bm(V_W)] + [
            pl.BlockSpec((up_rows, 2 * FFN_HIDDEN), lambda j: (slab(j), 0)),
            pl.BlockSpec((down_rows, D_MODEL), lambda j: (slab(j), 0)),
        ],
        compiler_params=_cparams(("arbitrary",)),
        name="in_proj",
    )(ffn_up, ffn_down, *h_args, modt, modt, g, w, rope, gq, gk, gm)


def _chunk_of(i, reverse, n_latent, n_chunks):
    if reverse:
        return n_chunks - 1 - i
    return jnp.where(i < n_chunks - n_latent, i + n_latent, i - (n_chunks - n_latent))


RG_ROWS = RG_T * BATCH
RG_NC = L_TOT // RG_T
RG_NLAT = SEQ // RG_T


def _softplus(z):
    return jnp.maximum(z, 0.0) + jnp.log1p(jnp.exp(-jnp.abs(z)))


def _rg_kernel(*refs, reverse):
    i = pl.program_id(0)

    @pl.when(i == 0)
    def _():
        refs[-1][...] = jnp.zeros_like(refs[-1])

    _rg_chunk(i, refs, reverse)


def _rg_chunk(i, refs, reverse):
    if reverse:
        (u_ref, up_ref, un_ref, cw_ref, cb_ref, w_ref, ba_ref, bx_ref, lam_ref,
         out_ref, uc_out, xbuf, abuf, bbuf, hst) = refs
    else:
        uc_ref, gate_ref, hb_ref, w_ref, ba_ref, bx_ref, lam_ref, out_ref, abuf, bbuf, hst = refs
    d = 1 if reverse else 0

    if reverse:
        c = _chunk_of(i, reverse, RG_NLAT, RG_NC)
        prev_zero = jnp.logical_or(c == 0, c == RG_NLAT)
        next_zero = jnp.logical_or(c == RG_NLAT - 1, c == RG_NC - 1)
        xbuf[0:BATCH, :] = up_ref[...] * jnp.where(prev_zero, 0.0, 1.0)
        xbuf[BATCH:BATCH + RG_ROWS, :] = u_ref[...]
        xbuf[BATCH + RG_ROWS:3 * BATCH + RG_ROWS, :] = un_ref[...] * jnp.where(next_zero, 0.0, 1.0)
        uc = cb_ref[...]
        for k in range(RG_CONV):
            uc = uc + xbuf[k * BATCH:k * BATCH + RG_ROWS, :] * cw_ref[0, k:k + 1, :]
        uc_out[...] = uc
    else:
        uc = uc_ref[...]

    ub = uc.astype(BF16)
    half = MIX_W // 2

    def gate(which, b_ref):
        w0 = which * 4 + d * 2
        z = jnp.concatenate([_dot(ub[:, :half], w_ref[w0]), _dot(ub[:, half:], w_ref[w0 + 1])], axis=1)
        return jax.nn.sigmoid(z + b_ref[0, d:d + 1, :])

    r = gate(0, ba_ref)
    ig = gate(1, bx_ref)
    log_a = (-RG_C) * r * _softplus(-lam_ref[0, d:d + 1, :])
    a = jnp.exp(log_a)
    abuf[...] = a
    bbuf[...] = jnp.sqrt(-jnp.tanh(log_a) * (a * a + 1.0)) * (ig * uc)

    h = hst[...]
    for s in range(RG_T):
        t = (RG_T - 1 - s) if reverse else s
        rows = slice(t * BATCH, (t + 1) * BATCH)
        h = abuf[rows, :] * h + bbuf[rows, :]
        bbuf[rows, :] = h
    hst[...] = h

    if reverse:
        out_ref[...] = bbuf[...]
    else:
        y = bbuf[...] + hb_ref[...]
        out_ref[...] = (y * _gelu_tanh(gate_ref[...])).astype(BF16)


def _rg_fwd_call(uc, gate, hb, w_gates, ba, bx, lam):
    main = pl.BlockSpec((RG_ROWS, MIX_W), lambda i: (_chunk_of(i, False, RG_NLAT, RG_NC), 0))
    gates = [w_gates, ba, bx, lam]
    return pl.pallas_call(
        functools.partial(_rg_kernel, reverse=False),
        out_shape=jax.ShapeDtypeStruct((L_TOT * BATCH, MIX_W), BF16),
        grid=(RG_NC,),
        in_specs=[main, main, main] + [_full(p.shape) for p in gates],
        out_specs=main,
        scratch_shapes=[pltpu.VMEM((RG_ROWS, MIX_W), F32), pltpu.VMEM((RG_ROWS, MIX_W), F32),
                        pltpu.VMEM((BATCH, MIX_W), F32)],
        compiler_params=_cparams(("arbitrary",)),
        name="rglru_fwd",
    )(uc, gate, hb, *gates)


def _head_rows(q_ref, tq, b):
    lane = lax.broadcasted_iota(jnp.int32, (tq, LANE), 1)
    low = lane < HEAD_DIM
    zero = jnp.zeros((tq, LANE), BF16)
    parts = []
    for p in range(MIX_W // LANE):
        qb = q_ref[b, :, p * LANE:(p + 1) * LANE]
        parts.append(jnp.where(low, qb, zero))
        parts.append(jnp.where(low, zero, qb))
    return parts, low


def _merge_heads(outs, low):
    return jnp.concatenate([jnp.where(low, outs[2 * p], outs[2 * p + 1]) for p in range(MIX_W // LANE)], axis=1)


def _qk(q, k):
    return lax.dot_general(q, k, (((1,), (1,)), ((), ())), preferred_element_type=F32)


def _gattn_kernel(q_ref, k_ref, v_ref, *rest):
    rg_in, o_ref, rg_rest = rest[:9], rest[9], rest[10:]
    i = pl.program_id(0)
    j = lax.rem(i, G_TILES)

    @pl.when(i == 0)
    def _():
        rg_rest[-1][...] = jnp.zeros_like(rg_rest[-1])

    def attend(k0, nk):
        _rg_chunk(i, rg_in + rg_rest, True)
        for b in range(G_NB):
            heads, low = _head_rows(q_ref, TQ_G, b)
            outs = []
            for q in heads:
                s = _qk(q, k_ref[b, k0:k0 + nk, :])
                m = jnp.max(s, axis=-1, keepdims=True)
                p = jnp.exp2(s - m).astype(BF16)
                acc = _dot(p, v_ref[b, k0:k0 + nk, :])
                outs.append(acc[:, :KV_W] / acc[:, KV_W:])
            o_ref[b] = _merge_heads(outs, low).astype(BF16)

    @pl.when(j < SEQ // TQ_G)
    def _():
        attend(0, L_TOT)

    @pl.when(j >= SEQ // TQ_G)
    def _():
        attend(SEQ, CTX_LEN)


G_TILES = L_TOT // TQ_G
assert (BATCH // G_NB) * G_TILES == RG_NC


def _gattn_call(q, k, v, u, cw, cb, w_gates, ba, bx, lam):
    def cidx(i):
        return _chunk_of(i, True, RG_NLAT, RG_NC)

    main = pl.BlockSpec((RG_ROWS, MIX_W), lambda i: (cidx(i), 0))
    prev = pl.BlockSpec((BATCH, MIX_W), lambda i: (jnp.maximum(cidx(i) * RG_T - 1, 0), 0))
    n_next = L_TOT // 2
    nxt = pl.BlockSpec((2 * BATCH, MIX_W), lambda i: (jnp.minimum((cidx(i) + 1) * (RG_T // 2), n_next - 1), 0))
    params = [cw, cb, w_gates, ba, bx, lam]
    rows = jax.ShapeDtypeStruct((L_TOT * BATCH, MIX_W), F32)
    attn_spec = pl.BlockSpec((G_NB, TQ_G, MIX_W), lambda i: (i // G_TILES, lax.rem(i, G_TILES), 0))
    return pl.pallas_call(
        _gattn_kernel,
        out_shape=[jax.ShapeDtypeStruct((BATCH, L_TOT, MIX_W), BF16), rows, rows],
        grid=(RG_NC,),
        in_specs=[
            attn_spec,
            pl.BlockSpec((G_NB, L_TOT, KV_W), lambda i: (i // G_TILES, 0, 0)),
            pl.BlockSpec((G_NB, L_TOT, V_W), lambda i: (i // G_TILES, 0, 0)),
            main, prev, nxt,
        ] + [_full(p.shape) for p in params],
        out_specs=[attn_spec, main, main],
        scratch_shapes=[
            pltpu.VMEM((RG_ROWS + 3 * BATCH, MIX_W), F32),
            pltpu.VMEM((RG_ROWS, MIX_W), F32),
            pltpu.VMEM((RG_ROWS, MIX_W), F32),
            pltpu.VMEM((BATCH, MIX_W), F32),
        ],
        compiler_params=_cparams(("arbitrary",)),
        name="global_attn_rglru_bwd",
    )(q, k, v, u, u, u, *params)


TQ_W = 2 * WINDOW
N_WBLK = SEQ // WINDOW
N_WSTEP = SEQ // TQ_W


def _wattn_kernel(q_ref, k_ref, v_ref, sink_ref, o_ref):
    i = pl.program_id(1)
    prev_start = pl.multiple_of(jnp.maximum(2 * i - 1, 0) * WINDOW, WINDOW)
    mid_start = pl.multiple_of(i * TQ_W, TQ_W)
    next_start = pl.multiple_of(jnp.minimum(2 * i + 2, N_WBLK - 1) * WINDOW, WINDOW)

    def rows(ref, b):
        return jnp.concatenate([ref[b, SEQ:L_TOT, :], ref[b, pl.ds(prev_start, WINDOW), :],
                                ref[b, pl.ds(mid_start, TQ_W), :], ref[b, pl.ds(next_start, WINDOW), :]], axis=0)

    nk = CTX_LEN + 2 * WINDOW + TQ_W
    r = lax.broadcasted_iota(jnp.int32, (TQ_W, nk), 0)
    col = lax.broadcasted_iota(jnp.int32, (TQ_W, nk), 1)
    c = col - CTX_LEN
    ninf = -jnp.inf
    pen_prev = jnp.where(i > 0, 0.0, ninf)
    pen_next = jnp.where(i < N_WSTEP - 1, 0.0, ninf)
    edge = jnp.where(c < WINDOW, pen_prev, jnp.where(c >= WINDOW + TQ_W, pen_next, 0.0))
    band = jnp.where(c >= r, jnp.where(c <= r + 2 * WINDOW, edge, ninf), ninf)
    bias = jnp.where(col < CTX_LEN, 0.0, band)
    for b in range(W_NB):
        heads, low = _head_rows(q_ref, TQ_W, b)
        kk = rows(k_ref, b)
        vv = rows(v_ref, b)
        outs = []
        for hd, q in enumerate(heads):
            s = _qk(q, kk) + bias
            sink = sink_ref[0, HEAD_PERM[hd]] * math.log2(math.e)
            m = jnp.maximum(jnp.max(s, axis=-1, keepdims=True), sink)
            p = jnp.exp2(s - m).astype(BF16)
            acc = _dot(p, vv)
            outs.append(acc[:, :KV_W] / (acc[:, KV_W:] + jnp.exp2(sink - m)))
        o_ref[b] = _merge_heads(outs, low).astype(BF16)


def _wattn_call(q, k, v, sink):
    return pl.pallas_call(
        _wattn_kernel,
        out_shape=jax.ShapeDtypeStruct((BATCH, SEQ, MIX_W), BF16),
        grid=(BATCH // W_NB, N_WSTEP),
        in_specs=[
            pl.BlockSpec((W_NB, TQ_W, MIX_W), lambda b, i: (b, i, 0)),
            pl.BlockSpec((W_NB, L_TOT, KV_W), lambda b, i: (b, 0, 0)),
            pl.BlockSpec((W_NB, L_TOT, V_W), lambda b, i: (b, 0, 0)),
            pl.BlockSpec(memory_space=pltpu.SMEM),
        ],
        out_specs=pl.BlockSpec((W_NB, TQ_W, MIX_W), lambda b, i: (b, i, 0)),
        compiler_params=_cparams(("parallel", "arbitrary")),
        name="window_attn",
    )(q, k, v, sink)


def _s5_disc_kernel(lr_ref, li_ref, ls_ref, br_ref, bi_ref, ar_ref, ai_ref, bbr_ref, bbi_ref):
    lr = lr_ref[0]
    li = li_ref[0]
    dt = jnp.exp(ls_ref[0])
    mag = jnp.exp(lr * dt)
    ang = li * dt
    abr = mag * jnp.cos(ang)
    abi = mag * jnp.sin(ang)
    den = lr * lr + li * li
    nr = abr - 1.0
    kr = (nr * lr + abi * li) / den
    ki = (abi * lr - nr * li) / den
    br = br_ref[0]
    bi = bi_ref[0]
    ar_ref[0] = abr
    ai_ref[0] = abi
    bbr_ref[0] = kr * br - ki * bi
    bbi_ref[0] = kr * bi + ki * br


def _s5_disc_call(lam_re, lam_im, log_step, bt_re, bt_im):
    n = S5_GROUPS * S5_STATE
    row = pl.BlockSpec((1, 1, n), lambda d: (d, 0, 0))
    mat = pl.BlockSpec((1, S5_GROUP, n), lambda d: (d, 0, 0))
    return pl.pallas_call(
        _s5_disc_kernel,
        out_shape=[jax.ShapeDtypeStruct((2, 1, n), F32)] * 2 + [jax.ShapeDtypeStruct((2, S5_GROUP, n), F32)] * 2,
        grid=(2,),
        in_specs=[row, row, row, mat, mat],
        out_specs=[row, row, mat, mat],
        compiler_params=_cparams(("parallel",)),
        name="s5_discretize",
    )(lam_re, lam_im, log_step, bt_re, bt_im)


S5_ROWS = S5_T * BATCH
S5_NC = L_TOT // S5_T
S5_NLAT = SEQ // S5_T
S5_SW = 2 * S5_BLK_STATE * S5_NBLK


def _s5_kernel(*refs, reverse):
    if reverse:
        u2_ref, un_ref, wd_ref, ar_ref, ai_ref, wr_ref, out_ref, buf0, buf1, xst = refs
    else:
        (u2_ref, un_ref, yb2_ref, wd_ref, ar_ref, ai_ref, wr_ref, dsk_ref, gw_ref, gb_ref,
         out_ref, buf0, buf1, xst) = refs
    nb = 2 * S5_BLK_STATE
    lo, hi = slice(0, S5_ROWS), slice(S5_ROWS, 2 * S5_ROWS)
    first, second = (hi, lo) if reverse else (lo, hi)

    def drive(u, buf):
        ub = u.astype(BF16)
        for k in range(S5_NBLK):
            buf[:, k * nb:(k + 1) * nb] = _dot(ub[:, k * LANE:(k + 1) * LANE], wd_ref[0, k])

    def scan(buf, x):
        for s in range(S5_T):
            t = (S5_T - 1 - s) if reverse else s
            rows = slice(t * BATCH, (t + 1) * BATCH)
            halves = []
            for k in range(S5_NBLK):
                halves.append(x[:, k * nb + S5_BLK_STATE:(k + 1) * nb])
                halves.append(x[:, k * nb:k * nb + S5_BLK_STATE])
            xsw = jnp.concatenate(halves, axis=1)
            x = ar_ref[0] * x + ai_ref[0] * xsw + buf[rows, :]
            buf[rows, :] = x
        return x

    def readout(buf, rows):
        y = jnp.concatenate(
            [_dot(buf[:, k * nb:(k + 1) * nb].astype(BF16), wr_ref[0, k]) for k in range(S5_NBLK)], axis=1)
        if reverse:
            out_ref[rows, :] = y
        else:
            y = u2_ref[rows, :] * dsk_ref[...] + y + yb2_ref[rows, :]
            z = _gelu_tanh(y)
            out_ref[rows, :] = (z * jax.nn.sigmoid(_dot(z.astype(BF16), gw_ref[...]) + gb_ref[...])).astype(BF16)

    @pl.when(pl.program_id(0) == 0)
    def _():
        xst[...] = jnp.zeros_like(xst)
        drive(u2_ref[first, :], buf0)

    x = xst[...]
    drive(u2_ref[second, :], buf1)
    x = scan(buf0, x)
    readout(buf0, first)
    drive(un_ref[...], buf0)
    x = scan(buf1, x)
    readout(buf1, second)
    xst[...] = x


def _s5_call(u, yb, wd, ar, ai, wr, dsk, gw, gb, reverse):
    def chunk(p):
        return _chunk_of(p, reverse, S5_NLAT, S5_NC)

    pair = pl.BlockSpec((2 * S5_ROWS, MIX_W), lambda k: (chunk(2 * k) // 2, 0))
    nxt = pl.BlockSpec((S5_ROWS, MIX_W), lambda k: (chunk(jnp.minimum(2 * k + 2, S5_NC - 1)), 0))
    d = 1 if reverse else 0
    dir_specs = [pl.BlockSpec((1,) + a.shape[1:], lambda k, n=a.ndim: (d,) + (0,) * (n - 1)) for a in (wd, ar, ai, wr)]
    if reverse:
        args = [u, u, wd, ar, ai, wr]
        specs = [pair, nxt] + dir_specs
        out_dtype = F32
    else:
        args = [u, u, yb, wd, ar, ai, wr, dsk, gw, gb]
        specs = [pair, nxt, pair] + dir_specs + [_full(a.shape) for a in (dsk, gw, gb)]
        out_dtype = BF16
    return pl.pallas_call(
        functools.partial(_s5_kernel, reverse=reverse),
        out_shape=jax.ShapeDtypeStruct((L_TOT * BATCH, MIX_W), out_dtype),
        grid=(S5_NC // 2,),
        in_specs=specs,
        out_specs=pair,
        scratch_shapes=[
            pltpu.VMEM((S5_ROWS, S5_SW), F32),
            pltpu.VMEM((S5_ROWS, S5_SW), F32),
            pltpu.VMEM((BATCH, S5_SW), F32),
        ],
        compiler_params=_cparams(("arbitrary",)),
        name="s5_bwd" if reverse else "s5_fwd",
    )(*args)


POST_ROWS = POST_T * BATCH
POST_HALO_T = BF16_ROWS // BATCH
POST_HALO = POST_HALO_T * BATCH
POST_NLAT = SEQ // POST_T
FFN_NCH = FFN_HIDDEN // FFN_CH


def _post_kernel(h_ref, hp_ref, hn_ref, a_ref, ap_ref, an_ref, b_ref, bp_ref, bn_ref, wo_ref,
                 g2_ref, sh_ref, sc_ref, g5_ref, g_ref, wu_ref, cw_ref, cb_ref, wd_ref, o_ref, *,
                 n_tiles, out_batch_major, layer):
    j = pl.program_id(0)
    first = jnp.logical_or(j == 0, j == POST_NLAT)
    last = jnp.logical_or(j == POST_NLAT - 1, j == n_tiles - 1)
    gate2 = g2_ref[0]

    def mix(h, a, b_tm):
        m = _dot(a, wo_ref[0:MIX_W, :]) + _dot(b_tm.astype(BF16), wo_ref[MIX_W:2 * MIX_W, :])
        return h + _per_sample(m, gate2, jnp.multiply)

    def norm(x):
        return _adaln(x, g_ref[layer, 1:2, :], sc_ref[0], sh_ref[0])

    def halo_b(ref, t0):
        x = jnp.swapaxes(ref[...].astype(F32), 0, 1)[t0:t0 + POST_HALO_T]
        return x.reshape(POST_HALO, MIX_W)

    h_mid = mix(h_ref[...], a_ref[...], _to_time_major(b_ref[...].astype(F32)))
    h_prev = mix(hp_ref[...], ap_ref[...], halo_b(bp_ref, BF16_ROWS - POST_HALO_T))
    h_next = mix(hn_ref[...], an_ref[...], halo_b(bn_ref, 0))
    xn = jnp.concatenate([
        (norm(h_prev) * jnp.where(first, 0.0, 1.0)).astype(BF16),
        norm(h_mid).astype(BF16),
        (norm(h_next) * jnp.where(last, 0.0, 1.0)).astype(BF16)], axis=0)

    acts = []
    for c in range(FFN_NCH):
        cv = slice(c * FFN_CH, (c + 1) * FFN_CH)
        cg = slice(FFN_HIDDEN + c * FFN_CH, FFN_HIDDEN + (c + 1) * FFN_CH)

        def conv(cols):
            hid = _dot(xn, wu_ref[:, cols])
            out = cb_ref[layer:layer + 1, cols]
            for k in range(FFN_CONV):
                r0 = POST_HALO + (k - 1) * BATCH
                out = out + hid[r0:r0 + POST_ROWS] * cw_ref[0, k:k + 1, cols]
            return out

        val = conv(cv)
        gate = conv(cg)
        acts.append((val * (gate * jax.nn.sigmoid(gate))).astype(BF16))
    ffn = _dot(jnp.concatenate(acts, axis=1), wd_ref[...])
    out = h_mid + _per_sample(ffn, g5_ref[0], jnp.multiply)
    if out_batch_major:
        o_ref[...] = _to_batch_major(out)
    else:
        o_ref[...] = out


def _post_call(h, a_tb, b_out, w_out, modt, g, wu, cw, cb, wd, layer, n_tiles, out_batch_major):
    per = POST_ROWS // POST_HALO
    n_halo = h.shape[0] // POST_HALO
    per_b = POST_T // BF16_ROWS
    n_halo_b = b_out.shape[1] // BF16_ROWS

    def main(width):
        return pl.BlockSpec((POST_ROWS, width), lambda j: (j, 0))

    def prev(width):
        return pl.BlockSpec((POST_HALO, width), lambda j: (jnp.maximum(j * per - 1, 0), 0))

    def nxt(width):
        return pl.BlockSpec((POST_HALO, width), lambda j: (jnp.minimum((j + 1) * per, n_halo - 1), 0))

    if out_batch_major:
        out_shape = jax.ShapeDtypeStruct((BATCH, n_tiles * POST_T, D_MODEL), F32)
        out_spec = pl.BlockSpec((BATCH, POST_T, D_MODEL), lambda j: (0, j, 0))
    else:
        out_shape = jax.ShapeDtypeStruct((n_tiles * POST_ROWS, D_MODEL), F32)
        out_spec = main(D_MODEL)
    return pl.pallas_call(
        functools.partial(_post_kernel, n_tiles=n_tiles, out_batch_major=out_batch_major, layer=layer),
        out_shape=out_shape,
        grid=(n_tiles,),
        in_specs=[
            main(D_MODEL), prev(D_MODEL), nxt(D_MODEL),
            main(MIX_W), prev(MIX_W), nxt(MIX_W),
            pl.BlockSpec((BATCH, POST_T, MIX_W), lambda j: (0, j, 0)),
            pl.BlockSpec((BATCH, BF16_ROWS, MIX_W), lambda j: (0, jnp.maximum(j * per_b - 1, 0), 0)),
            pl.BlockSpec((BATCH, BF16_ROWS, MIX_W), lambda j: (0, jnp.minimum((j + 1) * per_b, n_halo_b - 1), 0)),
            _full((2 * MIX_W, D_MODEL), single=True),
            _mod_spec(2, POST_NLAT), _mod_spec(3, POST_NLAT), _mod_spec(4, POST_NLAT), _mod_spec(5, POST_NLAT),
            _full(g.shape),
            _full((D_MODEL, 2 * FFN_HIDDEN), single=True),
            pl.BlockSpec((1, FFN_CONV, 2 * FFN_HIDDEN), lambda j: (layer, 0, 0)),
            _full(cb.shape),
            _full((FFN_HIDDEN, D_MODEL), single=True),
        ],
        out_specs=out_spec,
        compiler_params=_cparams(("parallel",)),
        name="post_ffn",
    )(h, h, h, a_tb, a_tb, a_tb, b_out, b_out, b_out, w_out, modt, modt, modt, modt, g, wu, cw, cb, wd)


def _rope_table():
    rows = SEQ // GRID_W
    row = jnp.repeat(jnp.arange(rows, dtype=F32), GRID_W)
    col = jnp.tile(jnp.arange(GRID_W, dtype=F32), rows)
    quarter = HEAD_DIM // 4
    inv_freq = ROPE_BASE ** (-jnp.arange(quarter, dtype=F32) / quarter)
    ang = jnp.stack([row[:, None] * inv_freq, col[:, None] * inv_freq], axis=1)
    cos = jnp.cos(ang)
    sin = jnp.sin(ang)
    zero = jnp.zeros_like(sin)

    def lanes(first, second):
        t = jnp.stack([first, second], axis=2).reshape(SEQ, HEAD_DIM)
        return jnp.tile(t, (1, LANE // HEAD_DIM))

    tab = jnp.concatenate([lanes(cos, cos), lanes(-sin, zero), lanes(zero, sin)], axis=1)
    ctx = jnp.concatenate([jnp.ones((CTX_LEN, LANE), F32), jnp.zeros((CTX_LEN, 2 * LANE), F32)], axis=1)
    return jnp.concatenate([tab, ctx], axis=0)


def _perm_heads_cols(w):
    return jnp.concatenate([w[:, h * HEAD_DIM:(h + 1) * HEAD_DIM] for h in HEAD_PERM], axis=1)


def _perm_heads_rows(w):
    return jnp.concatenate([w[h * HEAD_DIM:(h + 1) * HEAD_DIM] for h in HEAD_PERM], axis=0)


def _block_diag(w, per):
    n, a, b = w.shape
    eye = jnp.eye(per, dtype=w.dtype)
    w4 = w.reshape(n // per, per, a, b)
    return jnp.einsum("ihab,hk->ihakb", w4, eye).reshape(n // per, per * a, per * b)


def _w_in(w, q_off):
    w = w.astype(BF16)
    return jnp.concatenate([w[:, :q_off], _perm_heads_cols(w[:, q_off:q_off + MIX_W]), w[:, q_off + MIX_W:]], axis=1)


def _w_out(w):
    w = w.astype(BF16)
    return jnp.concatenate([w[:MIX_W], _perm_heads_rows(w[MIX_W:])], axis=0)


def kernel(x, c, ctx, c_ctx, mod_w, mod_b, norm_g, ffn_up, ffn_conv_w, ffn_conv_b, ffn_down, ev_w_in, ev_w_out, rg_conv_w, rg_conv_b, rg_wa, rg_ba, rg_wx, rg_bx, rg_lam, ga_qn, ga_kn, od_w_in, od_w_out, s5_lam_re, s5_lam_im, s5_log_step, s5_b_re, s5_b_im, s5_c_re, s5_c_im, s5_d, s5_glu_w, s5_glu_b, wa_qn, wa_kn, wa_sink):
    cvec = jnp.concatenate([c, jnp.broadcast_to(c_ctx[None], (BATCH, D_MODEL))], axis=0)
    mod = _mod_call(cvec, mod_w, mod_b).reshape(DEPTH, 2, BATCH, N_MOD * D_MODEL)
    rope = _rope_table()
    gm = _block_diag(jnp.full((LANE // HEAD_DIM, HEAD_DIM, HEAD_DIM), 1.0 / HEAD_DIM, F32), LANE // HEAD_DIM)[0].astype(BF16)

    h, u_tb, gate_tb, q, k, v, wu, wd = _in_call(
        (x, ctx), mod[0], norm_g, _w_in(ev_w_in[0], 2 * MIX_W), rope,
        ga_qn, ga_kn, gm, ffn_up, ffn_down, 0, n_tb=2, first_layer=True)
    w_gates = _block_diag(jnp.stack([rg_wa[0], rg_wx[0]]).astype(BF16).reshape(-1, HEAD_DIM, HEAD_DIM), 4)
    rg = (rg_conv_w, rg_conv_b, w_gates, rg_ba, rg_bx, rg_lam)
    b_out, hb, uc = _gattn_call(q, k, v, u_tb, *rg)
    a_out = _rg_fwd_call(uc, gate_tb, hb, *rg[2:])
    h = _post_call(h, a_out, b_out, _w_out(ev_w_out[0]), mod[0], norm_g, wu, ffn_conv_w, ffn_conv_b, wd, 0,
                   n_tiles=L_TOT // POST_T, out_batch_major=False)

    u_tb, q, k, v, wu, wd = _in_call(
        (h,), mod[1], norm_g, _w_in(od_w_in[0], MIX_W), rope,
        wa_qn, wa_kn, gm, ffn_up, ffn_down, 1, n_tb=1, first_layer=False)
    n_state = S5_GROUPS * S5_STATE
    bt_re = s5_b_re[0].transpose(0, 3, 1, 2).reshape(2, S5_GROUP, n_state)
    bt_im = s5_b_im[0].transpose(0, 3, 1, 2).reshape(2, S5_GROUP, n_state)
    log_step = jnp.repeat(s5_log_step[0], S5_STATE, axis=-1).reshape(2, 1, n_state)
    abr, abi, bbr, bbi = _s5_disc_call(s5_lam_re[0].reshape(2, 1, n_state), s5_lam_im[0].reshape(2, 1, n_state),
                                       log_step, bt_re, bt_im)
    gpb = S5_GROUPS // S5_NBLK
    eye = jnp.eye(gpb, dtype=BF16)
    bb = jnp.stack([bbr, bbi]).astype(BF16).reshape(2, 2, S5_GROUP, S5_NBLK, gpb, S5_STATE)
    w_drive = jnp.einsum("rdhbgp,gk->dbghrkp", bb, eye).reshape(
        2, S5_NBLK, gpb * S5_GROUP, 2 * S5_BLK_STATE)
    cc = jnp.stack([s5_c_re[0], -s5_c_im[0]]).astype(BF16).reshape(
        2, 2, S5_NBLK, gpb, S5_GROUP, S5_STATE)
    w_read = jnp.einsum("rdbghp,gk->dbrgpkh", cc, eye).reshape(
        2, S5_NBLK, 2 * S5_BLK_STATE, gpb * S5_GROUP)

    def state_rows(re, im):
        row = jnp.concatenate([re.reshape(2, S5_NBLK, S5_BLK_STATE), im.reshape(2, S5_NBLK, S5_BLK_STATE)],
                              axis=2).reshape(2, 1, S5_SW)
        return jnp.broadcast_to(row, (2, BATCH, S5_SW))

    s5 = (w_drive, state_rows(abr, abr), state_rows(-abi, abi), w_read)
    yb = _s5_call(u_tb, None, *s5, None, None, None, reverse=True)
    c_out = _s5_call(u_tb, yb, *s5, s5_d, s5_glu_w[0].astype(BF16), s5_glu_b, reverse=False)
    d_out = _wattn_call(q, k, v, wa_sink)
    return _post_call(h, c_out, d_out, _w_out(od_w_out[0]), mod[1], norm_g, wu, ffn_conv_w, ffn_conv_b, wd, 1,
                      n_tiles=SEQ // POST_T, out_batch_major=True)
```

```python
import functools
import math

import jax
import jax.numpy as jnp
from jax import lax
from jax.experimental import pallas as pl
from jax.experimental.pallas import tpu as pltpu

F32 = jnp.float32
BF16 = jnp.bfloat16

D_MODEL = 1024
BATCH = 8
SEQ = 2048
CTX_LEN = 256
L_TOT = SEQ + CTX_LEN
DEPTH = 2
GRID_W = 64
HEAD_DIM = 64
ROPE_BASE = 10000.0
NORM_EPS = 1e-6
WINDOW = 128
N_MOD = 6
MIX_W = 512
N_HEADS = 8
N_KV = 2
KV_W = N_KV * HEAD_DIM
V_W = 2 * KV_W
QKV_W = MIX_W + KV_W + KV_W
Q_SCALE = HEAD_DIM ** -0.5 * math.log2(math.e)
RG_CONV = 4
RG_C = 8.0
S5_GROUP = 16
S5_GROUPS = 32
S5_STATE = 64
S5_NBLK = 4
S5_BLK_STATE = (S5_GROUPS // S5_NBLK) * S5_STATE
FFN_HIDDEN = 2816
FFN_CONV = 3

LANE = 128
BF16_ROWS = 16
VMEM_LIMIT = 56 * 1024 * 1024

IN_T = 128
RG_T = 128
S5_T = 64
POST_T = 128
TQ_G = 256
G_NB = 4
W_NB = 8
FFN_CH = 256
HEAD_PERM = (0, 4, 1, 5, 2, 6, 3, 7)


def _cparams(sem):
    return pltpu.CompilerParams(dimension_semantics=sem, vmem_limit_bytes=VMEM_LIMIT)


def _dot(a, b):
    return jnp.dot(a, b, preferred_element_type=F32)


def _gelu_tanh(x):
    return x * (0.5 * (1.0 + jnp.tanh(math.sqrt(2.0 / math.pi) * (x + 0.044715 * (x * x * x)))))


def _rms_unit(x):
    ms = jnp.mean(x * x, axis=-1, keepdims=True)
    return x * lax.rsqrt(ms + NORM_EPS)


def _adaln(x, gain, scale, shift):
    y = _per_sample(_rms_unit(x), gain * (1.0 + scale), jnp.multiply)
    return _per_sample(y, shift, jnp.add)


def _to_time_major(x):
    b, t, c = x.shape
    return jnp.swapaxes(x, 0, 1).reshape(t * b, c)


def _to_batch_major(x):
    r, c = x.shape
    return jnp.swapaxes(x.reshape(r // BATCH, BATCH, c), 0, 1)


def _per_sample(x, v, op):
    r, c = x.shape
    return op(x.reshape(r // BATCH, BATCH, c), v[None]).reshape(r, c)


def _full(shape, single=False):
    kw = dict(pipeline_mode=pl.Buffered(1)) if single else {}
    return pl.BlockSpec(shape, lambda *_: (0,) * len(shape), **kw)


MOD_TN = 3072


def _mod_kernel(c_ref, wa_ref, wb_ref, b_ref, o_ref):
    c = c_ref[...]
    a = c * jax.nn.sigmoid(c)
    ah = a.astype(BF16)
    al = (a - ah.astype(F32)).astype(BF16)
    for i, w_ref in enumerate((wa_ref, wb_ref)):
        w = w_ref[0]
        wh = w.astype(BF16)
        wl = (w - wh.astype(F32)).astype(BF16)
        cols = slice(i * (MOD_TN // 2), (i + 1) * (MOD_TN // 2))
        o_ref[0, :, cols] = _dot(ah, wh) + _dot(ah, wl) + _dot(al, wh) + b_ref[0, :, cols]


def _mod_call(cvec, mod_w, mod_b):
    n = N_MOD * D_MODEL
    return pl.pallas_call(
        _mod_kernel,
        out_shape=jax.ShapeDtypeStruct((DEPTH, 2 * BATCH, n), F32),
        grid=(DEPTH, n // MOD_TN),
        in_specs=[
            pl.BlockSpec((2 * BATCH, D_MODEL), lambda l, k: (0, 0)),
            pl.BlockSpec((1, D_MODEL, MOD_TN // 2), lambda l, k: (l, 0, 2 * k)),
            pl.BlockSpec((1, D_MODEL, MOD_TN // 2), lambda l, k: (l, 0, 2 * k + 1)),
            pl.BlockSpec((1, 1, MOD_TN), lambda l, k: (l, 0, k)),
        ],
        out_specs=pl.BlockSpec((1, 2 * BATCH, MOD_TN), lambda l, k: (l, 0, k)),
        compiler_params=_cparams(("parallel", "parallel")),
        name="mod",
    )(cvec, mod_w, mod_w, mod_b.reshape(DEPTH, 1, n))


def _mod_spec(col, n_latent_tiles):
    return pl.BlockSpec((1, BATCH, D_MODEL), lambda j: (jnp.where(j < n_latent_tiles, 0, 1), 0, col))


IN_ROWS = IN_T * BATCH
IN_NLAT = SEQ // IN_T
IN_N = L_TOT // IN_T
IN_SUB = 2
CAST_STEPS = 16
assert CAST_STEPS <= IN_N
IN_SUB_T = IN_T // IN_SUB
IN_SUB_ROWS = IN_SUB_T * BATCH


def _in_kernel(fu_ref, fd_ref, *refs, n_tb, first_layer, layer):
    fu_out, fd_out = refs[-2:]
    refs = refs[:-2]

    @pl.when(pl.program_id(0) < CAST_STEPS)
    def _():
        fu_out[...] = fu_ref[0].astype(BF16)
        fd_out[...] = fd_ref[0].astype(BF16)

    if first_layer:
        x_ref, c_ref, sh_ref, sc_ref, g_ref, w_ref, rope_ref, gq_ref, gk_ref, gm_ref = refs[:10]
        h_out = refs[10]
        outs = refs[11:]
        is_latent = pl.program_id(0) < IN_NLAT
    else:
        h_ref, sh_ref, sc_ref, g_ref, w_ref, rope_ref, gq_ref, gk_ref, gm_ref = refs[:9]
        outs = refs[9:]
    q_ref, k_ref, v_ref = outs[n_tb:]
    off = n_tb * MIX_W
    gm = gm_ref[...]
    gq = jnp.concatenate([gq_ref[...]] * (LANE // HEAD_DIM), axis=1)
    gk = jnp.concatenate([gk_ref[...]] * (LANE // HEAD_DIM), axis=1)
    for s in range(IN_SUB):
        ts = slice(s * IN_SUB_T, (s + 1) * IN_SUB_T)
        rs = slice(s * IN_SUB_ROWS, (s + 1) * IN_SUB_ROWS)
        if first_layer:
            h = _to_time_major(jnp.where(is_latent, x_ref[:, ts, :], c_ref[:, ts, :]))
            h_out[rs, :] = h
        else:
            h = h_ref[rs, :]
        xn = _adaln(h, g_ref[layer, 0:1, :], sc_ref[0], sh_ref[0])
        proj = _dot(xn.astype(BF16), w_ref[...])
        for i in range(n_tb):
            outs[i][rs, :] = proj[:, i * MIX_W:(i + 1) * MIX_W]
        qkv = _to_batch_major(proj[:, off:off + QKV_W])
        cos = rope_ref[ts, 0:LANE][None]
        sin_up = rope_ref[ts, LANE:2 * LANE][None]
        sin_dn = rope_ref[ts, 2 * LANE:3 * LANE][None]
        for p in range(5):
            blk = qkv[:, :, p * LANE:(p + 1) * LANE].reshape(IN_SUB_ROWS, LANE)
            sq = blk * blk
            hi = sq.astype(BF16)
            lo = (sq - hi.astype(F32)).astype(BF16)
            ms = _dot(hi, gm) + _dot(lo, gm)
            g = gq if p < 4 else gk
            bn = blk * lax.rsqrt(ms + NORM_EPS) * g
            up = pltpu.roll(bn, LANE - HEAD_DIM // 4, 1).reshape(BATCH, IN_SUB_T, LANE)
            dn = pltpu.roll(bn, HEAD_DIM // 4, 1).reshape(BATCH, IN_SUB_T, LANE)
            ro = bn.reshape(BATCH, IN_SUB_T, LANE) * cos + up * sin_up + dn * sin_dn
            if p < 4:
                q_ref[:, ts, p * LANE:(p + 1) * LANE] = (ro * Q_SCALE).astype(BF16)
            else:
                k_ref[:, ts, :] = ro.astype(BF16)
        v = qkv[:, :, 5 * LANE:6 * LANE].astype(BF16)
        v_ref[:, ts, :] = jnp.concatenate([v, jnp.ones((BATCH, IN_SUB_T, LANE), BF16)], axis=2)


def _in_call(h_args, modt, g, w, rope, gq, gk, gm, ffn_up, ffn_down, layer, n_tb, first_layer):
    n = w.shape[1]
    tb = lambda width: pl.BlockSpec((IN_ROWS, width), lambda j: (j, 0))
    bm = lambda width: pl.BlockSpec((BATCH, IN_T, width), lambda j: (0, j, 0))
    up_rows = D_MODEL // CAST_STEPS
    down_rows = FFN_HIDDEN // CAST_STEPS
    slab = lambda j: jnp.minimum(j, CAST_STEPS - 1)
    if first_layer:
        h_specs = [
            pl.BlockSpec((BATCH, IN_T, D_MODEL), lambda j: (0, jnp.minimum(j, IN_NLAT - 1), 0)),
            pl.BlockSpec((BATCH, IN_T, D_MODEL), lambda j: (0, jnp.maximum(j - IN_NLAT, 0), 0)),
        ]
        extra_shape = [jax.ShapeDtypeStruct((L_TOT * BATCH, D_MODEL), F32)]
        extra_spec = [tb(D_MODEL)]
    else:
        h_specs = [tb(D_MODEL)]
        extra_shape, extra_spec = [], []
    return pl.pallas_call(
        functools.partial(_in_kernel, n_tb=n_tb, first_layer=first_layer, layer=layer),
        out_shape=extra_shape + [jax.ShapeDtypeStruct((L_TOT * BATCH, MIX_W), F32)] * n_tb + [
            jax.ShapeDtypeStruct((BATCH, L_TOT, MIX_W), BF16),
            jax.ShapeDtypeStruct((BATCH, L_TOT, KV_W), BF16),
            jax.ShapeDtypeStruct((BATCH, L_TOT, V_W), BF16),
            jax.ShapeDtypeStruct((D_MODEL, 2 * FFN_HIDDEN), BF16),
            jax.ShapeDtypeStruct((FFN_HIDDEN, D_MODEL), BF16),
        ],
        grid=(IN_N,),
        in_specs=[
            pl.BlockSpec((1, up_rows, 2 * FFN_HIDDEN), lambda j: (layer, slab(j), 0)),
            pl.BlockSpec((1, down_rows, D_MODEL), lambda j: (layer, slab(j), 0)),
        ] + h_specs + [
            _mod_spec(0, IN_NLAT), _mod_spec(1, IN_NLAT),
            _full(g.shape),
            _full((D_MODEL, n), single=True),
            pl.BlockSpec((IN_T, 3 * LANE), lambda j: (j, 0)),
            _full(gq.shape), _full(gk.shape), _full((LANE, LANE)),
        ],
        out_specs=extra_spec + [tb(MIX_W)] * n_tb + [bm(MIX_W), bm(KV_W), bm(V_W)] + [
            pl.BlockSpec((up_rows, 2 * FFN_HIDDEN), lambda j: (slab(j), 0)),
            pl.BlockSpec((down_rows, D_MODEL), lambda j: (slab(j), 0)),
        ],
        compiler_params=_cparams(("arbitrary",)),
        name="in_proj",
    )(ffn_up, ffn_down, *h_args, modt, modt, g, w, rope, gq, gk, gm)


def _chunk_of(i, reverse, n_latent, n_chunks):
    if reverse:
        return n_chunks - 1 - i
    return jnp.where(i < n_chunks - n_latent, i + n_latent, i - (n_chunks - n_latent))


RG_ROWS = RG_T * BATCH
RG_NC = L_TOT // RG_T
RG_NLAT = SEQ // RG_T


def _softplus(z):
    return jnp.maximum(z, 0.0) + jnp.log1p(jnp.exp(-jnp.abs(z)))


def _rg_kernel(*refs, reverse):
    i = pl.program_id(0)

    @pl.when(i == 0)
    def _():
        refs[-1][...] = jnp.zeros_like(refs[-1])

    _rg_chunk(i, refs, reverse)


def _rg_chunk(i, refs, reverse):
    if reverse:
        (u_ref, up_ref, un_ref, cw_ref, cb_ref, w_ref, ba_ref, bx_ref, lam_ref,
         out_ref, uc_out, xbuf, abuf, bbuf, hst) = refs
    else:
        uc_ref, gate_ref, hb_ref, w_ref, ba_ref, bx_ref, lam_ref, out_ref, abuf, bbuf, hst = refs
    d = 1 if reverse else 0

    if reverse:
        c = _chunk_of(i, reverse, RG_NLAT, RG_NC)
        prev_zero = jnp.logical_or(c == 0, c == RG_NLAT)
        next_zero = jnp.logical_or(c == RG_NLAT - 1, c == RG_NC - 1)
        xbuf[0:BATCH, :] = up_ref[...] * jnp.where(prev_zero, 0.0, 1.0)
        xbuf[BATCH:BATCH + RG_ROWS, :] = u_ref[...]
        xbuf[BATCH + RG_ROWS:3 * BATCH + RG_ROWS, :] = un_ref[...] * jnp.where(next_zero, 0.0, 1.0)
        uc = cb_ref[...]
        for k in range(RG_CONV):
            uc = uc + xbuf[k * BATCH:k * BATCH + RG_ROWS, :] * cw_ref[0, k:k + 1, :]
        uc_out[...] = uc
    else:
        uc = uc_ref[...]

    ub = uc.astype(BF16)
    half = MIX_W // 2

    def gate(which, b_ref):
        w0 = which * 4 + d * 2
        z = jnp.concatenate([_dot(ub[:, :half], w_ref[w0]), _dot(ub[:, half:], w_ref[w0 + 1])], axis=1)
        return jax.nn.sigmoid(z + b_ref[0, d:d + 1, :])

    r = gate(0, ba_ref)
    ig = gate(1, bx_ref)
    log_a = (-RG_C) * r * _softplus(-lam_ref[0, d:d + 1, :])
    a = jnp.exp(log_a)
    abuf[...] = a
    bbuf[...] = jnp.sqrt(-jnp.tanh(log_a) * (a * a + 1.0)) * (ig * uc)

    h = hst[...]
    for s in range(RG_T):
        t = (RG_T - 1 - s) if reverse else s
        rows = slice(t * BATCH, (t + 1) * BATCH)
        h = abuf[rows, :] * h + bbuf[rows, :]
        bbuf[rows, :] = h
    hst[...] = h

    if reverse:
        out_ref[...] = bbuf[...]
    else:
        y = bbuf[...] + hb_ref[...]
        out_ref[...] = (y * _gelu_tanh(gate_ref[...])).astype(BF16)


def _rg_fwd_call(uc, gate, hb, w_gates, ba, bx, lam):
    main = pl.BlockSpec((RG_ROWS, MIX_W), lambda i: (_chunk_of(i, False, RG_NLAT, RG_NC), 0))
    gates = [w_gates, ba, bx, lam]
    return pl.pallas_call(
        functools.partial(_rg_kernel, reverse=False),
        out_shape=jax.ShapeDtypeStruct((L_TOT * BATCH, MIX_W), BF16),
        grid=(RG_NC,),
        in_specs=[main, main, main] + [_full(p.shape) for p in gates],
        out_specs=main,
        scratch_shapes=[pltpu.VMEM((RG_ROWS, MIX_W), F32), pltpu.VMEM((RG_ROWS, MIX_W), F32),
                        pltpu.VMEM((BATCH, MIX_W), F32)],
        compiler_params=_cparams(("arbitrary",)),
        name="rglru_fwd",
    )(uc, gate, hb, *gates)


def _head_rows(q_ref, tq, b):
    lane = lax.broadcasted_iota(jnp.int32, (tq, LANE), 1)
    low = lane < HEAD_DIM
    zero = jnp.zeros((tq, LANE), BF16)
    parts = []
    for p in range(MIX_W // LANE):
        qb = q_ref[b, :, p * LANE:(p + 1) * LANE]
        parts.append(jnp.where(low, qb, zero))
        parts.append(jnp.where(low, zero, qb))
    return parts, low


def _merge_heads(outs, low):
    return jnp.concatenate([jnp.where(low, outs[2 * p], outs[2 * p + 1]) for p in range(MIX_W // LANE)], axis=1)


def _qk(q, k):
    return lax.dot_general(q, k, (((1,), (1,)), ((), ())), preferred_element_type=F32)


def _gattn_kernel(q_ref, k_ref, v_ref, *rest):
    rg_in, o_ref, rg_rest = rest[:9], rest[9], rest[10:]
    i = pl.program_id(0)
    j = lax.rem(i, G_TILES)

    @pl.when(i == 0)
    def _():
        rg_rest[-1][...] = jnp.zeros_like(rg_rest[-1])

    def attend(k0, nk):
        _rg_chunk(i, rg_in + rg_rest, True)
        for b in range(G_NB):
            heads, low = _head_rows(q_ref, TQ_G, b)
            outs = []
            for q in heads:
                s = _qk(q, k_ref[b, k0:k0 + nk, :])
                m = jnp.max(s, axis=-1, keepdims=True)
                p = jnp.exp2(s - m).astype(BF16)
                acc = _dot(p, v_ref[b, k0:k0 + nk, :])
                outs.append(acc[:, :KV_W] / acc[:, KV_W:])
            o_ref[b] = _merge_heads(outs, low).astype(BF16)

    @pl.when(j < SEQ // TQ_G)
    def _():
        attend(0, L_TOT)

    @pl.when(j >= SEQ // TQ_G)
    def _():
        attend(SEQ, CTX_LEN)


G_TILES = L_TOT // TQ_G
assert (BATCH // G_NB) * G_TILES == RG_NC


def _gattn_call(q, k, v, u, cw, cb, w_gates, ba, bx, lam):
    def cidx(i):
        return _chunk_of(i, True, RG_NLAT, RG_NC)

    main = pl.BlockSpec((RG_ROWS, MIX_W), lambda i: (cidx(i), 0))
    prev = pl.BlockSpec((BATCH, MIX_W), lambda i: (jnp.maximum(cidx(i) * RG_T - 1, 0), 0))
    n_next = L_TOT // 2
    nxt = pl.BlockSpec((2 * BATCH, MIX_W), lambda i: (jnp.minimum((cidx(i) + 1) * (RG_T // 2), n_next - 1), 0))
    params = [cw, cb, w_gates, ba, bx, lam]
    rows = jax.ShapeDtypeStruct((L_TOT * BATCH, MIX_W), F32)
    attn_spec = pl.BlockSpec((G_NB, TQ_G, MIX_W), lambda i: (i // G_TILES, lax.rem(i, G_TILES), 0))
    return pl.pallas_call(
        _gattn_kernel,
        out_shape=[jax.ShapeDtypeStruct((BATCH, L_TOT, MIX_W), BF16), rows, rows],
        grid=(RG_NC,),
        in_specs=[
            attn_spec,
            pl.BlockSpec((G_NB, L_TOT, KV_W), lambda i: (i // G_TILES, 0, 0), pipeline_mode=pl.Buffered(1)),
            pl.BlockSpec((G_NB, L_TOT, V_W), lambda i: (i // G_TILES, 0, 0), pipeline_mode=pl.Buffered(1)),
            main, prev, nxt,
        ] + [_full(p.shape) for p in params],
        out_specs=[attn_spec, main, main],
        scratch_shapes=[
            pltpu.VMEM((RG_ROWS + 3 * BATCH, MIX_W), F32),
            pltpu.VMEM((RG_ROWS, MIX_W), F32),
            pltpu.VMEM((RG_ROWS, MIX_W), F32),
            pltpu.VMEM((BATCH, MIX_W), F32),
        ],
        compiler_params=_cparams(("arbitrary",)),
        name="global_attn_rglru_bwd",
    )(q, k, v, u, u, u, *params)


TQ_W = 2 * WINDOW
N_WBLK = SEQ // WINDOW
N_WSTEP = SEQ // TQ_W


def _wattn_kernel(q_ref, k_ref, v_ref, sink_ref, o_ref):
    i = pl.program_id(1)
    prev_start = pl.multiple_of(jnp.maximum(2 * i - 1, 0) * WINDOW, WINDOW)
    mid_start = pl.multiple_of(i * TQ_W, TQ_W)
    next_start = pl.multiple_of(jnp.minimum(2 * i + 2, N_WBLK - 1) * WINDOW, WINDOW)

    def rows(ref, b):
        return jnp.concatenate([ref[b, SEQ:L_TOT, :], ref[b, pl.ds(prev_start, WINDOW), :],
                                ref[b, pl.ds(mid_start, TQ_W), :], ref[b, pl.ds(next_start, WINDOW), :]], axis=0)

    nk = CTX_LEN + 2 * WINDOW + TQ_W
    r = lax.broadcasted_iota(jnp.int32, (TQ_W, nk), 0)
    col = lax.broadcasted_iota(jnp.int32, (TQ_W, nk), 1)
    c = col - CTX_LEN
    ninf = -jnp.inf
    pen_prev = jnp.where(i > 0, 0.0, ninf)
    pen_next = jnp.where(i < N_WSTEP - 1, 0.0, ninf)
    edge = jnp.where(c < WINDOW, pen_prev, jnp.where(c >= WINDOW + TQ_W, pen_next, 0.0))
    band = jnp.where(c >= r, jnp.where(c <= r + 2 * WINDOW, edge, ninf), ninf)
    bias = jnp.where(col < CTX_LEN, 0.0, band)
    for b in range(W_NB):
        heads, low = _head_rows(q_ref, TQ_W, b)
        kk = rows(k_ref, b)
        vv = rows(v_ref, b)
        outs = []
        for hd, q in enumerate(heads):
            s = _qk(q, kk) + bias
            sink = sink_ref[0, HEAD_PERM[hd]] * math.log2(math.e)
            m = jnp.maximum(jnp.max(s, axis=-1, keepdims=True), sink)
            p = jnp.exp2(s - m).astype(BF16)
            acc = _dot(p, vv)
            outs.append(acc[:, :KV_W] / (acc[:, KV_W:] + jnp.exp2(sink - m)))
        o_ref[b] = _merge_heads(outs, low).astype(BF16)


def _wattn_call(q, k, v, sink):
    return pl.pallas_call(
        _wattn_kernel,
        out_shape=jax.ShapeDtypeStruct((BATCH, SEQ, MIX_W), BF16),
        grid=(BATCH // W_NB, N_WSTEP),
        in_specs=[
            pl.BlockSpec((W_NB, TQ_W, MIX_W), lambda b, i: (b, i, 0)),
            pl.BlockSpec((W_NB, L_TOT, KV_W), lambda b, i: (b, 0, 0), pipeline_mode=pl.Buffered(1)),
            pl.BlockSpec((W_NB, L_TOT, V_W), lambda b, i: (b, 0, 0), pipeline_mode=pl.Buffered(1)),
            pl.BlockSpec(memory_space=pltpu.SMEM),
        ],
        out_specs=pl.BlockSpec((W_NB, TQ_W, MIX_W), lambda b, i: (b, i, 0)),
        compiler_params=_cparams(("parallel", "arbitrary")),
        name="window_attn",
    )(q, k, v, sink)


def _s5_disc_kernel(lr_ref, li_ref, ls_ref, br_ref, bi_ref, ar_ref, ai_ref, bbr_ref, bbi_ref):
    lr = lr_ref[0]
    li = li_ref[0]
    dt = jnp.exp(ls_ref[0])
    mag = jnp.exp(lr * dt)
    ang = li * dt
    abr = mag * jnp.cos(ang)
    abi = mag * jnp.sin(ang)
    den = lr * lr + li * li
    nr = abr - 1.0
    kr = (nr * lr + abi * li) / den
    ki = (abi * lr - nr * li) / den
    br = br_ref[0]
    bi = bi_ref[0]
    ar_ref[0] = abr
    ai_ref[0] = abi
    bbr_ref[0] = kr * br - ki * bi
    bbi_ref[0] = kr * bi + ki * br


def _s5_disc_call(lam_re, lam_im, log_step, bt_re, bt_im):
    n = S5_GROUPS * S5_STATE
    row = pl.BlockSpec((1, 1, n), lambda d: (d, 0, 0))
    mat = pl.BlockSpec((1, S5_GROUP, n), lambda d: (d, 0, 0))
    return pl.pallas_call(
        _s5_disc_kernel,
        out_shape=[jax.ShapeDtypeStruct((2, 1, n), F32)] * 2 + [jax.ShapeDtypeStruct((2, S5_GROUP, n), F32)] * 2,
        grid=(2,),
        in_specs=[row, row, row, mat, mat],
        out_specs=[row, row, mat, mat],
        compiler_params=_cparams(("parallel",)),
        name="s5_discretize",
    )(lam_re, lam_im, log_step, bt_re, bt_im)


S5_ROWS = S5_T * BATCH
S5_NC = L_TOT // S5_T
S5_NLAT = SEQ // S5_T
S5_SW = 2 * S5_BLK_STATE * S5_NBLK


def _s5_kernel(*refs, reverse):
    if reverse:
        u2_ref, un_ref, wd_ref, ar_ref, ai_ref, wr_ref, out_ref, buf0, buf1, xst = refs
    else:
        (u2_ref, un_ref, yb2_ref, wd_ref, ar_ref, ai_ref, wr_ref, dsk_ref, gw_ref, gb_ref,
         out_ref, buf0, buf1, xst) = refs
    nb = 2 * S5_BLK_STATE
    lo, hi = slice(0, S5_ROWS), slice(S5_ROWS, 2 * S5_ROWS)
    first, second = (hi, lo) if reverse else (lo, hi)

    def drive(u, buf):
        ub = u.astype(BF16)
        for k in range(S5_NBLK):
            buf[:, k * nb:(k + 1) * nb] = _dot(ub[:, k * LANE:(k + 1) * LANE], wd_ref[0, k])

    def scan(buf, x):
        for s in range(S5_T):
            t = (S5_T - 1 - s) if reverse else s
            rows = slice(t * BATCH, (t + 1) * BATCH)
            halves = []
            for k in range(S5_NBLK):
                halves.append(x[:, k * nb + S5_BLK_STATE:(k + 1) * nb])
                halves.append(x[:, k * nb:k * nb + S5_BLK_STATE])
            xsw = jnp.concatenate(halves, axis=1)
            x = ar_ref[0] * x + ai_ref[0] * xsw + buf[rows, :]
            buf[rows, :] = x
        return x

    def readout(buf, rows):
        y = jnp.concatenate(
            [_dot(buf[:, k * nb:(k + 1) * nb].astype(BF16), wr_ref[0, k]) for k in range(S5_NBLK)], axis=1)
        if reverse:
            out_ref[rows, :] = y
        else:
            y = u2_ref[rows, :] * dsk_ref[...] + y + yb2_ref[rows, :]
            z = _gelu_tanh(y)
            out_ref[rows, :] = (z * jax.nn.sigmoid(_dot(z.astype(BF16), gw_ref[...]) + gb_ref[...])).astype(BF16)

    @pl.when(pl.program_id(0) == 0)
    def _():
        xst[...] = jnp.zeros_like(xst)
        drive(u2_ref[first, :], buf0)

    x = xst[...]
    drive(u2_ref[second, :], buf1)
    x = scan(buf0, x)
    readout(buf0, first)
    drive(un_ref[...], buf0)
    x = scan(buf1, x)
    readout(buf1, second)
    xst[...] = x


def _s5_call(u, yb, wd, ar, ai, wr, dsk, gw, gb, reverse):
    def chunk(p):
        return _chunk_of(p, reverse, S5_NLAT, S5_NC)

    pair = pl.BlockSpec((2 * S5_ROWS, MIX_W), lambda k: (chunk(2 * k) // 2, 0))
    nxt = pl.BlockSpec((S5_ROWS, MIX_W), lambda k: (chunk(jnp.minimum(2 * k + 2, S5_NC - 1)), 0))
    d = 1 if reverse else 0
    dir_specs = [pl.BlockSpec((1,) + a.shape[1:], lambda k, n=a.ndim: (d,) + (0,) * (n - 1)) for a in (wd, ar, ai, wr)]
    if reverse:
        args = [u, u, wd, ar, ai, wr]
        specs = [pair, nxt] + dir_specs
        out_dtype = F32
    else:
        args = [u, u, yb, wd, ar, ai, wr, dsk, gw, gb]
        specs = [pair, nxt, pair] + dir_specs + [_full(a.shape) for a in (dsk, gw, gb)]
        out_dtype = BF16
    return pl.pallas_call(
        functools.partial(_s5_kernel, reverse=reverse),
        out_shape=jax.ShapeDtypeStruct((L_TOT * BATCH, MIX_W), out_dtype),
        grid=(S5_NC // 2,),
        in_specs=specs,
        out_specs=pair,
        scratch_shapes=[
            pltpu.VMEM((S5_ROWS, S5_SW), F32),
            pltpu.VMEM((S5_ROWS, S5_SW), F32),
            pltpu.VMEM((BATCH, S5_SW), F32),
        ],
        compiler_params=_cparams(("arbitrary",)),
        name="s5_bwd" if reverse else "s5_fwd",
    )(*args)


POST_ROWS = POST_T * BATCH
POST_HALO_T = BF16_ROWS // BATCH
POST_HALO = POST_HALO_T * BATCH
POST_NLAT = SEQ // POST_T
FFN_NCH = FFN_HIDDEN // FFN_CH


def _post_kernel(h_ref, hp_ref, hn_ref, a_ref, ap_ref, an_ref, b_ref, bp_ref, bn_ref, wo_ref,
                 g2_ref, sh_ref, sc_ref, g5_ref, g_ref, wu_ref, cw_ref, cb_ref, wd_ref, o_ref, *,
                 n_tiles, out_batch_major, layer):
    j = pl.program_id(0)
    first = jnp.logical_or(j == 0, j == POST_NLAT)
    last = jnp.logical_or(j == POST_NLAT - 1, j == n_tiles - 1)
    gate2 = g2_ref[0]

    def mix(h, a, b_tm):
        m = _dot(a, wo_ref[0:MIX_W, :]) + _dot(b_tm.astype(BF16), wo_ref[MIX_W:2 * MIX_W, :])
        return h + _per_sample(m, gate2, jnp.multiply)

    def norm(x):
        return _adaln(x, g_ref[layer, 1:2, :], sc_ref[0], sh_ref[0])

    def halo_b(ref, t0):
        x = jnp.swapaxes(ref[...].astype(F32), 0, 1)[t0:t0 + POST_HALO_T]
        return x.reshape(POST_HALO, MIX_W)

    h_mid = mix(h_ref[...], a_ref[...], _to_time_major(b_ref[...].astype(F32)))
    h_prev = mix(hp_ref[...], ap_ref[...], halo_b(bp_ref, BF16_ROWS - POST_HALO_T))
    h_next = mix(hn_ref[...], an_ref[...], halo_b(bn_ref, 0))
    xn = jnp.concatenate([
        (norm(h_prev) * jnp.where(first, 0.0, 1.0)).astype(BF16),
        norm(h_mid).astype(BF16),
        (norm(h_next) * jnp.where(last, 0.0, 1.0)).astype(BF16)], axis=0)

    acts = []
    for c in range(FFN_NCH):
        cv = slice(c * FFN_CH, (c + 1) * FFN_CH)
        cg = slice(FFN_HIDDEN + c * FFN_CH, FFN_HIDDEN + (c + 1) * FFN_CH)

        def conv(cols):
            hid = _dot(xn, wu_ref[:, cols])
            out = cb_ref[layer:layer + 1, cols]
            for k in range(FFN_CONV):
                r0 = POST_HALO + (k - 1) * BATCH
                out = out + hid[r0:r0 + POST_ROWS] * cw_ref[0, k:k + 1, cols]
            return out

        val = conv(cv)
        gate = conv(cg)
        acts.append((val * (gate * jax.nn.sigmoid(gate))).astype(BF16))
    ffn = _dot(jnp.concatenate(acts, axis=1), wd_ref[...])
    out = h_mid + _per_sample(ffn, g5_ref[0], jnp.multiply)
    if out_batch_major:
        o_ref[...] = _to_batch_major(out)
    else:
        o_ref[...] = out


def _post_call(h, a_tb, b_out, w_out, modt, g, wu, cw, cb, wd, layer, n_tiles, out_batch_major):
    per = POST_ROWS // POST_HALO
    n_halo = h.shape[0] // POST_HALO
    per_b = POST_T // BF16_ROWS
    n_halo_b = b_out.shape[1] // BF16_ROWS

    def main(width):
        return pl.BlockSpec((POST_ROWS, width), lambda j: (j, 0))

    def prev(width):
        return pl.BlockSpec((POST_HALO, width), lambda j: (jnp.maximum(j * per - 1, 0), 0))

    def nxt(width):
        return pl.BlockSpec((POST_HALO, width), lambda j: (jnp.minimum((j + 1) * per, n_halo - 1), 0))

    if out_batch_major:
        out_shape = jax.ShapeDtypeStruct((BATCH, n_tiles * POST_T, D_MODEL), F32)
        out_spec = pl.BlockSpec((BATCH, POST_T, D_MODEL), lambda j: (0, j, 0))
    else:
        out_shape = jax.ShapeDtypeStruct((n_tiles * POST_ROWS, D_MODEL), F32)
        out_spec = main(D_MODEL)
    return pl.pallas_call(
        functools.partial(_post_kernel, n_tiles=n_tiles, out_batch_major=out_batch_major, layer=layer),
        out_shape=out_shape,
        grid=(n_tiles,),
        in_specs=[
            main(D_MODEL), prev(D_MODEL), nxt(D_MODEL),
            main(MIX_W), prev(MIX_W), nxt(MIX_W),
            pl.BlockSpec((BATCH, POST_T, MIX_W), lambda j: (0, j, 0)),
            pl.BlockSpec((BATCH, BF16_ROWS, MIX_W), lambda j: (0, jnp.maximum(j * per_b - 1, 0), 0)),
            pl.BlockSpec((BATCH, BF16_ROWS, MIX_W), lambda j: (0, jnp.minimum((j + 1) * per_b, n_halo_b - 1), 0)),
            _full((2 * MIX_W, D_MODEL), single=True),
            _mod_spec(2, POST_NLAT), _mod_spec(3, POST_NLAT), _mod_spec(4, POST_NLAT), _mod_spec(5, POST_NLAT),
            _full(g.shape),
            _full((D_MODEL, 2 * FFN_HIDDEN), single=True),
            pl.BlockSpec((1, FFN_CONV, 2 * FFN_HIDDEN), lambda j: (layer, 0, 0)),
            _full(cb.shape),
            _full((FFN_HIDDEN, D_MODEL), single=True),
        ],
        out_specs=out_spec,
        compiler_params=_cparams(("parallel",)),
        name="post_ffn",
    )(h, h, h, a_tb, a_tb, a_tb, b_out, b_out, b_out, w_out, modt, modt, modt, modt, g, wu, cw, cb, wd)


def _rope_table():
    rows = SEQ // GRID_W
    row = jnp.repeat(jnp.arange(rows, dtype=F32), GRID_W)
    col = jnp.tile(jnp.arange(GRID_W, dtype=F32), rows)
    quarter = HEAD_DIM // 4
    inv_freq = ROPE_BASE ** (-jnp.arange(quarter, dtype=F32) / quarter)
    ang = jnp.stack([row[:, None] * inv_freq, col[:, None] * inv_freq], axis=1)
    cos = jnp.cos(ang)
    sin = jnp.sin(ang)
    zero = jnp.zeros_like(sin)

    def lanes(first, second):
        t = jnp.stack([first, second], axis=2).reshape(SEQ, HEAD_DIM)
        return jnp.tile(t, (1, LANE // HEAD_DIM))

    tab = jnp.concatenate([lanes(cos, cos), lanes(-sin, zero), lanes(zero, sin)], axis=1)
    ctx = jnp.concatenate([jnp.ones((CTX_LEN, LANE), F32), jnp.zeros((CTX_LEN, 2 * LANE), F32)], axis=1)
    return jnp.concatenate([tab, ctx], axis=0)


def _perm_heads_cols(w):
    return jnp.concatenate([w[:, h * HEAD_DIM:(h + 1) * HEAD_DIM] for h in HEAD_PERM], axis=1)


def _perm_heads_rows(w):
    return jnp.concatenate([w[h * HEAD_DIM:(h + 1) * HEAD_DIM] for h in HEAD_PERM], axis=0)


def _block_diag(w, per):
    n, a, b = w.shape
    eye = jnp.eye(per, dtype=w.dtype)
    w4 = w.reshape(n // per, per, a, b)
    return jnp.einsum("ihab,hk->ihakb", w4, eye).reshape(n // per, per * a, per * b)


def _w_in(w, q_off):
    w = w.astype(BF16)
    return jnp.concatenate([w[:, :q_off], _perm_heads_cols(w[:, q_off:q_off + MIX_W]), w[:, q_off + MIX_W:]], axis=1)


def _w_out(w):
    w = w.astype(BF16)
    return jnp.concatenate([w[:MIX_W], _perm_heads_rows(w[MIX_W:])], axis=0)


def kernel(x, c, ctx, c_ctx, mod_w, mod_b, norm_g, ffn_up, ffn_conv_w, ffn_conv_b, ffn_down, ev_w_in, ev_w_out, rg_conv_w, rg_conv_b, rg_wa, rg_ba, rg_wx, rg_bx, rg_lam, ga_qn, ga_kn, od_w_in, od_w_out, s5_lam_re, s5_lam_im, s5_log_step, s5_b_re, s5_b_im, s5_c_re, s5_c_im, s5_d, s5_glu_w, s5_glu_b, wa_qn, wa_kn, wa_sink):
    cvec = jnp.concatenate([c, jnp.broadcast_to(c_ctx[None], (BATCH, D_MODEL))], axis=0)
    mod = _mod_call(cvec, mod_w, mod_b).reshape(DEPTH, 2, BATCH, N_MOD * D_MODEL)
    rope = _rope_table()
    gm = _block_diag(jnp.full((LANE // HEAD_DIM, HEAD_DIM, HEAD_DIM), 1.0 / HEAD_DIM, F32), LANE // HEAD_DIM)[0].astype(BF16)

    h, u_tb, gate_tb, q, k, v, wu, wd = _in_call(
        (x, ctx), mod[0], norm_g, _w_in(ev_w_in[0], 2 * MIX_W), rope,
        ga_qn, ga_kn, gm, ffn_up, ffn_down, 0, n_tb=2, first_layer=True)
    w_gates = _block_diag(jnp.stack([rg_wa[0], rg_wx[0]]).astype(BF16).reshape(-1, HEAD_DIM, HEAD_DIM), 4)
    rg = (rg_conv_w, rg_conv_b, w_gates, rg_ba, rg_bx, rg_lam)
    b_out, hb, uc = _gattn_call(q, k, v, u_tb, *rg)
    a_out = _rg_fwd_call(uc, gate_tb, hb, *rg[2:])
    h = _post_call(h, a_out, b_out, _w_out(ev_w_out[0]), mod[0], norm_g, wu, ffn_conv_w, ffn_conv_b, wd, 0,
                   n_tiles=L_TOT // POST_T, out_batch_major=False)

    u_tb, q, k, v, wu, wd = _in_call(
        (h,), mod[1], norm_g, _w_in(od_w_in[0], MIX_W), rope,
        wa_qn, wa_kn, gm, ffn_up, ffn_down, 1, n_tb=1, first_layer=False)
    n_state = S5_GROUPS * S5_STATE
    bt_re = s5_b_re[0].transpose(0, 3, 1, 2).reshape(2, S5_GROUP, n_state)
    bt_im = s5_b_im[0].transpose(0, 3, 1, 2).reshape(2, S5_GROUP, n_state)
    log_step = jnp.repeat(s5_log_step[0], S5_STATE, axis=-1).reshape(2, 1, n_state)
    abr, abi, bbr, bbi = _s5_disc_call(s5_lam_re[0].reshape(2, 1, n_state), s5_lam_im[0].reshape(2, 1, n_state),
                                       log_step, bt_re, bt_im)
    gpb = S5_GROUPS // S5_NBLK
    eye = jnp.eye(gpb, dtype=BF16)
    bb = jnp.stack([bbr, bbi]).astype(BF16).reshape(2, 2, S5_GROUP, S5_NBLK, gpb, S5_STATE)
    w_drive = jnp.einsum("rdhbgp,gk->dbghrkp", bb, eye).reshape(
        2, S5_NBLK, gpb * S5_GROUP, 2 * S5_BLK_STATE)
    cc = jnp.stack([s5_c_re[0], -s5_c_im[0]]).astype(BF16).reshape(
        2, 2, S5_NBLK, gpb, S5_GROUP, S5_STATE)
    w_read = jnp.einsum("rdbghp,gk->dbrgpkh", cc, eye).reshape(
        2, S5_NBLK, 2 * S5_BLK_STATE, gpb * S5_GROUP)

    def state_rows(re, im):
        row = jnp.concatenate([re.reshape(2, S5_NBLK, S5_BLK_STATE), im.reshape(2, S5_NBLK, S5_BLK_STATE)],
                              axis=2).reshape(2, 1, S5_SW)
        return jnp.broadcast_to(row, (2, BATCH, S5_SW))

    s5 = (w_drive, state_rows(abr, abr), state_rows(-abi, abi), w_read)
    yb = _s5_call(u_tb, None, *s5, None, None, None, reverse=True)
    c_out = _s5_call(u_tb, yb, *s5, s5_d, s5_glu_w[0].astype(BF16), s5_glu_b, reverse=False)
    d_out = _wattn_call(q, k, v, wa_sink)
    return _post_call(h, c_out, d_out, _w_out(od_w_out[0]), mod[1], norm_g, wu, ffn_conv_w, ffn_conv_b, wd, 1,
                      n_tiles=SEQ // POST_T, out_batch_major=True)
```

```python
import functools
import math

import jax
import jax.numpy as jnp
from jax import lax
from jax.experimental import pallas as pl
from jax.experimental.pallas import tpu as pltpu

F32 = jnp.float32
BF16 = jnp.bfloat16

D_MODEL = 1024
BATCH = 8
SEQ = 2048
CTX_LEN = 256
L_TOT = SEQ + CTX_LEN
DEPTH = 2
GRID_W = 64
HEAD_DIM = 64
ROPE_BASE = 10000.0
NORM_EPS = 1e-6
WINDOW = 128
N_MOD = 6
MIX_W = 512
N_HEADS = 8
N_KV = 2
KV_W = N_KV * HEAD_DIM
V_W = 2 * KV_W
QKV_W = MIX_W + KV_W + KV_W
Q_SCALE = HEAD_DIM ** -0.5 * math.log2(math.e)
RG_CONV = 4
RG_C = 8.0
S5_GROUP = 16
S5_GROUPS = 32
S5_STATE = 64
S5_NBLK = 4
S5_BLK_STATE = (S5_GROUPS // S5_NBLK) * S5_STATE
FFN_HIDDEN = 2816
FFN_CONV = 3

LANE = 128
BF16_ROWS = 16
VMEM_LIMIT = 56 * 1024 * 1024

IN_T = 128
RG_T = 128
S5_T = 64
POST_T = 128
TQ_G = 256
G_NB = 4
W_NB = 8
FFN_CH = 256
HEAD_PERM = (0, 4, 1, 5, 2, 6, 3, 7)


def _cparams(sem):
    return pltpu.CompilerParams(dimension_semantics=sem, vmem_limit_bytes=VMEM_LIMIT)


def _dot(a, b):
    return jnp.dot(a, b, preferred_element_type=F32)


def _gelu_tanh(x):
    return x * (0.5 * (1.0 + jnp.tanh(math.sqrt(2.0 / math.pi) * (x + 0.044715 * (x * x * x)))))


def _rms_unit(x):
    ms = jnp.mean(x * x, axis=-1, keepdims=True)
    return x * lax.rsqrt(ms + NORM_EPS)


def _adaln(x, gain, scale, shift):
    y = _per_sample(_rms_unit(x), gain * (1.0 + scale), jnp.multiply)
    return _per_sample(y, shift, jnp.add)


def _to_time_major(x):
    b, t, c = x.shape
    return jnp.swapaxes(x, 0, 1).reshape(t * b, c)


def _to_batch_major(x):
    r, c = x.shape
    return jnp.swapaxes(x.reshape(r // BATCH, BATCH, c), 0, 1)


def _per_sample(x, v, op):
    r, c = x.shape
    return op(x.reshape(r // BATCH, BATCH, c), v[None]).reshape(r, c)


def _full(shape, single=False):
    kw = dict(pipeline_mode=pl.Buffered(1)) if single else {}
    return pl.BlockSpec(shape, lambda *_: (0,) * len(shape), **kw)


MOD_TN = 3072


def _mod_kernel(c_ref, wa_ref, wb_ref, b_ref, o_ref):
    c = c_ref[...]
    a = c * jax.nn.sigmoid(c)
    ah = a.astype(BF16)
    al = (a - ah.astype(F32)).astype(BF16)
    for i, w_ref in enumerate((wa_ref, wb_ref)):
        w = w_ref[0]
        wh = w.astype(BF16)
        wl = (w - wh.astype(F32)).astype(BF16)
        cols = slice(i * (MOD_TN // 2), (i + 1) * (MOD_TN // 2))
        o_ref[0, :, cols] = _dot(ah, wh) + _dot(ah, wl) + _dot(al, wh) + b_ref[0, :, cols]


def _mod_call(cvec, mod_w, mod_b):
    n = N_MOD * D_MODEL
    return pl.pallas_call(
        _mod_kernel,
        out_shape=jax.ShapeDtypeStruct((DEPTH, 2 * BATCH, n), F32),
        grid=(DEPTH, n // MOD_TN),
        in_specs=[
            pl.BlockSpec((2 * BATCH, D_MODEL), lambda l, k: (0, 0)),
            pl.BlockSpec((1, D_MODEL, MOD_TN // 2), lambda l, k: (l, 0, 2 * k)),
            pl.BlockSpec((1, D_MODEL, MOD_TN // 2), lambda l, k: (l, 0, 2 * k + 1)),
            pl.BlockSpec((1, 1, MOD_TN), lambda l, k: (l, 0, k)),
        ],
        out_specs=pl.BlockSpec((1, 2 * BATCH, MOD_TN), lambda l, k: (l, 0, k)),
        compiler_params=_cparams(("parallel", "parallel")),
        name="mod",
    )(cvec, mod_w, mod_w, mod_b.reshape(DEPTH, 1, n))


def _mod_spec(col, n_latent_tiles):
    return pl.BlockSpec((1, BATCH, D_MODEL), lambda j: (jnp.where(j < n_latent_tiles, 0, 1), 0, col))


IN_ROWS = IN_T * BATCH
IN_NLAT = SEQ // IN_T
IN_N = L_TOT // IN_T
IN_SUB = 2
CAST_STEPS = 16
assert CAST_STEPS <= IN_N
IN_SUB_T = IN_T // IN_SUB
IN_SUB_ROWS = IN_SUB_T * BATCH


def _in_kernel(fu_ref, fd_ref, *refs, n_tb, first_layer, layer):
    fu_out, fd_out = refs[-2:]
    refs = refs[:-2]

    @pl.when(pl.program_id(0) < CAST_STEPS)
    def _():
        fu_out[...] = fu_ref[0].astype(BF16)
        fd_out[...] = fd_ref[0].astype(BF16)

    if first_layer:
        x_ref, c_ref, sh_ref, sc_ref, g_ref, w_ref, rope_ref, gq_ref, gk_ref, gm_ref = refs[:10]
        h_out = refs[10]
        outs = refs[11:]
        is_latent = pl.program_id(0) < IN_NLAT
    else:
        h_ref, sh_ref, sc_ref, g_ref, w_ref, rope_ref, gq_ref, gk_ref, gm_ref = refs[:9]
        outs = refs[9:]
    q_ref, k_ref, v_ref = outs[n_tb:]
    off = n_tb * MIX_W
    gm = gm_ref[...]
    gq = jnp.concatenate([gq_ref[...]] * (LANE // HEAD_DIM), axis=1)
    gk = jnp.concatenate([gk_ref[...]] * (LANE // HEAD_DIM), axis=1)
    for s in range(IN_SUB):
        ts = slice(s * IN_SUB_T, (s + 1) * IN_SUB_T)
        rs = slice(s * IN_SUB_ROWS, (s + 1) * IN_SUB_ROWS)
        if first_layer:
            h = _to_time_major(jnp.where(is_latent, x_ref[:, ts, :], c_ref[:, ts, :]))
            h_out[rs, :] = h
        else:
            h = h_ref[rs, :]
        xn = _adaln(h, g_ref[layer, 0:1, :], sc_ref[0], sh_ref[0])
        proj = _dot(xn.astype(BF16), w_ref[...])
        for i in range(n_tb):
            outs[i][rs, :] = proj[:, i * MIX_W:(i + 1) * MIX_W]
        qkv = _to_batch_major(proj[:, off:off + QKV_W])
        cos = rope_ref[ts, 0:LANE][None]
        sin_up = rope_ref[ts, LANE:2 * LANE][None]
        sin_dn = rope_ref[ts, 2 * LANE:3 * LANE][None]
        for p in range(5):
            blk = qkv[:, :, p * LANE:(p + 1) * LANE].reshape(IN_SUB_ROWS, LANE)
            sq = blk * blk
            hi = sq.astype(BF16)
            lo = (sq - hi.astype(F32)).astype(BF16)
            ms = _dot(hi, gm) + _dot(lo, gm)
            g = gq if p < 4 else gk
            bn = blk * lax.rsqrt(ms + NORM_EPS) * g
            up = pltpu.roll(bn, LANE - HEAD_DIM // 4, 1).reshape(BATCH, IN_SUB_T, LANE)
            dn = pltpu.roll(bn, HEAD_DIM // 4, 1).reshape(BATCH, IN_SUB_T, LANE)
            ro = bn.reshape(BATCH, IN_SUB_T, LANE) * cos + up * sin_up + dn * sin_dn
            if p < 4:
                q_ref[:, ts, p * LANE:(p + 1) * LANE] = (ro * Q_SCALE).astype(BF16)
            else:
                k_ref[:, ts, :] = ro.astype(BF16)
        v = qkv[:, :, 5 * LANE:6 * LANE].astype(BF16)
        v_ref[:, ts, :] = jnp.concatenate([v, jnp.ones((BATCH, IN_SUB_T, LANE), BF16)], axis=2)


def _in_call(h_args, modt, g, w, rope, gq, gk, gm, ffn_up, ffn_down, layer, n_tb, first_layer):
    n = w.shape[1]
    tb = lambda width: pl.BlockSpec((IN_ROWS, width), lambda j: (j, 0))
    bm = lambda width: pl.BlockSpec((BATCH, IN_T, width), lambda j: (0, j, 0))
    up_rows = D_MODEL // CAST_STEPS
    down_rows = FFN_HIDDEN // CAST_STEPS
    slab = lambda j: jnp.minimum(j, CAST_STEPS - 1)
    if first_layer:
        h_specs = [
            pl.BlockSpec((BATCH, IN_T, D_MODEL), lambda j: (0, jnp.minimum(j, IN_NLAT - 1), 0)),
            pl.BlockSpec((BATCH, IN_T, D_MODEL), lambda j: (0, jnp.maximum(j - IN_NLAT, 0), 0)),
        ]
        extra_shape = [jax.ShapeDtypeStruct((L_TOT * BATCH, D_MODEL), F32)]
        extra_spec = [tb(D_MODEL)]
    else:
        h_specs = [tb(D_MODEL)]
        extra_shape, extra_spec = [], []
    return pl.pallas_call(
        functools.partial(_in_kernel, n_tb=n_tb, first_layer=first_layer, layer=layer),
        out_shape=extra_shape + [jax.ShapeDtypeStruct((L_TOT * BATCH, MIX_W), F32)] * n_tb + [
            jax.ShapeDtypeStruct((BATCH, L_TOT, MIX_W), BF16),
            jax.ShapeDtypeStruct((BATCH, L_TOT, KV_W), BF16),
            jax.ShapeDtypeStruct((BATCH, L_TOT, V_W), BF16),
            jax.ShapeDtypeStruct((D_MODEL, 2 * FFN_HIDDEN), BF16),
            jax.ShapeDtypeStruct((FFN_HIDDEN, D_MODEL), BF16),
        ],
        grid=(IN_N,),
        in_specs=[
            pl.BlockSpec((1, up_rows, 2 * FFN_HIDDEN), lambda j: (layer, slab(j), 0)),
            pl.BlockSpec((1, down_rows, D_MODEL), lambda j: (layer, slab(j), 0)),
        ] + h_specs + [
            _mod_spec(0, IN_NLAT), _mod_spec(1, IN_NLAT),
            _full(g.shape),
            _full((D_MODEL, n), single=True),
            pl.BlockSpec((IN_T, 3 * LANE), lambda j: (j, 0)),
            _full(gq.shape), _full(gk.shape), _full((LANE, LANE)),
        ],
        out_specs=extra_spec + [tb(MIX_W)] * n_tb + [bm(MIX_W), bm(KV_W), bm(V_W)] + [
            pl.BlockSpec((up_rows, 2 * FFN_HIDDEN), lambda j: (slab(j), 0)),
            pl.BlockSpec((down_rows, D_MODEL), lambda j: (slab(j), 0)),
        ],
        compiler_params=_cparams(("arbitrary",)),
        name="in_proj",
    )(ffn_up, ffn_down, *h_args, modt, modt, g, w, rope, gq, gk, gm)


def _chunk_of(i, reverse, n_latent, n_chunks):
    if reverse:
        return n_chunks - 1 - i
    return jnp.where(i < n_chunks - n_latent, i + n_latent, i - (n_chunks - n_latent))


RG_ROWS = RG_T * BATCH
RG_NC = L_TOT // RG_T
RG_NLAT = SEQ // RG_T


def _softplus(z):
    return jnp.maximum(z, 0.0) + jnp.log1p(jnp.exp(-jnp.abs(z)))


def _rg_kernel(*refs, reverse):
    i = pl.program_id(0)

    @pl.when(i == 0)
    def _():
        refs[-1][...] = jnp.zeros_like(refs[-1])

    _rg_chunk(i, refs, reverse)


def _rg_chunk(i, refs, reverse):
    if reverse:
        (u_ref, up_ref, un_ref, cw_ref, cb_ref, w_ref, ba_ref, bx_ref, lam_ref,
         out_ref, uc_out, xbuf, abuf, bbuf, hst) = refs
    else:
        uc_ref, gate_ref, hb_ref, w_ref, ba_ref, bx_ref, lam_ref, out_ref, abuf, bbuf, hst = refs
    d = 1 if reverse else 0

    if reverse:
        c = _chunk_of(i, reverse, RG_NLAT, RG_NC)
        prev_zero = jnp.logical_or(c == 0, c == RG_NLAT)
        next_zero = jnp.logical_or(c == RG_NLAT - 1, c == RG_NC - 1)
        xbuf[0:BATCH, :] = up_ref[...] * jnp.where(prev_zero, 0.0, 1.0)
        xbuf[BATCH:BATCH + RG_ROWS, :] = u_ref[...]
        xbuf[BATCH + RG_ROWS:3 * BATCH + RG_ROWS, :] = un_ref[...] * jnp.where(next_zero, 0.0, 1.0)
        uc = cb_ref[...]
        for k in range(RG_CONV):
            uc = uc + xbuf[k * BATCH:k * BATCH + RG_ROWS, :] * cw_ref[0, k:k + 1, :]
        uc_out[...] = uc
    else:
        uc = uc_ref[...]

    ub = uc.astype(BF16)
    half = MIX_W // 2

    def gate(which, b_ref):
        w0 = which * 4 + d * 2
        z = jnp.concatenate([_dot(ub[:, :half], w_ref[w0]), _dot(ub[:, half:], w_ref[w0 + 1])], axis=1)
        return jax.nn.sigmoid(z + b_ref[0, d:d + 1, :])

    r = gate(0, ba_ref)
    ig = gate(1, bx_ref)
    log_a = (-RG_C) * r * _softplus(-lam_ref[0, d:d + 1, :])
    a = jnp.exp(log_a)
    abuf[...] = a
    bbuf[...] = jnp.sqrt(-jnp.tanh(log_a) * (a * a + 1.0)) * (ig * uc)

    h = hst[...]
    for s in range(RG_T):
        t = (RG_T - 1 - s) if reverse else s
        rows = slice(t * BATCH, (t + 1) * BATCH)
        h = abuf[rows, :] * h + bbuf[rows, :]
        bbuf[rows, :] = h
    hst[...] = h

    if reverse:
        out_ref[...] = bbuf[...]
    else:
        y = bbuf[...] + hb_ref[...]
        out_ref[...] = (y * _gelu_tanh(gate_ref[...])).astype(BF16)


def _rg_fwd_call(uc, gate, hb, w_gates, ba, bx, lam):
    main = pl.BlockSpec((RG_ROWS, MIX_W), lambda i: (_chunk_of(i, False, RG_NLAT, RG_NC), 0))
    gates = [w_gates, ba, bx, lam]
    return pl.pallas_call(
        functools.partial(_rg_kernel, reverse=False),
        out_shape=jax.ShapeDtypeStruct((L_TOT * BATCH, MIX_W), BF16),
        grid=(RG_NC,),
        in_specs=[main, main, main] + [_full(p.shape) for p in gates],
        out_specs=main,
        scratch_shapes=[pltpu.VMEM((RG_ROWS, MIX_W), F32), pltpu.VMEM((RG_ROWS, MIX_W), F32),
                        pltpu.VMEM((BATCH, MIX_W), F32)],
        compiler_params=_cparams(("arbitrary",)),
        name="rglru_fwd",
    )(uc, gate, hb, *gates)


def _head_rows(q_ref, tq, b):
    lane = lax.broadcasted_iota(jnp.int32, (tq, LANE), 1)
    low = lane < HEAD_DIM
    zero = jnp.zeros((tq, LANE), BF16)
    parts = []
    for p in range(MIX_W // LANE):
        qb = q_ref[b, :, p * LANE:(p + 1) * LANE]
        parts.append(jnp.where(low, qb, zero))
        parts.append(jnp.where(low, zero, qb))
    return parts, low


def _merge_heads(outs, low):
    return jnp.concatenate([jnp.where(low, outs[2 * p], outs[2 * p + 1]) for p in range(MIX_W // LANE)], axis=1)


def _qk(q, k):
    return lax.dot_general(q, k, (((1,), (1,)), ((), ())), preferred_element_type=F32)


def _gattn_kernel(q_ref, k_ref, v_ref, *rest):
    rg_in, o_ref, rg_rest = rest[:9], rest[9], rest[10:]
    i = pl.program_id(0)
    j = lax.rem(i, G_TILES)

    @pl.when(i == 0)
    def _():
        rg_rest[-1][...] = jnp.zeros_like(rg_rest[-1])

    def attend(k0, nk):
        _rg_chunk(i, rg_in + rg_rest, True)
        for b in range(G_NB):
            heads, low = _head_rows(q_ref, TQ_G, b)
            outs = []
            for q in heads:
                s = _qk(q, k_ref[b, k0:k0 + nk, :])
                m = jnp.max(s, axis=-1, keepdims=True)
                p = jnp.exp2(s - m).astype(BF16)
                acc = _dot(p, v_ref[b, k0:k0 + nk, :])
                outs.append(acc[:, :KV_W] / acc[:, KV_W:])
            o_ref[b] = _merge_heads(outs, low).astype(BF16)

    @pl.when(j < SEQ // TQ_G)
    def _():
        attend(0, L_TOT)

    @pl.when(j >= SEQ // TQ_G)
    def _():
        attend(SEQ, CTX_LEN)


G_TILES = L_TOT // TQ_G
assert (BATCH // G_NB) * G_TILES == RG_NC


def _gattn_call(q, k, v, u, cw, cb, w_gates, ba, bx, lam):
    def cidx(i):
        return _chunk_of(i, True, RG_NLAT, RG_NC)

    main = pl.BlockSpec((RG_ROWS, MIX_W), lambda i: (cidx(i), 0))
    prev = pl.BlockSpec((BATCH, MIX_W), lambda i: (jnp.maximum(cidx(i) * RG_T - 1, 0), 0))
    n_next = L_TOT // 2
    nxt = pl.BlockSpec((2 * BATCH, MIX_W), lambda i: (jnp.minimum((cidx(i) + 1) * (RG_T // 2), n_next - 1), 0))
    params = [cw, cb, w_gates, ba, bx, lam]
    rows = jax.ShapeDtypeStruct((L_TOT * BATCH, MIX_W), F32)
    attn_spec = pl.BlockSpec((G_NB, TQ_G, MIX_W), lambda i: (i // G_TILES, lax.rem(i, G_TILES), 0))
    return pl.pallas_call(
        _gattn_kernel,
        out_shape=[jax.ShapeDtypeStruct((BATCH, L_TOT, MIX_W), BF16), rows, rows],
        grid=(RG_NC,),
        in_specs=[
            attn_spec,
            pl.BlockSpec((G_NB, L_TOT, KV_W), lambda i: (i // G_TILES, 0, 0), pipeline_mode=pl.Buffered(1)),
            pl.BlockSpec((G_NB, L_TOT, V_W), lambda i: (i // G_TILES, 0, 0), pipeline_mode=pl.Buffered(1)),
            main, prev, nxt,
        ] + [_full(p.shape) for p in params],
        out_specs=[attn_spec, main, main],
        scratch_shapes=[
            pltpu.VMEM((RG_ROWS + 3 * BATCH, MIX_W), F32),
            pltpu.VMEM((RG_ROWS, MIX_W), F32),
            pltpu.VMEM((RG_ROWS, MIX_W), F32),
            pltpu.VMEM((BATCH, MIX_W), F32),
        ],
        compiler_params=_cparams(("arbitrary",)),
        name="global_attn_rglru_bwd",
    )(q, k, v, u, u, u, *params)


TQ_W = 2 * WINDOW
N_WBLK = SEQ // WINDOW
N_WSTEP = SEQ // TQ_W


def _wattn_kernel(q_ref, k_ref, v_ref, sink_ref, o_ref):
    i = pl.program_id(1)
    prev_start = pl.multiple_of(jnp.maximum(2 * i - 1, 0) * WINDOW, WINDOW)
    mid_start = pl.multiple_of(i * TQ_W, TQ_W)
    next_start = pl.multiple_of(jnp.minimum(2 * i + 2, N_WBLK - 1) * WINDOW, WINDOW)

    def rows(ref, b):
        return jnp.concatenate([ref[b, SEQ:L_TOT, :], ref[b, pl.ds(prev_start, WINDOW), :],
                                ref[b, pl.ds(mid_start, TQ_W), :], ref[b, pl.ds(next_start, WINDOW), :]], axis=0)

    nk = CTX_LEN + 2 * WINDOW + TQ_W
    r = lax.broadcasted_iota(jnp.int32, (TQ_W, nk), 0)
    col = lax.broadcasted_iota(jnp.int32, (TQ_W, nk), 1)
    c = col - CTX_LEN
    ninf = -jnp.inf
    pen_prev = jnp.where(i > 0, 0.0, ninf)
    pen_next = jnp.where(i < N_WSTEP - 1, 0.0, ninf)
    edge = jnp.where(c < WINDOW, pen_prev, jnp.where(c >= WINDOW + TQ_W, pen_next, 0.0))
    band = jnp.where(c >= r, jnp.where(c <= r + 2 * WINDOW, edge, ninf), ninf)
    bias = jnp.where(col < CTX_LEN, 0.0, band)
    for b in range(W_NB):
        heads, low = _head_rows(q_ref, TQ_W, b)
        kk = rows(k_ref, b)
        vv = rows(v_ref, b)
        outs = []
        for hd, q in enumerate(heads):
            s = _qk(q, kk) + bias
            sink = sink_ref[0, HEAD_PERM[hd]] * math.log2(math.e)
            m = jnp.maximum(jnp.max(s, axis=-1, keepdims=True), sink)
            p = jnp.exp2(s - m).astype(BF16)
            acc = _dot(p, vv)
            outs.append(acc[:, :KV_W] / (acc[:, KV_W:] + jnp.exp2(sink - m)))
        o_ref[b] = _merge_heads(outs, low).astype(BF16)


def _wattn_call(q, k, v, sink):
    return pl.pallas_call(
        _wattn_kernel,
        out_shape=jax.ShapeDtypeStruct((BATCH, SEQ, MIX_W), BF16),
        grid=(BATCH // W_NB, N_WSTEP),
        in_specs=[
            pl.BlockSpec((W_NB, TQ_W, MIX_W), lambda b, i: (b, i, 0)),
            pl.BlockSpec((W_NB, L_TOT, KV_W), lambda b, i: (b, 0, 0), pipeline_mode=pl.Buffered(1)),
            pl.BlockSpec((W_NB, L_TOT, V_W), lambda b, i: (b, 0, 0), pipeline_mode=pl.Buffered(1)),
            pl.BlockSpec(memory_space=pltpu.SMEM),
        ],
        out_specs=pl.BlockSpec((W_NB, TQ_W, MIX_W), lambda b, i: (b, i, 0)),
        compiler_params=_cparams(("parallel", "arbitrary")),
        name="window_attn",
    )(q, k, v, sink)


def _s5_disc_kernel(lr_ref, li_ref, ls_ref, br_ref, bi_ref, ar_ref, ai_ref, bbr_ref, bbi_ref):
    lr = lr_ref[0]
    li = li_ref[0]
    dt = jnp.exp(ls_ref[0])
    mag = jnp.exp(lr * dt)
    ang = li * dt
    abr = mag * jnp.cos(ang)
    abi = mag * jnp.sin(ang)
    den = lr * lr + li * li
    nr = abr - 1.0
    kr = (nr * lr + abi * li) / den
    ki = (abi * lr - nr * li) / den
    br = br_ref[0]
    bi = bi_ref[0]
    ar_ref[0] = abr
    ai_ref[0] = abi
    bbr_ref[0] = kr * br - ki * bi
    bbi_ref[0] = kr * bi + ki * br


def _s5_disc_call(lam_re, lam_im, log_step, bt_re, bt_im):
    n = S5_GROUPS * S5_STATE
    row = pl.BlockSpec((1, 1, n), lambda d: (d, 0, 0))
    mat = pl.BlockSpec((1, S5_GROUP, n), lambda d: (d, 0, 0))
    return pl.pallas_call(
        _s5_disc_kernel,
        out_shape=[jax.ShapeDtypeStruct((2, 1, n), F32)] * 2 + [jax.ShapeDtypeStruct((2, S5_GROUP, n), F32)] * 2,
        grid=(2,),
        in_specs=[row, row, row, mat, mat],
        out_specs=[row, row, mat, mat],
        compiler_params=_cparams(("parallel",)),
        name="s5_discretize",
    )(lam_re, lam_im, log_step, bt_re, bt_im)


S5_ROWS = S5_T * BATCH
S5_NC = L_TOT // S5_T
S5_NLAT = SEQ // S5_T
S5_SW = 2 * S5_BLK_STATE * S5_NBLK


def _s5_kernel(*refs, reverse):
    if reverse:
        u2_ref, un_ref, wd_ref, ar_ref, ai_ref, wr_ref, out_ref, buf0, buf1, xst = refs
    else:
        (u2_ref, un_ref, yb2_ref, wd_ref, ar_ref, ai_ref, wr_ref, dsk_ref, gw_ref, gb_ref,
         out_ref, buf0, buf1, xst) = refs
    nb = 2 * S5_BLK_STATE
    lo, hi = slice(0, S5_ROWS), slice(S5_ROWS, 2 * S5_ROWS)
    first, second = (hi, lo) if reverse else (lo, hi)

    def drive(u, buf):
        ub = u.astype(BF16)
        for k in range(S5_NBLK):
            buf[:, k * nb:(k + 1) * nb] = _dot(ub[:, k * LANE:(k + 1) * LANE], wd_ref[0, k])

    def scan(buf, x):
        for s in range(S5_T):
            t = (S5_T - 1 - s) if reverse else s
            rows = slice(t * BATCH, (t + 1) * BATCH)
            halves = []
            for k in range(S5_NBLK):
                halves.append(x[:, k * nb + S5_BLK_STATE:(k + 1) * nb])
                halves.append(x[:, k * nb:k * nb + S5_BLK_STATE])
            xsw = jnp.concatenate(halves, axis=1)
            x = ar_ref[0] * x + ai_ref[0] * xsw + buf[rows, :]
            buf[rows, :] = x
        return x

    def readout(buf, rows):
        y = jnp.concatenate(
            [_dot(buf[:, k * nb:(k + 1) * nb].astype(BF16), wr_ref[0, k]) for k in range(S5_NBLK)], axis=1)
        if reverse:
            out_ref[rows, :] = y
        else:
            y = u2_ref[rows, :] * dsk_ref[...] + y + yb2_ref[rows, :]
            z = _gelu_tanh(y)
            out_ref[rows, :] = (z * jax.nn.sigmoid(_dot(z.astype(BF16), gw_ref[...]) + gb_ref[...])).astype(BF16)

    @pl.when(pl.program_id(0) == 0)
    def _():
        xst[...] = jnp.zeros_like(xst)
        drive(u2_ref[first, :], buf0)

    x = xst[...]
    drive(u2_ref[second, :], buf1)
    x = scan(buf0, x)
    readout(buf0, first)
    drive(un_ref[...], buf0)
    x = scan(buf1, x)
    readout(buf1, second)
    xst[...] = x


def _s5_call(u, yb, wd, ar, ai, wr, dsk, gw, gb, reverse):
    def chunk(p):
        return _chunk_of(p, reverse, S5_NLAT, S5_NC)

    pair = pl.BlockSpec((2 * S5_ROWS, MIX_W), lambda k: (chunk(2 * k) // 2, 0))
    nxt = pl.BlockSpec((S5_ROWS, MIX_W), lambda k: (chunk(jnp.minimum(2 * k + 2, S5_NC - 1)), 0))
    d = 1 if reverse else 0
    dir_specs = [pl.BlockSpec((1,) + a.shape[1:], lambda k, n=a.ndim: (d,) + (0,) * (n - 1),
                              pipeline_mode=pl.Buffered(1)) for a in (wd, ar, ai, wr)]
    if reverse:
        args = [u, u, wd, ar, ai, wr]
        specs = [pair, nxt] + dir_specs
        out_dtype = F32
    else:
        args = [u, u, yb, wd, ar, ai, wr, dsk, gw, gb]
        specs = [pair, nxt, pair] + dir_specs + [_full(a.shape) for a in (dsk, gw, gb)]
        out_dtype = BF16
    return pl.pallas_call(
        functools.partial(_s5_kernel, reverse=reverse),
        out_shape=jax.ShapeDtypeStruct((L_TOT * BATCH, MIX_W), out_dtype),
        grid=(S5_NC // 2,),
        in_specs=specs,
        out_specs=pair,
        scratch_shapes=[
            pltpu.VMEM((S5_ROWS, S5_SW), F32),
            pltpu.VMEM((S5_ROWS, S5_SW), F32),
            pltpu.VMEM((BATCH, S5_SW), F32),
        ],
        compiler_params=_cparams(("arbitrary",)),
        name="s5_bwd" if reverse else "s5_fwd",
    )(*args)


POST_ROWS = POST_T * BATCH
POST_HALO_T = BF16_ROWS // BATCH
POST_HALO = POST_HALO_T * BATCH
POST_NLAT = SEQ // POST_T
FFN_NCH = FFN_HIDDEN // FFN_CH


def _post_kernel(h_ref, hp_ref, hn_ref, a_ref, ap_ref, an_ref, b_ref, bp_ref, bn_ref, wo_ref,
                 g2_ref, sh_ref, sc_ref, g5_ref, g_ref, wu_ref, cw_ref, cb_ref, wd_ref, o_ref, *,
                 n_tiles, out_batch_major, layer):
    j = pl.program_id(0)
    first = jnp.logical_or(j == 0, j == POST_NLAT)
    last = jnp.logical_or(j == POST_NLAT - 1, j == n_tiles - 1)
    gate2 = g2_ref[0]

    def mix(h, a, b_tm):
        m = _dot(a, wo_ref[0:MIX_W, :]) + _dot(b_tm.astype(BF16), wo_ref[MIX_W:2 * MIX_W, :])
        return h + _per_sample(m, gate2, jnp.multiply)

    def norm(x):
        return _adaln(x, g_ref[layer, 1:2, :], sc_ref[0], sh_ref[0])

    def halo_b(ref, t0):
        x = jnp.swapaxes(ref[...].astype(F32), 0, 1)[t0:t0 + POST_HALO_T]
        return x.reshape(POST_HALO, MIX_W)

    h_mid = mix(h_ref[...], a_ref[...], _to_time_major(b_ref[...].astype(F32)))
    h_prev = mix(hp_ref[...], ap_ref[...], halo_b(bp_ref, BF16_ROWS - POST_HALO_T))
    h_next = mix(hn_ref[...], an_ref[...], halo_b(bn_ref, 0))
    xn = jnp.concatenate([
        (norm(h_prev) * jnp.where(first, 0.0, 1.0)).astype(BF16),
        norm(h_mid).astype(BF16),
        (norm(h_next) * jnp.where(last, 0.0, 1.0)).astype(BF16)], axis=0)

    acts = []
    for c in range(FFN_NCH):
        cv = slice(c * FFN_CH, (c + 1) * FFN_CH)
        cg = slice(FFN_HIDDEN + c * FFN_CH, FFN_HIDDEN + (c + 1) * FFN_CH)

        def conv(cols):
            hid = _dot(xn, wu_ref[:, cols])
            out = cb_ref[layer:layer + 1, cols]
            for k in range(FFN_CONV):
                r0 = POST_HALO + (k - 1) * BATCH
                out = out + hid[r0:r0 + POST_ROWS] * cw_ref[0, k:k + 1, cols]
            return out

        val = conv(cv)
        gate = conv(cg)
        acts.append((val * (gate * jax.nn.sigmoid(gate))).astype(BF16))
    ffn = _dot(jnp.concatenate(acts, axis=1), wd_ref[...])
    out = h_mid + _per_sample(ffn, g5_ref[0], jnp.multiply)
    if out_batch_major:
        o_ref[...] = _to_batch_major(out)
    else:
        o_ref[...] = out


def _post_call(h, a_tb, b_out, w_out, modt, g, wu, cw, cb, wd, layer, n_tiles, out_batch_major):
    per = POST_ROWS // POST_HALO
    n_halo = h.shape[0] // POST_HALO
    per_b = POST_T // BF16_ROWS
    n_halo_b = b_out.shape[1] // BF16_ROWS

    def main(width):
        return pl.BlockSpec((POST_ROWS, width), lambda j: (j, 0))

    def prev(width):
        return pl.BlockSpec((POST_HALO, width), lambda j: (jnp.maximum(j * per - 1, 0), 0))

    def nxt(width):
        return pl.BlockSpec((POST_HALO, width), lambda j: (jnp.minimum((j + 1) * per, n_halo - 1), 0))

    if out_batch_major:
        out_shape = jax.ShapeDtypeStruct((BATCH, n_tiles * POST_T, D_MODEL), F32)
        out_spec = pl.BlockSpec((BATCH, POST_T, D_MODEL), lambda j: (0, j, 0))
    else:
        out_shape = jax.ShapeDtypeStruct((n_tiles * POST_ROWS, D_MODEL), F32)
        out_spec = main(D_MODEL)
    return pl.pallas_call(
        functools.partial(_post_kernel, n_tiles=n_tiles, out_batch_major=out_batch_major, layer=layer),
        out_shape=out_shape,
        grid=(n_tiles,),
        in_specs=[
            main(D_MODEL), prev(D_MODEL), nxt(D_MODEL),
            main(MIX_W), prev(MIX_W), nxt(MIX_W),
            pl.BlockSpec((BATCH, POST_T, MIX_W), lambda j: (0, j, 0)),
            pl.BlockSpec((BATCH, BF16_ROWS, MIX_W), lambda j: (0, jnp.maximum(j * per_b - 1, 0), 0)),
            pl.BlockSpec((BATCH, BF16_ROWS, MIX_W), lambda j: (0, jnp.minimum((j + 1) * per_b, n_halo_b - 1), 0)),
            _full((2 * MIX_W, D_MODEL), single=True),
            _mod_spec(2, POST_NLAT), _mod_spec(3, POST_NLAT), _mod_spec(4, POST_NLAT), _mod_spec(5, POST_NLAT),
            _full(g.shape),
            _full((D_MODEL, 2 * FFN_HIDDEN), single=True),
            pl.BlockSpec((1, FFN_CONV, 2 * FFN_HIDDEN), lambda j: (layer, 0, 0)),
            _full(cb.shape),
            _full((FFN_HIDDEN, D_MODEL), single=True),
        ],
        out_specs=out_spec,
        compiler_params=_cparams(("parallel",)),
        name="post_ffn",
    )(h, h, h, a_tb, a_tb, a_tb, b_out, b_out, b_out, w_out, modt, modt, modt, modt, g, wu, cw, cb, wd)


def _rope_table():
    rows = SEQ // GRID_W
    row = jnp.repeat(jnp.arange(rows, dtype=F32), GRID_W)
    col = jnp.tile(jnp.arange(GRID_W, dtype=F32), rows)
    quarter = HEAD_DIM // 4
    inv_freq = ROPE_BASE ** (-jnp.arange(quarter, dtype=F32) / quarter)
    ang = jnp.stack([row[:, None] * inv_freq, col[:, None] * inv_freq], axis=1)
    cos = jnp.cos(ang)
    sin = jnp.sin(ang)
    zero = jnp.zeros_like(sin)

    def lanes(first, second):
        t = jnp.stack([first, second], axis=2).reshape(SEQ, HEAD_DIM)
        return jnp.tile(t, (1, LANE // HEAD_DIM))

    tab = jnp.concatenate([lanes(cos, cos), lanes(-sin, zero), lanes(zero, sin)], axis=1)
    ctx = jnp.concatenate([jnp.ones((CTX_LEN, LANE), F32), jnp.zeros((CTX_LEN, 2 * LANE), F32)], axis=1)
    return jnp.concatenate([tab, ctx], axis=0)


def _perm_heads_cols(w):
    return jnp.concatenate([w[:, h * HEAD_DIM:(h + 1) * HEAD_DIM] for h in HEAD_PERM], axis=1)


def _perm_heads_rows(w):
    return jnp.concatenate([w[h * HEAD_DIM:(h + 1) * HEAD_DIM] for h in HEAD_PERM], axis=0)


def _block_diag(w, per):
    n, a, b = w.shape
    eye = jnp.eye(per, dtype=w.dtype)
    w4 = w.reshape(n // per, per, a, b)
    return jnp.einsum("ihab,hk->ihakb", w4, eye).reshape(n // per, per * a, per * b)


def _w_in(w, q_off):
    w = w.astype(BF16)
    return jnp.concatenate([w[:, :q_off], _perm_heads_cols(w[:, q_off:q_off + MIX_W]), w[:, q_off + MIX_W:]], axis=1)


def _w_out(w):
    w = w.astype(BF16)
    return jnp.concatenate([w[:MIX_W], _perm_heads_rows(w[MIX_W:])], axis=0)


def kernel(x, c, ctx, c_ctx, mod_w, mod_b, norm_g, ffn_up, ffn_conv_w, ffn_conv_b, ffn_down, ev_w_in, ev_w_out, rg_conv_w, rg_conv_b, rg_wa, rg_ba, rg_wx, rg_bx, rg_lam, ga_qn, ga_kn, od_w_in, od_w_out, s5_lam_re, s5_lam_im, s5_log_step, s5_b_re, s5_b_im, s5_c_re, s5_c_im, s5_d, s5_glu_w, s5_glu_b, wa_qn, wa_kn, wa_sink):
    cvec = jnp.concatenate([c, jnp.broadcast_to(c_ctx[None], (BATCH, D_MODEL))], axis=0)
    mod = _mod_call(cvec, mod_w, mod_b).reshape(DEPTH, 2, BATCH, N_MOD * D_MODEL)
    rope = _rope_table()
    gm = _block_diag(jnp.full((LANE // HEAD_DIM, HEAD_DIM, HEAD_DIM), 1.0 / HEAD_DIM, F32), LANE // HEAD_DIM)[0].astype(BF16)

    h, u_tb, gate_tb, q, k, v, wu, wd = _in_call(
        (x, ctx), mod[0], norm_g, _w_in(ev_w_in[0], 2 * MIX_W), rope,
        ga_qn, ga_kn, gm, ffn_up, ffn_down, 0, n_tb=2, first_layer=True)
    w_gates = _block_diag(jnp.stack([rg_wa[0], rg_wx[0]]).astype(BF16).reshape(-1, HEAD_DIM, HEAD_DIM), 4)
    rg = (rg_conv_w, rg_conv_b, w_gates, rg_ba, rg_bx, rg_lam)
    b_out, hb, uc = _gattn_call(q, k, v, u_tb, *rg)
    a_out = _rg_fwd_call(uc, gate_tb, hb, *rg[2:])
    h = _post_call(h, a_out, b_out, _w_out(ev_w_out[0]), mod[0], norm_g, wu, ffn_conv_w, ffn_conv_b, wd, 0,
                   n_tiles=L_TOT // POST_T, out_batch_major=False)

    u_tb, q, k, v, wu, wd = _in_call(
        (h,), mod[1], norm_g, _w_in(od_w_in[0], MIX_W), rope,
        wa_qn, wa_kn, gm, ffn_up, ffn_down, 1, n_tb=1, first_layer=False)
    n_state = S5_GROUPS * S5_STATE
    bt_re = s5_b_re[0].transpose(0, 3, 1, 2).reshape(2, S5_GROUP, n_state)
    bt_im = s5_b_im[0].transpose(0, 3, 1, 2).reshape(2, S5_GROUP, n_state)
    log_step = jnp.repeat(s5_log_step[0], S5_STATE, axis=-1).reshape(2, 1, n_state)
    abr, abi, bbr, bbi = _s5_disc_call(s5_lam_re[0].reshape(2, 1, n_state), s5_lam_im[0].reshape(2, 1, n_state),
                                       log_step, bt_re, bt_im)
    gpb = S5_GROUPS // S5_NBLK
    eye = jnp.eye(gpb, dtype=BF16)
    bb = jnp.stack([bbr, bbi]).astype(BF16).reshape(2, 2, S5_GROUP, S5_NBLK, gpb, S5_STATE)
    w_drive = jnp.einsum("rdhbgp,gk->dbghrkp", bb, eye).reshape(
        2, S5_NBLK, gpb * S5_GROUP, 2 * S5_BLK_STATE)
    cc = jnp.stack([s5_c_re[0], -s5_c_im[0]]).astype(BF16).reshape(
        2, 2, S5_NBLK, gpb, S5_GROUP, S5_STATE)
    w_read = jnp.einsum("rdbghp,gk->dbrgpkh", cc, eye).reshape(
        2, S5_NBLK, 2 * S5_BLK_STATE, gpb * S5_GROUP)

    def state_rows(re, im):
        row = jnp.concatenate([re.reshape(2, S5_NBLK, S5_BLK_STATE), im.reshape(2, S5_NBLK, S5_BLK_STATE)],
                              axis=2).reshape(2, 1, S5_SW)
        return jnp.broadcast_to(row, (2, BATCH, S5_SW))

    s5 = (w_drive, state_rows(abr, abr), state_rows(-abi, abi), w_read)
    yb = _s5_call(u_tb, None, *s5, None, None, None, reverse=True)
    c_out = _s5_call(u_tb, yb, *s5, s5_d, s5_glu_w[0].astype(BF16), s5_glu_b, reverse=False)
    d_out = _wattn_call(q, k, v, wa_sink)
    return _post_call(h, c_out, d_out, _w_out(od_w_out[0]), mod[1], norm_g, wu, ffn_conv_w, ffn_conv_b, wd, 1,
                      n_tiles=SEQ // POST_T, out_batch_major=True)
```
